```python
import math
import jax, jax.numpy as jnp
from jax import lax
import numpy as np

D_MODEL = 1024
BATCH = 2
SEQ = 8192
DEPTH = 4

HEAD_DIM = 64
N_HEADS_A = 4
N_HEADS_B = 4
N_HEADS_C = 4
N_HEADS_D = 4
N_KV_D = 2
DIFF_DIM = HEAD_DIM // 2
IDX_HEADS = 8
IDX_DIM = 64
TOPK_MAX = 256
WINDOW = 128
BLOCK = 128
D_FF = 4 * D_MODEL
D_PLE = 256
ROPE_THETA = 10000.0
EPS = 1e-6
NEG_INF = -1e30
MIX_WIDTH = (N_HEADS_A + N_HEADS_B + N_HEADS_C + N_HEADS_D) * HEAD_DIM
SPLIT_WIDTHS = (
    N_HEADS_A * HEAD_DIM, N_HEADS_A * HEAD_DIM, N_HEADS_A * HEAD_DIM,
    IDX_HEADS * IDX_DIM, IDX_DIM, IDX_HEADS,
    N_HEADS_B * HEAD_DIM, N_HEADS_B * HEAD_DIM, N_HEADS_B * HEAD_DIM,
    N_HEADS_B,
    N_HEADS_C * HEAD_DIM, N_HEADS_C * HEAD_DIM, N_HEADS_C * HEAD_DIM,
    N_HEADS_D * HEAD_DIM, N_KV_D * HEAD_DIM, N_KV_D * HEAD_DIM,
)
D_IN = sum(SPLIT_WIDTHS)

kernel_name = 'hybrid_parallel_heads_dsa_fox_diff_swa'


def rmsnorm(x, g):
    xf = x.astype(jnp.float32)
    y = xf * lax.rsqrt(jnp.mean(xf * xf, axis=-1, keepdims=True) + EPS)
    return (y * g.astype(jnp.float32)).astype(x.dtype)


def rope_tables(positions, dim):
    half = dim // 2
    inv_freq = ROPE_THETA ** (-jnp.arange(half, dtype=jnp.float32) / half)
    ang = positions.astype(jnp.float32)[..., None] * inv_freq
    return jnp.cos(ang)[:, :, None, :], jnp.sin(ang)[:, :, None, :]


def apply_rope(x, tables):
    cos, sin = tables
    half = x.shape[-1] // 2
    xf = x.astype(jnp.float32)
    x1, x2 = xf[..., :half], xf[..., half:]
    return jnp.concatenate([x1 * cos - x2 * sin, x2 * cos + x1 * sin], axis=-1).astype(x.dtype)


def to_blocks(a):
    b, s = a.shape[:2]
    return jnp.moveaxis(a.reshape(b, s // BLOCK, BLOCK, *a.shape[2:]), 1, 0)


def from_blocks(o):
    nb, b = o.shape[:2]
    return jnp.moveaxis(o, 0, 1).reshape(b, nb * BLOCK, *o.shape[3:])


def dsa_attention(q, k, v, iq, ik, iw):
    s_len = k.shape[1]
    topk = min(TOPK_MAX, s_len // 4)
    key_pos = jnp.arange(s_len)
    scale = HEAD_DIM ** -0.5

    def one_block(args):
        qb, iqb, iwb, n = args
        t = n * BLOCK + jnp.arange(BLOCK)
        causal = key_pos[None, :] <= t[:, None]
        dots = jnp.einsum('bthd,bsd->bths', iqb, ik)
        score = jnp.einsum('bth,bths->bts', iwb, jax.nn.relu(dots)).astype(jnp.float32)
        score = jnp.where(causal[None], score, NEG_INF)
        _, idx = lax.top_k(score, topk)
        valid = idx <= t[None, :, None]
        k_sel = jax.vmap(lambda kk, ii: kk[ii])(k, idx)
        v_sel = jax.vmap(lambda vv, ii: vv[ii])(v, idx)
        logits = jnp.einsum('bthd,btjhd->bhtj', qb, k_sel).astype(jnp.float32) * scale
        logits = jnp.where(valid[:, None], logits, NEG_INF)
        probs = jax.nn.softmax(logits, axis=-1).astype(v.dtype)
        return jnp.einsum('bhtj,btjhd->bthd', probs, v_sel)

    nb = s_len // BLOCK
    out = lax.map(one_block, (to_blocks(q), to_blocks(iq), to_blocks(iw), jnp.arange(nb)))
    return from_blocks(out)


def forgetting_attention(q, k, v, log_f):
    s_len = k.shape[1]
    cum_bsh = jnp.cumsum(log_f, axis=1)
    cum = jnp.moveaxis(cum_bsh, 2, 1)
    key_pos = jnp.arange(s_len)
    scale = HEAD_DIM ** -0.5

    def one_block(args):
        qb, cb, n = args
        t = n * BLOCK + jnp.arange(BLOCK)
        causal = key_pos[None, :] <= t[:, None]
        logits = jnp.einsum('bthd,bshd->bhts', qb, k).astype(jnp.float32) * scale
        decay = jnp.moveaxis(cb, 2, 1)[..., :, None] - cum[:, :, None, :]
        logits = jnp.where(causal[None, None], logits + decay, NEG_INF)
        probs = jax.nn.softmax(logits, axis=-1).astype(v.dtype)
        return jnp.einsum('bhts,bshd->bthd', probs, v)

    nb = s_len // BLOCK
    out = lax.map(one_block, (to_blocks(q), to_blocks(cum_bsh), jnp.arange(nb)))
    return from_blocks(out)


def differential_attention(q, k, v, lam, lam_init, subln_g):
    s_len = k.shape[1]
    key_pos = jnp.arange(s_len)
    scale = DIFF_DIM ** -0.5

    def one_block(args):
        qb, n = args
        t = n * BLOCK + jnp.arange(BLOCK)
        causal = key_pos[None, :] <= t[:, None]
        logits = jnp.einsum('bthmd,bshmd->bmhts', qb, k).astype(jnp.float32) * scale
        logits = jnp.where(causal[None, None, None], logits, NEG_INF)
        probs = jax.nn.softmax(logits, axis=-1)
        diff = probs[:, 0] - lam * probs[:, 1]
        return jnp.einsum('bhts,bshd->bthd', diff.astype(v.dtype), v)

    nb = s_len // BLOCK
    out = from_blocks(lax.map(one_block, (to_blocks(q), jnp.arange(nb))))
    return rmsnorm(out, subln_g) * (1.0 - lam_init)


def sliding_window_sink_attention(q, k, v, sinks):
    b, s_len, n_h, dh = q.shape
    n_kv = k.shape[2]
    grp = n_h // n_kv
    nb = s_len // BLOCK
    qb = q.reshape(b, nb, BLOCK, n_kv, grp, dh)
    kb = k.reshape(b, nb, BLOCK, n_kv, dh)
    vb = v.reshape(b, nb, BLOCK, n_kv, dh)
    pad = ((0, 0), (1, 0), (0, 0), (0, 0), (0, 0))
    k_win = jnp.concatenate([jnp.pad(kb, pad)[:, :-1], kb], axis=2)
    v_win = jnp.concatenate([jnp.pad(vb, pad)[:, :-1], vb], axis=2)
    logits = jnp.einsum('bnikgd,bnjkd->bnkgij', qb, k_win).astype(jnp.float32) * (dh ** -0.5)
    i = jnp.arange(BLOCK)[:, None]
    j = jnp.arange(2 * BLOCK)[None, :]
    rel = i + BLOCK - j
    band = (rel >= 0) & (rel < WINDOW)
    in_range = (jnp.arange(nb)[:, None, None] > 0) | (j[None] >= BLOCK)
    mask = band[None] & in_range
    logits = jnp.where(mask[None, :, None, None], logits, NEG_INF)
    sink = jnp.broadcast_to(sinks.astype(jnp.float32).reshape(n_kv, grp)[None, None, :, :, None, None],
                            logits.shape[:-1] + (1,))
    probs = jax.nn.softmax(jnp.concatenate([logits, sink], axis=-1), axis=-1)[..., :-1]
    out = jnp.einsum('bnkgij,bnjkd->bnikgd', probs.astype(v.dtype), v_win)
    return out.reshape(b, s_len, n_h, dh)


def hybrid_layer(x, p_i, rope64, rope32, lam_init, w_in, b_forget, lq1, lk1, lq2, lk2,
                 diff_subln, sinks, w_out, g_pre_mix, g_post_mix, g_pre_mlp, g_post_mlp,
                 w_up, w_down, w_ple_proj, w_ple_gate):
    b, s_len, _ = x.shape
    h = rmsnorm(x, g_pre_mix)
    proj = h @ w_in
    points = np.cumsum(SPLIT_WIDTHS)[:-1].tolist()
    (a_q, a_k, a_v, a_iq, a_ik, a_iw, b_q, b_k, b_v, b_f,
     c_q, c_k, c_v, d_q, d_k, d_v) = jnp.split(proj, points, axis=-1)

    def heads(t, n, d):
        return t.reshape(b, s_len, n, d)

    iw = a_iw * (IDX_HEADS ** -0.5 * IDX_DIM ** -0.5)
    o_a = dsa_attention(apply_rope(heads(a_q, N_HEADS_A, HEAD_DIM), rope64),
                        apply_rope(heads(a_k, N_HEADS_A, HEAD_DIM), rope64),
                        heads(a_v, N_HEADS_A, HEAD_DIM),
                        apply_rope(heads(a_iq, IDX_HEADS, IDX_DIM), rope64),
                        apply_rope(a_ik[:, :, None, :], rope64)[:, :, 0],
                        iw)
    log_f = jax.nn.log_sigmoid((b_f + b_forget).astype(jnp.float32))
    o_b = forgetting_attention(heads(b_q, N_HEADS_B, HEAD_DIM), heads(b_k, N_HEADS_B, HEAD_DIM),
                               heads(b_v, N_HEADS_B, HEAD_DIM), log_f)
    cq = apply_rope(heads(c_q, 2 * N_HEADS_C, DIFF_DIM), rope32).reshape(b, s_len, N_HEADS_C, 2, DIFF_DIM)
    ck = apply_rope(heads(c_k, 2 * N_HEADS_C, DIFF_DIM), rope32).reshape(b, s_len, N_HEADS_C, 2, DIFF_DIM)
    f32 = jnp.float32
    lam = (jnp.exp(jnp.sum(lq1.astype(f32) * lk1.astype(f32)))
           - jnp.exp(jnp.sum(lq2.astype(f32) * lk2.astype(f32))) + lam_init)
    o_c = differential_attention(cq, ck, heads(c_v, N_HEADS_C, HEAD_DIM), lam, lam_init, diff_subln)
    o_d = sliding_window_sink_attention(apply_rope(heads(d_q, N_HEADS_D, HEAD_DIM), rope64),
                                        apply_rope(heads(d_k, N_KV_D, HEAD_DIM), rope64),
                                        heads(d_v, N_KV_D, HEAD_DIM), sinks)

    mix = jnp.concatenate([o_a.reshape(b, s_len, -1), o_b.reshape(b, s_len, -1),
                           o_c.reshape(b, s_len, -1), o_d.reshape(b, s_len, -1)], axis=-1)
    x = x + rmsnorm(mix @ w_out, g_post_mix)

    h = rmsnorm(x, g_pre_mlp)
    m = jnp.square(jax.nn.relu(h @ w_up)) @ w_down
    x = x + rmsnorm(m, g_post_mlp)

    x = x + jax.nn.sigmoid(x @ w_ple_gate) * (p_i @ w_ple_proj)
    return x


def setup_inputs(seed: int = 0) -> dict:
    key = jax.random.key(seed)
    ks = jax.random.split(key, 22)

    def normal(k, shape, scale):
        return jax.random.normal(k, shape, jnp.float32) * scale

    def gain(k, shape):
        return 1.0 + 0.02 * jax.random.normal(k, shape, jnp.float32)

    return {
        'x': normal(ks[0], (BATCH, SEQ, D_MODEL), 1.0),
        'p': normal(ks[1], (DEPTH, BATCH, SEQ, D_PLE), 1.0),
        'positions': jnp.broadcast_to(jnp.arange(SEQ, dtype=jnp.int32), (BATCH, SEQ)),
        'w_in': normal(ks[2], (DEPTH, D_MODEL, D_IN), D_MODEL ** -0.5),
        'b_forget': 2.0 + normal(ks[3], (DEPTH, N_HEADS_B), 0.5),
        'lambda_q1': normal(ks[4], (DEPTH, DIFF_DIM), 0.1),
        'lambda_k1': normal(ks[5], (DEPTH, DIFF_DIM), 0.1),
        'lambda_q2': normal(ks[6], (DEPTH, DIFF_DIM), 0.1),
        'lambda_k2': normal(ks[7], (DEPTH, DIFF_DIM), 0.1),
        'diff_subln': gain(ks[8], (DEPTH, HEAD_DIM)),
        'sinks': normal(ks[9], (DEPTH, N_HEADS_D), 1.0),
        'w_out': normal(ks[10], (DEPTH, MIX_WIDTH, D_MODEL), MIX_WIDTH ** -0.5),
        'norm_pre_mix': gain(ks[11], (DEPTH, D_MODEL)),
        'norm_post_mix': gain(ks[12], (DEPTH, D_MODEL)),
        'norm_pre_mlp': gain(ks[13], (DEPTH, D_MODEL)),
        'norm_post_mlp': gain(ks[14], (DEPTH, D_MODEL)),
        'w_mlp_up': normal(ks[15], (DEPTH, D_MODEL, D_FF), D_MODEL ** -0.5),
        'w_mlp_down': normal(ks[16], (DEPTH, D_FF, D_MODEL), D_FF ** -0.5),
        'w_ple_proj': normal(ks[17], (DEPTH, D_PLE, D_MODEL), D_PLE ** -0.5),
        'w_ple_gate': normal(ks[18], (DEPTH, D_MODEL, D_MODEL), D_MODEL ** -0.5),
    }


def reference(x, p, positions, w_in, b_forget, lambda_q1, lambda_k1, lambda_q2, lambda_k2,
              diff_subln, sinks, w_out, norm_pre_mix, norm_post_mix, norm_pre_mlp, norm_post_mlp,
              w_mlp_up, w_mlp_down, w_ple_proj, w_ple_gate):
    rope64 = rope_tables(positions, HEAD_DIM)
    rope32 = rope_tables(positions, DIFF_DIM)
    for i in range(DEPTH):
        lam_init = 0.8 - 0.6 * math.exp(-0.3 * i)
        x = hybrid_layer(x, p[i], rope64, rope32, lam_init, w_in[i], b_forget[i],
                         lambda_q1[i], lambda_k1[i], lambda_q2[i], lambda_k2[i],
                         diff_subln[i], sinks[i], w_out[i], norm_pre_mix[i], norm_post_mix[i],
                         norm_pre_mlp[i], norm_post_mlp[i], w_mlp_up[i], w_mlp_down[i],
                         w_ple_proj[i], w_ple_gate[i])
    return x
```

```python
import functools
import math

import jax
import jax.numpy as jnp
from jax import lax
from jax.experimental import pallas as pl
from jax.experimental.pallas import tpu as pltpu

F32, BF16, I32 = jnp.float32, jnp.bfloat16, jnp.int32

D_MODEL = 1024
HEAD_DIM = 64
DIFF_DIM = 32
IDX_HEADS = 8
IDX_DIM = 64
TOPK_MAX = 256
WINDOW = 128
D_FF = 4 * D_MODEL
D_PLE = 256
ROPE_THETA = 10000.0
EPS = 1e-6
NEG_INF = -1e30
LANES = 128
GROUP = 256
INT_MIN = -2147483648

_SEC = {}
_o = 0
for _name, _w in (("a_q", 256), ("a_k", 256), ("a_v", 256), ("iq", 512), ("ik", 64), ("iw", 8),
                  ("b_q", 256), ("b_k", 256), ("b_v", 256), ("b_f", 4),
                  ("c_q", 256), ("c_k", 256), ("c_v", 256), ("d_q", 256), ("d_k", 128), ("d_v", 128)):
    _SEC[_name] = (_o, _w)
    _o += _w
D_IN = _o

MAIN_GROUPS = ("a_q", "a_k", "a_v", "b_q", "b_k", "b_v", "c_q", "c_k", "c_v", "d_q", "d_k", "d_v")
GIDX = {n: i for i, n in enumerate(MAIN_GROUPS)}
MAIN_W = GROUP * len(MAIN_GROUPS)
ROPE64_GROUPS = ("a_q", "a_k", "d_q", "d_k")
ROPE32_GROUPS = ("c_q", "c_k")
Q_SCALE = {"a_q": HEAD_DIM ** -0.5, "b_q": HEAD_DIM ** -0.5, "c_q": DIFF_DIM ** -0.5, "d_q": HEAD_DIM ** -0.5}
IDX_W = 512 + LANES
MISC_IW = 64
MISC_F = 72

VMEM_LIMIT = 56 * 1024 * 1024


def _cparams(sem):
    return pltpu.CompilerParams(dimension_semantics=sem, vmem_limit_bytes=VMEM_LIMIT)


def _rms(x, g):
    return x * lax.rsqrt(jnp.mean(x * x, axis=-1, keepdims=True) + EPS) * g


def _dot_nt(a, b, precision=None):
    return lax.dot_general(a, b, (((1,), (1,)), ((), ())), precision=precision,
                           preferred_element_type=F32)


def _rope_chunk(xc, cos, sin_signed, half, lane):
    fwd = pltpu.roll(xc, LANES - half, axis=1)
    bwd = pltpu.roll(xc, half, axis=1)
    partner = jnp.where((lane % (2 * half)) < half, fwd, bwd)
    return xc * cos + partner * sin_signed


def _proj_kernel(x_ref, g_ref, wm_ref, wi_ref, c64_ref, s64_ref, c32_ref, s32_ref, bf_ref,
                 main_ref, iq_ref, misc_ref, carry_ref, *, tm):
    t = pl.program_id(1)
    x = x_ref[0]
    h = _rms(x, g_ref[...])
    pm = jnp.dot(h.astype(BF16), wm_ref[...], preferred_element_type=F32)
    pi = jnp.dot(h, wi_ref[...], precision=lax.Precision.HIGHEST,
                 preferred_element_type=F32)
    lane = lax.broadcasted_iota(I32, (1, LANES), 1)
    c64, s64, c32, s32 = c64_ref[0], s64_ref[0], c32_ref[0], s32_ref[0]

    for name in MAIN_GROUPS:
        base = GIDX[name] * GROUP
        for c in range(GROUP // LANES):
            lo = base + c * LANES
            v = pm[:, lo:lo + LANES]
            if name in ROPE64_GROUPS:
                v = _rope_chunk(v, c64, s64, HEAD_DIM // 2, lane)
            elif name in ROPE32_GROUPS:
                v = _rope_chunk(v, c32, s32, DIFF_DIM // 2, lane)
            if name in Q_SCALE:
                v = v * Q_SCALE[name]
            main_ref[0, :, lo:lo + LANES] = v.astype(BF16)

    for c in range(512 // LANES):
        v = _rope_chunk(pi[:, c * LANES:(c + 1) * LANES], c64, s64, IDX_DIM // 2, lane)
        iq_ref[0, 2 * c] = jnp.where(lane < IDX_DIM, v, 0.0)
        iq_ref[0, 2 * c + 1] = jnp.where(lane < IDX_DIM, pltpu.roll(v, IDX_DIM, axis=1), 0.0)

    misc = pi[:, 512:512 + LANES]
    ik = _rope_chunk(misc, c64, s64, IDX_DIM // 2, lane)
    iw = misc * (IDX_HEADS ** -0.5 * IDX_DIM ** -0.5)
    z = misc + bf_ref[...]
    logf = jnp.minimum(z, 0.0) - jnp.log1p(jnp.exp(-jnp.abs(z)))
    is_f = (lane >= MISC_F) & (lane < MISC_F + 4)
    logf = jnp.where(is_f, logf, 0.0)
    row = lax.broadcasted_iota(I32, (tm, tm), 0)
    col = lax.broadcasted_iota(I32, (tm, tm), 1)
    tri = (col <= row).astype(F32)

    @pl.when(t == 0)
    def _():
        carry_ref[...] = jnp.zeros_like(carry_ref)

    cum = jnp.dot(tri, logf, precision=lax.Precision.HIGHEST, preferred_element_type=F32) + carry_ref[...]
    carry_ref[...] = cum[tm - 1:tm, :]
    misc_ref[0] = jnp.where(lane < IDX_DIM, ik, jnp.where(lane < MISC_F, iw, cum))


def _proj_call(x, g, w_main, w_idx, tabs, bf_row, *, tm):
    b, s, d = x.shape
    tok = lambda w: pl.BlockSpec((1, tm, w), lambda bi, ti: (bi, ti, 0))
    full = lambda a: pl.BlockSpec(a.shape, lambda bi, ti: (0,) * a.ndim)
    return pl.pallas_call(
        functools.partial(_proj_kernel, tm=tm),
        grid=(b, s // tm),
        in_specs=[tok(d), full(g), full(w_main), full(w_idx), tok(LANES), tok(LANES), tok(LANES), tok(LANES),
                  full(bf_row)],
        out_specs=[tok(MAIN_W),
                   pl.BlockSpec((1, IDX_HEADS, tm, LANES), lambda bi, ti: (bi, 0, ti, 0)),
                   tok(LANES)],
        out_shape=[jax.ShapeDtypeStruct((b, s, MAIN_W), BF16),
                   jax.ShapeDtypeStruct((b, IDX_HEADS, s, LANES), F32),
                   jax.ShapeDtypeStruct((b, s, LANES), F32)],
        scratch_shapes=[pltpu.VMEM((1, LANES), F32)],
        compiler_params=_cparams(("arbitrary", "arbitrary")),
        name="proj",
    )(x, g, w_main, w_idx, *tabs, bf_row)


def _softmax_step(s, m_ref, l_ref, idx):
    m_old = m_ref[idx]
    m_new = jnp.maximum(m_old, jnp.max(s, axis=-1, keepdims=True))
    alpha = jnp.exp(m_old - m_new)
    p = jnp.exp(s - m_new)
    l_ref[idx] = alpha * l_ref[idx] + jnp.sum(p, axis=-1, keepdims=True)
    m_ref[idx] = m_new
    return p, alpha


def _init_state(m_ref, l_ref, acc_ref):
    m_ref[...] = jnp.full(m_ref.shape, NEG_INF, F32)
    l_ref[...] = jnp.zeros(l_ref.shape, F32)
    acc_ref[...] = jnp.zeros(acc_ref.shape, F32)


def _causal(tq, tk):
    return lax.broadcasted_iota(I32, (tq, tk), 1) <= lax.broadcasted_iota(I32, (tq, tk), 0)


def _dsa_kernel(iq_ref, miscq_ref, misck_ref, q_ref, k_ref, v_ref, o_ref,
                key_ref, m_ref, l_ref, acc_ref, carry_ref, *, t, topk):
    n = pl.program_id(1)
    lane = lax.broadcasted_iota(I32, (1, LANES), 1)
    hi = lane >= HEAD_DIM
    causal = _causal(t, t)

    def score_tile(j, diag):
        start = pl.multiple_of(j * t, t)
        ik = misck_ref[0, pl.ds(start, t), :]
        sc = jnp.zeros((t, t), F32)
        for h in range(IDX_HEADS):
            d = _dot_nt(iq_ref[0, h], ik, precision=lax.Precision.HIGHEST)
            sc = sc + miscq_ref[0, :, MISC_IW + h:MISC_IW + h + 1] * jnp.maximum(d, 0.0)
        if diag:
            sc = jnp.where(causal, sc, NEG_INF)
        sc = jnp.where(sc == 0.0, 0.0, sc)
        bits = lax.bitcast_convert_type(sc, I32)
        key_ref[j] = bits ^ ((bits >> 31) & jnp.int32(0x7FFFFFFF))

    def _score_body(j, carry):
        score_tile(j, False)
        return carry

    lax.fori_loop(0, n, _score_body, 0)
    score_tile(n, True)

    def count(pred):
        def body(j, acc):
            c = jnp.where(pred(key_ref[j]), 1.0, 0.0)
            r = c[:, 0:LANES]
            for u in range(1, t // LANES):
                r = r + c[:, u * LANES:(u + 1) * LANES]
            return acc + r
        acc = lax.fori_loop(0, n + 1, body, jnp.zeros((t, LANES), F32))
        return jnp.sum(acc, axis=-1, keepdims=True)

    def bit_body(i, u):
        cand = u | jnp.left_shift(jnp.int32(1), 31 - i)
        thr = cand ^ jnp.int32(INT_MIN)
        cnt = count(lambda kt: kt >= thr)
        return jnp.where(cnt >= float(topk), cand, u)

    u = lax.fori_loop(0, 32, bit_body, jnp.zeros((t, 1), I32))
    thr = u ^ jnp.int32(INT_MIN)
    need = float(topk) - count(lambda kt: kt > thr)

    _init_state(m_ref, l_ref, acc_ref)
    carry_ref[...] = jnp.zeros(carry_ref.shape, F32)
    upper = (lax.broadcasted_iota(I32, (t, t), 0) <= lax.broadcasted_iota(I32, (t, t), 1)).astype(BF16)

    def attn_tile(j, diag):
        start = pl.multiple_of(j * t, t)
        kt = key_ref[j]
        eq = kt == thr
        incl = jnp.dot(jnp.where(eq, 1.0, 0.0).astype(BF16), upper, preferred_element_type=F32) + carry_ref[...]
        sel = (kt > thr) | (eq & (incl <= need))
        carry_ref[...] = incl[:, t - 1:t]
        if diag:
            sel = sel & causal
        ks = k_ref[0, pl.ds(start, t), :]
        vs = v_ref[0, pl.ds(start, t), :]
        for g in range(2):
            qg = q_ref[0, :, g * LANES:(g + 1) * LANES]
            kg = ks[:, g * LANES:(g + 1) * LANES]
            vg = vs[:, g * LANES:(g + 1) * LANES]
            upd = jnp.zeros((t, LANES), F32)
            alphas = []
            for r in range(2):
                msk = hi if r else jnp.logical_not(hi)
                s = _dot_nt(qg, jnp.where(msk, kg, jnp.zeros_like(kg)))
                s = jnp.where(sel, s, -jnp.inf)
                p, alpha = _softmax_step(s, m_ref, l_ref, 2 * g + r)
                upd = upd + jnp.dot(p.astype(BF16), jnp.where(msk, vg, jnp.zeros_like(vg)),
                                    preferred_element_type=F32)
                alphas.append(alpha)
            acc_ref[g] = acc_ref[g] * jnp.where(hi, alphas[1], alphas[0]) + upd

    def _attn_body(j, carry):
        attn_tile(j, False)
        return carry

    lax.fori_loop(0, n, _attn_body, 0)
    attn_tile(n, True)
    for g in range(2):
        linv = jnp.where(hi, l_ref[2 * g + 1], l_ref[2 * g])
        o_ref[0, :, g * LANES:(g + 1) * LANES] = (acc_ref[g] / linv).astype(BF16)


def _dsa_call(main, iq, misc, *, t, topk):
    b, s, _ = main.shape
    return pl.pallas_call(
        functools.partial(_dsa_kernel, t=t, topk=topk),
        grid=(b, s // t),
        in_specs=[pl.BlockSpec((1, IDX_HEADS, t, LANES), lambda bi, ni: (bi, 0, ni, 0)),
                  pl.BlockSpec((1, t, LANES), lambda bi, ni: (bi, ni, 0)),
                  pl.BlockSpec((1, s, LANES), lambda bi, ni: (bi, 0, 0)),
                  pl.BlockSpec((1, t, GROUP), lambda bi, ni: (bi, ni, GIDX["a_q"])),
                  pl.BlockSpec((1, s, GROUP), lambda bi, ni: (bi, 0, GIDX["a_k"])),
                  pl.BlockSpec((1, s, GROUP), lambda bi, ni: (bi, 0, GIDX["a_v"]))],
        out_specs=pl.BlockSpec((1, t, GROUP), lambda bi, ni: (bi, ni, 0)),
        out_shape=jax.ShapeDtypeStruct((b, s, GROUP), BF16),
        scratch_shapes=[pltpu.VMEM((s // t, t, t), I32),
                        pltpu.VMEM((4, t, 1), F32), pltpu.VMEM((4, t, 1), F32),
                        pltpu.VMEM((2, t, LANES), F32), pltpu.VMEM((t, 1), F32)],
        compiler_params=_cparams(("arbitrary", "arbitrary")),
        name="dsa",
    )(iq, misc, misc, main, main, main)


def _fox_kernel(q_ref, k_ref, v_ref, c_ref, o_ref, m_ref, l_ref, acc_ref, *, t):
    n = pl.program_id(1)
    lane = lax.broadcasted_iota(I32, (1, LANES), 1)
    hi = lane >= HEAD_DIM
    causal = _causal(t, t)
    _init_state(m_ref, l_ref, acc_ref)

    def tile(j, diag):
        start = pl.multiple_of(j * t, t)
        ks = k_ref[0, pl.ds(start, t), :]
        vs = v_ref[0, pl.ds(start, t), :]
        ck = c_ref[0, j]
        for g in range(2):
            qg = q_ref[0, :, g * LANES:(g + 1) * LANES]
            kg = ks[:, g * LANES:(g + 1) * LANES]
            vg = vs[:, g * LANES:(g + 1) * LANES]
            upd = jnp.zeros((t, LANES), F32)
            alphas = []
            for r in range(2):
                h = 2 * g + r
                msk = hi if r else jnp.logical_not(hi)
                s = _dot_nt(qg, jnp.where(msk, kg, jnp.zeros_like(kg))) - ck[h:h + 1, :]
                if diag:
                    s = jnp.where(causal, s, -jnp.inf)
                p, alpha = _softmax_step(s, m_ref, l_ref, h)
                upd = upd + jnp.dot(p.astype(BF16), jnp.where(msk, vg, jnp.zeros_like(vg)),
                                    preferred_element_type=F32)
                alphas.append(alpha)
            acc_ref[g] = acc_ref[g] * jnp.where(hi, alphas[1], alphas[0]) + upd

    def _body(j, carry):
        tile(j, False)
        return carry

    lax.fori_loop(0, n, _body, 0)
    tile(n, True)
    for g in range(2):
        linv = jnp.where(hi, l_ref[2 * g + 1], l_ref[2 * g])
        o_ref[0, :, g * LANES:(g + 1) * LANES] = (acc_ref[g] / linv).astype(BF16)


def _fox_call(main, crow, *, t):
    b, s, _ = main.shape
    return pl.pallas_call(
        functools.partial(_fox_kernel, t=t),
        grid=(b, s // t),
        in_specs=[pl.BlockSpec((1, t, GROUP), lambda bi, ni: (bi, ni, GIDX["b_q"])),
                  pl.BlockSpec((1, s, GROUP), lambda bi, ni: (bi, 0, GIDX["b_k"])),
                  pl.BlockSpec((1, s, GROUP), lambda bi, ni: (bi, 0, GIDX["b_v"])),
                  pl.BlockSpec((1, s // t, 4, t), lambda bi, ni: (bi, 0, 0, 0))],
        out_specs=pl.BlockSpec((1, t, GROUP), lambda bi, ni: (bi, ni, 0)),
        out_shape=jax.ShapeDtypeStruct((b, s, GROUP), BF16),
        scratch_shapes=[pltpu.VMEM((4, t, 1), F32), pltpu.VMEM((4, t, 1), F32),
                        pltpu.VMEM((2, t, LANES), F32)],
        compiler_params=_cparams(("arbitrary", "arbitrary")),
        name="fox",
    )(main, main, main, crow)


def _diff_kernel(lam_ref, q_ref, k_ref, v_ref, gs_ref, o_ref, m_ref, l_ref, acc_ref, *, t):
    n = pl.program_id(1)
    lane = lax.broadcasted_iota(I32, (1, LANES), 1)
    hi = lane >= HEAD_DIM
    causal = _causal(t, t)
    _init_state(m_ref, l_ref, acc_ref)

    def tile(j, diag):
        start = pl.multiple_of(j * t, t)
        ks = k_ref[0, pl.ds(start, t), :]
        vs = v_ref[0, pl.ds(start, t), :]
        for g in range(2):
            qg = q_ref[0, :, g * LANES:(g + 1) * LANES]
            kg = ks[:, g * LANES:(g + 1) * LANES]
            vg = vs[:, g * LANES:(g + 1) * LANES]
            for mm in range(2):
                upd = jnp.zeros((t, LANES), F32)
                alphas = []
                for r in range(2):
                    lo = r * HEAD_DIM + mm * DIFF_DIM
                    mmask = (lane >= lo) & (lane < lo + DIFF_DIM)
                    hmask = hi if r else jnp.logical_not(hi)
                    s = _dot_nt(qg, jnp.where(mmask, kg, jnp.zeros_like(kg)))
                    if diag:
                        s = jnp.where(causal, s, -jnp.inf)
                    p, alpha = _softmax_step(s, m_ref, l_ref, (2 * g + r) * 2 + mm)
                    upd = upd + jnp.dot(p.astype(BF16), jnp.where(hmask, vg, jnp.zeros_like(vg)),
                                        preferred_element_type=F32)
                    alphas.append(alpha)
                a = mm * 2 + g
                acc_ref[a] = acc_ref[a] * jnp.where(hi, alphas[1], alphas[0]) + upd

    def _body(j, carry):
        tile(j, False)
        return carry

    lax.fori_loop(0, n, _body, 0)
    tile(n, True)
    lam = lam_ref[0]
    out_scale = lam_ref[1]
    for g in range(2):
        o1 = acc_ref[g] / jnp.where(hi, l_ref[(2 * g + 1) * 2], l_ref[(2 * g) * 2])
        o2 = acc_ref[2 + g] / jnp.where(hi, l_ref[(2 * g + 1) * 2 + 1], l_ref[(2 * g) * 2 + 1])
        o = o1 - lam * o2
        sq = o * o
        ms_lo = jnp.sum(jnp.where(hi, 0.0, sq), axis=-1, keepdims=True) / HEAD_DIM
        ms_hi = jnp.sum(jnp.where(hi, sq, 0.0), axis=-1, keepdims=True) / HEAD_DIM
        y = o * lax.rsqrt(jnp.where(hi, ms_hi, ms_lo) + EPS) * gs_ref[...]
        o_ref[0, :, g * LANES:(g + 1) * LANES] = (y * out_scale).astype(BF16)


def _diff_call(lam2, main, gs_row, *, t):
    b, s, _ = main.shape
    return pl.pallas_call(
        functools.partial(_diff_kernel, t=t),
        grid=(b, s // t),
        in_specs=[pl.BlockSpec(memory_space=pltpu.SMEM),
                  pl.BlockSpec((1, t, GROUP), lambda bi, ni: (bi, ni, GIDX["c_q"])),
                  pl.BlockSpec((1, s, GROUP), lambda bi, ni: (bi, 0, GIDX["c_k"])),
                  pl.BlockSpec((1, s, GROUP), lambda bi, ni: (bi, 0, GIDX["c_v"])),
                  pl.BlockSpec((1, LANES), lambda bi, ni: (0, 0))],
        out_specs=pl.BlockSpec((1, t, GROUP), lambda bi, ni: (bi, ni, 0)),
        out_shape=jax.ShapeDtypeStruct((b, s, GROUP), BF16),
        scratch_shapes=[pltpu.VMEM((8, t, 1), F32), pltpu.VMEM((8, t, 1), F32),
                        pltpu.VMEM((4, t, LANES), F32)],
        compiler_params=_cparams(("arbitrary", "arbitrary")),
        name="diff",
    )(lam2, main, main, main, gs_row)


def _swa_kernel(sink_ref, q_ref, kp_ref, kc_ref, vp_ref, vc_ref, o_ref, *, t):
    n = pl.program_id(1)
    lane = lax.broadcasted_iota(I32, (1, LANES), 1)
    hi = lane >= HEAD_DIM
    row = lax.broadcasted_iota(I32, (t, t), 0)
    col = lax.broadcasted_iota(I32, (t, t), 1)
    mask_prev = col > row + (t - WINDOW) + jnp.where(n > 0, 0, t)
    mask_cur = (col <= row) & (col > row - WINDOW)
    for g in range(2):
        qg = q_ref[0, :, g * LANES:(g + 1) * LANES]
        outs = []
        for r in range(2):
            msk = hi if r else jnp.logical_not(hi)
            sel = lambda a: jnp.where(msk, a, jnp.zeros_like(a))
            kp, kc = kp_ref[0, :, g * LANES:(g + 1) * LANES], kc_ref[0, :, g * LANES:(g + 1) * LANES]
            vp, vc = vp_ref[0, :, g * LANES:(g + 1) * LANES], vc_ref[0, :, g * LANES:(g + 1) * LANES]
            sp = jnp.where(mask_prev, _dot_nt(qg, sel(kp)), -jnp.inf)
            sc = jnp.where(mask_cur, _dot_nt(qg, sel(kc)), -jnp.inf)
            sink = sink_ref[2 * g + r]
            m = jnp.maximum(jnp.maximum(jnp.max(sp, axis=-1, keepdims=True),
                                        jnp.max(sc, axis=-1, keepdims=True)), sink)
            pp = jnp.exp(sp - m)
            pc = jnp.exp(sc - m)
            den = (jnp.sum(pp, axis=-1, keepdims=True) + jnp.sum(pc, axis=-1, keepdims=True)
                   + jnp.exp(sink - m))
            o = (jnp.dot(pp.astype(BF16), sel(vp), preferred_element_type=F32)
                 + jnp.dot(pc.astype(BF16), sel(vc), preferred_element_type=F32))
            outs.append(o / den)
        o_ref[0, :, g * LANES:(g + 1) * LANES] = (outs[0] + outs[1]).astype(BF16)


def _swa_call(sinks, main, *, t):
    b, s, _ = main.shape
    prev = lambda gi: pl.BlockSpec((1, t, GROUP), lambda bi, ni: (bi, jnp.maximum(ni - 1, 0), gi))
    cur = lambda gi: pl.BlockSpec((1, t, GROUP), lambda bi, ni: (bi, ni, gi))
    return pl.pallas_call(
        functools.partial(_swa_kernel, t=t),
        grid=(b, s // t),
        in_specs=[pl.BlockSpec(memory_space=pltpu.SMEM),
                  cur(GIDX["d_q"]), prev(GIDX["d_k"]), cur(GIDX["d_k"]), prev(GIDX["d_v"]), cur(GIDX["d_v"])],
        out_specs=pl.BlockSpec((1, t, GROUP), lambda bi, ni: (bi, ni, 0)),
        out_shape=jax.ShapeDtypeStruct((b, s, GROUP), BF16),
        compiler_params=_cparams(("arbitrary", "arbitrary")),
        name="swa",
    )(sinks, main, main, main, main, main)


def _outproj_kernel(oa_ref, ob_ref, oc_ref, od_ref, x_ref, w_ref, g_ref, y_ref):
    acc = jnp.zeros(x_ref.shape[1:], F32)
    for i, o_ref in enumerate((oa_ref, ob_ref, oc_ref, od_ref)):
        acc = acc + jnp.dot(o_ref[0], w_ref[i * GROUP:(i + 1) * GROUP, :], preferred_element_type=F32)
    y_ref[0] = x_ref[0] + _rms(acc, g_ref[...])


def _mlp_kernel(x_ref, gpre_ref, wu_ref, wd_ref, gpost_ref, y_ref, *, chunk):
    x = x_ref[0]
    h = _rms(x, gpre_ref[...]).astype(BF16)
    acc = jnp.zeros(x.shape, F32)
    for c in range(D_FF // chunk):
        u = jnp.dot(h, wu_ref[:, c * chunk:(c + 1) * chunk], preferred_element_type=F32)
        u = jnp.square(jnp.maximum(u, 0.0)).astype(BF16)
        acc = acc + jnp.dot(u, wd_ref[c * chunk:(c + 1) * chunk, :], preferred_element_type=F32)
    y_ref[0] = x + _rms(acc, gpost_ref[...])


def _ple_kernel(x_ref, p_ref, wg_ref, wp_ref, y_ref):
    x = x_ref[0]
    gate = jax.nn.sigmoid(jnp.dot(x.astype(BF16), wg_ref[...], preferred_element_type=F32))
    emb = jnp.dot(p_ref[0].astype(BF16), wp_ref[...], preferred_element_type=F32)
    y_ref[0] = x + gate * emb


def _token_call(body, tok_inputs, full_inputs, order, out_w, *, tm, name):
    b, s = tok_inputs[0].shape[:2]
    tok = lambda a: pl.BlockSpec((1, tm, a.shape[2]), lambda bi, ti: (bi, ti, 0))
    full = lambda a: pl.BlockSpec(a.shape, lambda bi, ti: (0,) * a.ndim)
    ops, specs = [], []
    ti = fi = 0
    for kind in order:
        if kind == "t":
            ops.append(tok_inputs[ti]); specs.append(tok(tok_inputs[ti])); ti += 1
        else:
            ops.append(full_inputs[fi]); specs.append(full(full_inputs[fi])); fi += 1
    return pl.pallas_call(
        body, grid=(b, s // tm), in_specs=specs,
        out_specs=pl.BlockSpec((1, tm, out_w), lambda bi, ti_: (bi, ti_, 0)),
        out_shape=jax.ShapeDtypeStruct((b, s, out_w), F32),
        compiler_params=_cparams(("arbitrary", "arbitrary")),
        name=name,
    )(*ops)


def _dup_kv(w):
    return jnp.concatenate([w[:, :HEAD_DIM], w[:, :HEAD_DIM], w[:, HEAD_DIM:], w[:, HEAD_DIM:]], axis=1)


def _prep_w_in(w):
    sec = lambda n: w[:, _SEC[n][0]:_SEC[n][0] + _SEC[n][1]]
    cols = [_dup_kv(sec(n)) if n in ("d_k", "d_v") else sec(n) for n in MAIN_GROUPS]
    w_main = jnp.concatenate(cols, axis=1).astype(BF16)
    pad = jnp.zeros((w.shape[0], LANES - IDX_DIM - IDX_HEADS - 4), w.dtype)
    w_idx = jnp.concatenate([sec("iq"), sec("ik"), sec("iw"), sec("b_f"), pad], axis=1)
    return w_main, w_idx


def _rope_tables(positions):
    pos = positions.astype(F32)[..., None]
    lane = jnp.arange(LANES)

    def tabs(dim):
        half = dim // 2
        inv_freq = ROPE_THETA ** (-jnp.arange(half, dtype=F32) / half)
        ang = pos * inv_freq[lane % half]
        sign = jnp.where((lane % dim) < half, -1.0, 1.0).astype(F32)
        return jnp.cos(ang), jnp.sin(ang) * sign

    c64, s64 = tabs(HEAD_DIM)
    c32, s32 = tabs(DIFF_DIM)
    return c64, s64, c32, s32


def kernel(x, p, positions, w_in, b_forget, lambda_q1, lambda_k1, lambda_q2, lambda_k2, diff_subln, sinks,
           w_out, norm_pre_mix, norm_post_mix, norm_pre_mlp, norm_post_mlp, w_mlp_up, w_mlp_down,
           w_ple_proj, w_ple_gate):
    b, s, d = x.shape
    depth = w_in.shape[0]
    t_attn = min(256, s)
    t_swa = WINDOW
    tm = min(256, s)
    topk = min(TOPK_MAX, s // 4)
    tabs = _rope_tables(positions)
    row = lambda v: v.reshape(1, -1).astype(F32)

    for i in range(depth):
        lam_init = 0.8 - 0.6 * math.exp(-0.3 * i)
        w_main, w_idx = _prep_w_in(w_in[i])
        bf_row = jnp.zeros((1, LANES), F32).at[0, MISC_F:MISC_F + 4].set(b_forget[i])
        main, iq, misc = _proj_call(x, row(norm_pre_mix[i]), w_main, w_idx, tabs, bf_row, tm=tm)

        o_a = _dsa_call(main, iq, misc, t=t_attn, topk=topk)

        cum = misc[:, :, MISC_F:MISC_F + 4]
        crow = cum.reshape(b, s // t_attn, t_attn, 4).transpose(0, 1, 3, 2)
        o_b = _fox_call(main, crow, t=t_attn)

        lam = (jnp.exp(jnp.sum(lambda_q1[i] * lambda_k1[i])) - jnp.exp(jnp.sum(lambda_q2[i] * lambda_k2[i]))
               + lam_init)
        lam2 = jnp.stack([lam, jnp.asarray(1.0 - lam_init, F32)]).astype(F32)
        gs_row = jnp.concatenate([diff_subln[i], diff_subln[i]]).reshape(1, LANES).astype(F32)
        o_c = _diff_call(lam2, main, gs_row, t=t_attn)

        o_d = _swa_call(sinks[i].astype(F32), main, t=t_swa)

        x = _token_call(_outproj_kernel, [o_a, o_b, o_c, o_d, x], [w_out[i].astype(BF16), row(norm_post_mix[i])],
                        "tttttff", d, tm=tm, name="outproj")
        x = _token_call(functools.partial(_mlp_kernel, chunk=1024), [x],
                        [row(norm_pre_mlp[i]), w_mlp_up[i].astype(BF16), w_mlp_down[i].astype(BF16),
                         row(norm_post_mlp[i])], "tffff", d, tm=tm, name="mlp")
        x = _token_call(_ple_kernel, [x, p[i]], [w_ple_gate[i].astype(BF16), w_ple_proj[i].astype(BF16)],
                        "ttff", d, tm=tm, name="ple")
    return x
```

```python
import functools
import math

import jax
import jax.numpy as jnp
from jax import lax
from jax.experimental import pallas as pl
from jax.experimental.pallas import tpu as pltpu

F32, BF16, I32 = jnp.float32, jnp.bfloat16, jnp.int32

D_MODEL = 1024
HEAD_DIM = 64
DIFF_DIM = 32
IDX_HEADS = 8
IDX_DIM = 64
TOPK_MAX = 256
WINDOW = 128
D_FF = 4 * D_MODEL
D_PLE = 256
ROPE_THETA = 10000.0
EPS = 1e-6
NEG_INF = -1e30
LANES = 128
SUBLANES = 8
GROUP = 256
INT_MIN = -2147483648

_SEC = {}
_o = 0
for _name, _w in (("a_q", 256), ("a_k", 256), ("a_v", 256), ("iq", 512), ("ik", 64), ("iw", 8),
                  ("b_q", 256), ("b_k", 256), ("b_v", 256), ("b_f", 4),
                  ("c_q", 256), ("c_k", 256), ("c_v", 256), ("d_q", 256), ("d_k", 128), ("d_v", 128)):
    _SEC[_name] = (_o, _w)
    _o += _w
D_IN = _o

QK_GROUPS = ("a_q", "a_k", "b_q", "b_k", "c_q", "c_k", "d_q", "d_k")
V_GROUPS = ("a_v", "b_v", "c_v", "d_v")
MAIN_W = GROUP * (len(QK_GROUPS) + len(V_GROUPS))
V_BASE = GROUP * len(QK_GROUPS)
QK_BLK = {"a_q": 0, "a_k": 1, "b_q": 2, "b_k": 3, "c_q": 5, "c_k": 6, "d_q": 7, "d_k": 8}
QK_W = GROUP * 9
V_ROW_BLK = {n: i for i, n in enumerate(V_GROUPS)}
ROPE64_GROUPS = ("a_q", "a_k", "d_q", "d_k")
ROPE32_GROUPS = ("c_q", "c_k")
Q_SCALE = {"a_q": HEAD_DIM ** -0.5, "b_q": HEAD_DIM ** -0.5, "c_q": DIFF_DIM ** -0.5, "d_q": HEAD_DIM ** -0.5}
IDX_W = 512 + LANES
MISC_IW = 64
MISC_F = 72
IDX_CAT = 256

VMEM_LIMIT = 56 * 1024 * 1024


def _cparams(sem):
    return pltpu.CompilerParams(dimension_semantics=sem, vmem_limit_bytes=VMEM_LIMIT)


def _rms(x, g):
    return x * lax.rsqrt(jnp.mean(x * x, axis=-1, keepdims=True) + EPS) * g


def _dot_nt(a, b):
    return lax.dot_general(a, b, (((1,), (1,)), ((), ())), preferred_element_type=F32)


def _bf16_part(x):
    return x.astype(BF16).astype(F32)


def _split2(x):
    hi = _bf16_part(x)
    return hi, _bf16_part(x - hi)


def _split3(x):
    hi = _bf16_part(x)
    mid = _bf16_part(x - hi)
    return hi, mid, _bf16_part(x - hi - mid)


def _rope_chunk(xc, cos, sin_signed, half, lane):
    fwd = pltpu.roll(xc, LANES - half, axis=1)
    bwd = pltpu.roll(xc, half, axis=1)
    partner = jnp.where((lane % (2 * half)) < half, fwd, bwd)
    return xc * cos + partner * sin_signed


def _proj_kernel(x_ref, g_ref, wm_ref, wi_ref, c64_ref, s64_ref, c32_ref, s32_ref, bf_ref,
                 qk_ref, vt_ref, iq_ref, ik_ref, iw_ref, carry_ref, *, tm):
    t = pl.program_id(1)
    x = x_ref[0]
    h = _rms(x, g_ref[...])
    pm = jnp.dot(h.astype(BF16), wm_ref[...], preferred_element_type=F32)
    pi = jnp.dot(h, wi_ref[...], precision=lax.Precision.HIGHEST,
                 preferred_element_type=F32)
    lane = lax.broadcasted_iota(I32, (1, LANES), 1)
    lo_half = lane < HEAD_DIM
    c64, s64, c32, s32 = c64_ref[0], s64_ref[0], c32_ref[0], s32_ref[0]

    misc = pi[:, 512:512 + LANES]
    z = misc + bf_ref[...]
    logf = jnp.minimum(z, 0.0) - jnp.log1p(jnp.exp(-jnp.abs(z)))
    logf = jnp.where((lane >= MISC_F) & (lane < MISC_F + 4), logf, 0.0)
    tri = (lax.broadcasted_iota(I32, (tm, tm), 1) <= lax.broadcasted_iota(I32, (tm, tm), 0)).astype(F32)

    @pl.when(t == 0)
    def _():
        carry_ref[...] = jnp.zeros_like(carry_ref)

    cum = jnp.dot(tri, logf, precision=lax.Precision.HIGHEST, preferred_element_type=F32) + carry_ref[...]
    carry_ref[...] = cum[tm - 1:tm, :]
    nhi, nmid, nlo = _split3(-cum)
    gate_bias = jnp.where(lane < 4, pltpu.roll(nhi, LANES - MISC_F, axis=1),
                          jnp.where(lane < 8, pltpu.roll(nmid, LANES - MISC_F + 4, axis=1),
                                    jnp.where(lane < 12, pltpu.roll(nlo, LANES - MISC_F + 8, axis=1),
                                              jnp.zeros_like(nlo))))

    for gi, name in enumerate(QK_GROUPS):
        for c in range(GROUP // LANES):
            lo = gi * GROUP + c * LANES
            v = pm[:, lo:lo + LANES]
            if name in ROPE64_GROUPS:
                v = _rope_chunk(v, c64, s64, HEAD_DIM // 2, lane)
            elif name in ROPE32_GROUPS:
                v = _rope_chunk(v, c32, s32, DIFF_DIM // 2, lane)
            if name in Q_SCALE:
                v = v * Q_SCALE[name]
            if name == "b_k":
                out = (QK_BLK[name] + c) * GROUP
                qk_ref[0, :, out:out + LANES] = v.astype(BF16)
                qk_ref[0, :, out + LANES:out + GROUP] = gate_bias.astype(BF16)
            else:
                out = QK_BLK[name] * GROUP + c * LANES
                qk_ref[0, :, out:out + LANES] = v.astype(BF16)

    vt_ref[0, 0] = pm[:, V_BASE:V_BASE + len(V_GROUPS) * GROUP].T.astype(BF16)

    def cat_q(q):
        hi, lo = _split2(q)
        return jnp.where(lo_half, hi, pltpu.roll(lo, HEAD_DIM, axis=1)), hi

    for c in range(512 // LANES):
        v = _rope_chunk(pi[:, c * LANES:(c + 1) * LANES], c64, s64, IDX_DIM // 2, lane)
        for r, q in enumerate((jnp.where(lo_half, v, 0.0), jnp.where(lo_half, pltpu.roll(v, HEAD_DIM, axis=1), 0.0))):
            a, b2 = cat_q(q)
            iq_ref[0, 2 * c + r, :, 0:LANES] = a.astype(BF16)
            iq_ref[0, 2 * c + r, :, LANES:IDX_CAT] = b2.astype(BF16)
    ik = jnp.where(lo_half, _rope_chunk(misc, c64, s64, IDX_DIM // 2, lane), 0.0)
    khi, klo = _split2(ik)
    ik_ref[0, :, 0:LANES] = jnp.where(lo_half, khi, pltpu.roll(khi, HEAD_DIM, axis=1)).astype(BF16)
    ik_ref[0, :, LANES:IDX_CAT] = klo.astype(BF16)
    iw_t = (misc * (IDX_HEADS ** -0.5 * IDX_DIM ** -0.5)).T
    iw_ref[0, 0] = iw_t[MISC_IW:MISC_IW + IDX_HEADS, :]


def _proj_call(x, g, w_main, w_idx, tabs, bf_row, *, tm):
    b, s, d = x.shape
    tok = lambda w: pl.BlockSpec((1, tm, w), lambda bi, ti: (bi, ti, 0))
    full = lambda a: pl.BlockSpec(a.shape, lambda bi, ti: (0,) * a.ndim)
    return pl.pallas_call(
        functools.partial(_proj_kernel, tm=tm),
        grid=(b, s // tm),
        in_specs=[tok(d), full(g), full(w_main), full(w_idx), tok(LANES), tok(LANES), tok(LANES), tok(LANES),
                  full(bf_row)],
        out_specs=[tok(QK_W),
                   pl.BlockSpec((1, 1, len(V_GROUPS) * GROUP, tm), lambda bi, ti: (bi, ti, 0, 0)),
                   pl.BlockSpec((1, IDX_HEADS, tm, IDX_CAT), lambda bi, ti: (bi, 0, ti, 0)),
                   tok(IDX_CAT),
                   pl.BlockSpec((1, 1, IDX_HEADS, tm), lambda bi, ti: (bi, ti, 0, 0))],
        out_shape=[jax.ShapeDtypeStruct((b, s, QK_W), BF16),
                   jax.ShapeDtypeStruct((b, s // tm, len(V_GROUPS) * GROUP, tm), BF16),
                   jax.ShapeDtypeStruct((b, IDX_HEADS, s, IDX_CAT), BF16),
                   jax.ShapeDtypeStruct((b, s, IDX_CAT), BF16),
                   jax.ShapeDtypeStruct((b, s // tm, IDX_HEADS, tm), F32)],
        scratch_shapes=[pltpu.VMEM((1, LANES), F32)],
        compiler_params=_cparams(("arbitrary", "arbitrary")),
        name="proj",
    )(x, g, w_main, w_idx, *tabs, bf_row)


def _softmax_step(s, m_ref, l_ref, idx):
    m_old = m_ref[idx]
    m_new = jnp.maximum(m_old, jnp.max(s, axis=0, keepdims=True))
    alpha = jnp.exp(m_old - m_new)
    p = jnp.exp(s - m_new)
    l_ref[idx] = alpha * l_ref[idx] + jnp.sum(p, axis=0, keepdims=True)
    m_ref[idx] = m_new
    return p, alpha


def _init_state(m_ref, l_ref, acc_ref):
    m_ref[...] = jnp.full(m_ref.shape, NEG_INF, F32)
    l_ref[...] = jnp.zeros(l_ref.shape, F32)
    acc_ref[...] = jnp.zeros(acc_ref.shape, F32)


def _causal_t(t):
    return lax.broadcasted_iota(I32, (t, t), 0) <= lax.broadcasted_iota(I32, (t, t), 1)


def _head_mask(r):
    lane = lax.broadcasted_iota(I32, (1, LANES), 1)
    return (lane >= HEAD_DIM) if r else (lane < HEAD_DIM)


def _accumulate(acc_ref, a, r, alpha, vt_rows, p):
    rows = slice(r * HEAD_DIM, (r + 1) * HEAD_DIM)
    acc_ref[a, rows, :] = acc_ref[a, rows, :] * alpha + jnp.dot(vt_rows, p.astype(BF16),
                                                               preferred_element_type=F32)


def _normalised_t(acc_ref, a, l_ref, idx0, idx1):
    return jnp.concatenate([acc_ref[a, 0:HEAD_DIM, :] / l_ref[idx0],
                            acc_ref[a, HEAD_DIM:LANES, :] / l_ref[idx1]], axis=0)


def _dsa_kernel(iq_ref, iw_ref, ik_ref, q_ref, k_ref, vt_ref, o_ref,
                key_ref, m_ref, l_ref, acc_ref, carry_ref, *, t, topk):
    n = pl.program_id(1)
    causal = _causal_t(t)

    def score_tile(j, diag):
        start = pl.multiple_of(j * t, t)
        ik = ik_ref[0, pl.ds(start, t), :]
        sc = jnp.zeros((t, t), F32)
        for h in range(IDX_HEADS):
            d = _dot_nt(ik, iq_ref[0, h])
            sc = sc + iw_ref[0, 0, h:h + 1, :] * jnp.maximum(d, 0.0)
        if diag:
            sc = jnp.where(causal, sc, NEG_INF)
        sc = jnp.where(sc == 0.0, 0.0, sc)
        bits = lax.bitcast_convert_type(sc, I32)
        key_ref[j] = bits ^ ((bits >> 31) & jnp.int32(0x7FFFFFFF))

    def _score_body(j, carry):
        score_tile(j, False)
        return carry

    lax.fori_loop(0, n, _score_body, 0)
    score_tile(n, True)

    def count(pred):
        def body(j, acc):
            c = jnp.where(pred(key_ref[j]), 1.0, 0.0)
            r = c[0:SUBLANES, :]
            for u in range(1, t // SUBLANES):
                r = r + c[u * SUBLANES:(u + 1) * SUBLANES, :]
            return acc + r
        acc = lax.fori_loop(0, n + 1, body, jnp.zeros((SUBLANES, t), F32))
        return jnp.sum(acc, axis=0, keepdims=True)

    def bit_body(i, u):
        cand = u | jnp.left_shift(jnp.int32(1), 31 - i)
        thr = cand ^ jnp.int32(INT_MIN)
        cnt = count(lambda kt: kt >= thr)
        return jnp.where(cnt >= float(topk), cand, u)

    u = lax.fori_loop(0, 32, bit_body, jnp.zeros((1, t), I32))
    thr = u ^ jnp.int32(INT_MIN)
    n_gt = count(lambda kt: kt > thr)
    n_eq = count(lambda kt: kt == thr)
    need = float(topk) - n_gt
    has_ties = jnp.max(n_eq - need) > 0.0

    _init_state(m_ref, l_ref, acc_ref)
    qz = [jnp.where(_head_mask(h % 2), q_ref[0, :, (h // 2) * LANES:(h // 2 + 1) * LANES], jnp.zeros((), BF16))
          for h in range(4)]

    def attn_tile(j, diag, ties):
        start = pl.multiple_of(j * t, t)
        kt = key_ref[j]
        if ties:
            eq = kt == thr
            lower = (lax.broadcasted_iota(I32, (t, t), 1) <= lax.broadcasted_iota(I32, (t, t), 0)).astype(BF16)
            incl = jnp.dot(lower, jnp.where(eq, 1.0, 0.0).astype(BF16),
                           preferred_element_type=F32) + carry_ref[...]
            sel = (kt > thr) | (eq & (incl <= need))
            carry_ref[...] = incl[t - 1:t, :]
        else:
            sel = kt >= thr
        if diag:
            sel = sel & causal
        ks = k_ref[0, pl.ds(start, t), :]
        vt = vt_ref[0, j]
        scores = [_dot_nt(ks[:, (h // 2) * LANES:(h // 2 + 1) * LANES], qz[h]) for h in range(4)]
        for h in range(4):
            g, r = divmod(h, 2)
            p, alpha = _softmax_step(jnp.where(sel, scores[h], -jnp.inf), m_ref, l_ref, h)
            _accumulate(acc_ref, g, r, alpha, vt[h * HEAD_DIM:(h + 1) * HEAD_DIM, :], p)

    def attend(ties):
        carry_ref[...] = jnp.zeros(carry_ref.shape, F32)

        def _body(j, carry):
            attn_tile(j, False, ties)
            return carry

        lax.fori_loop(0, n, _body, 0)
        attn_tile(n, True, ties)

    @pl.when(has_ties)
    def _():
        attend(True)

    @pl.when(jnp.logical_not(has_ties))
    def _():
        attend(False)

    for g in range(2):
        o_ref[0, :, g * LANES:(g + 1) * LANES] = _normalised_t(acc_ref, g, l_ref, 2 * g, 2 * g + 1).T.astype(BF16)


def _dsa_call(qk, vt, iq, ik, iw, *, t, topk):
    b, s, _ = qk.shape
    nt = s // t
    return pl.pallas_call(
        functools.partial(_dsa_kernel, t=t, topk=topk),
        grid=(b, nt),
        in_specs=[pl.BlockSpec((1, IDX_HEADS, t, IDX_CAT), lambda bi, ni: (bi, 0, ni, 0)),
                  pl.BlockSpec((1, 1, IDX_HEADS, t), lambda bi, ni: (bi, ni, 0, 0)),
                  pl.BlockSpec((1, s, IDX_CAT), lambda bi, ni: (bi, 0, 0)),
                  pl.BlockSpec((1, t, GROUP), lambda bi, ni: (bi, ni, QK_BLK["a_q"])),
                  pl.BlockSpec((1, s, GROUP), lambda bi, ni: (bi, 0, QK_BLK["a_k"])),
                  pl.BlockSpec((1, nt, GROUP, t), lambda bi, ni: (bi, 0, V_ROW_BLK["a_v"], 0))],
        out_specs=pl.BlockSpec((1, t, GROUP), lambda bi, ni: (bi, ni, 0)),
        out_shape=jax.ShapeDtypeStruct((b, s, GROUP), BF16),
        scratch_shapes=[pltpu.VMEM((nt, t, t), I32),
                        pltpu.VMEM((4, 1, t), F32), pltpu.VMEM((4, 1, t), F32),
                        pltpu.VMEM((2, LANES, t), F32), pltpu.VMEM((1, t), F32)],
        compiler_params=_cparams(("arbitrary", "arbitrary")),
        name="dsa",
    )(iq, iw, ik, qk, qk, vt)


def _fox_kernel(q_ref, k0_ref, k1_ref, vt_ref, o_ref, m_ref, l_ref, acc_ref, *, t):
    n = pl.program_id(1)
    causal = _causal_t(t)
    _init_state(m_ref, l_ref, acc_ref)
    lane = lax.broadcasted_iota(I32, (1, LANES), 1)
    qx = []
    for h in range(4):
        qz = jnp.where(_head_mask(h % 2), q_ref[0, :, (h // 2) * LANES:(h // 2 + 1) * LANES], jnp.zeros((), BF16))
        ones = jnp.where((lane == h) | (lane == 4 + h) | (lane == 8 + h), 1.0, 0.0).astype(BF16)
        qx.append(jnp.concatenate([qz, jnp.broadcast_to(ones, (t, LANES))], axis=1))
    k_refs = (k0_ref, k1_ref)

    def tile(j, diag):
        start = pl.multiple_of(j * t, t)
        vt = vt_ref[0, j]
        scores = [_dot_nt(k_refs[h // 2][0, pl.ds(start, t), :], qx[h]) for h in range(4)]
        for h in range(4):
            g, r = divmod(h, 2)
            s = jnp.where(causal, scores[h], -jnp.inf) if diag else scores[h]
            p, alpha = _softmax_step(s, m_ref, l_ref, h)
            _accumulate(acc_ref, g, r, alpha, vt[h * HEAD_DIM:(h + 1) * HEAD_DIM, :], p)

    def _body(j, carry):
        tile(j, False)
        return carry

    lax.fori_loop(0, n, _body, 0)
    tile(n, True)
    for g in range(2):
        o_ref[0, :, g * LANES:(g + 1) * LANES] = _normalised_t(acc_ref, g, l_ref, 2 * g, 2 * g + 1).T.astype(BF16)


def _fox_call(qk, vt, *, t):
    b, s, _ = qk.shape
    nt = s // t
    return pl.pallas_call(
        functools.partial(_fox_kernel, t=t),
        grid=(b, nt),
        in_specs=[pl.BlockSpec((1, t, GROUP), lambda bi, ni: (bi, ni, QK_BLK["b_q"])),
                  pl.BlockSpec((1, s, GROUP), lambda bi, ni: (bi, 0, QK_BLK["b_k"])),
                  pl.BlockSpec((1, s, GROUP), lambda bi, ni: (bi, 0, QK_BLK["b_k"] + 1)),
                  pl.BlockSpec((1, nt, GROUP, t), lambda bi, ni: (bi, 0, V_ROW_BLK["b_v"], 0))],
        out_specs=pl.BlockSpec((1, t, GROUP), lambda bi, ni: (bi, ni, 0)),
        out_shape=jax.ShapeDtypeStruct((b, s, GROUP), BF16),
        scratch_shapes=[pltpu.VMEM((4, 1, t), F32), pltpu.VMEM((4, 1, t), F32), pltpu.VMEM((2, LANES, t), F32)],
        compiler_params=_cparams(("arbitrary", "arbitrary")),
        name="fox",
    )(qk, qk, qk, vt)


def _diff_kernel(lam_ref, q_ref, k_ref, vt_ref, gs_ref, o_ref, m_ref, l_ref, acc_ref, *, t):
    n = pl.program_id(1)
    causal = _causal_t(t)
    _init_state(m_ref, l_ref, acc_ref)
    lane = lax.broadcasted_iota(I32, (1, LANES), 1)
    qz = {}
    for h in range(4):
        for mm in range(2):
            lo = (h % 2) * HEAD_DIM + mm * DIFF_DIM
            qz[h, mm] = jnp.where((lane >= lo) & (lane < lo + DIFF_DIM),
                                  q_ref[0, :, (h // 2) * LANES:(h // 2 + 1) * LANES], jnp.zeros((), BF16))

    def tile(j, diag):
        start = pl.multiple_of(j * t, t)
        ks = k_ref[0, pl.ds(start, t), :]
        vt = vt_ref[0, j]
        scores = {(h, mm): _dot_nt(ks[:, (h // 2) * LANES:(h // 2 + 1) * LANES], qz[h, mm])
                  for h in range(4) for mm in range(2)}
        for h in range(4):
            g, r = divmod(h, 2)
            for mm in range(2):
                s = jnp.where(causal, scores[h, mm], -jnp.inf) if diag else scores[h, mm]
                p, alpha = _softmax_step(s, m_ref, l_ref, 2 * h + mm)
                _accumulate(acc_ref, 2 * mm + g, r, alpha, vt[h * HEAD_DIM:(h + 1) * HEAD_DIM, :], p)

    def _body(j, carry):
        tile(j, False)
        return carry

    lax.fori_loop(0, n, _body, 0)
    tile(n, True)
    lam = lam_ref[0]
    out_scale = lam_ref[1]
    for g in range(2):
        o1 = _normalised_t(acc_ref, g, l_ref, 4 * g, 4 * g + 2)
        o2 = _normalised_t(acc_ref, 2 + g, l_ref, 4 * g + 1, 4 * g + 3)
        o = o1 - lam * o2
        sq = o * o
        ms = jnp.concatenate(
            [jnp.broadcast_to(jnp.mean(sq[r * HEAD_DIM:(r + 1) * HEAD_DIM], axis=0, keepdims=True), (HEAD_DIM, t))
             for r in range(2)], axis=0)
        y = o * lax.rsqrt(ms + EPS) * gs_ref[...] * out_scale
        o_ref[0, :, g * LANES:(g + 1) * LANES] = y.T.astype(BF16)


def _diff_call(lam2, qk, vt, gs_full, *, t):
    b, s, _ = qk.shape
    nt = s // t
    return pl.pallas_call(
        functools.partial(_diff_kernel, t=t),
        grid=(b, nt),
        in_specs=[pl.BlockSpec(memory_space=pltpu.SMEM),
                  pl.BlockSpec((1, t, GROUP), lambda bi, ni: (bi, ni, QK_BLK["c_q"])),
                  pl.BlockSpec((1, s, GROUP), lambda bi, ni: (bi, 0, QK_BLK["c_k"])),
                  pl.BlockSpec((1, nt, GROUP, t), lambda bi, ni: (bi, 0, V_ROW_BLK["c_v"], 0)),
                  pl.BlockSpec((LANES, t), lambda bi, ni: (0, 0))],
        out_specs=pl.BlockSpec((1, t, GROUP), lambda bi, ni: (bi, ni, 0)),
        out_shape=jax.ShapeDtypeStruct((b, s, GROUP), BF16),
        scratch_shapes=[pltpu.VMEM((8, 1, t), F32), pltpu.VMEM((8, 1, t), F32), pltpu.VMEM((4, LANES, t), F32)],
        compiler_params=_cparams(("arbitrary", "arbitrary")),
        name="diff",
    )(lam2, qk, qk, vt, gs_full)


def _swa_kernel(sink_ref, q_ref, kp_ref, kc_ref, vtp_ref, vtc_ref, o_ref, *, t):
    n = pl.program_id(1)
    prow = lax.broadcasted_iota(I32, (WINDOW, t), 0)
    pcol = lax.broadcasted_iota(I32, (WINDOW, t), 1)
    mask_prev = (prow > pcol) & (pcol + jnp.where(n > 0, 0, t) < WINDOW)
    crow = lax.broadcasted_iota(I32, (t, t), 0)
    ccol = lax.broadcasted_iota(I32, (t, t), 1)
    mask_cur = (crow <= ccol) & (crow > ccol - WINDOW)
    vtp = vtp_ref[0, 0][:, t - WINDOW:t]
    vtc = vtc_ref[0, 0]
    for g in range(2):
        outs = []
        for r in range(2):
            h = 2 * g + r
            qz = jnp.where(_head_mask(r), q_ref[0, :, g * LANES:(g + 1) * LANES], jnp.zeros((), BF16))
            sp = jnp.where(mask_prev, _dot_nt(kp_ref[0, :, g * LANES:(g + 1) * LANES], qz), -jnp.inf)
            sc = jnp.where(mask_cur, _dot_nt(kc_ref[0, :, g * LANES:(g + 1) * LANES], qz), -jnp.inf)
            sink = sink_ref[h]
            m = jnp.maximum(jnp.maximum(jnp.max(sp, axis=0, keepdims=True),
                                        jnp.max(sc, axis=0, keepdims=True)), sink)
            pp = jnp.exp(sp - m)
            pc = jnp.exp(sc - m)
            den = jnp.sum(pp, axis=0, keepdims=True) + jnp.sum(pc, axis=0, keepdims=True) + jnp.exp(sink - m)
            rows = slice(h * HEAD_DIM, (h + 1) * HEAD_DIM)
            o = (jnp.dot(vtp[rows, :], pp.astype(BF16), preferred_element_type=F32)
                 + jnp.dot(vtc[rows, :], pc.astype(BF16), preferred_element_type=F32))
            outs.append(o / den)
        o_ref[0, :, g * LANES:(g + 1) * LANES] = jnp.concatenate(outs, axis=0).T.astype(BF16)


def _swa_call(sinks, qk, vt, *, t):
    b, s, _ = qk.shape
    nt = s // t
    per = t // WINDOW
    return pl.pallas_call(
        functools.partial(_swa_kernel, t=t),
        grid=(b, nt),
        in_specs=[pl.BlockSpec(memory_space=pltpu.SMEM),
                  pl.BlockSpec((1, t, GROUP), lambda bi, ni: (bi, ni, QK_BLK["d_q"])),
                  pl.BlockSpec((1, WINDOW, GROUP), lambda bi, ni: (bi, jnp.maximum(ni * per - 1, 0), QK_BLK["d_k"])),
                  pl.BlockSpec((1, t, GROUP), lambda bi, ni: (bi, ni, QK_BLK["d_k"])),
                  pl.BlockSpec((1, 1, GROUP, t), lambda bi, ni: (bi, jnp.maximum(ni - 1, 0), V_ROW_BLK["d_v"], 0)),
                  pl.BlockSpec((1, 1, GROUP, t), lambda bi, ni: (bi, ni, V_ROW_BLK["d_v"], 0))],
        out_specs=pl.BlockSpec((1, t, GROUP), lambda bi, ni: (bi, ni, 0)),
        out_shape=jax.ShapeDtypeStruct((b, s, GROUP), BF16),
        compiler_params=_cparams(("arbitrary", "arbitrary")),
        name="swa",
    )(sinks, qk, qk, qk, vt, vt)


def _outproj_kernel(oa_ref, ob_ref, oc_ref, od_ref, x_ref, w_ref, g_ref, y_ref):
    acc = jnp.zeros(x_ref.shape[1:], F32)
    for i, o_ref in enumerate((oa_ref, ob_ref, oc_ref, od_ref)):
        acc = acc + jnp.dot(o_ref[0], w_ref[i * GROUP:(i + 1) * GROUP, :], preferred_element_type=F32)
    y_ref[0] = x_ref[0] + _rms(acc, g_ref[...])


def _mlp_kernel(x_ref, gpre_ref, wu_ref, wd_ref, gpost_ref, y_ref, *, chunk):
    x = x_ref[0]
    h = _rms(x, gpre_ref[...]).astype(BF16)
    acc = jnp.zeros(x.shape, F32)
    for c in range(D_FF // chunk):
        u = jnp.dot(h, wu_ref[:, c * chunk:(c + 1) * chunk], preferred_element_type=F32)
        u = jnp.square(jnp.maximum(u, 0.0)).astype(BF16)
        acc = acc + jnp.dot(u, wd_ref[c * chunk:(c + 1) * chunk, :], preferred_element_type=F32)
    y_ref[0] = x + _rms(acc, gpost_ref[...])


def _ple_kernel(x_ref, p_ref, wg_ref, wp_ref, y_ref):
    x = x_ref[0]
    gate = jax.nn.sigmoid(jnp.dot(x.astype(BF16), wg_ref[...], preferred_element_type=F32))
    emb = jnp.dot(p_ref[0].astype(BF16), wp_ref[...], preferred_element_type=F32)
    y_ref[0] = x + gate * emb


def _token_call(body, tok_inputs, full_inputs, order, out_w, *, tm, name):
    b, s = tok_inputs[0].shape[:2]
    tok = lambda a: pl.BlockSpec((1, tm, a.shape[2]), lambda bi, ti: (bi, ti, 0))
    full = lambda a: pl.BlockSpec(a.shape, lambda bi, ti: (0,) * a.ndim)
    ops, specs = [], []
    ti = fi = 0
    for kind in order:
        if kind == "t":
            ops.append(tok_inputs[ti]); specs.append(tok(tok_inputs[ti])); ti += 1
        else:
            ops.append(full_inputs[fi]); specs.append(full(full_inputs[fi])); fi += 1
    return pl.pallas_call(
        body, grid=(b, s // tm), in_specs=specs,
        out_specs=pl.BlockSpec((1, tm, out_w), lambda bi, ti_: (bi, ti_, 0)),
        out_shape=jax.ShapeDtypeStruct((b, s, out_w), F32),
        compiler_params=_cparams(("arbitrary", "arbitrary")),
        name=name,
    )(*ops)


def _dup_kv(w):
    return jnp.concatenate([w[:, :HEAD_DIM], w[:, :HEAD_DIM], w[:, HEAD_DIM:], w[:, HEAD_DIM:]], axis=1)


def _prep_w_in(w):
    sec = lambda n: w[:, _SEC[n][0]:_SEC[n][0] + _SEC[n][1]]
    cols = [_dup_kv(sec(n)) if n in ("d_k", "d_v") else sec(n) for n in QK_GROUPS + V_GROUPS]
    w_main = jnp.concatenate(cols, axis=1).astype(BF16)
    pad = jnp.zeros((w.shape[0], LANES - IDX_DIM - IDX_HEADS - 4), w.dtype)
    w_idx = jnp.concatenate([sec("iq"), sec("ik"), sec("iw"), sec("b_f"), pad], axis=1)
    return w_main, w_idx


def _rope_tables(positions):
    pos = positions.astype(F32)[..., None]
    lane = jnp.arange(LANES)

    def tabs(dim):
        half = dim // 2
        inv_freq = ROPE_THETA ** (-jnp.arange(half, dtype=F32) / half)
        ang = pos * inv_freq[lane % half]
        sign = jnp.where((lane % dim) < half, -1.0, 1.0).astype(F32)
        return jnp.cos(ang), jnp.sin(ang) * sign

    c64, s64 = tabs(HEAD_DIM)
    c32, s32 = tabs(DIFF_DIM)
    return c64, s64, c32, s32


def kernel(x, p, positions, w_in, b_forget, lambda_q1, lambda_k1, lambda_q2, lambda_k2, diff_subln, sinks,
           w_out, norm_pre_mix, norm_post_mix, norm_pre_mlp, norm_post_mlp, w_mlp_up, w_mlp_down,
           w_ple_proj, w_ple_gate):
    b, s, d = x.shape
    depth = w_in.shape[0]
    t = min(256, s)
    topk = min(TOPK_MAX, s // 4)
    tabs = _rope_tables(positions)
    row = lambda v: v.reshape(1, -1).astype(F32)

    for i in range(depth):
        lam_init = 0.8 - 0.6 * math.exp(-0.3 * i)
        w_main, w_idx = _prep_w_in(w_in[i])
        bf_row = jnp.zeros((1, LANES), F32).at[0, MISC_F:MISC_F + 4].set(b_forget[i])
        qk, vt, iq, ik, iw = _proj_call(x, row(norm_pre_mix[i]), w_main, w_idx, tabs, bf_row, tm=t)

        o_a = _dsa_call(qk, vt, iq, ik, iw, t=t, topk=topk)
        o_b = _fox_call(qk, vt, t=t)
        lam = (jnp.exp(jnp.sum(lambda_q1[i] * lambda_k1[i])) - jnp.exp(jnp.sum(lambda_q2[i] * lambda_k2[i]))
               + lam_init)
        lam2 = jnp.stack([lam, jnp.asarray(1.0 - lam_init, F32)]).astype(F32)
        gs_full = jnp.broadcast_to(jnp.concatenate([diff_subln[i], diff_subln[i]]).astype(F32)[:, None], (LANES, t))
        o_c = _diff_call(lam2, qk, vt, gs_full, t=t)
        o_d = _swa_call(sinks[i].astype(F32), qk, vt, t=t)

        x = _token_call(_outproj_kernel, [o_a, o_b, o_c, o_d, x], [w_out[i].astype(BF16), row(norm_post_mix[i])],
                        "tttttff", d, tm=t, name="outproj")
        x = _token_call(functools.partial(_mlp_kernel, chunk=1024), [x],
                        [row(norm_pre_mlp[i]), w_mlp_up[i].astype(BF16), w_mlp_down[i].astype(BF16),
                         row(norm_post_mlp[i])], "tffff", d, tm=t, name="mlp")
        x = _token_call(_ple_kernel, [x, p[i]], [w_ple_gate[i].astype(BF16), w_ple_proj[i].astype(BF16)],
                        "ttff", d, tm=t, name="ple")
    return x
```

```python
import functools
import math

import jax
import jax.numpy as jnp
from jax import lax
from jax.experimental import pallas as pl
from jax.experimental.pallas import tpu as pltpu

F32, BF16, I32 = jnp.float32, jnp.bfloat16, jnp.int32

D_MODEL = 1024
HEAD_DIM = 64
DIFF_DIM = 32
IDX_HEADS = 8
IDX_DIM = 64
TOPK_MAX = 256
WINDOW = 128
D_FF = 4 * D_MODEL
D_PLE = 256
ROPE_THETA = 10000.0
EPS = 1e-6
NEG_INF = -1e30
LANES = 128
SUBLANES = 8
GROUP = 256
INT_MIN = -2147483648

_SEC = {}
_o = 0
for _name, _w in (("a_q", 256), ("a_k", 256), ("a_v", 256), ("iq", 512), ("ik", 64), ("iw", 8),
                  ("b_q", 256), ("b_k", 256), ("b_v", 256), ("b_f", 4),
                  ("c_q", 256), ("c_k", 256), ("c_v", 256), ("d_q", 256), ("d_k", 128), ("d_v", 128)):
    _SEC[_name] = (_o, _w)
    _o += _w
D_IN = _o

QK_GROUPS = ("a_q", "a_k", "b_q", "b_k", "c_q", "c_k", "d_q", "d_k")
V_GROUPS = ("a_v", "b_v", "c_v", "d_v")
MAIN_W = GROUP * (len(QK_GROUPS) + len(V_GROUPS))
V_BASE = GROUP * len(QK_GROUPS)
QK_BLK = {"a_q": 0, "a_k": 1, "b_q": 2, "b_k": 3, "c_q": 5, "c_k": 6, "d_q": 7, "d_k": 8}
QK_W = GROUP * 9
V_ROW_BLK = {n: i for i, n in enumerate(V_GROUPS)}
ROPE64_GROUPS = ("a_q", "a_k", "d_q", "d_k")
ROPE32_GROUPS = ("c_q", "c_k")
LOG2E = math.log2(math.e)
Q_SCALE = {"a_q": HEAD_DIM ** -0.5 * LOG2E, "b_q": HEAD_DIM ** -0.5 * LOG2E, "c_q": DIFF_DIM ** -0.5 * LOG2E,
           "d_q": HEAD_DIM ** -0.5 * LOG2E}
IDX_W = 512 + LANES
MISC_IW = 64
MISC_F = 72
IDX_CAT = 256

VMEM_LIMIT = 56 * 1024 * 1024


def _cparams(sem):
    return pltpu.CompilerParams(dimension_semantics=sem, vmem_limit_bytes=VMEM_LIMIT)


def _rms(x, g):
    return x * lax.rsqrt(jnp.mean(x * x, axis=-1, keepdims=True) + EPS) * g


def _dot_nt(a, b):
    return lax.dot_general(a, b, (((1,), (1,)), ((), ())), preferred_element_type=F32)


def _bf16_part(x):
    return x.astype(BF16).astype(F32)


def _split2(x):
    hi = _bf16_part(x)
    return hi, _bf16_part(x - hi)


def _split3(x):
    hi = _bf16_part(x)
    mid = _bf16_part(x - hi)
    return hi, mid, _bf16_part(x - hi - mid)


def _rope_chunk(xc, cos, sin_signed, half, lane):
    fwd = pltpu.roll(xc, LANES - half, axis=1)
    bwd = pltpu.roll(xc, half, axis=1)
    partner = jnp.where((lane % (2 * half)) < half, fwd, bwd)
    return xc * cos + partner * sin_signed


def _proj_kernel(x_ref, g_ref, wm_ref, wi_ref, c64_ref, s64_ref, c32_ref, s32_ref, bf_ref,
                 qk_ref, vt_ref, iq_ref, ik_ref, iw_ref, carry_ref, *, tm):
    t = pl.program_id(1)
    x = x_ref[0]
    h = _rms(x, g_ref[...])
    pm = jnp.dot(h.astype(BF16), wm_ref[...], preferred_element_type=F32)
    pi = jnp.dot(h, wi_ref[...], precision=lax.Precision.HIGHEST,
                 preferred_element_type=F32)
    lane = lax.broadcasted_iota(I32, (1, LANES), 1)
    lo_half = lane < HEAD_DIM
    c64, s64, c32, s32 = c64_ref[0], s64_ref[0], c32_ref[0], s32_ref[0]

    misc = pi[:, 512:512 + LANES]
    z = misc + bf_ref[...]
    logf = jnp.minimum(z, 0.0) - jnp.log1p(jnp.exp(-jnp.abs(z)))
    logf = jnp.where((lane >= MISC_F) & (lane < MISC_F + 4), logf, 0.0)
    tri = (lax.broadcasted_iota(I32, (tm, tm), 1) <= lax.broadcasted_iota(I32, (tm, tm), 0)).astype(F32)

    @pl.when(t == 0)
    def _():
        carry_ref[...] = jnp.zeros_like(carry_ref)

    cum = jnp.dot(tri, logf, precision=lax.Precision.HIGHEST, preferred_element_type=F32) + carry_ref[...]
    carry_ref[...] = cum[tm - 1:tm, :]
    nhi, nmid, nlo = _split3(-LOG2E * cum)
    gate_bias = jnp.where(lane < 4, pltpu.roll(nhi, LANES - MISC_F, axis=1),
                          jnp.where(lane < 8, pltpu.roll(nmid, LANES - MISC_F + 4, axis=1),
                                    jnp.where(lane < 12, pltpu.roll(nlo, LANES - MISC_F + 8, axis=1),
                                              jnp.zeros_like(nlo))))

    for gi, name in enumerate(QK_GROUPS):
        for c in range(GROUP // LANES):
            lo = gi * GROUP + c * LANES
            v = pm[:, lo:lo + LANES]
            if name in ROPE64_GROUPS:
                v = _rope_chunk(v, c64, s64, HEAD_DIM // 2, lane)
            elif name in ROPE32_GROUPS:
                v = _rope_chunk(v, c32, s32, DIFF_DIM // 2, lane)
            if name in Q_SCALE:
                v = v * Q_SCALE[name]
            if name == "b_k":
                out = (QK_BLK[name] + c) * GROUP
                qk_ref[0, :, out:out + LANES] = v.astype(BF16)
                qk_ref[0, :, out + LANES:out + GROUP] = gate_bias.astype(BF16)
            else:
                out = QK_BLK[name] * GROUP + c * LANES
                qk_ref[0, :, out:out + LANES] = v.astype(BF16)

    vt_ref[0, 0] = pm[:, V_BASE:V_BASE + len(V_GROUPS) * GROUP].T.astype(BF16)

    def cat_q(q):
        hi, lo = _split2(q)
        return jnp.where(lo_half, hi, pltpu.roll(lo, HEAD_DIM, axis=1)), hi

    for c in range(512 // LANES):
        v = _rope_chunk(pi[:, c * LANES:(c + 1) * LANES], c64, s64, IDX_DIM // 2, lane)
        for r, q in enumerate((jnp.where(lo_half, v, 0.0), jnp.where(lo_half, pltpu.roll(v, HEAD_DIM, axis=1), 0.0))):
            a, b2 = cat_q(q)
            iq_ref[0, 2 * c + r, :, 0:LANES] = a.astype(BF16)
            iq_ref[0, 2 * c + r, :, LANES:IDX_CAT] = b2.astype(BF16)
    ik = jnp.where(lo_half, _rope_chunk(misc, c64, s64, IDX_DIM // 2, lane), 0.0)
    khi, klo = _split2(ik)
    ik_ref[0, :, 0:LANES] = jnp.where(lo_half, khi, pltpu.roll(khi, HEAD_DIM, axis=1)).astype(BF16)
    ik_ref[0, :, LANES:IDX_CAT] = klo.astype(BF16)
    iw_t = (misc * (IDX_HEADS ** -0.5 * IDX_DIM ** -0.5)).T
    iw_ref[0, 0] = iw_t[MISC_IW:MISC_IW + IDX_HEADS, :]


def _proj_call(x, g, w_main, w_idx, tabs, bf_row, *, tm):
    b, s, d = x.shape
    tok = lambda w: pl.BlockSpec((1, tm, w), lambda bi, ti: (bi, ti, 0))
    full = lambda a: pl.BlockSpec(a.shape, lambda bi, ti: (0,) * a.ndim)
    return pl.pallas_call(
        functools.partial(_proj_kernel, tm=tm),
        grid=(b, s // tm),
        in_specs=[tok(d), full(g), full(w_main), full(w_idx), tok(LANES), tok(LANES), tok(LANES), tok(LANES),
                  full(bf_row)],
        out_specs=[tok(QK_W),
                   pl.BlockSpec((1, 1, len(V_GROUPS) * GROUP, tm), lambda bi, ti: (bi, ti, 0, 0)),
                   pl.BlockSpec((1, IDX_HEADS, tm, IDX_CAT), lambda bi, ti: (bi, 0, ti, 0)),
                   tok(IDX_CAT),
                   pl.BlockSpec((1, 1, IDX_HEADS, tm), lambda bi, ti: (bi, ti, 0, 0))],
        out_shape=[jax.ShapeDtypeStruct((b, s, QK_W), BF16),
                   jax.ShapeDtypeStruct((b, s // tm, len(V_GROUPS) * GROUP, tm), BF16),
                   jax.ShapeDtypeStruct((b, IDX_HEADS, s, IDX_CAT), BF16),
                   jax.ShapeDtypeStruct((b, s, IDX_CAT), BF16),
                   jax.ShapeDtypeStruct((b, s // tm, IDX_HEADS, tm), F32)],
        scratch_shapes=[pltpu.VMEM((1, LANES), F32)],
        compiler_params=_cparams(("arbitrary", "arbitrary")),
        name="proj",
    )(x, g, w_main, w_idx, *tabs, bf_row)


def _softmax_step(s, m_ref, l_ref, idx):
    m_old = m_ref[idx]
    m_new = jnp.maximum(m_old, jnp.max(s, axis=0, keepdims=True))
    alpha = jnp.exp2(m_old - m_new)
    p = jnp.exp2(s - m_new)
    l_ref[idx] = alpha * l_ref[idx] + jnp.sum(p, axis=0, keepdims=True)
    m_ref[idx] = m_new
    return p, alpha


def _init_state(m_ref, l_ref, acc_ref):
    m_ref[...] = jnp.full(m_ref.shape, NEG_INF, F32)
    l_ref[...] = jnp.zeros(l_ref.shape, F32)
    acc_ref[...] = jnp.zeros(acc_ref.shape, F32)


def _causal_chunk(w, t):
    return lax.broadcasted_iota(I32, (w * t, t), 0) <= lax.broadcasted_iota(I32, (w * t, t), 1) + (w - 1) * t


def _head_mask(r):
    lane = lax.broadcasted_iota(I32, (1, LANES), 1)
    return (lane >= HEAD_DIM) if r else (lane < HEAD_DIM)


def _key_chunks(n, chunk_fn):
    def _body(pair, carry):
        chunk_fn(2 * pair, 2, False)
        return carry

    lax.fori_loop(0, n >> 1, _body, 0)
    odd = (n & 1) == 1

    @pl.when(odd)
    def _():
        chunk_fn(n - 1, 2, True)

    @pl.when(jnp.logical_not(odd))
    def _():
        chunk_fn(n, 1, True)


def _accumulate(acc_ref, a, r, alpha, vt_tiles, p, t):
    rows = slice(r * HEAD_DIM, (r + 1) * HEAD_DIM)
    pb = p.astype(BF16)
    pv = jnp.dot(vt_tiles[0], pb[0:t], preferred_element_type=F32)
    for i in range(1, len(vt_tiles)):
        pv = pv + jnp.dot(vt_tiles[i], pb[i * t:(i + 1) * t], preferred_element_type=F32)
    acc_ref[a, rows, :] = acc_ref[a, rows, :] * alpha + pv


def _normalised_t(acc_ref, a, l_ref, idx0, idx1):
    return jnp.concatenate([acc_ref[a, 0:HEAD_DIM, :] / l_ref[idx0],
                            acc_ref[a, HEAD_DIM:LANES, :] / l_ref[idx1]], axis=0)


def _dsa_kernel(iq_ref, iw_ref, ik_ref, q_ref, k_ref, vt_ref, o_ref,
                key_ref, m_ref, l_ref, acc_ref, carry_ref, *, t, topk):
    n = pl.program_id(1)

    def score_chunk(j0, w, diag):
        start = pl.multiple_of(j0 * t, t)
        ik = ik_ref[0, pl.ds(start, w * t), :]
        sc = jnp.zeros((w * t, t), F32)
        for h in range(IDX_HEADS):
            d = _dot_nt(ik, iq_ref[0, h])
            sc = sc + iw_ref[0, 0, h:h + 1, :] * jnp.maximum(d, 0.0)
        if diag:
            sc = jnp.where(_causal_chunk(w, t), sc, NEG_INF)
        sc = jnp.where(sc == 0.0, 0.0, sc)
        bits = lax.bitcast_convert_type(sc, I32)
        key = bits ^ ((bits >> 31) & jnp.int32(0x7FFFFFFF))
        for i in range(w):
            key_ref[j0 + i] = key[i * t:(i + 1) * t]

    _key_chunks(n, score_chunk)

    def count(pred):
        def body(j, acc):
            c = jnp.where(pred(key_ref[j]), 1.0, 0.0)
            parts = [c[u * SUBLANES:(u + 1) * SUBLANES, :] for u in range(t // SUBLANES)]
            while len(parts) > 1:
                parts = [parts[i] + parts[i + 1] for i in range(0, len(parts), 2)]
            return acc + parts[0]
        acc = lax.fori_loop(0, n + 1, body, jnp.zeros((SUBLANES, t), F32))
        return jnp.sum(acc, axis=0, keepdims=True)

    def bit_body(i, carry):
        u, n_ge = carry
        cand = u | jnp.left_shift(jnp.int32(1), 31 - i)
        cand_thr = cand ^ jnp.int32(INT_MIN)
        cnt = count(lambda kt: kt >= cand_thr)
        keep = cnt >= float(topk)
        return jnp.where(keep, cand, u), jnp.where(keep, cnt, n_ge)

    in_scope = ((n + 1) * t).astype(F32)
    u, n_ge = lax.fori_loop(0, 32, bit_body, (jnp.zeros((1, t), I32), jnp.zeros((1, t), F32) + in_scope))
    thr = u ^ jnp.int32(INT_MIN)
    has_ties = jnp.max(n_ge) > float(topk)

    _init_state(m_ref, l_ref, acc_ref)
    qz = [jnp.where(_head_mask(h % 2), q_ref[0, :, (h // 2) * LANES:(h // 2 + 1) * LANES], jnp.zeros((), BF16))
          for h in range(4)]

    def attend(ties):
        if ties:
            carry_ref[...] = jnp.zeros(carry_ref.shape, F32)
            need = float(topk) - count(lambda kt: kt > thr)
            lower = (lax.broadcasted_iota(I32, (t, t), 1) <= lax.broadcasted_iota(I32, (t, t), 0)).astype(BF16)

        def chunk(j0, w, diag):
            start = pl.multiple_of(j0 * t, t)
            ks = k_ref[0, pl.ds(start, w * t), :]
            scores = [_dot_nt(ks[:, (h // 2) * LANES:(h // 2 + 1) * LANES], qz[h]) for h in range(4)]
            if ties:
                parts = []
                for i in range(w):
                    kt = key_ref[j0 + i]
                    eq = kt == thr
                    incl = jnp.dot(lower, jnp.where(eq, 1.0, 0.0).astype(BF16),
                                   preferred_element_type=F32) + carry_ref[...]
                    carry_ref[...] = incl[t - 1:t, :]
                    parts.append(jnp.where((kt > thr) | (eq & (incl <= need)), 1.0, 0.0))
                sel = (parts[0] if w == 1 else jnp.concatenate(parts, axis=0)) > 0.5
            else:
                kt = key_ref[j0] if w == 1 else jnp.concatenate([key_ref[j0 + i] for i in range(w)], axis=0)
                sel = kt >= thr
            if diag:
                sel = sel & _causal_chunk(w, t)
            for h in range(4):
                g, r = divmod(h, 2)
                p, alpha = _softmax_step(jnp.where(sel, scores[h], -jnp.inf), m_ref, l_ref, h)
                _accumulate(acc_ref, g, r, alpha,
                            [vt_ref[0, j0 + i, h * HEAD_DIM:(h + 1) * HEAD_DIM, :] for i in range(w)], p, t)

        _key_chunks(n, chunk)

    @pl.when(has_ties)
    def _():
        attend(True)

    @pl.when(jnp.logical_not(has_ties))
    def _():
        attend(False)

    for g in range(2):
        o_ref[0, :, g * LANES:(g + 1) * LANES] = _normalised_t(acc_ref, g, l_ref, 2 * g, 2 * g + 1).T.astype(BF16)


def _dsa_call(qk, vt, iq, ik, iw, *, t, topk):
    b, s, _ = qk.shape
    nt = s // t
    return pl.pallas_call(
        functools.partial(_dsa_kernel, t=t, topk=topk),
        grid=(b, nt),
        in_specs=[pl.BlockSpec((1, IDX_HEADS, t, IDX_CAT), lambda bi, ni: (bi, 0, ni, 0)),
                  pl.BlockSpec((1, 1, IDX_HEADS, t), lambda bi, ni: (bi, ni, 0, 0)),
                  pl.BlockSpec((1, s, IDX_CAT), lambda bi, ni: (bi, 0, 0)),
                  pl.BlockSpec((1, t, GROUP), lambda bi, ni: (bi, ni, QK_BLK["a_q"])),
                  pl.BlockSpec((1, s, GROUP), lambda bi, ni: (bi, 0, QK_BLK["a_k"])),
                  pl.BlockSpec((1, nt, GROUP, t), lambda bi, ni: (bi, 0, V_ROW_BLK["a_v"], 0))],
        out_specs=pl.BlockSpec((1, t, GROUP), lambda bi, ni: (bi, ni, 0)),
        out_shape=jax.ShapeDtypeStruct((b, s, GROUP), BF16),
        scratch_shapes=[pltpu.VMEM((nt, t, t), I32),
                        pltpu.VMEM((4, 1, t), F32), pltpu.VMEM((4, 1, t), F32),
                        pltpu.VMEM((2, LANES, t), F32), pltpu.VMEM((1, t), F32)],
        compiler_params=_cparams(("arbitrary", "arbitrary")),
        name="dsa",
    )(iq, iw, ik, qk, qk, vt)


def _fox_kernel(q_ref, k0_ref, k1_ref, vt_ref, o_ref, m_ref, l_ref, acc_ref, *, t):
    n = pl.program_id(1)
    _init_state(m_ref, l_ref, acc_ref)
    lane = lax.broadcasted_iota(I32, (1, LANES), 1)
    qx = []
    for h in range(4):
        qz = jnp.where(_head_mask(h % 2), q_ref[0, :, (h // 2) * LANES:(h // 2 + 1) * LANES], jnp.zeros((), BF16))
        ones = jnp.where((lane == h) | (lane == 4 + h) | (lane == 8 + h), 1.0, 0.0).astype(BF16)
        qx.append(jnp.concatenate([qz, jnp.broadcast_to(ones, (t, LANES))], axis=1))
    k_refs = (k0_ref, k1_ref)

    def chunk(j0, w, diag):
        start = pl.multiple_of(j0 * t, t)
        scores = [_dot_nt(k_refs[h // 2][0, pl.ds(start, w * t), :], qx[h]) for h in range(4)]
        for h in range(4):
            g, r = divmod(h, 2)
            s = jnp.where(_causal_chunk(w, t), scores[h], -jnp.inf) if diag else scores[h]
            p, alpha = _softmax_step(s, m_ref, l_ref, h)
            _accumulate(acc_ref, g, r, alpha,
                        [vt_ref[0, j0 + i, h * HEAD_DIM:(h + 1) * HEAD_DIM, :] for i in range(w)], p, t)

    _key_chunks(n, chunk)
    for g in range(2):
        o_ref[0, :, g * LANES:(g + 1) * LANES] = _normalised_t(acc_ref, g, l_ref, 2 * g, 2 * g + 1).T.astype(BF16)


def _fox_call(qk, vt, *, t):
    b, s, _ = qk.shape
    nt = s // t
    return pl.pallas_call(
        functools.partial(_fox_kernel, t=t),
        grid=(b, nt),
        in_specs=[pl.BlockSpec((1, t, GROUP), lambda bi, ni: (bi, ni, QK_BLK["b_q"])),
                  pl.BlockSpec((1, s, GROUP), lambda bi, ni: (bi, 0, QK_BLK["b_k"])),
                  pl.BlockSpec((1, s, GROUP), lambda bi, ni: (bi, 0, QK_BLK["b_k"] + 1)),
                  pl.BlockSpec((1, nt, GROUP, t), lambda bi, ni: (bi, 0, V_ROW_BLK["b_v"], 0))],
        out_specs=pl.BlockSpec((1, t, GROUP), lambda bi, ni: (bi, ni, 0)),
        out_shape=jax.ShapeDtypeStruct((b, s, GROUP), BF16),
        scratch_shapes=[pltpu.VMEM((4, 1, t), F32), pltpu.VMEM((4, 1, t), F32), pltpu.VMEM((2, LANES, t), F32)],
        compiler_params=_cparams(("arbitrary", "arbitrary")),
        name="fox",
    )(qk, qk, qk, vt)


def _diff_kernel(lam_ref, q_ref, k_ref, vt_ref, gs_ref, o_ref, m_ref, l_ref, acc_ref, *, t):
    n = pl.program_id(1)
    _init_state(m_ref, l_ref, acc_ref)
    lane = lax.broadcasted_iota(I32, (1, LANES), 1)
    qz = {}
    for h in range(4):
        for mm in range(2):
            lo = (h % 2) * HEAD_DIM + mm * DIFF_DIM
            qz[h, mm] = jnp.where((lane >= lo) & (lane < lo + DIFF_DIM),
                                  q_ref[0, :, (h // 2) * LANES:(h // 2 + 1) * LANES], jnp.zeros((), BF16))

    def chunk(j0, w, diag):
        start = pl.multiple_of(j0 * t, t)
        ks = k_ref[0, pl.ds(start, w * t), :]
        scores = {(h, mm): _dot_nt(ks[:, (h // 2) * LANES:(h // 2 + 1) * LANES], qz[h, mm])
                  for h in range(4) for mm in range(2)}
        for h in range(4):
            g, r = divmod(h, 2)
            vts = [vt_ref[0, j0 + i, h * HEAD_DIM:(h + 1) * HEAD_DIM, :] for i in range(w)]
            for mm in range(2):
                s = jnp.where(_causal_chunk(w, t), scores[h, mm], -jnp.inf) if diag else scores[h, mm]
                p, alpha = _softmax_step(s, m_ref, l_ref, 2 * h + mm)
                _accumulate(acc_ref, 2 * mm + g, r, alpha, vts, p, t)

    _key_chunks(n, chunk)
    lam = lam_ref[0]
    out_scale = lam_ref[1]
    for g in range(2):
        o1 = _normalised_t(acc_ref, g, l_ref, 4 * g, 4 * g + 2)
        o2 = _normalised_t(acc_ref, 2 + g, l_ref, 4 * g + 1, 4 * g + 3)
        o = o1 - lam * o2
        sq = o * o
        ms = jnp.concatenate(
            [jnp.broadcast_to(jnp.mean(sq[r * HEAD_DIM:(r + 1) * HEAD_DIM], axis=0, keepdims=True), (HEAD_DIM, t))
             for r in range(2)], axis=0)
        y = o * lax.rsqrt(ms + EPS) * gs_ref[...] * out_scale
        o_ref[0, :, g * LANES:(g + 1) * LANES] = y.T.astype(BF16)


def _diff_call(lam2, qk, vt, gs_full, *, t):
    b, s, _ = qk.shape
    nt = s // t
    return pl.pallas_call(
        functools.partial(_diff_kernel, t=t),
        grid=(b, nt),
        in_specs=[pl.BlockSpec(memory_space=pltpu.SMEM),
                  pl.BlockSpec((1, t, GROUP), lambda bi, ni: (bi, ni, QK_BLK["c_q"])),
                  pl.BlockSpec((1, s, GROUP), lambda bi, ni: (bi, 0, QK_BLK["c_k"])),
                  pl.BlockSpec((1, nt, GROUP, t), lambda bi, ni: (bi, 0, V_ROW_BLK["c_v"], 0)),
                  pl.BlockSpec((LANES, t), lambda bi, ni: (0, 0))],
        out_specs=pl.BlockSpec((1, t, GROUP), lambda bi, ni: (bi, ni, 0)),
        out_shape=jax.ShapeDtypeStruct((b, s, GROUP), BF16),
        scratch_shapes=[pltpu.VMEM((8, 1, t), F32), pltpu.VMEM((8, 1, t), F32), pltpu.VMEM((4, LANES, t), F32)],
        compiler_params=_cparams(("arbitrary", "arbitrary")),
        name="diff",
    )(lam2, qk, qk, vt, gs_full)


def _swa_kernel(sink_ref, q_ref, kp_ref, kc_ref, vtp_ref, vtc_ref, o_ref, *, t):
    n = pl.program_id(1)
    prow = lax.broadcasted_iota(I32, (WINDOW, t), 0)
    pcol = lax.broadcasted_iota(I32, (WINDOW, t), 1)
    mask_prev = (prow > pcol) & (pcol + jnp.where(n > 0, 0, t) < WINDOW)
    crow = lax.broadcasted_iota(I32, (t, t), 0)
    ccol = lax.broadcasted_iota(I32, (t, t), 1)
    mask_cur = (crow <= ccol) & (crow > ccol - WINDOW)
    vtp = vtp_ref[0, 0][:, t - WINDOW:t]
    vtc = vtc_ref[0, 0]
    for g in range(2):
        outs = []
        for r in range(2):
            h = 2 * g + r
            qz = jnp.where(_head_mask(r), q_ref[0, :, g * LANES:(g + 1) * LANES], jnp.zeros((), BF16))
            sp = jnp.where(mask_prev, _dot_nt(kp_ref[0, :, g * LANES:(g + 1) * LANES], qz), -jnp.inf)
            sc = jnp.where(mask_cur, _dot_nt(kc_ref[0, :, g * LANES:(g + 1) * LANES], qz), -jnp.inf)
            sink = sink_ref[h] * LOG2E
            m = jnp.maximum(jnp.maximum(jnp.max(sp, axis=0, keepdims=True),
                                        jnp.max(sc, axis=0, keepdims=True)), sink)
            pp = jnp.exp2(sp - m)
            pc = jnp.exp2(sc - m)
            den = jnp.sum(pp, axis=0, keepdims=True) + jnp.sum(pc, axis=0, keepdims=True) + jnp.exp2(sink - m)
            rows = slice(h * HEAD_DIM, (h + 1) * HEAD_DIM)
            o = (jnp.dot(vtp[rows, :], pp.astype(BF16), preferred_element_type=F32)
                 + jnp.dot(vtc[rows, :], pc.astype(BF16), preferred_element_type=F32))
            outs.append(o / den)
        o_ref[0, :, g * LANES:(g + 1) * LANES] = jnp.concatenate(outs, axis=0).T.astype(BF16)


def _swa_call(sinks, qk, vt, *, t):
    b, s, _ = qk.shape
    nt = s // t
    per = t // WINDOW
    return pl.pallas_call(
        functools.partial(_swa_kernel, t=t),
        grid=(b, nt),
        in_specs=[pl.BlockSpec(memory_space=pltpu.SMEM),
                  pl.BlockSpec((1, t, GROUP), lambda bi, ni: (bi, ni, QK_BLK["d_q"])),
                  pl.BlockSpec((1, WINDOW, GROUP), lambda bi, ni: (bi, jnp.maximum(ni * per - 1, 0), QK_BLK["d_k"])),
                  pl.BlockSpec((1, t, GROUP), lambda bi, ni: (bi, ni, QK_BLK["d_k"])),
                  pl.BlockSpec((1, 1, GROUP, t), lambda bi, ni: (bi, jnp.maximum(ni - 1, 0), V_ROW_BLK["d_v"], 0)),
                  pl.BlockSpec((1, 1, GROUP, t), lambda bi, ni: (bi, ni, V_ROW_BLK["d_v"], 0))],
        out_specs=pl.BlockSpec((1, t, GROUP), lambda bi, ni: (bi, ni, 0)),
        out_shape=jax.ShapeDtypeStruct((b, s, GROUP), BF16),
        compiler_params=_cparams(("arbitrary", "arbitrary")),
        name="swa",
    )(sinks, qk, qk, qk, vt, vt)


def _outproj_kernel(oa_ref, ob_ref, oc_ref, od_ref, x_ref, w_ref, g_ref, y_ref):
    acc = jnp.zeros(x_ref.shape[1:], F32)
    for i, o_ref in enumerate((oa_ref, ob_ref, oc_ref, od_ref)):
        acc = acc + jnp.dot(o_ref[0], w_ref[i * GROUP:(i + 1) * GROUP, :], preferred_element_type=F32)
    y_ref[0] = x_ref[0] + _rms(acc, g_ref[...])


def _mlp_kernel(x_ref, gpre_ref, wu_ref, wd_ref, gpost_ref, y_ref, *, chunk):
    x = x_ref[0]
    h = _rms(x, gpre_ref[...]).astype(BF16)
    acc = jnp.zeros(x.shape, F32)
    for c in range(D_FF // chunk):
        u = jnp.dot(h, wu_ref[:, c * chunk:(c + 1) * chunk], preferred_element_type=F32)
        u = jnp.square(jnp.maximum(u, 0.0)).astype(BF16)
        acc = acc + jnp.dot(u, wd_ref[c * chunk:(c + 1) * chunk, :], preferred_element_type=F32)
    y_ref[0] = x + _rms(acc, gpost_ref[...])


def _ple_kernel(x_ref, p_ref, wg_ref, wp_ref, y_ref):
    x = x_ref[0]
    gate = jax.nn.sigmoid(jnp.dot(x.astype(BF16), wg_ref[...], preferred_element_type=F32))
    emb = jnp.dot(p_ref[0].astype(BF16), wp_ref[...], preferred_element_type=F32)
    y_ref[0] = x + gate * emb


def _token_call(body, tok_inputs, full_inputs, order, out_w, *, tm, name):
    b, s = tok_inputs[0].shape[:2]
    tok = lambda a: pl.BlockSpec((1, tm, a.shape[2]), lambda bi, ti: (bi, ti, 0))
    full = lambda a: pl.BlockSpec(a.shape, lambda bi, ti: (0,) * a.ndim)
    ops, specs = [], []
    ti = fi = 0
    for kind in order:
        if kind == "t":
            ops.append(tok_inputs[ti]); specs.append(tok(tok_inputs[ti])); ti += 1
        else:
            ops.append(full_inputs[fi]); specs.append(full(full_inputs[fi])); fi += 1
    return pl.pallas_call(
        body, grid=(b, s // tm), in_specs=specs,
        out_specs=pl.BlockSpec((1, tm, out_w), lambda bi, ti_: (bi, ti_, 0)),
        out_shape=jax.ShapeDtypeStruct((b, s, out_w), F32),
        compiler_params=_cparams(("arbitrary", "arbitrary")),
        name=name,
    )(*ops)


def _dup_kv(w):
    return jnp.concatenate([w[:, :HEAD_DIM], w[:, :HEAD_DIM], w[:, HEAD_DIM:], w[:, HEAD_DIM:]], axis=1)


def _prep_w_in(w):
    sec = lambda n: w[:, _SEC[n][0]:_SEC[n][0] + _SEC[n][1]]
    cols = [_dup_kv(sec(n)) if n in ("d_k", "d_v") else sec(n) for n in QK_GROUPS + V_GROUPS]
    w_main = jnp.concatenate(cols, axis=1).astype(BF16)
    pad = jnp.zeros((w.shape[0], LANES - IDX_DIM - IDX_HEADS - 4), w.dtype)
    w_idx = jnp.concatenate([sec("iq"), sec("ik"), sec("iw"), sec("b_f"), pad], axis=1)
    return w_main, w_idx


def _rope_tables(positions):
    pos = positions.astype(F32)[..., None]
    lane = jnp.arange(LANES)

    def tabs(dim):
        half = dim // 2
        inv_freq = ROPE_THETA ** (-jnp.arange(half, dtype=F32) / half)
        ang = pos * inv_freq[lane % half]
        sign = jnp.where((lane % dim) < half, -1.0, 1.0).astype(F32)
        return jnp.cos(ang), jnp.sin(ang) * sign

    c64, s64 = tabs(HEAD_DIM)
    c32, s32 = tabs(DIFF_DIM)
    return c64, s64, c32, s32


def kernel(x, p, positions, w_in, b_forget, lambda_q1, lambda_k1, lambda_q2, lambda_k2, diff_subln, sinks,
           w_out, norm_pre_mix, norm_post_mix, norm_pre_mlp, norm_post_mlp, w_mlp_up, w_mlp_down,
           w_ple_proj, w_ple_gate):
    b, s, d = x.shape
    depth = w_in.shape[0]
    t = min(256, s)
    topk = min(TOPK_MAX, s // 4)
    tabs = _rope_tables(positions)
    row = lambda v: v.reshape(1, -1).astype(F32)

    for i in range(depth):
        lam_init = 0.8 - 0.6 * math.exp(-0.3 * i)
        w_main, w_idx = _prep_w_in(w_in[i])
        bf_row = jnp.zeros((1, LANES), F32).at[0, MISC_F:MISC_F + 4].set(b_forget[i])
        qk, vt, iq, ik, iw = _proj_call(x, row(norm_pre_mix[i]), w_main, w_idx, tabs, bf_row, tm=t)

        o_a = _dsa_call(qk, vt, iq, ik, iw, t=t, topk=topk)
        o_b = _fox_call(qk, vt, t=t)
        lam = (jnp.exp(jnp.sum(lambda_q1[i] * lambda_k1[i])) - jnp.exp(jnp.sum(lambda_q2[i] * lambda_k2[i]))
               + lam_init)
        lam2 = jnp.stack([lam, jnp.asarray(1.0 - lam_init, F32)]).astype(F32)
        gs_full = jnp.broadcast_to(jnp.concatenate([diff_subln[i], diff_subln[i]]).astype(F32)[:, None], (LANES, t))
        o_c = _diff_call(lam2, qk, vt, gs_full, t=t)
        o_d = _swa_call(sinks[i].astype(F32), qk, vt, t=t)

        x = _token_call(_outproj_kernel, [o_a, o_b, o_c, o_d, x], [w_out[i].astype(BF16), row(norm_post_mix[i])],
                        "tttttff", d, tm=t, name="outproj")
        x = _token_call(functools.partial(_mlp_kernel, chunk=1024), [x],
                        [row(norm_pre_mlp[i]), w_mlp_up[i].astype(BF16), w_mlp_down[i].astype(BF16),
                         row(norm_post_mlp[i])], "tffff", d, tm=t, name="mlp")
        x = _token_call(_ple_kernel, [x, p[i]], [w_ple_gate[i].astype(BF16), w_ple_proj[i].astype(BF16)],
                        "ttff", d, tm=t, name="ple")
    return x
```

```python
import functools
import math

import jax
import jax.numpy as jnp
from jax import lax
from jax.experimental import pallas as pl
from jax.experimental.pallas import tpu as pltpu

F32, BF16, I32 = jnp.float32, jnp.bfloat16, jnp.int32

D_MODEL = 1024
HEAD_DIM = 64
DIFF_DIM = 32
IDX_HEADS = 8
IDX_DIM = 64
TOPK_MAX = 256
WINDOW = 128
D_FF = 4 * D_MODEL
D_PLE = 256
ROPE_THETA = 10000.0
EPS = 1e-6
NEG_INF = -1e30
LANES = 128
SUBLANES = 8
GROUP = 256
INT_MIN = -2147483648
FIELD_BITS = 15
FIELD_GUARDS = -2147450880

_SEC = {}
_o = 0
for _name, _w in (("a_q", 256), ("a_k", 256), ("a_v", 256), ("iq", 512), ("ik", 64), ("iw", 8),
                  ("b_q", 256), ("b_k", 256), ("b_v", 256), ("b_f", 4),
                  ("c_q", 256), ("c_k", 256), ("c_v", 256), ("d_q", 256), ("d_k", 128), ("d_v", 128)):
    _SEC[_name] = (_o, _w)
    _o += _w
D_IN = _o

QK_GROUPS = ("a_q", "a_k", "b_q", "b_k", "c_q", "c_k", "d_q", "d_k")
V_GROUPS = ("a_v", "b_v", "c_v", "d_v")
MAIN_W = GROUP * (len(QK_GROUPS) + len(V_GROUPS))
V_BASE = GROUP * len(QK_GROUPS)
QK_BLK = {"a_q": 0, "a_k": 1, "b_q": 2, "b_k": 3, "c_q": 5, "c_k": 6, "d_q": 7, "d_k": 8}
QK_W = GROUP * 9
V_ROW_BLK = {n: i for i, n in enumerate(V_GROUPS)}
ROPE64_GROUPS = ("a_q", "a_k", "d_q", "d_k")
ROPE32_GROUPS = ("c_q", "c_k")
LOG2E = math.log2(math.e)
Q_SCALE = {"a_q": HEAD_DIM ** -0.5 * LOG2E, "b_q": HEAD_DIM ** -0.5 * LOG2E, "c_q": DIFF_DIM ** -0.5 * LOG2E,
           "d_q": HEAD_DIM ** -0.5 * LOG2E}
IDX_W = 512 + LANES
MISC_IW = 64
MISC_F = 72
IDX_CAT = 256

VMEM_LIMIT = 56 * 1024 * 1024


def _cparams(sem):
    return pltpu.CompilerParams(dimension_semantics=sem, vmem_limit_bytes=VMEM_LIMIT)


def _rms(x, g):
    return x * lax.rsqrt(jnp.mean(x * x, axis=-1, keepdims=True) + EPS) * g


def _dot_nt(a, b):
    return lax.dot_general(a, b, (((1,), (1,)), ((), ())), preferred_element_type=F32)


def _dot(a, b):
    return jnp.dot(a, b, preferred_element_type=F32)


def _transpose_bf16(a):
    return a.astype(F32).T.astype(BF16)


def _bf16_part(x):
    return x.astype(BF16).astype(F32)


def _split2(x):
    hi = _bf16_part(x)
    return hi, _bf16_part(x - hi)


def _split3(x):
    hi = _bf16_part(x)
    mid = _bf16_part(x - hi)
    return hi, mid, _bf16_part(x - hi - mid)


def _rope_chunk(xc, cos, sin_signed, half, lane):
    fwd = pltpu.roll(xc, LANES - half, axis=1)
    bwd = pltpu.roll(xc, half, axis=1)
    partner = jnp.where((lane % (2 * half)) < half, fwd, bwd)
    return xc * cos + partner * sin_signed


def _proj_kernel(x_ref, g_ref, wm_ref, wi_ref, c64_ref, s64_ref, c32_ref, s32_ref, bf_ref,
                 qk_ref, vt_ref, iq_ref, ik_ref, iw_ref, carry_ref, *, tm):
    t = pl.program_id(1)
    x = x_ref[0]
    h = _rms(x, g_ref[...])
    pm = jnp.dot(h.astype(BF16), wm_ref[...], preferred_element_type=F32)
    pi = jnp.dot(h, wi_ref[...], precision=lax.Precision.HIGHEST,
                 preferred_element_type=F32)
    lane = lax.broadcasted_iota(I32, (1, LANES), 1)
    lo_half = lane < HEAD_DIM
    c64, s64, c32, s32 = c64_ref[0], s64_ref[0], c32_ref[0], s32_ref[0]

    misc = pi[:, 512:512 + LANES]
    z = misc + bf_ref[...]
    logf = jnp.minimum(z, 0.0) - jnp.log1p(jnp.exp(-jnp.abs(z)))
    logf = jnp.where((lane >= MISC_F) & (lane < MISC_F + 4), logf, 0.0)
    tri = (lax.broadcasted_iota(I32, (tm, tm), 1) <= lax.broadcasted_iota(I32, (tm, tm), 0)).astype(F32)

    @pl.when(t == 0)
    def _():
        carry_ref[...] = jnp.zeros_like(carry_ref)

    cum = jnp.dot(tri, logf, precision=lax.Precision.HIGHEST, preferred_element_type=F32) + carry_ref[...]
    carry_ref[...] = cum[tm - 1:tm, :]
    nhi, nmid, nlo = _split3(-LOG2E * cum)
    gate_bias = jnp.where(lane < 4, pltpu.roll(nhi, LANES - MISC_F, axis=1),
                          jnp.where(lane < 8, pltpu.roll(nmid, LANES - MISC_F + 4, axis=1),
                                    jnp.where(lane < 12, pltpu.roll(nlo, LANES - MISC_F + 8, axis=1),
                                              jnp.zeros_like(nlo))))

    for gi, name in enumerate(QK_GROUPS):
        for c in range(GROUP // LANES):
            lo = gi * GROUP + c * LANES
            v = pm[:, lo:lo + LANES]
            if name in ROPE64_GROUPS:
                v = _rope_chunk(v, c64, s64, HEAD_DIM // 2, lane)
            elif name in ROPE32_GROUPS:
                v = _rope_chunk(v, c32, s32, DIFF_DIM // 2, lane)
            if name in Q_SCALE:
                v = v * Q_SCALE[name]
            if name == "b_k":
                out = (QK_BLK[name] + c) * GROUP
                qk_ref[0, :, out:out + LANES] = v.astype(BF16)
                qk_ref[0, :, out + LANES:out + GROUP] = gate_bias.astype(BF16)
            else:
                out = QK_BLK[name] * GROUP + c * LANES
                qk_ref[0, :, out:out + LANES] = v.astype(BF16)

    vt_ref[0, 0] = pm[:, V_BASE:V_BASE + len(V_GROUPS) * GROUP].T.astype(BF16)

    def cat_q(q):
        hi, lo = _split2(q)
        return jnp.where(lo_half, hi, pltpu.roll(lo, HEAD_DIM, axis=1)), hi

    for c in range(512 // LANES):
        v = _rope_chunk(pi[:, c * LANES:(c + 1) * LANES], c64, s64, IDX_DIM // 2, lane)
        for r, q in enumerate((jnp.where(lo_half, v, 0.0), jnp.where(lo_half, pltpu.roll(v, HEAD_DIM, axis=1), 0.0))):
            a, b2 = cat_q(q)
            iq_ref[0, 2 * c + r, :, 0:LANES] = a.astype(BF16)
            iq_ref[0, 2 * c + r, :, LANES:IDX_CAT] = b2.astype(BF16)
    ik = jnp.where(lo_half, _rope_chunk(misc, c64, s64, IDX_DIM // 2, lane), 0.0)
    khi, klo = _split2(ik)
    ik_ref[0, :, 0:LANES] = jnp.where(lo_half, khi, pltpu.roll(khi, HEAD_DIM, axis=1)).astype(BF16)
    ik_ref[0, :, LANES:IDX_CAT] = klo.astype(BF16)
    iw_t = (misc * (IDX_HEADS ** -0.5 * IDX_DIM ** -0.5)).T
    iw_ref[0, 0] = iw_t[MISC_IW:MISC_IW + IDX_HEADS, :]


def _proj_call(x, g, w_main, w_idx, tabs, bf_row, *, tm):
    b, s, d = x.shape
    tok = lambda w: pl.BlockSpec((1, tm, w), lambda bi, ti: (bi, ti, 0))
    full = lambda a: pl.BlockSpec(a.shape, lambda bi, ti: (0,) * a.ndim)
    return pl.pallas_call(
        functools.partial(_proj_kernel, tm=tm),
        grid=(b, s // tm),
        in_specs=[tok(d), full(g), full(w_main), full(w_idx), tok(LANES), tok(LANES), tok(LANES), tok(LANES),
                  full(bf_row)],
        out_specs=[tok(QK_W),
                   pl.BlockSpec((1, 1, len(V_GROUPS) * GROUP, tm), lambda bi, ti: (bi, ti, 0, 0)),
                   pl.BlockSpec((1, IDX_HEADS, tm, IDX_CAT), lambda bi, ti: (bi, 0, ti, 0)),
                   tok(IDX_CAT),
                   pl.BlockSpec((1, 1, IDX_HEADS, tm), lambda bi, ti: (bi, ti, 0, 0))],
        out_shape=[jax.ShapeDtypeStruct((b, s, QK_W), BF16),
                   jax.ShapeDtypeStruct((b, s // tm, len(V_GROUPS) * GROUP, tm), BF16),
                   jax.ShapeDtypeStruct((b, IDX_HEADS, s, IDX_CAT), BF16),
                   jax.ShapeDtypeStruct((b, s, IDX_CAT), BF16),
                   jax.ShapeDtypeStruct((b, s // tm, IDX_HEADS, tm), F32)],
        scratch_shapes=[pltpu.VMEM((1, LANES), F32)],
        compiler_params=_cparams(("arbitrary", "arbitrary")),
        name="proj",
    )(x, g, w_main, w_idx, *tabs, bf_row)


def _softmax_step(s, m_ref, l_ref, idx):
    m_old = m_ref[idx]
    m_new = jnp.maximum(m_old, jnp.max(s, axis=0, keepdims=True))
    alpha = jnp.exp2(m_old - m_new)
    p = jnp.exp2(s - m_new)
    l_ref[idx] = alpha * l_ref[idx] + jnp.sum(p, axis=0, keepdims=True)
    m_ref[idx] = m_new
    return p, alpha


def _init_state(m_ref, l_ref, acc_ref):
    m_ref[...] = jnp.full(m_ref.shape, NEG_INF, F32)
    l_ref[...] = jnp.zeros(l_ref.shape, F32)
    acc_ref[...] = jnp.zeros(acc_ref.shape, F32)


def _causal_chunk(w, t):
    return lax.broadcasted_iota(I32, (w * t, t), 0) <= lax.broadcasted_iota(I32, (w * t, t), 1) + (w - 1) * t


def _head_mask(r):
    lane = lax.broadcasted_iota(I32, (1, LANES), 1)
    return (lane >= HEAD_DIM) if r else (lane < HEAD_DIM)


def _key_chunks(n, chunk_fn):
    def _body(pair, carry):
        chunk_fn(2 * pair, 2, False)
        return carry

    lax.fori_loop(0, n >> 1, _body, 0)
    odd = (n & 1) == 1

    @pl.when(odd)
    def _():
        chunk_fn(n - 1, 2, True)

    @pl.when(jnp.logical_not(odd))
    def _():
        chunk_fn(n, 1, True)


def _accumulate(acc_ref, a, r, alpha, vt_tiles, p, t):
    rows = slice(r * HEAD_DIM, (r + 1) * HEAD_DIM)
    pb = p.astype(BF16)
    pv = jnp.dot(vt_tiles[0], pb[0:t], preferred_element_type=F32)
    for i in range(1, len(vt_tiles)):
        pv = pv + jnp.dot(vt_tiles[i], pb[i * t:(i + 1) * t], preferred_element_type=F32)
    acc_ref[a, rows, :] = acc_ref[a, rows, :] * alpha + pv


def _normalised_t(acc_ref, a, l_ref, idx0, idx1):
    return jnp.concatenate([acc_ref[a, 0:HEAD_DIM, :] / l_ref[idx0],
                            acc_ref[a, HEAD_DIM:LANES, :] / l_ref[idx1]], axis=0)


def _dsa_kernel(iq_ref, iw_ref, ik_ref, q_ref, k_ref, vt_ref, o_ref,
                key_ref, pack_ref, m_ref, l_ref, acc_ref, carry_ref, iqt_ref, qzt_ref, *, t, topk):
    n = pl.program_id(1)
    for h in range(IDX_HEADS):
        iqt_ref[h] = _transpose_bf16(iq_ref[0, h])
    for h in range(4):
        qzt_ref[h] = _transpose_bf16(jnp.where(_head_mask(h % 2), q_ref[0, :, (h // 2) * LANES:(h // 2 + 1) * LANES],
                                               jnp.zeros((), BF16)))

    def score_chunk(j0, w, diag):
        start = pl.multiple_of(j0 * t, t)
        ik = ik_ref[0, pl.ds(start, w * t), :]
        sc = jnp.zeros((w * t, t), F32)
        for h in range(IDX_HEADS):
            d = _dot(ik, iqt_ref[h])
            sc = sc + iw_ref[0, 0, h:h + 1, :] * jnp.maximum(d, 0.0)
        if diag:
            sc = jnp.where(_causal_chunk(w, t), sc, NEG_INF)
        sc = jnp.where(sc == 0.0, 0.0, sc)
        bits = lax.bitcast_convert_type(sc, I32)
        key = bits ^ ((bits >> 31) & jnp.int32(0x7FFFFFFF))
        top = lax.shift_right_logical(key ^ jnp.int32(INT_MIN), 32 - FIELD_BITS)
        for i in range(w):
            key_ref[j0 + i] = key[i * t:(i + 1) * t]
            pack_ref[j0 + i] = (jnp.left_shift(top[i * t:i * t + t // 2], 16) | top[i * t + t // 2:(i + 1) * t]
                                | jnp.int32(FIELD_GUARDS))

    _key_chunks(n, score_chunk)

    def tree_sum(parts):
        while len(parts) > 1:
            parts = [parts[i] + parts[i + 1] for i in range(0, len(parts), 2)]
        return parts[0]

    def count(pred):
        def body(j, acc):
            c = jnp.where(pred(key_ref[j]), 1.0, 0.0)
            return acc + tree_sum([c[r * SUBLANES:(r + 1) * SUBLANES, :] for r in range(t // SUBLANES)])
        acc = lax.fori_loop(0, n + 1, body, jnp.zeros((SUBLANES, t), F32))
        return jnp.sum(acc, axis=0, keepdims=True)

    def count_top(cand):
        both = jnp.left_shift(cand, 16) | cand

        def body(j, acc):
            z = pack_ref[j] - both
            c = lax.shift_right_logical(z, FIELD_BITS) & jnp.int32(0x00010001)
            return acc + tree_sum([c[r * SUBLANES:(r + 1) * SUBLANES, :] for r in range(t // 2 // SUBLANES)])
        acc = lax.fori_loop(0, n + 1, body, jnp.zeros((SUBLANES, t), I32))
        per_lane = lax.shift_right_logical(acc, 16) + (acc & jnp.int32(0xFFFF))
        return jnp.sum(per_lane.astype(F32), axis=0, keepdims=True)

    def bisect(n_bits, low_bit, count_fn, carry):
        def body(i, carry):
            u, n_ge = carry
            cand = u | jnp.left_shift(jnp.int32(1), low_bit + n_bits - 1 - i)
            cnt = count_fn(cand)
            keep = cnt >= float(topk)
            return jnp.where(keep, cand, u), jnp.where(keep, cnt, n_ge)
        return lax.fori_loop(0, n_bits, body, carry)

    in_scope = ((n + 1) * t).astype(F32)
    top, n_ge = bisect(FIELD_BITS, 0, count_top, (jnp.zeros((1, t), I32), jnp.zeros((1, t), F32) + in_scope))
    u, n_ge = bisect(32 - FIELD_BITS, 0, lambda cand: count(lambda kt: kt >= (cand ^ jnp.int32(INT_MIN))),
                     (jnp.left_shift(top, 32 - FIELD_BITS), n_ge))
    thr = u ^ jnp.int32(INT_MIN)
    has_ties = jnp.max(n_ge) > float(topk)

    _init_state(m_ref, l_ref, acc_ref)

    def attend(ties):
        if ties:
            carry_ref[...] = jnp.zeros(carry_ref.shape, F32)
            need = float(topk) - count(lambda kt: kt > thr)
            lower = (lax.broadcasted_iota(I32, (t, t), 1) <= lax.broadcasted_iota(I32, (t, t), 0)).astype(BF16)

        def chunk(j0, w, diag):
            start = pl.multiple_of(j0 * t, t)
            ks = k_ref[0, pl.ds(start, w * t), :]
            scores = [_dot(ks[:, (h // 2) * LANES:(h // 2 + 1) * LANES], qzt_ref[h]) for h in range(4)]
            if ties:
                parts = []
                for i in range(w):
                    kt = key_ref[j0 + i]
                    eq = kt == thr
                    incl = jnp.dot(lower, jnp.where(eq, 1.0, 0.0).astype(BF16),
                                   preferred_element_type=F32) + carry_ref[...]
                    carry_ref[...] = incl[t - 1:t, :]
                    parts.append(jnp.where((kt > thr) | (eq & (incl <= need)), 1.0, 0.0))
                sel = (parts[0] if w == 1 else jnp.concatenate(parts, axis=0)) > 0.5
            else:
                kt = key_ref[j0] if w == 1 else jnp.concatenate([key_ref[j0 + i] for i in range(w)], axis=0)
                sel = kt >= thr
            if diag:
                sel = sel & _causal_chunk(w, t)
            for h in range(4):
                g, r = divmod(h, 2)
                p, alpha = _softmax_step(jnp.where(sel, scores[h], -jnp.inf), m_ref, l_ref, h)
                _accumulate(acc_ref, g, r, alpha,
                            [vt_ref[0, j0 + i, h * HEAD_DIM:(h + 1) * HEAD_DIM, :] for i in range(w)], p, t)

        _key_chunks(n, chunk)

    @pl.when(has_ties)
    def _():
        attend(True)

    @pl.when(jnp.logical_not(has_ties))
    def _():
        attend(False)

    for g in range(2):
        o_ref[0, :, g * LANES:(g + 1) * LANES] = _normalised_t(acc_ref, g, l_ref, 2 * g, 2 * g + 1).T.astype(BF16)


def _dsa_call(qk, vt, iq, ik, iw, *, t, topk):
    b, s, _ = qk.shape
    nt = s // t
    return pl.pallas_call(
        functools.partial(_dsa_kernel, t=t, topk=topk),
        grid=(b, nt),
        in_specs=[pl.BlockSpec((1, IDX_HEADS, t, IDX_CAT), lambda bi, ni: (bi, 0, ni, 0)),
                  pl.BlockSpec((1, 1, IDX_HEADS, t), lambda bi, ni: (bi, ni, 0, 0)),
                  pl.BlockSpec((1, s, IDX_CAT), lambda bi, ni: (bi, 0, 0)),
                  pl.BlockSpec((1, t, GROUP), lambda bi, ni: (bi, ni, QK_BLK["a_q"])),
                  pl.BlockSpec((1, s, GROUP), lambda bi, ni: (bi, 0, QK_BLK["a_k"])),
                  pl.BlockSpec((1, nt, GROUP, t), lambda bi, ni: (bi, 0, V_ROW_BLK["a_v"], 0))],
        out_specs=pl.BlockSpec((1, t, GROUP), lambda bi, ni: (bi, ni, 0)),
        out_shape=jax.ShapeDtypeStruct((b, s, GROUP), BF16),
        scratch_shapes=[pltpu.VMEM((nt, t, t), I32), pltpu.VMEM((nt, t // 2, t), I32),
                        pltpu.VMEM((4, 1, t), F32), pltpu.VMEM((4, 1, t), F32),
                        pltpu.VMEM((2, LANES, t), F32), pltpu.VMEM((1, t), F32),
                        pltpu.VMEM((IDX_HEADS, IDX_CAT, t), BF16), pltpu.VMEM((4, LANES, t), BF16)],
        compiler_params=_cparams(("arbitrary", "arbitrary")),
        name="dsa",
    )(iq, iw, ik, qk, qk, vt)


def _fox_kernel(q_ref, k0_ref, k1_ref, vt_ref, o_ref, m_ref, l_ref, acc_ref, qxt_ref, *, t):
    n = pl.program_id(1)
    _init_state(m_ref, l_ref, acc_ref)
    lane = lax.broadcasted_iota(I32, (1, LANES), 1)
    for h in range(4):
        qz = jnp.where(_head_mask(h % 2), q_ref[0, :, (h // 2) * LANES:(h // 2 + 1) * LANES], jnp.zeros((), BF16))
        ones = jnp.where((lane == h) | (lane == 4 + h) | (lane == 8 + h), 1.0, 0.0).astype(BF16)
        qxt_ref[h] = _transpose_bf16(jnp.concatenate([qz, jnp.broadcast_to(ones, (t, LANES))], axis=1))
    k_refs = (k0_ref, k1_ref)

    def chunk(j0, w, diag):
        start = pl.multiple_of(j0 * t, t)
        scores = [_dot(k_refs[h // 2][0, pl.ds(start, w * t), :], qxt_ref[h]) for h in range(4)]
        for h in range(4):
            g, r = divmod(h, 2)
            s = jnp.where(_causal_chunk(w, t), scores[h], -jnp.inf) if diag else scores[h]
            p, alpha = _softmax_step(s, m_ref, l_ref, h)
            _accumulate(acc_ref, g, r, alpha,
                        [vt_ref[0, j0 + i, h * HEAD_DIM:(h + 1) * HEAD_DIM, :] for i in range(w)], p, t)

    _key_chunks(n, chunk)
    for g in range(2):
        o_ref[0, :, g * LANES:(g + 1) * LANES] = _normalised_t(acc_ref, g, l_ref, 2 * g, 2 * g + 1).T.astype(BF16)


def _fox_call(qk, vt, *, t):
    b, s, _ = qk.shape
    nt = s // t
    return pl.pallas_call(
        functools.partial(_fox_kernel, t=t),
        grid=(b, nt),
        in_specs=[pl.BlockSpec((1, t, GROUP), lambda bi, ni: (bi, ni, QK_BLK["b_q"])),
                  pl.BlockSpec((1, s, GROUP), lambda bi, ni: (bi, 0, QK_BLK["b_k"])),
                  pl.BlockSpec((1, s, GROUP), lambda bi, ni: (bi, 0, QK_BLK["b_k"] + 1)),
                  pl.BlockSpec((1, nt, GROUP, t), lambda bi, ni: (bi, 0, V_ROW_BLK["b_v"], 0))],
        out_specs=pl.BlockSpec((1, t, GROUP), lambda bi, ni: (bi, ni, 0)),
        out_shape=jax.ShapeDtypeStruct((b, s, GROUP), BF16),
        scratch_shapes=[pltpu.VMEM((4, 1, t), F32), pltpu.VMEM((4, 1, t), F32), pltpu.VMEM((2, LANES, t), F32),
                        pltpu.VMEM((4, GROUP, t), BF16)],
        compiler_params=_cparams(("arbitrary", "arbitrary")),
        name="fox",
    )(qk, qk, qk, vt)


def _diff_kernel(lam_ref, q_ref, k_ref, vt_ref, gs_ref, o_ref, m_ref, l_ref, acc_ref, qzt_ref, *, t):
    n = pl.program_id(1)
    _init_state(m_ref, l_ref, acc_ref)
    lane = lax.broadcasted_iota(I32, (1, LANES), 1)
    for h in range(4):
        for mm in range(2):
            lo = (h % 2) * HEAD_DIM + mm * DIFF_DIM
            qzt_ref[2 * h + mm] = _transpose_bf16(
                jnp.where((lane >= lo) & (lane < lo + DIFF_DIM),
                          q_ref[0, :, (h // 2) * LANES:(h // 2 + 1) * LANES], jnp.zeros((), BF16)))

    def chunk(j0, w, diag):
        start = pl.multiple_of(j0 * t, t)
        ks = k_ref[0, pl.ds(start, w * t), :]
        scores = {(h, mm): _dot(ks[:, (h // 2) * LANES:(h // 2 + 1) * LANES], qzt_ref[2 * h + mm])
                  for h in range(4) for mm in range(2)}
        for h in range(4):
            g, r = divmod(h, 2)
            vts = [vt_ref[0, j0 + i, h * HEAD_DIM:(h + 1) * HEAD_DIM, :] for i in range(w)]
            for mm in range(2):
                s = jnp.where(_causal_chunk(w, t), scores[h, mm], -jnp.inf) if diag else scores[h, mm]
                p, alpha = _softmax_step(s, m_ref, l_ref, 2 * h + mm)
                _accumulate(acc_ref, 2 * mm + g, r, alpha, vts, p, t)

    _key_chunks(n, chunk)
    lam = lam_ref[0]
    out_scale = lam_ref[1]
    for g in range(2):
        o1 = _normalised_t(acc_ref, g, l_ref, 4 * g, 4 * g + 2)
        o2 = _normalised_t(acc_ref, 2 + g, l_ref, 4 * g + 1, 4 * g + 3)
        o = o1 - lam * o2
        sq = o * o
        ms = jnp.concatenate(
            [jnp.broadcast_to(jnp.mean(sq[r * HEAD_DIM:(r + 1) * HEAD_DIM], axis=0, keepdims=True), (HEAD_DIM, t))
             for r in range(2)], axis=0)
        y = o * lax.rsqrt(ms + EPS) * gs_ref[...] * out_scale
        o_ref[0, :, g * LANES:(g + 1) * LANES] = y.T.astype(BF16)


def _diff_call(lam2, qk, vt, gs_full, *, t):
    b, s, _ = qk.shape
    nt = s // t
    return pl.pallas_call(
        functools.partial(_diff_kernel, t=t),
        grid=(b, nt),
        in_specs=[pl.BlockSpec(memory_space=pltpu.SMEM),
                  pl.BlockSpec((1, t, GROUP), lambda bi, ni: (bi, ni, QK_BLK["c_q"])),
                  pl.BlockSpec((1, s, GROUP), lambda bi, ni: (bi, 0, QK_BLK["c_k"])),
                  pl.BlockSpec((1, nt, GROUP, t), lambda bi, ni: (bi, 0, V_ROW_BLK["c_v"], 0)),
                  pl.BlockSpec((LANES, t), lambda bi, ni: (0, 0))],
        out_specs=pl.BlockSpec((1, t, GROUP), lambda bi, ni: (bi, ni, 0)),
        out_shape=jax.ShapeDtypeStruct((b, s, GROUP), BF16),
        scratch_shapes=[pltpu.VMEM((8, 1, t), F32), pltpu.VMEM((8, 1, t), F32), pltpu.VMEM((4, LANES, t), F32),
                        pltpu.VMEM((8, LANES, t), BF16)],
        compiler_params=_cparams(("arbitrary", "arbitrary")),
        name="diff",
    )(lam2, qk, qk, vt, gs_full)


def _swa_kernel(sink_ref, q_ref, kp_ref, kc_ref, vtp_ref, vtc_ref, o_ref, *, t):
    n = pl.program_id(1)
    prow = lax.broadcasted_iota(I32, (WINDOW, t), 0)
    pcol = lax.broadcasted_iota(I32, (WINDOW, t), 1)
    mask_prev = (prow > pcol) & (pcol + jnp.where(n > 0, 0, t) < WINDOW)
    crow = lax.broadcasted_iota(I32, (t, t), 0)
    ccol = lax.broadcasted_iota(I32, (t, t), 1)
    mask_cur = (crow <= ccol) & (crow > ccol - WINDOW)
    vtp = vtp_ref[0, 0][:, t - WINDOW:t]
    vtc = vtc_ref[0, 0]
    for g in range(2):
        outs = []
        for r in range(2):
            h = 2 * g + r
            qz = jnp.where(_head_mask(r), q_ref[0, :, g * LANES:(g + 1) * LANES], jnp.zeros((), BF16))
            sp = jnp.where(mask_prev, _dot_nt(kp_ref[0, :, g * LANES:(g + 1) * LANES], qz), -jnp.inf)
            sc = jnp.where(mask_cur, _dot_nt(kc_ref[0, :, g * LANES:(g + 1) * LANES], qz), -jnp.inf)
            sink = sink_ref[h] * LOG2E
            m = jnp.maximum(jnp.maximum(jnp.max(sp, axis=0, keepdims=True),
                                        jnp.max(sc, axis=0, keepdims=True)), sink)
            pp = jnp.exp2(sp - m)
            pc = jnp.exp2(sc - m)
            den = jnp.sum(pp, axis=0, keepdims=True) + jnp.sum(pc, axis=0, keepdims=True) + jnp.exp2(sink - m)
            rows = slice(h * HEAD_DIM, (h + 1) * HEAD_DIM)
            o = (jnp.dot(vtp[rows, :], pp.astype(BF16), preferred_element_type=F32)
                 + jnp.dot(vtc[rows, :], pc.astype(BF16), preferred_element_type=F32))
            outs.append(o / den)
        o_ref[0, :, g * LANES:(g + 1) * LANES] = jnp.concatenate(outs, axis=0).T.astype(BF16)


def _swa_call(sinks, qk, vt, *, t):
    b, s, _ = qk.shape
    nt = s // t
    per = t // WINDOW
    return pl.pallas_call(
        functools.partial(_swa_kernel, t=t),
        grid=(b, nt),
        in_specs=[pl.BlockSpec(memory_space=pltpu.SMEM),
                  pl.BlockSpec((1, t, GROUP), lambda bi, ni: (bi, ni, QK_BLK["d_q"])),
                  pl.BlockSpec((1, WINDOW, GROUP), lambda bi, ni: (bi, jnp.maximum(ni * per - 1, 0), QK_BLK["d_k"])),
                  pl.BlockSpec((1, t, GROUP), lambda bi, ni: (bi, ni, QK_BLK["d_k"])),
                  pl.BlockSpec((1, 1, GROUP, t), lambda bi, ni: (bi, jnp.maximum(ni - 1, 0), V_ROW_BLK["d_v"], 0)),
                  pl.BlockSpec((1, 1, GROUP, t), lambda bi, ni: (bi, ni, V_ROW_BLK["d_v"], 0))],
        out_specs=pl.BlockSpec((1, t, GROUP), lambda bi, ni: (bi, ni, 0)),
        out_shape=jax.ShapeDtypeStruct((b, s, GROUP), BF16),
        compiler_params=_cparams(("arbitrary", "arbitrary")),
        name="swa",
    )(sinks, qk, qk, qk, vt, vt)


def _outproj_kernel(oa_ref, ob_ref, oc_ref, od_ref, x_ref, w_ref, g_ref, y_ref):
    acc = jnp.zeros(x_ref.shape[1:], F32)
    for i, o_ref in enumerate((oa_ref, ob_ref, oc_ref, od_ref)):
        acc = acc + jnp.dot(o_ref[0], w_ref[i * GROUP:(i + 1) * GROUP, :], preferred_element_type=F32)
    y_ref[0] = x_ref[0] + _rms(acc, g_ref[...])


def _mlp_kernel(x_ref, gpre_ref, wu_ref, wd_ref, gpost_ref, y_ref, *, chunk):
    x = x_ref[0]
    h = _rms(x, gpre_ref[...]).astype(BF16)
    acc = jnp.zeros(x.shape, F32)
    for c in range(D_FF // chunk):
        u = jnp.dot(h, wu_ref[:, c * chunk:(c + 1) * chunk], preferred_element_type=F32)
        u = jnp.square(jnp.maximum(u, 0.0)).astype(BF16)
        acc = acc + jnp.dot(u, wd_ref[c * chunk:(c + 1) * chunk, :], preferred_element_type=F32)
    y_ref[0] = x + _rms(acc, gpost_ref[...])


def _ple_kernel(x_ref, p_ref, wg_ref, wp_ref, y_ref):
    x = x_ref[0]
    gate = jax.nn.sigmoid(jnp.dot(x.astype(BF16), wg_ref[...], preferred_element_type=F32))
    emb = jnp.dot(p_ref[0].astype(BF16), wp_ref[...], preferred_element_type=F32)
    y_ref[0] = x + gate * emb


def _token_call(body, tok_inputs, full_inputs, order, out_w, *, tm, name):
    b, s = tok_inputs[0].shape[:2]
    tok = lambda a: pl.BlockSpec((1, tm, a.shape[2]), lambda bi, ti: (bi, ti, 0))
    full = lambda a: pl.BlockSpec(a.shape, lambda bi, ti: (0,) * a.ndim)
    ops, specs = [], []
    ti = fi = 0
    for kind in order:
        if kind == "t":
            ops.append(tok_inputs[ti]); specs.append(tok(tok_inputs[ti])); ti += 1
        else:
            ops.append(full_inputs[fi]); specs.append(full(full_inputs[fi])); fi += 1
    return pl.pallas_call(
        body, grid=(b, s // tm), in_specs=specs,
        out_specs=pl.BlockSpec((1, tm, out_w), lambda bi, ti_: (bi, ti_, 0)),
        out_shape=jax.ShapeDtypeStruct((b, s, out_w), F32),
        compiler_params=_cparams(("arbitrary", "arbitrary")),
        name=name,
    )(*ops)


def _dup_kv(w):
    return jnp.concatenate([w[:, :HEAD_DIM], w[:, :HEAD_DIM], w[:, HEAD_DIM:], w[:, HEAD_DIM:]], axis=1)


def _prep_w_in(w):
    sec = lambda n: w[:, _SEC[n][0]:_SEC[n][0] + _SEC[n][1]]
    cols = [_dup_kv(sec(n)) if n in ("d_k", "d_v") else sec(n) for n in QK_GROUPS + V_GROUPS]
    w_main = jnp.concatenate(cols, axis=1).astype(BF16)
    pad = jnp.zeros((w.shape[0], LANES - IDX_DIM - IDX_HEADS - 4), w.dtype)
    w_idx = jnp.concatenate([sec("iq"), sec("ik"), sec("iw"), sec("b_f"), pad], axis=1)
    return w_main, w_idx


def _rope_tables(positions):
    pos = positions.astype(F32)[..., None]
    lane = jnp.arange(LANES)

    def tabs(dim):
        half = dim // 2
        inv_freq = ROPE_THETA ** (-jnp.arange(half, dtype=F32) / half)
        ang = pos * inv_freq[lane % half]
        sign = jnp.where((lane % dim) < half, -1.0, 1.0).astype(F32)
        return jnp.cos(ang), jnp.sin(ang) * sign

    c64, s64 = tabs(HEAD_DIM)
    c32, s32 = tabs(DIFF_DIM)
    return c64, s64, c32, s32


def kernel(x, p, positions, w_in, b_forget, lambda_q1, lambda_k1, lambda_q2, lambda_k2, diff_subln, sinks,
           w_out, norm_pre_mix, norm_post_mix, norm_pre_mlp, norm_post_mlp, w_mlp_up, w_mlp_down,
           w_ple_proj, w_ple_gate):
    b, s, d = x.shape
    depth = w_in.shape[0]
    t = min(256, s)
    topk = min(TOPK_MAX, s // 4)
    tabs = _rope_tables(positions)
    row = lambda v: v.reshape(1, -1).astype(F32)

    for i in range(depth):
        lam_init = 0.8 - 0.6 * math.exp(-0.3 * i)
        w_main, w_idx = _prep_w_in(w_in[i])
        bf_row = jnp.zeros((1, LANES), F32).at[0, MISC_F:MISC_F + 4].set(b_forget[i])
        qk, vt, iq, ik, iw = _proj_call(x, row(norm_pre_mix[i]), w_main, w_idx, tabs, bf_row, tm=t)

        o_a = _dsa_call(qk, vt, iq, ik, iw, t=t, topk=topk)
        o_b = _fox_call(qk, vt, t=t)
        lam = (jnp.exp(jnp.sum(lambda_q1[i] * lambda_k1[i])) - jnp.exp(jnp.sum(lambda_q2[i] * lambda_k2[i]))
               + lam_init)
        lam2 = jnp.stack([lam, jnp.asarray(1.0 - lam_init, F32)]).astype(F32)
        gs_full = jnp.broadcast_to(jnp.concatenate([diff_subln[i], diff_subln[i]]).astype(F32)[:, None], (LANES, t))
        o_c = _diff_call(lam2, qk, vt, gs_full, t=t)
        o_d = _swa_call(sinks[i].astype(F32), qk, vt, t=t)

        x = _token_call(_outproj_kernel, [o_a, o_b, o_c, o_d, x], [w_out[i].astype(BF16), row(norm_post_mix[i])],
                        "tttttff", d, tm=t, name="outproj")
        x = _token_call(functools.partial(_mlp_kernel, chunk=1024), [x],
                        [row(norm_pre_mlp[i]), w_mlp_up[i].astype(BF16), w_mlp_down[i].astype(BF16),
                         row(norm_post_mlp[i])], "tffff", d, tm=t, name="mlp")
        x = _token_call(_ple_kernel, [x, p[i]], [w_ple_gate[i].astype(BF16), w_ple_proj[i].astype(BF16)],
                        "ttff", d, tm=t, name="ple")
    return x
```

```python
import functools
import math

import jax
import jax.numpy as jnp
from jax import lax
from jax.experimental import pallas as pl
from jax.experimental.pallas import tpu as pltpu

F32, BF16, I32 = jnp.float32, jnp.bfloat16, jnp.int32

D_MODEL = 1024
HEAD_DIM = 64
DIFF_DIM = 32
IDX_HEADS = 8
IDX_DIM = 64
TOPK_MAX = 256
WINDOW = 128
D_FF = 4 * D_MODEL
D_PLE = 256
ROPE_THETA = 10000.0
EPS = 1e-6
NEG_INF = -1e30
LANES = 128
SUBLANES = 8
GROUP = 256
INT_MIN = -2147483648
FIELD_BITS = 15
FIELD_GUARDS = -2147450880

_SEC = {}
_o = 0
for _name, _w in (("a_q", 256), ("a_k", 256), ("a_v", 256), ("iq", 512), ("ik", 64), ("iw", 8),
                  ("b_q", 256), ("b_k", 256), ("b_v", 256), ("b_f", 4),
                  ("c_q", 256), ("c_k", 256), ("c_v", 256), ("d_q", 256), ("d_k", 128), ("d_v", 128)):
    _SEC[_name] = (_o, _w)
    _o += _w
D_IN = _o

QK_GROUPS = ("a_q", "a_k", "b_q", "b_k", "c_q", "c_k", "d_q", "d_k")
V_GROUPS = ("a_v", "b_v", "c_v", "d_v")
MAIN_W = GROUP * (len(QK_GROUPS) + len(V_GROUPS))
V_BASE = GROUP * len(QK_GROUPS)
QK_BLK = {"a_q": 0, "a_k": 1, "b_q": 2, "b_k": 3, "c_q": 5, "c_k": 6, "d_q": 7, "d_k": 8}
QK_W = GROUP * 9
V_ROW_BLK = {n: i for i, n in enumerate(V_GROUPS)}
ROPE64_GROUPS = ("a_q", "a_k", "d_q", "d_k")
ROPE32_GROUPS = ("c_q", "c_k")
LOG2E = math.log2(math.e)
Q_SCALE = {"a_q": HEAD_DIM ** -0.5 * LOG2E, "b_q": HEAD_DIM ** -0.5 * LOG2E, "c_q": DIFF_DIM ** -0.5 * LOG2E,
           "d_q": HEAD_DIM ** -0.5 * LOG2E}
IDX_W = 512 + LANES
MISC_IW = 64
MISC_F = 72
IDX_CAT = 256

CHUNK_TILES = 4
CHUNK_TILES_DIFF = 2
VMEM_LIMIT = 56 * 1024 * 1024


def _cparams(sem):
    return pltpu.CompilerParams(dimension_semantics=sem, vmem_limit_bytes=VMEM_LIMIT)


def _rms(x, g):
    return x * lax.rsqrt(jnp.mean(x * x, axis=-1, keepdims=True) + EPS) * g


def _dot_nt(a, b):
    return lax.dot_general(a, b, (((1,), (1,)), ((), ())), preferred_element_type=F32)


def _dot(a, b):
    return jnp.dot(a, b, preferred_element_type=F32)


def _transpose_bf16(a):
    return a.astype(F32).T.astype(BF16)


def _bf16_part(x):
    return x.astype(BF16).astype(F32)


def _split2(x):
    hi = _bf16_part(x)
    return hi, _bf16_part(x - hi)


def _split3(x):
    hi = _bf16_part(x)
    mid = _bf16_part(x - hi)
    return hi, mid, _bf16_part(x - hi - mid)


def _rope_chunk(xc, cos, sin_signed, half, lane):
    fwd = pltpu.roll(xc, LANES - half, axis=1)
    bwd = pltpu.roll(xc, half, axis=1)
    partner = jnp.where((lane % (2 * half)) < half, fwd, bwd)
    return xc * cos + partner * sin_signed


def _proj_kernel(x_ref, g_ref, wm_ref, wi_ref, c64_ref, s64_ref, c32_ref, s32_ref, bf_ref,
                 qk_ref, vt_ref, iq_ref, ik_ref, iw_ref, carry_ref, *, tm):
    t = pl.program_id(1)
    x = x_ref[0]
    h = _rms(x, g_ref[...])
    pm = jnp.dot(h.astype(BF16), wm_ref[...], preferred_element_type=F32)
    h_hi, h_lo = _split2(h)
    pi = jnp.dot(jnp.concatenate([h_hi, h_lo, h_hi], axis=1).astype(BF16), wi_ref[...],
                 preferred_element_type=F32)
    lane = lax.broadcasted_iota(I32, (1, LANES), 1)
    lo_half = lane < HEAD_DIM
    c64, s64, c32, s32 = c64_ref[0], s64_ref[0], c32_ref[0], s32_ref[0]

    misc = pi[:, 512:512 + LANES]
    z = misc + bf_ref[...]
    logf = jnp.minimum(z, 0.0) - jnp.log1p(jnp.exp(-jnp.abs(z)))
    logf = jnp.where((lane >= MISC_F) & (lane < MISC_F + 4), logf, 0.0)
    tri = (lax.broadcasted_iota(I32, (tm, tm), 1) <= lax.broadcasted_iota(I32, (tm, tm), 0)).astype(BF16)

    @pl.when(t == 0)
    def _():
        carry_ref[...] = jnp.zeros_like(carry_ref)

    pieces = jnp.dot(tri, jnp.concatenate(_split3(logf), axis=1).astype(BF16), preferred_element_type=F32)
    cum = pieces[:, 0:LANES] + pieces[:, LANES:2 * LANES] + pieces[:, 2 * LANES:3 * LANES] + carry_ref[...]
    carry_ref[...] = cum[tm - 1:tm, :]
    nhi, nmid, nlo = _split3(-LOG2E * cum)
    gate_bias = jnp.where(lane < 4, pltpu.roll(nhi, LANES - MISC_F, axis=1),
                          jnp.where(lane < 8, pltpu.roll(nmid, LANES - MISC_F + 4, axis=1),
                                    jnp.where(lane < 12, pltpu.roll(nlo, LANES - MISC_F + 8, axis=1),
                                              jnp.zeros_like(nlo))))

    for gi, name in enumerate(QK_GROUPS):
        for c in range(GROUP // LANES):
            lo = gi * GROUP + c * LANES
            v = pm[:, lo:lo + LANES]
            if name in ROPE64_GROUPS:
                v = _rope_chunk(v, c64, s64, HEAD_DIM // 2, lane)
            elif name in ROPE32_GROUPS:
                v = _rope_chunk(v, c32, s32, DIFF_DIM // 2, lane)
            if name in Q_SCALE:
                v = v * Q_SCALE[name]
            if name == "b_k":
                out = (QK_BLK[name] + c) * GROUP
                qk_ref[0, :, out:out + LANES] = v.astype(BF16)
                qk_ref[0, :, out + LANES:out + GROUP] = gate_bias.astype(BF16)
            else:
                out = QK_BLK[name] * GROUP + c * LANES
                qk_ref[0, :, out:out + LANES] = v.astype(BF16)

    vt_ref[0, 0] = pm[:, V_BASE:V_BASE + len(V_GROUPS) * GROUP].T.astype(BF16)

    def cat_q(q):
        hi, lo = _split2(q)
        return jnp.where(lo_half, hi, pltpu.roll(lo, HEAD_DIM, axis=1)), hi

    for c in range(512 // LANES):
        v = _rope_chunk(pi[:, c * LANES:(c + 1) * LANES], c64, s64, IDX_DIM // 2, lane)
        for r, q in enumerate((jnp.where(lo_half, v, 0.0), jnp.where(lo_half, pltpu.roll(v, HEAD_DIM, axis=1), 0.0))):
            a, b2 = cat_q(q)
            iq_ref[0, 2 * c + r, :, 0:LANES] = a.astype(BF16)
            iq_ref[0, 2 * c + r, :, LANES:IDX_CAT] = b2.astype(BF16)
    ik = jnp.where(lo_half, _rope_chunk(misc, c64, s64, IDX_DIM // 2, lane), 0.0)
    khi, klo = _split2(ik)
    ik_ref[0, :, 0:LANES] = jnp.where(lo_half, khi, pltpu.roll(khi, HEAD_DIM, axis=1)).astype(BF16)
    ik_ref[0, :, LANES:IDX_CAT] = klo.astype(BF16)
    iw_t = (misc * (IDX_HEADS ** -0.5 * IDX_DIM ** -0.5)).T
    iw_ref[0, 0] = iw_t[MISC_IW:MISC_IW + IDX_HEADS, :]


def _proj_call(x, g, w_main, w_idx, tabs, bf_row, *, tm):
    b, s, d = x.shape
    tok = lambda w: pl.BlockSpec((1, tm, w), lambda bi, ti: (bi, ti, 0))
    full = lambda a: pl.BlockSpec(a.shape, lambda bi, ti: (0,) * a.ndim)
    return pl.pallas_call(
        functools.partial(_proj_kernel, tm=tm),
        grid=(b, s // tm),
        in_specs=[tok(d), full(g), full(w_main), full(w_idx), tok(LANES), tok(LANES), tok(LANES), tok(LANES),
                  full(bf_row)],
        out_specs=[tok(QK_W),
                   pl.BlockSpec((1, 1, len(V_GROUPS) * GROUP, tm), lambda bi, ti: (bi, ti, 0, 0)),
                   pl.BlockSpec((1, IDX_HEADS, tm, IDX_CAT), lambda bi, ti: (bi, 0, ti, 0)),
                   tok(IDX_CAT),
                   pl.BlockSpec((1, 1, IDX_HEADS, tm), lambda bi, ti: (bi, ti, 0, 0))],
        out_shape=[jax.ShapeDtypeStruct((b, s, QK_W), BF16),
                   jax.ShapeDtypeStruct((b, s // tm, len(V_GROUPS) * GROUP, tm), BF16),
                   jax.ShapeDtypeStruct((b, IDX_HEADS, s, IDX_CAT), BF16),
                   jax.ShapeDtypeStruct((b, s, IDX_CAT), BF16),
                   jax.ShapeDtypeStruct((b, s // tm, IDX_HEADS, tm), F32)],
        scratch_shapes=[pltpu.VMEM((1, LANES), F32)],
        compiler_params=_cparams(("arbitrary", "arbitrary")),
        name="proj",
    )(x, g, w_main, w_idx, *tabs, bf_row)


def _softmax_step(s, m_ref, l_ref, idx):
    m_old = m_ref[idx]
    m_new = jnp.maximum(m_old, jnp.max(s, axis=0, keepdims=True))
    alpha = jnp.exp2(m_old - m_new)
    p = jnp.exp2(s - m_new)
    l_ref[idx] = alpha * l_ref[idx] + jnp.sum(p, axis=0, keepdims=True)
    m_ref[idx] = m_new
    return p, alpha


def _init_state(m_ref, l_ref, acc_ref):
    m_ref[...] = jnp.full(m_ref.shape, NEG_INF, F32)
    l_ref[...] = jnp.zeros(l_ref.shape, F32)
    acc_ref[...] = jnp.zeros(acc_ref.shape, F32)


def _causal_chunk(w, t):
    return lax.broadcasted_iota(I32, (w * t, t), 0) <= lax.broadcasted_iota(I32, (w * t, t), 1) + (w - 1) * t


def _head_mask(r):
    lane = lax.broadcasted_iota(I32, (1, LANES), 1)
    return (lane >= HEAD_DIM) if r else (lane < HEAD_DIM)


def _key_chunks(n, chunk_fn, width):
    def _body(c, carry):
        chunk_fn(width * c, width, False)
        return carry

    n_full = n // width
    lax.fori_loop(0, n_full, _body, 0)
    rest = n - n_full * width
    for k in range(width):
        @pl.when(rest == k)
        def _(k=k):
            chunk_fn(n - k, k + 1, True)


def _accumulate(acc_ref, a, r, alpha, vt_tiles, p, t):
    rows = slice(r * HEAD_DIM, (r + 1) * HEAD_DIM)
    pb = p.astype(BF16)
    pv = jnp.dot(vt_tiles[0], pb[0:t], preferred_element_type=F32)
    for i in range(1, len(vt_tiles)):
        pv = pv + jnp.dot(vt_tiles[i], pb[i * t:(i + 1) * t], preferred_element_type=F32)
    acc_ref[a, rows, :] = acc_ref[a, rows, :] * alpha + pv


def _normalised_t(acc_ref, a, l_ref, idx0, idx1):
    return jnp.concatenate([acc_ref[a, 0:HEAD_DIM, :] / l_ref[idx0],
                            acc_ref[a, HEAD_DIM:LANES, :] / l_ref[idx1]], axis=0)


def _dsa_kernel(iq_ref, iw_ref, ik_ref, q_ref, k_ref, vt_ref, o_ref,
                key_ref, pack_ref, m_ref, l_ref, acc_ref, carry_ref, iqt_ref, qzt_ref, *, t, topk):
    n = pl.program_id(1)
    for h in range(IDX_HEADS):
        iqt_ref[h] = _transpose_bf16(iq_ref[0, h])
    for h in range(4):
        qzt_ref[h] = _transpose_bf16(jnp.where(_head_mask(h % 2), q_ref[0, :, (h // 2) * LANES:(h // 2 + 1) * LANES],
                                               jnp.zeros((), BF16)))

    def score_chunk(j0, w, diag):
        start = pl.multiple_of(j0 * t, t)
        ik = ik_ref[0, pl.ds(start, w * t), :]
        sc = jnp.zeros((w * t, t), F32)
        for h in range(IDX_HEADS):
            d = _dot(ik, iqt_ref[h])
            sc = sc + iw_ref[0, 0, h:h + 1, :] * jnp.maximum(d, 0.0)
        if diag:
            sc = jnp.where(_causal_chunk(w, t), sc, NEG_INF)
        sc = jnp.where(sc == 0.0, 0.0, sc)
        bits = lax.bitcast_convert_type(sc, I32)
        key = bits ^ ((bits >> 31) & jnp.int32(0x7FFFFFFF))
        top = lax.shift_right_logical(key ^ jnp.int32(INT_MIN), 32 - FIELD_BITS)
        for i in range(w):
            key_ref[j0 + i] = key[i * t:(i + 1) * t]
            pack_ref[j0 + i] = (jnp.left_shift(top[i * t:i * t + t // 2], 16) | top[i * t + t // 2:(i + 1) * t]
                                | jnp.int32(FIELD_GUARDS))

    _key_chunks(n, score_chunk, CHUNK_TILES)

    def tree_sum(parts):
        while len(parts) > 1:
            parts = [parts[i] + parts[i + 1] for i in range(0, len(parts), 2)]
        return parts[0]

    def count(pred):
        def body(j, acc):
            c = jnp.where(pred(key_ref[j]), 1.0, 0.0)
            return acc + tree_sum([c[r * SUBLANES:(r + 1) * SUBLANES, :] for r in range(t // SUBLANES)])
        acc = lax.fori_loop(0, n + 1, body, jnp.zeros((SUBLANES, t), F32))
        return jnp.sum(acc, axis=0, keepdims=True)

    def count_top(cand):
        both = jnp.left_shift(cand, 16) | cand

        def body(j, acc):
            z = pack_ref[j] - both
            c = lax.shift_right_logical(z, FIELD_BITS) & jnp.int32(0x00010001)
            return acc + tree_sum([c[r * SUBLANES:(r + 1) * SUBLANES, :] for r in range(t // 2 // SUBLANES)])
        acc = lax.fori_loop(0, n + 1, body, jnp.zeros((SUBLANES, t), I32))
        per_lane = lax.shift_right_logical(acc, 16) + (acc & jnp.int32(0xFFFF))
        return jnp.sum(per_lane.astype(F32), axis=0, keepdims=True)

    def bisect(n_bits, low_bit, count_fn, carry):
        def body(i, carry):
            u, n_ge = carry
            cand = u | jnp.left_shift(jnp.int32(1), low_bit + n_bits - 1 - i)
            cnt = count_fn(cand)
            keep = cnt >= float(topk)
            return jnp.where(keep, cand, u), jnp.where(keep, cnt, n_ge)
        return lax.fori_loop(0, n_bits, body, carry)

    in_scope = ((n + 1) * t).astype(F32)
    top, n_ge = bisect(FIELD_BITS, 0, count_top, (jnp.zeros((1, t), I32), jnp.zeros((1, t), F32) + in_scope))
    u, n_ge = bisect(32 - FIELD_BITS, 0, lambda cand: count(lambda kt: kt >= (cand ^ jnp.int32(INT_MIN))),
                     (jnp.left_shift(top, 32 - FIELD_BITS), n_ge))
    thr = u ^ jnp.int32(INT_MIN)
    has_ties = jnp.max(n_ge) > float(topk)

    _init_state(m_ref, l_ref, acc_ref)

    def attend(ties):
        if ties:
            carry_ref[...] = jnp.zeros(carry_ref.shape, F32)
            need = float(topk) - count(lambda kt: kt > thr)
            lower = (lax.broadcasted_iota(I32, (t, t), 1) <= lax.broadcasted_iota(I32, (t, t), 0)).astype(BF16)

        def chunk(j0, w, diag):
            start = pl.multiple_of(j0 * t, t)
            ks = k_ref[0, pl.ds(start, w * t), :]
            scores = [_dot(ks[:, (h // 2) * LANES:(h // 2 + 1) * LANES], qzt_ref[h]) for h in range(4)]
            if ties:
                parts = []
                for i in range(w):
                    kt = key_ref[j0 + i]
                    eq = kt == thr
                    incl = jnp.dot(lower, jnp.where(eq, 1.0, 0.0).astype(BF16),
                                   preferred_element_type=F32) + carry_ref[...]
                    carry_ref[...] = incl[t - 1:t, :]
                    parts.append(jnp.where((kt > thr) | (eq & (incl <= need)), 1.0, 0.0))
                sel = (parts[0] if w == 1 else jnp.concatenate(parts, axis=0)) > 0.5
            else:
                kt = key_ref[j0] if w == 1 else jnp.concatenate([key_ref[j0 + i] for i in range(w)], axis=0)
                sel = kt >= thr
            if diag:
                sel = sel & _causal_chunk(w, t)
            for h in range(4):
                g, r = divmod(h, 2)
                p, alpha = _softmax_step(jnp.where(sel, scores[h], -jnp.inf), m_ref, l_ref, h)
                _accumulate(acc_ref, g, r, alpha,
                            [vt_ref[0, j0 + i, h * HEAD_DIM:(h + 1) * HEAD_DIM, :] for i in range(w)], p, t)

        _key_chunks(n, chunk, CHUNK_TILES)

    @pl.when(has_ties)
    def _():
        attend(True)

    @pl.when(jnp.logical_not(has_ties))
    def _():
        attend(False)

    for g in range(2):
        o_ref[0, :, g * LANES:(g + 1) * LANES] = _normalised_t(acc_ref, g, l_ref, 2 * g, 2 * g + 1).T.astype(BF16)


def _dsa_call(qk, vt, iq, ik, iw, *, t, topk):
    b, s, _ = qk.shape
    nt = s // t
    return pl.pallas_call(
        functools.partial(_dsa_kernel, t=t, topk=topk),
        grid=(b, nt),
        in_specs=[pl.BlockSpec((1, IDX_HEADS, t, IDX_CAT), lambda bi, ni: (bi, 0, ni, 0)),
                  pl.BlockSpec((1, 1, IDX_HEADS, t), lambda bi, ni: (bi, ni, 0, 0)),
                  pl.BlockSpec((1, s, IDX_CAT), lambda bi, ni: (bi, 0, 0)),
                  pl.BlockSpec((1, t, GROUP), lambda bi, ni: (bi, ni, QK_BLK["a_q"])),
                  pl.BlockSpec((1, s, GROUP), lambda bi, ni: (bi, 0, QK_BLK["a_k"])),
                  pl.BlockSpec((1, nt, GROUP, t), lambda bi, ni: (bi, 0, V_ROW_BLK["a_v"], 0))],
        out_specs=pl.BlockSpec((1, t, GROUP), lambda bi, ni: (bi, ni, 0)),
        out_shape=jax.ShapeDtypeStruct((b, s, GROUP), BF16),
        scratch_shapes=[pltpu.VMEM((nt, t, t), I32), pltpu.VMEM((nt, t // 2, t), I32),
                        pltpu.VMEM((4, 1, t), F32), pltpu.VMEM((4, 1, t), F32),
                        pltpu.VMEM((2, LANES, t), F32), pltpu.VMEM((1, t), F32),
                        pltpu.VMEM((IDX_HEADS, IDX_CAT, t), BF16), pltpu.VMEM((4, LANES, t), BF16)],
        compiler_params=_cparams(("arbitrary", "arbitrary")),
        name="dsa",
    )(iq, iw, ik, qk, qk, vt)


def _fox_kernel(q_ref, k0_ref, k1_ref, vt_ref, o_ref, m_ref, l_ref, acc_ref, qxt_ref, *, t):
    n = pl.program_id(1)
    _init_state(m_ref, l_ref, acc_ref)
    lane = lax.broadcasted_iota(I32, (1, LANES), 1)
    for h in range(4):
        qz = jnp.where(_head_mask(h % 2), q_ref[0, :, (h // 2) * LANES:(h // 2 + 1) * LANES], jnp.zeros((), BF16))
        ones = jnp.where((lane == h) | (lane == 4 + h) | (lane == 8 + h), 1.0, 0.0).astype(BF16)
        qxt_ref[h] = _transpose_bf16(jnp.concatenate([qz, jnp.broadcast_to(ones, (t, LANES))], axis=1))
    k_refs = (k0_ref, k1_ref)

    def chunk(j0, w, diag):
        start = pl.multiple_of(j0 * t, t)
        scores = [_dot(k_refs[h // 2][0, pl.ds(start, w * t), :], qxt_ref[h]) for h in range(4)]
        for h in range(4):
            g, r = divmod(h, 2)
            s = jnp.where(_causal_chunk(w, t), scores[h], -jnp.inf) if diag else scores[h]
            p, alpha = _softmax_step(s, m_ref, l_ref, h)
            _accumulate(acc_ref, g, r, alpha,
                        [vt_ref[0, j0 + i, h * HEAD_DIM:(h + 1) * HEAD_DIM, :] for i in range(w)], p, t)

    _key_chunks(n, chunk, CHUNK_TILES)
    for g in range(2):
        o_ref[0, :, g * LANES:(g + 1) * LANES] = _normalised_t(acc_ref, g, l_ref, 2 * g, 2 * g + 1).T.astype(BF16)


def _fox_call(qk, vt, *, t):
    b, s, _ = qk.shape
    nt = s // t
    return pl.pallas_call(
        functools.partial(_fox_kernel, t=t),
        grid=(b, nt),
        in_specs=[pl.BlockSpec((1, t, GROUP), lambda bi, ni: (bi, ni, QK_BLK["b_q"])),
                  pl.BlockSpec((1, s, GROUP), lambda bi, ni: (bi, 0, QK_BLK["b_k"])),
                  pl.BlockSpec((1, s, GROUP), lambda bi, ni: (bi, 0, QK_BLK["b_k"] + 1)),
                  pl.BlockSpec((1, nt, GROUP, t), lambda bi, ni: (bi, 0, V_ROW_BLK["b_v"], 0))],
        out_specs=pl.BlockSpec((1, t, GROUP), lambda bi, ni: (bi, ni, 0)),
        out_shape=jax.ShapeDtypeStruct((b, s, GROUP), BF16),
        scratch_shapes=[pltpu.VMEM((4, 1, t), F32), pltpu.VMEM((4, 1, t), F32), pltpu.VMEM((2, LANES, t), F32),
                        pltpu.VMEM((4, GROUP, t), BF16)],
        compiler_params=_cparams(("arbitrary", "arbitrary")),
        name="fox",
    )(qk, qk, qk, vt)


def _diff_kernel(lam_ref, q_ref, k_ref, vt_ref, gs_ref, o_ref, m_ref, l_ref, acc_ref, qzt_ref, *, t):
    n = pl.program_id(1)
    _init_state(m_ref, l_ref, acc_ref)
    lane = lax.broadcasted_iota(I32, (1, LANES), 1)
    for h in range(4):
        for mm in range(2):
            lo = (h % 2) * HEAD_DIM + mm * DIFF_DIM
            qzt_ref[2 * h + mm] = _transpose_bf16(
                jnp.where((lane >= lo) & (lane < lo + DIFF_DIM),
                          q_ref[0, :, (h // 2) * LANES:(h // 2 + 1) * LANES], jnp.zeros((), BF16)))

    def chunk(j0, w, diag):
        start = pl.multiple_of(j0 * t, t)
        ks = k_ref[0, pl.ds(start, w * t), :]
        scores = {(h, mm): _dot(ks[:, (h // 2) * LANES:(h // 2 + 1) * LANES], qzt_ref[2 * h + mm])
                  for h in range(4) for mm in range(2)}
        for h in range(4):
            g, r = divmod(h, 2)
            vts = [vt_ref[0, j0 + i, h * HEAD_DIM:(h + 1) * HEAD_DIM, :] for i in range(w)]
            for mm in range(2):
                s = jnp.where(_causal_chunk(w, t), scores[h, mm], -jnp.inf) if diag else scores[h, mm]
                p, alpha = _softmax_step(s, m_ref, l_ref, 2 * h + mm)
                _accumulate(acc_ref, 2 * mm + g, r, alpha, vts, p, t)

    _key_chunks(n, chunk, CHUNK_TILES_DIFF)
    lam = lam_ref[0]
    out_scale = lam_ref[1]
    for g in range(2):
        o1 = _normalised_t(acc_ref, g, l_ref, 4 * g, 4 * g + 2)
        o2 = _normalised_t(acc_ref, 2 + g, l_ref, 4 * g + 1, 4 * g + 3)
        o = o1 - lam * o2
        sq = o * o
        ms = jnp.concatenate(
            [jnp.broadcast_to(jnp.mean(sq[r * HEAD_DIM:(r + 1) * HEAD_DIM], axis=0, keepdims=True), (HEAD_DIM, t))
             for r in range(2)], axis=0)
        y = o * lax.rsqrt(ms + EPS) * gs_ref[...] * out_scale
        o_ref[0, :, g * LANES:(g + 1) * LANES] = y.T.astype(BF16)


def _diff_call(lam2, qk, vt, gs_full, *, t):
    b, s, _ = qk.shape
    nt = s // t
    return pl.pallas_call(
        functools.partial(_diff_kernel, t=t),
        grid=(b, nt),
        in_specs=[pl.BlockSpec(memory_space=pltpu.SMEM),
                  pl.BlockSpec((1, t, GROUP), lambda bi, ni: (bi, ni, QK_BLK["c_q"])),
                  pl.BlockSpec((1, s, GROUP), lambda bi, ni: (bi, 0, QK_BLK["c_k"])),
                  pl.BlockSpec((1, nt, GROUP, t), lambda bi, ni: (bi, 0, V_ROW_BLK["c_v"], 0)),
                  pl.BlockSpec((LANES, t), lambda bi, ni: (0, 0))],
        out_specs=pl.BlockSpec((1, t, GROUP), lambda bi, ni: (bi, ni, 0)),
        out_shape=jax.ShapeDtypeStruct((b, s, GROUP), BF16),
        scratch_shapes=[pltpu.VMEM((8, 1, t), F32), pltpu.VMEM((8, 1, t), F32), pltpu.VMEM((4, LANES, t), F32),
                        pltpu.VMEM((8, LANES, t), BF16)],
        compiler_params=_cparams(("arbitrary", "arbitrary")),
        name="diff",
    )(lam2, qk, qk, vt, gs_full)


def _swa_kernel(sink_ref, q_ref, kp_ref, kc_ref, vtp_ref, vtc_ref, o_ref, *, t):
    n = pl.program_id(1)
    prow = lax.broadcasted_iota(I32, (WINDOW, t), 0)
    pcol = lax.broadcasted_iota(I32, (WINDOW, t), 1)
    mask_prev = (prow > pcol) & (pcol + jnp.where(n > 0, 0, t) < WINDOW)
    crow = lax.broadcasted_iota(I32, (t, t), 0)
    ccol = lax.broadcasted_iota(I32, (t, t), 1)
    mask_cur = (crow <= ccol) & (crow > ccol - WINDOW)
    vtp = vtp_ref[0, 0][:, t - WINDOW:t]
    vtc = vtc_ref[0, 0]
    for g in range(2):
        outs = []
        for r in range(2):
            h = 2 * g + r
            qz = jnp.where(_head_mask(r), q_ref[0, :, g * LANES:(g + 1) * LANES], jnp.zeros((), BF16))
            sp = jnp.where(mask_prev, _dot_nt(kp_ref[0, :, g * LANES:(g + 1) * LANES], qz), -jnp.inf)
            sc = jnp.where(mask_cur, _dot_nt(kc_ref[0, :, g * LANES:(g + 1) * LANES], qz), -jnp.inf)
            sink = sink_ref[h] * LOG2E
            m = jnp.maximum(jnp.maximum(jnp.max(sp, axis=0, keepdims=True),
                                        jnp.max(sc, axis=0, keepdims=True)), sink)
            pp = jnp.exp2(sp - m)
            pc = jnp.exp2(sc - m)
            den = jnp.sum(pp, axis=0, keepdims=True) + jnp.sum(pc, axis=0, keepdims=True) + jnp.exp2(sink - m)
            rows = slice(h * HEAD_DIM, (h + 1) * HEAD_DIM)
            o = (jnp.dot(vtp[rows, :], pp.astype(BF16), preferred_element_type=F32)
                 + jnp.dot(vtc[rows, :], pc.astype(BF16), preferred_element_type=F32))
            outs.append(o / den)
        o_ref[0, :, g * LANES:(g + 1) * LANES] = jnp.concatenate(outs, axis=0).T.astype(BF16)


def _swa_call(sinks, qk, vt, *, t):
    b, s, _ = qk.shape
    nt = s // t
    per = t // WINDOW
    return pl.pallas_call(
        functools.partial(_swa_kernel, t=t),
        grid=(b, nt),
        in_specs=[pl.BlockSpec(memory_space=pltpu.SMEM),
                  pl.BlockSpec((1, t, GROUP), lambda bi, ni: (bi, ni, QK_BLK["d_q"])),
                  pl.BlockSpec((1, WINDOW, GROUP), lambda bi, ni: (bi, jnp.maximum(ni * per - 1, 0), QK_BLK["d_k"])),
                  pl.BlockSpec((1, t, GROUP), lambda bi, ni: (bi, ni, QK_BLK["d_k"])),
                  pl.BlockSpec((1, 1, GROUP, t), lambda bi, ni: (bi, jnp.maximum(ni - 1, 0), V_ROW_BLK["d_v"], 0)),
                  pl.BlockSpec((1, 1, GROUP, t), lambda bi, ni: (bi, ni, V_ROW_BLK["d_v"], 0))],
        out_specs=pl.BlockSpec((1, t, GROUP), lambda bi, ni: (bi, ni, 0)),
        out_shape=jax.ShapeDtypeStruct((b, s, GROUP), BF16),
        compiler_params=_cparams(("arbitrary", "arbitrary")),
        name="swa",
    )(sinks, qk, qk, qk, vt, vt)


def _outproj_kernel(oa_ref, ob_ref, oc_ref, od_ref, x_ref, w_ref, g_ref, y_ref):
    acc = jnp.zeros(x_ref.shape[1:], F32)
    for i, o_ref in enumerate((oa_ref, ob_ref, oc_ref, od_ref)):
        acc = acc + jnp.dot(o_ref[0], w_ref[i * GROUP:(i + 1) * GROUP, :], preferred_element_type=F32)
    y_ref[0] = x_ref[0] + _rms(acc, g_ref[...])


def _mlp_kernel(x_ref, gpre_ref, wu_ref, wd_ref, gpost_ref, y_ref, *, chunk):
    x = x_ref[0]
    h = _rms(x, gpre_ref[...]).astype(BF16)
    acc = jnp.zeros(x.shape, F32)
    for c in range(D_FF // chunk):
        u = jnp.dot(h, wu_ref[:, c * chunk:(c + 1) * chunk], preferred_element_type=F32)
        u = jnp.square(jnp.maximum(u, 0.0)).astype(BF16)
        acc = acc + jnp.dot(u, wd_ref[c * chunk:(c + 1) * chunk, :], preferred_element_type=F32)
    y_ref[0] = x + _rms(acc, gpost_ref[...])


def _ple_kernel(x_ref, p_ref, wg_ref, wp_ref, y_ref):
    x = x_ref[0]
    gate = jax.nn.sigmoid(jnp.dot(x.astype(BF16), wg_ref[...], preferred_element_type=F32))
    emb = jnp.dot(p_ref[0].astype(BF16), wp_ref[...], preferred_element_type=F32)
    y_ref[0] = x + gate * emb


def _token_call(body, tok_inputs, full_inputs, order, out_w, *, tm, name):
    b, s = tok_inputs[0].shape[:2]
    tok = lambda a: pl.BlockSpec((1, tm, a.shape[2]), lambda bi, ti: (bi, ti, 0))
    full = lambda a: pl.BlockSpec(a.shape, lambda bi, ti: (0,) * a.ndim)
    ops, specs = [], []
    ti = fi = 0
    for kind in order:
        if kind == "t":
            ops.append(tok_inputs[ti]); specs.append(tok(tok_inputs[ti])); ti += 1
        else:
            ops.append(full_inputs[fi]); specs.append(full(full_inputs[fi])); fi += 1
    return pl.pallas_call(
        body, grid=(b, s // tm), in_specs=specs,
        out_specs=pl.BlockSpec((1, tm, out_w), lambda bi, ti_: (bi, ti_, 0)),
        out_shape=jax.ShapeDtypeStruct((b, s, out_w), F32),
        compiler_params=_cparams(("arbitrary", "arbitrary")),
        name=name,
    )(*ops)


def _dup_kv(w):
    return jnp.concatenate([w[:, :HEAD_DIM], w[:, :HEAD_DIM], w[:, HEAD_DIM:], w[:, HEAD_DIM:]], axis=1)


def _prep_w_in(w):
    sec = lambda n: w[:, _SEC[n][0]:_SEC[n][0] + _SEC[n][1]]
    cols = [_dup_kv(sec(n)) if n in ("d_k", "d_v") else sec(n) for n in QK_GROUPS + V_GROUPS]
    w_main = jnp.concatenate(cols, axis=1).astype(BF16)
    pad = jnp.zeros((w.shape[0], LANES - IDX_DIM - IDX_HEADS - 4), w.dtype)
    w_idx = jnp.concatenate([sec("iq"), sec("ik"), sec("iw"), sec("b_f"), pad], axis=1)
    w_hi = w_idx.astype(BF16)
    w_lo = (w_idx - w_hi.astype(F32)).astype(BF16)
    return w_main, jnp.concatenate([w_hi, w_hi, w_lo], axis=0)


def _rope_tables(positions):
    pos = positions.astype(F32)[..., None]
    lane = jnp.arange(LANES)

    def tabs(dim):
        half = dim // 2
        inv_freq = ROPE_THETA ** (-jnp.arange(half, dtype=F32) / half)
        ang = pos * inv_freq[lane % half]
        sign = jnp.where((lane % dim) < half, -1.0, 1.0).astype(F32)
        return jnp.cos(ang), jnp.sin(ang) * sign

    c64, s64 = tabs(HEAD_DIM)
    c32, s32 = tabs(DIFF_DIM)
    return c64, s64, c32, s32


def kernel(x, p, positions, w_in, b_forget, lambda_q1, lambda_k1, lambda_q2, lambda_k2, diff_subln, sinks,
           w_out, norm_pre_mix, norm_post_mix, norm_pre_mlp, norm_post_mlp, w_mlp_up, w_mlp_down,
           w_ple_proj, w_ple_gate):
    b, s, d = x.shape
    depth = w_in.shape[0]
    t = min(256, s)
    topk = min(TOPK_MAX, s // 4)
    tabs = _rope_tables(positions)
    row = lambda v: v.reshape(1, -1).astype(F32)

    for i in range(depth):
        lam_init = 0.8 - 0.6 * math.exp(-0.3 * i)
        w_main, w_idx = _prep_w_in(w_in[i])
        bf_row = jnp.zeros((1, LANES), F32).at[0, MISC_F:MISC_F + 4].set(b_forget[i])
        qk, vt, iq, ik, iw = _proj_call(x, row(norm_pre_mix[i]), w_main, w_idx, tabs, bf_row, tm=t)

        o_a = _dsa_call(qk, vt, iq, ik, iw, t=t, topk=topk)
        o_b = _fox_call(qk, vt, t=t)
        lam = (jnp.exp(jnp.sum(lambda_q1[i] * lambda_k1[i])) - jnp.exp(jnp.sum(lambda_q2[i] * lambda_k2[i]))
               + lam_init)
        lam2 = jnp.stack([lam, jnp.asarray(1.0 - lam_init, F32)]).astype(F32)
        gs_full = jnp.broadcast_to(jnp.concatenate([diff_subln[i], diff_subln[i]]).astype(F32)[:, None], (LANES, t))
        o_c = _diff_call(lam2, qk, vt, gs_full, t=t)
        o_d = _swa_call(sinks[i].astype(F32), qk, vt, t=t)

        x = _token_call(_outproj_kernel, [o_a, o_b, o_c, o_d, x], [w_out[i].astype(BF16), row(norm_post_mix[i])],
                        "tttttff", d, tm=t, name="outproj")
        x = _token_call(functools.partial(_mlp_kernel, chunk=1024), [x],
                        [row(norm_pre_mlp[i]), w_mlp_up[i].astype(BF16), w_mlp_down[i].astype(BF16),
                         row(norm_post_mlp[i])], "tffff", d, tm=t, name="mlp")
        x = _token_call(_ple_kernel, [x, p[i]], [w_ple_gate[i].astype(BF16), w_ple_proj[i].astype(BF16)],
                        "ttff", d, tm=t, name="ple")
    return x
```

```python
import functools
import math

import jax
import jax.numpy as jnp
from jax import lax
from jax.experimental import pallas as pl
from jax.experimental.pallas import tpu as pltpu

F32, BF16, I32 = jnp.float32, jnp.bfloat16, jnp.int32

D_MODEL = 1024
HEAD_DIM = 64
DIFF_DIM = 32
IDX_HEADS = 8
IDX_DIM = 64
TOPK_MAX = 256
WINDOW = 128
D_FF = 4 * D_MODEL
D_PLE = 256
ROPE_THETA = 10000.0
EPS = 1e-6
NEG_INF = -1e30
LANES = 128
SUBLANES = 8
GROUP = 256
INT_MIN = -2147483648
FIELD_BITS = 15
FIELD_GUARDS = -2147450880

_SEC = {}
_o = 0
for _name, _w in (("a_q", 256), ("a_k", 256), ("a_v", 256), ("iq", 512), ("ik", 64), ("iw", 8),
                  ("b_q", 256), ("b_k", 256), ("b_v", 256), ("b_f", 4),
                  ("c_q", 256), ("c_k", 256), ("c_v", 256), ("d_q", 256), ("d_k", 128), ("d_v", 128)):
    _SEC[_name] = (_o, _w)
    _o += _w
D_IN = _o

QK_GROUPS = ("a_q", "a_k", "b_q", "b_k", "c_q", "c_k", "d_q", "d_k")
V_GROUPS = ("a_v", "b_v", "c_v", "d_v")
MAIN_W = GROUP * (len(QK_GROUPS) + len(V_GROUPS))
V_BASE = GROUP * len(QK_GROUPS)
QK_BLK = {"a_q": 0, "a_k": 1, "b_q": 2, "b_k": 3, "c_q": 5, "c_k": 6, "d_q": 7, "d_k": 8}
QK_W = GROUP * 9
V_ROW_BLK = {n: i for i, n in enumerate(V_GROUPS)}
ROPE64_GROUPS = ("a_q", "a_k", "d_q", "d_k")
ROPE32_GROUPS = ("c_q", "c_k")
LOG2E = math.log2(math.e)
Q_SCALE = {"a_q": HEAD_DIM ** -0.5 * LOG2E, "b_q": HEAD_DIM ** -0.5 * LOG2E, "c_q": DIFF_DIM ** -0.5 * LOG2E,
           "d_q": HEAD_DIM ** -0.5 * LOG2E}
IDX_W = 512 + LANES
MISC_IW = 64
MISC_F = 72
IDX_CAT = 256

CHUNK_TILES = 4
CHUNK_TILES_DIFF = 2
VMEM_LIMIT = 56 * 1024 * 1024


def _cparams(sem):
    return pltpu.CompilerParams(dimension_semantics=sem, vmem_limit_bytes=VMEM_LIMIT)


def _rms(x, g):
    return x * lax.rsqrt(jnp.mean(x * x, axis=-1, keepdims=True) + EPS) * g


def _dot_nt(a, b):
    return lax.dot_general(a, b, (((1,), (1,)), ((), ())), preferred_element_type=F32)


def _dot(a, b):
    return jnp.dot(a, b, preferred_element_type=F32)


def _transpose_bf16(a):
    return a.astype(F32).T.astype(BF16)


def _bf16_part(x):
    return x.astype(BF16).astype(F32)


def _split2(x):
    hi = _bf16_part(x)
    return hi, _bf16_part(x - hi)


def _split3(x):
    hi = _bf16_part(x)
    mid = _bf16_part(x - hi)
    return hi, mid, _bf16_part(x - hi - mid)


def _rope_chunk(xc, cos, sin_signed, half, lane):
    fwd = pltpu.roll(xc, LANES - half, axis=1)
    bwd = pltpu.roll(xc, half, axis=1)
    partner = jnp.where((lane % (2 * half)) < half, fwd, bwd)
    return xc * cos + partner * sin_signed


def _proj_kernel(x_ref, g_ref, wm_ref, wi_ref, c64_ref, s64_ref, c32_ref, s32_ref, bf_ref,
                 qk_ref, vt_ref, iq_ref, ik_ref, iw_ref, carry_ref, *, tm):
    t = pl.program_id(1)
    x = x_ref[0]
    h = _rms(x, g_ref[...])
    pm = jnp.dot(h.astype(BF16), wm_ref[...], preferred_element_type=F32)
    h_hi, h_lo = _split2(h)
    pi = jnp.dot(jnp.concatenate([h_hi, h_lo, h_hi], axis=1).astype(BF16), wi_ref[...],
                 preferred_element_type=F32)
    lane = lax.broadcasted_iota(I32, (1, LANES), 1)
    lo_half = lane < HEAD_DIM
    c64, s64, c32, s32 = c64_ref[0], s64_ref[0], c32_ref[0], s32_ref[0]

    misc = pi[:, 512:512 + LANES]
    z = misc + bf_ref[...]
    logf = jnp.minimum(z, 0.0) - jnp.log1p(jnp.exp(-jnp.abs(z)))
    logf = jnp.where((lane >= MISC_F) & (lane < MISC_F + 4), logf, 0.0)
    tri = (lax.broadcasted_iota(I32, (tm, tm), 1) <= lax.broadcasted_iota(I32, (tm, tm), 0)).astype(BF16)

    @pl.when(t == 0)
    def _():
        carry_ref[...] = jnp.zeros_like(carry_ref)

    pieces = jnp.dot(tri, jnp.concatenate(_split3(logf), axis=1).astype(BF16), preferred_element_type=F32)
    cum = pieces[:, 0:LANES] + pieces[:, LANES:2 * LANES] + pieces[:, 2 * LANES:3 * LANES] + carry_ref[...]
    carry_ref[...] = cum[tm - 1:tm, :]
    nhi, nmid, nlo = _split3(-LOG2E * cum)
    gate_bias = jnp.where(lane < 4, pltpu.roll(nhi, LANES - MISC_F, axis=1),
                          jnp.where(lane < 8, pltpu.roll(nmid, LANES - MISC_F + 4, axis=1),
                                    jnp.where(lane < 12, pltpu.roll(nlo, LANES - MISC_F + 8, axis=1),
                                              jnp.zeros_like(nlo))))

    for gi, name in enumerate(QK_GROUPS):
        for c in range(GROUP // LANES):
            lo = gi * GROUP + c * LANES
            v = pm[:, lo:lo + LANES]
            if name in ROPE64_GROUPS:
                v = _rope_chunk(v, c64, s64, HEAD_DIM // 2, lane)
            elif name in ROPE32_GROUPS:
                v = _rope_chunk(v, c32, s32, DIFF_DIM // 2, lane)
            if name in Q_SCALE:
                v = v * Q_SCALE[name]
            if name == "b_k":
                out = (QK_BLK[name] + c) * GROUP
                qk_ref[0, :, out:out + LANES] = v.astype(BF16)
                qk_ref[0, :, out + LANES:out + GROUP] = gate_bias.astype(BF16)
            else:
                out = QK_BLK[name] * GROUP + c * LANES
                qk_ref[0, :, out:out + LANES] = v.astype(BF16)

    vt_ref[0, 0] = pm[:, V_BASE:V_BASE + len(V_GROUPS) * GROUP].T.astype(BF16)

    def cat_q(q):
        hi, lo = _split2(q)
        return jnp.where(lo_half, hi, pltpu.roll(lo, HEAD_DIM, axis=1)), hi

    for c in range(512 // LANES):
        v = _rope_chunk(pi[:, c * LANES:(c + 1) * LANES], c64, s64, IDX_DIM // 2, lane)
        for r, q in enumerate((jnp.where(lo_half, v, 0.0), jnp.where(lo_half, pltpu.roll(v, HEAD_DIM, axis=1), 0.0))):
            a, b2 = cat_q(q)
            iq_ref[0, 2 * c + r, :, 0:LANES] = a.astype(BF16)
            iq_ref[0, 2 * c + r, :, LANES:IDX_CAT] = b2.astype(BF16)
    ik = jnp.where(lo_half, _rope_chunk(misc, c64, s64, IDX_DIM // 2, lane), 0.0)
    khi, klo = _split2(ik)
    ik_ref[0, :, 0:LANES] = jnp.where(lo_half, khi, pltpu.roll(khi, HEAD_DIM, axis=1)).astype(BF16)
    ik_ref[0, :, LANES:IDX_CAT] = klo.astype(BF16)
    iw_t = (misc * (IDX_HEADS ** -0.5 * IDX_DIM ** -0.5)).T
    iw_ref[0, 0] = iw_t[MISC_IW:MISC_IW + IDX_HEADS, :]


def _proj_call(x, g, w_main, w_idx, tabs, bf_row, *, tm):
    b, s, d = x.shape
    tok = lambda w: pl.BlockSpec((1, tm, w), lambda bi, ti: (bi, ti, 0))
    full = lambda a: pl.BlockSpec(a.shape, lambda bi, ti: (0,) * a.ndim)
    return pl.pallas_call(
        functools.partial(_proj_kernel, tm=tm),
        grid=(b, s // tm),
        in_specs=[tok(d), full(g), full(w_main), full(w_idx), tok(LANES), tok(LANES), tok(LANES), tok(LANES),
                  full(bf_row)],
        out_specs=[tok(QK_W),
                   pl.BlockSpec((1, 1, len(V_GROUPS) * GROUP, tm), lambda bi, ti: (bi, ti, 0, 0)),
                   pl.BlockSpec((1, IDX_HEADS, tm, IDX_CAT), lambda bi, ti: (bi, 0, ti, 0)),
                   tok(IDX_CAT),
                   pl.BlockSpec((1, 1, IDX_HEADS, tm), lambda bi, ti: (bi, ti, 0, 0))],
        out_shape=[jax.ShapeDtypeStruct((b, s, QK_W), BF16),
                   jax.ShapeDtypeStruct((b, s // tm, len(V_GROUPS) * GROUP, tm), BF16),
                   jax.ShapeDtypeStruct((b, IDX_HEADS, s, IDX_CAT), BF16),
                   jax.ShapeDtypeStruct((b, s, IDX_CAT), BF16),
                   jax.ShapeDtypeStruct((b, s // tm, IDX_HEADS, tm), F32)],
        scratch_shapes=[pltpu.VMEM((1, LANES), F32)],
        compiler_params=_cparams(("arbitrary", "arbitrary")),
        name="proj",
    )(x, g, w_main, w_idx, *tabs, bf_row)


def _softmax_step(s, m_ref, l_ref, idx):
    m_old = m_ref[idx]
    m_new = jnp.maximum(m_old, jnp.max(s, axis=0, keepdims=True))
    alpha = jnp.exp2(m_old - m_new)
    p = jnp.exp2(s - m_new)
    l_ref[idx] = alpha * l_ref[idx] + jnp.sum(p, axis=0, keepdims=True)
    m_ref[idx] = m_new
    return p, alpha


def _init_state(m_ref, l_ref, acc_ref):
    m_ref[...] = jnp.full(m_ref.shape, NEG_INF, F32)
    l_ref[...] = jnp.zeros(l_ref.shape, F32)
    acc_ref[...] = jnp.zeros(acc_ref.shape, F32)


def _causal_chunk(w, t):
    return lax.broadcasted_iota(I32, (w * t, t), 0) <= lax.broadcasted_iota(I32, (w * t, t), 1) + (w - 1) * t


def _head_mask(r):
    lane = lax.broadcasted_iota(I32, (1, LANES), 1)
    return (lane >= HEAD_DIM) if r else (lane < HEAD_DIM)


def _key_chunks(n, chunk_fn, width):
    def _body(c, carry):
        chunk_fn(width * c, width, False)
        return carry

    n_full = n // width
    lax.fori_loop(0, n_full, _body, 0)
    rest = n - n_full * width
    for k in range(width):
        @pl.when(rest == k)
        def _(k=k):
            chunk_fn(n - k, k + 1, True)


def _accumulate(acc_ref, a, r, alpha, vt_tiles, p, t):
    rows = slice(r * HEAD_DIM, (r + 1) * HEAD_DIM)
    pb = p.astype(BF16)
    pv = jnp.dot(vt_tiles[0], pb[0:t], preferred_element_type=F32)
    for i in range(1, len(vt_tiles)):
        pv = pv + jnp.dot(vt_tiles[i], pb[i * t:(i + 1) * t], preferred_element_type=F32)
    acc_ref[a, rows, :] = acc_ref[a, rows, :] * alpha + pv


def _normalised_t(acc_ref, a, l_ref, idx0, idx1):
    return jnp.concatenate([acc_ref[a, 0:HEAD_DIM, :] / l_ref[idx0],
                            acc_ref[a, HEAD_DIM:LANES, :] / l_ref[idx1]], axis=0)


def _dsa_kernel(iq_ref, iw_ref, ik_ref, q_ref, k_ref, vt_ref, o_ref,
                key_ref, pack_ref, m_ref, l_ref, acc_ref, carry_ref, iqt_ref, qzt_ref, *, t, topk):
    n = pl.program_id(1)
    for h in range(IDX_HEADS):
        iqt_ref[h] = _transpose_bf16(iq_ref[0, h])
    for h in range(4):
        qzt_ref[h] = _transpose_bf16(jnp.where(_head_mask(h % 2), q_ref[0, :, (h // 2) * LANES:(h // 2 + 1) * LANES],
                                               jnp.zeros((), BF16)))

    def score_chunk(j0, w, diag):
        start = pl.multiple_of(j0 * t, t)
        ik = ik_ref[0, pl.ds(start, w * t), :]
        sc = jnp.zeros((w * t, t), F32)
        for h in range(IDX_HEADS):
            d = _dot(ik, iqt_ref[h])
            sc = sc + iw_ref[0, 0, h:h + 1, :] * jnp.maximum(d, 0.0)
        if diag:
            sc = jnp.where(_causal_chunk(w, t), sc, NEG_INF)
        sc = jnp.where(sc == 0.0, 0.0, sc)
        bits = lax.bitcast_convert_type(sc, I32)
        key = bits ^ ((bits >> 31) & jnp.int32(0x7FFFFFFF))
        top = lax.shift_right_logical(key ^ jnp.int32(INT_MIN), 32 - FIELD_BITS)
        for i in range(w):
            key_ref[j0 + i] = key[i * t:(i + 1) * t]
            pack_ref[j0 + i] = (jnp.left_shift(top[i * t:i * t + t // 2], 16) | top[i * t + t // 2:(i + 1) * t]
                                | jnp.int32(FIELD_GUARDS))

    _key_chunks(n, score_chunk, CHUNK_TILES)

    def tree_sum(parts):
        while len(parts) > 1:
            parts = [parts[i] + parts[i + 1] for i in range(0, len(parts), 2)]
        return parts[0]

    def count(pred):
        def body(j, acc):
            c = jnp.where(pred(key_ref[j]), 1.0, 0.0)
            return acc + tree_sum([c[r * SUBLANES:(r + 1) * SUBLANES, :] for r in range(t // SUBLANES)])
        acc = lax.fori_loop(0, n + 1, body, jnp.zeros((SUBLANES, t), F32))
        return jnp.sum(acc, axis=0, keepdims=True)

    def count_top(cand):
        both = jnp.left_shift(cand, 16) | cand

        def body(j, acc):
            z = pack_ref[j] - both
            c = lax.shift_right_logical(z, FIELD_BITS) & jnp.int32(0x00010001)
            return acc + tree_sum([c[r * SUBLANES:(r + 1) * SUBLANES, :] for r in range(t // 2 // SUBLANES)])
        acc = lax.fori_loop(0, n + 1, body, jnp.zeros((SUBLANES, t), I32))
        per_lane = lax.shift_right_logical(acc, 16) + (acc & jnp.int32(0xFFFF))
        return jnp.sum(per_lane.astype(F32), axis=0, keepdims=True)

    def bisect(n_bits, low_bit, count_fn, carry):
        def body(i, carry):
            u, n_ge = carry
            cand = u | jnp.left_shift(jnp.int32(1), low_bit + n_bits - 1 - i)
            cnt = count_fn(cand)
            keep = cnt >= float(topk)
            return jnp.where(keep, cand, u), jnp.where(keep, cnt, n_ge)
        return lax.fori_loop(0, n_bits, body, carry)

    in_scope = ((n + 1) * t).astype(F32)
    top, n_ge = bisect(FIELD_BITS, 0, count_top, (jnp.zeros((1, t), I32), jnp.zeros((1, t), F32) + in_scope))
    u, n_ge = bisect(32 - FIELD_BITS, 0, lambda cand: count(lambda kt: kt >= (cand ^ jnp.int32(INT_MIN))),
                     (jnp.left_shift(top, 32 - FIELD_BITS), n_ge))
    thr = u ^ jnp.int32(INT_MIN)
    has_ties = jnp.max(n_ge) > float(topk)

    _init_state(m_ref, l_ref, acc_ref)

    def attend(ties):
        if ties:
            carry_ref[...] = jnp.zeros(carry_ref.shape, F32)
            need = float(topk) - count(lambda kt: kt > thr)
            lower = (lax.broadcasted_iota(I32, (t, t), 1) <= lax.broadcasted_iota(I32, (t, t), 0)).astype(BF16)

        def chunk(j0, w, diag):
            start = pl.multiple_of(j0 * t, t)
            ks = k_ref[0, pl.ds(start, w * t), :]
            scores = [_dot(ks[:, (h // 2) * LANES:(h // 2 + 1) * LANES], qzt_ref[h]) for h in range(4)]
            if ties:
                parts = []
                for i in range(w):
                    kt = key_ref[j0 + i]
                    eq = kt == thr
                    incl = jnp.dot(lower, jnp.where(eq, 1.0, 0.0).astype(BF16),
                                   preferred_element_type=F32) + carry_ref[...]
                    carry_ref[...] = incl[t - 1:t, :]
                    parts.append(jnp.where((kt > thr) | (eq & (incl <= need)), 1.0, 0.0))
                sel = (parts[0] if w == 1 else jnp.concatenate(parts, axis=0)) > 0.5
            else:
                kt = key_ref[j0] if w == 1 else jnp.concatenate([key_ref[j0 + i] for i in range(w)], axis=0)
                sel = kt >= thr
            if diag:
                sel = sel & _causal_chunk(w, t)
            for h in range(4):
                g, r = divmod(h, 2)
                p, alpha = _softmax_step(jnp.where(sel, scores[h], -jnp.inf), m_ref, l_ref, h)
                _accumulate(acc_ref, g, r, alpha,
                            [vt_ref[0, j0 + i, h * HEAD_DIM:(h + 1) * HEAD_DIM, :] for i in range(w)], p, t)

        _key_chunks(n, chunk, CHUNK_TILES)

    @pl.when(has_ties)
    def _():
        attend(True)

    @pl.when(jnp.logical_not(has_ties))
    def _():
        attend(False)

    for g in range(2):
        o_ref[0, :, g * LANES:(g + 1) * LANES] = _normalised_t(acc_ref, g, l_ref, 2 * g, 2 * g + 1).T.astype(BF16)


def _dsa_call(qk, vt, iq, ik, iw, *, t, topk):
    b, s, _ = qk.shape
    nt = s // t
    return pl.pallas_call(
        functools.partial(_dsa_kernel, t=t, topk=topk),
        grid=(b, nt),
        in_specs=[pl.BlockSpec((1, IDX_HEADS, t, IDX_CAT), lambda bi, ni: (bi, 0, ni, 0)),
                  pl.BlockSpec((1, 1, IDX_HEADS, t), lambda bi, ni: (bi, ni, 0, 0)),
                  pl.BlockSpec((1, s, IDX_CAT), lambda bi, ni: (bi, 0, 0)),
                  pl.BlockSpec((1, t, GROUP), lambda bi, ni: (bi, ni, QK_BLK["a_q"])),
                  pl.BlockSpec((1, s, GROUP), lambda bi, ni: (bi, 0, QK_BLK["a_k"])),
                  pl.BlockSpec((1, nt, GROUP, t), lambda bi, ni: (bi, 0, V_ROW_BLK["a_v"], 0))],
        out_specs=pl.BlockSpec((1, t, GROUP), lambda bi, ni: (bi, ni, 0)),
        out_shape=jax.ShapeDtypeStruct((b, s, GROUP), BF16),
        scratch_shapes=[pltpu.VMEM((nt, t, t), I32), pltpu.VMEM((nt, t // 2, t), I32),
                        pltpu.VMEM((4, 1, t), F32), pltpu.VMEM((4, 1, t), F32),
                        pltpu.VMEM((2, LANES, t), F32), pltpu.VMEM((1, t), F32),
                        pltpu.VMEM((IDX_HEADS, IDX_CAT, t), BF16), pltpu.VMEM((4, LANES, t), BF16)],
        compiler_params=_cparams(("arbitrary", "arbitrary")),
        name="dsa",
    )(iq, iw, ik, qk, qk, vt)


def _fox_kernel(q_ref, k0_ref, k1_ref, vt_ref, o_ref, m_ref, l_ref, acc_ref, qxt_ref, *, t):
    n = pl.program_id(1)
    _init_state(m_ref, l_ref, acc_ref)
    lane = lax.broadcasted_iota(I32, (1, LANES), 1)
    for h in range(4):
        qz = jnp.where(_head_mask(h % 2), q_ref[0, :, (h // 2) * LANES:(h // 2 + 1) * LANES], jnp.zeros((), BF16))
        ones = jnp.where((lane == h) | (lane == 4 + h) | (lane == 8 + h), 1.0, 0.0).astype(BF16)
        qxt_ref[h] = _transpose_bf16(jnp.concatenate([qz, jnp.broadcast_to(ones, (t, LANES))], axis=1))
    k_refs = (k0_ref, k1_ref)

    def chunk(j0, w, diag):
        start = pl.multiple_of(j0 * t, t)
        scores = [_dot(k_refs[h // 2][0, pl.ds(start, w * t), :], qxt_ref[h]) for h in range(4)]
        for h in range(4):
            g, r = divmod(h, 2)
            s = jnp.where(_causal_chunk(w, t), scores[h], -jnp.inf) if diag else scores[h]
            p, alpha = _softmax_step(s, m_ref, l_ref, h)
            _accumulate(acc_ref, g, r, alpha,
                        [vt_ref[0, j0 + i, h * HEAD_DIM:(h + 1) * HEAD_DIM, :] for i in range(w)], p, t)

    _key_chunks(n, chunk, CHUNK_TILES)
    for g in range(2):
        o_ref[0, :, g * LANES:(g + 1) * LANES] = _normalised_t(acc_ref, g, l_ref, 2 * g, 2 * g + 1).T.astype(BF16)


def _fox_call(qk, vt, *, t):
    b, s, _ = qk.shape
    nt = s // t
    return pl.pallas_call(
        functools.partial(_fox_kernel, t=t),
        grid=(b, nt),
        in_specs=[pl.BlockSpec((1, t, GROUP), lambda bi, ni: (bi, ni, QK_BLK["b_q"])),
                  pl.BlockSpec((1, s, GROUP), lambda bi, ni: (bi, 0, QK_BLK["b_k"])),
                  pl.BlockSpec((1, s, GROUP), lambda bi, ni: (bi, 0, QK_BLK["b_k"] + 1)),
                  pl.BlockSpec((1, nt, GROUP, t), lambda bi, ni: (bi, 0, V_ROW_BLK["b_v"], 0))],
        out_specs=pl.BlockSpec((1, t, GROUP), lambda bi, ni: (bi, ni, 0)),
        out_shape=jax.ShapeDtypeStruct((b, s, GROUP), BF16),
        scratch_shapes=[pltpu.VMEM((4, 1, t), F32), pltpu.VMEM((4, 1, t), F32), pltpu.VMEM((2, LANES, t), F32),
                        pltpu.VMEM((4, GROUP, t), BF16)],
        compiler_params=_cparams(("arbitrary", "arbitrary")),
        name="fox",
    )(qk, qk, qk, vt)


def _diff_kernel(lam_ref, q_ref, k_ref, vt_ref, gs_ref, o_ref, m_ref, l_ref, acc_ref, qzt_ref, *, t):
    n = pl.program_id(1)
    _init_state(m_ref, l_ref, acc_ref)
    lane = lax.broadcasted_iota(I32, (1, LANES), 1)
    for h in range(4):
        for mm in range(2):
            lo = (h % 2) * HEAD_DIM + mm * DIFF_DIM
            qzt_ref[2 * h + mm] = _transpose_bf16(
                jnp.where((lane >= lo) & (lane < lo + DIFF_DIM),
                          q_ref[0, :, (h // 2) * LANES:(h // 2 + 1) * LANES], jnp.zeros((), BF16)))

    def chunk(j0, w, diag):
        start = pl.multiple_of(j0 * t, t)
        ks = k_ref[0, pl.ds(start, w * t), :]
        scores = {(h, mm): _dot(ks[:, (h // 2) * LANES:(h // 2 + 1) * LANES], qzt_ref[2 * h + mm])
                  for h in range(4) for mm in range(2)}
        for h in range(4):
            g, r = divmod(h, 2)
            vts = [vt_ref[0, j0 + i, h * HEAD_DIM:(h + 1) * HEAD_DIM, :] for i in range(w)]
            for mm in range(2):
                s = jnp.where(_causal_chunk(w, t), scores[h, mm], -jnp.inf) if diag else scores[h, mm]
                p, alpha = _softmax_step(s, m_ref, l_ref, 2 * h + mm)
                _accumulate(acc_ref, 2 * mm + g, r, alpha, vts, p, t)

    _key_chunks(n, chunk, CHUNK_TILES_DIFF)
    lam = lam_ref[0]
    out_scale = lam_ref[1]
    for g in range(2):
        o1 = _normalised_t(acc_ref, g, l_ref, 4 * g, 4 * g + 2)
        o2 = _normalised_t(acc_ref, 2 + g, l_ref, 4 * g + 1, 4 * g + 3)
        o = o1 - lam * o2
        sq = o * o
        ms = jnp.concatenate(
            [jnp.broadcast_to(jnp.mean(sq[r * HEAD_DIM:(r + 1) * HEAD_DIM], axis=0, keepdims=True), (HEAD_DIM, t))
             for r in range(2)], axis=0)
        y = o * lax.rsqrt(ms + EPS) * gs_ref[...] * out_scale
        o_ref[0, :, g * LANES:(g + 1) * LANES] = y.T.astype(BF16)


def _diff_call(lam2, qk, vt, gs_full, *, t):
    b, s, _ = qk.shape
    nt = s // t
    return pl.pallas_call(
        functools.partial(_diff_kernel, t=t),
        grid=(b, nt),
        in_specs=[pl.BlockSpec(memory_space=pltpu.SMEM),
                  pl.BlockSpec((1, t, GROUP), lambda bi, ni: (bi, ni, QK_BLK["c_q"])),
                  pl.BlockSpec((1, s, GROUP), lambda bi, ni: (bi, 0, QK_BLK["c_k"])),
                  pl.BlockSpec((1, nt, GROUP, t), lambda bi, ni: (bi, 0, V_ROW_BLK["c_v"], 0)),
                  pl.BlockSpec((LANES, t), lambda bi, ni: (0, 0))],
        out_specs=pl.BlockSpec((1, t, GROUP), lambda bi, ni: (bi, ni, 0)),
        out_shape=jax.ShapeDtypeStruct((b, s, GROUP), BF16),
        scratch_shapes=[pltpu.VMEM((8, 1, t), F32), pltpu.VMEM((8, 1, t), F32), pltpu.VMEM((4, LANES, t), F32),
                        pltpu.VMEM((8, LANES, t), BF16)],
        compiler_params=_cparams(("arbitrary", "arbitrary")),
        name="diff",
    )(lam2, qk, qk, vt, gs_full)


def _swa_kernel(sink_ref, q_ref, kp_ref, kc_ref, vtp_ref, vtc_ref, o_ref, *, t):
    n = pl.program_id(1)
    prow = lax.broadcasted_iota(I32, (WINDOW, t), 0)
    pcol = lax.broadcasted_iota(I32, (WINDOW, t), 1)
    mask_prev = (prow > pcol) & (pcol + jnp.where(n > 0, 0, t) < WINDOW)
    crow = lax.broadcasted_iota(I32, (t, t), 0)
    ccol = lax.broadcasted_iota(I32, (t, t), 1)
    mask_cur = (crow <= ccol) & (crow > ccol - WINDOW)
    vtp = vtp_ref[0, 0][:, t - WINDOW:t]
    vtc = vtc_ref[0, 0]
    for g in range(2):
        outs = []
        for r in range(2):
            h = 2 * g + r
            qz = jnp.where(_head_mask(r), q_ref[0, :, g * LANES:(g + 1) * LANES], jnp.zeros((), BF16))
            sp = jnp.where(mask_prev, _dot_nt(kp_ref[0, :, g * LANES:(g + 1) * LANES], qz), -jnp.inf)
            sc = jnp.where(mask_cur, _dot_nt(kc_ref[0, :, g * LANES:(g + 1) * LANES], qz), -jnp.inf)
            sink = sink_ref[h] * LOG2E
            m = jnp.maximum(jnp.maximum(jnp.max(sp, axis=0, keepdims=True),
                                        jnp.max(sc, axis=0, keepdims=True)), sink)
            pp = jnp.exp2(sp - m)
            pc = jnp.exp2(sc - m)
            den = jnp.sum(pp, axis=0, keepdims=True) + jnp.sum(pc, axis=0, keepdims=True) + jnp.exp2(sink - m)
            rows = slice(h * HEAD_DIM, (h + 1) * HEAD_DIM)
            o = (jnp.dot(vtp[rows, :], pp.astype(BF16), preferred_element_type=F32)
                 + jnp.dot(vtc[rows, :], pc.astype(BF16), preferred_element_type=F32))
            outs.append(o / den)
        o_ref[0, :, g * LANES:(g + 1) * LANES] = jnp.concatenate(outs, axis=0).T.astype(BF16)


def _swa_call(sinks, qk, vt, *, t):
    b, s, _ = qk.shape
    nt = s // t
    per = t // WINDOW
    return pl.pallas_call(
        functools.partial(_swa_kernel, t=t),
        grid=(b, nt),
        in_specs=[pl.BlockSpec(memory_space=pltpu.SMEM),
                  pl.BlockSpec((1, t, GROUP), lambda bi, ni: (bi, ni, QK_BLK["d_q"])),
                  pl.BlockSpec((1, WINDOW, GROUP), lambda bi, ni: (bi, jnp.maximum(ni * per - 1, 0), QK_BLK["d_k"])),
                  pl.BlockSpec((1, t, GROUP), lambda bi, ni: (bi, ni, QK_BLK["d_k"])),
                  pl.BlockSpec((1, 1, GROUP, t), lambda bi, ni: (bi, jnp.maximum(ni - 1, 0), V_ROW_BLK["d_v"], 0)),
                  pl.BlockSpec((1, 1, GROUP, t), lambda bi, ni: (bi, ni, V_ROW_BLK["d_v"], 0))],
        out_specs=pl.BlockSpec((1, t, GROUP), lambda bi, ni: (bi, ni, 0)),
        out_shape=jax.ShapeDtypeStruct((b, s, GROUP), BF16),
        compiler_params=_cparams(("arbitrary", "arbitrary")),
        name="swa",
    )(sinks, qk, qk, qk, vt, vt)


def _post_kernel(oa_ref, ob_ref, oc_ref, od_ref, x_ref, p_ref, wo_ref, gmix_ref, gpre_ref, wu_ref, wd_ref, gpost_ref,
                 wg_ref, wp_ref, y_ref, *, chunk):
    acc = jnp.zeros(x_ref.shape[1:], F32)
    for i, o_ref in enumerate((oa_ref, ob_ref, oc_ref, od_ref)):
        acc = acc + jnp.dot(o_ref[0], wo_ref[i * GROUP:(i + 1) * GROUP, :], preferred_element_type=F32)
    x = x_ref[0] + _rms(acc, gmix_ref[...])
    h = _rms(x, gpre_ref[...]).astype(BF16)
    acc = jnp.zeros(x.shape, F32)
    for c in range(D_FF // chunk):
        u = jnp.dot(h, wu_ref[:, c * chunk:(c + 1) * chunk], preferred_element_type=F32)
        u = jnp.square(jnp.maximum(u, 0.0)).astype(BF16)
        acc = acc + jnp.dot(u, wd_ref[c * chunk:(c + 1) * chunk, :], preferred_element_type=F32)
    x = x + _rms(acc, gpost_ref[...])
    gate = jax.nn.sigmoid(jnp.dot(x.astype(BF16), wg_ref[...], preferred_element_type=F32))
    emb = jnp.dot(p_ref[0].astype(BF16), wp_ref[...], preferred_element_type=F32)
    y_ref[0] = x + gate * emb


def _post_call(tok_inputs, full_inputs, *, tm):
    b, s, d = tok_inputs[4].shape
    tok = lambda a: pl.BlockSpec((1, tm, a.shape[2]), lambda bi, ti: (bi, ti, 0))
    full = lambda a: pl.BlockSpec(a.shape, lambda bi, ti: (0,) * a.ndim, pipeline_mode=pl.Buffered(1))
    return pl.pallas_call(
        functools.partial(_post_kernel, chunk=1024),
        grid=(b, s // tm),
        in_specs=[tok(a) for a in tok_inputs] + [full(a) for a in full_inputs],
        out_specs=pl.BlockSpec((1, tm, d), lambda bi, ti: (bi, ti, 0)),
        out_shape=jax.ShapeDtypeStruct((b, s, d), F32),
        compiler_params=_cparams(("arbitrary", "arbitrary")),
        name="post",
    )(*tok_inputs, *full_inputs)


def _dup_kv(w):
    return jnp.concatenate([w[:, :HEAD_DIM], w[:, :HEAD_DIM], w[:, HEAD_DIM:], w[:, HEAD_DIM:]], axis=1)


def _prep_w_in(w):
    sec = lambda n: w[:, _SEC[n][0]:_SEC[n][0] + _SEC[n][1]]
    cols = [_dup_kv(sec(n)) if n in ("d_k", "d_v") else sec(n) for n in QK_GROUPS + V_GROUPS]
    w_main = jnp.concatenate(cols, axis=1).astype(BF16)
    pad = jnp.zeros((w.shape[0], LANES - IDX_DIM - IDX_HEADS - 4), w.dtype)
    w_idx = jnp.concatenate([sec("iq"), sec("ik"), sec("iw"), sec("b_f"), pad], axis=1)
    w_hi = w_idx.astype(BF16)
    w_lo = (w_idx - w_hi.astype(F32)).astype(BF16)
    return w_main, jnp.concatenate([w_hi, w_hi, w_lo], axis=0)


def _rope_tables(positions):
    pos = positions.astype(F32)[..., None]
    lane = jnp.arange(LANES)

    def tabs(dim):
        half = dim // 2
        inv_freq = ROPE_THETA ** (-jnp.arange(half, dtype=F32) / half)
        ang = pos * inv_freq
        sign = jnp.where((lane % dim) < half, -1.0, 1.0).astype(F32)
        reps = (1, 1, LANES // half)
        return jnp.tile(jnp.cos(ang), reps), jnp.tile(jnp.sin(ang), reps) * sign

    c64, s64 = tabs(HEAD_DIM)
    c32, s32 = tabs(DIFF_DIM)
    return c64, s64, c32, s32


def kernel(x, p, positions, w_in, b_forget, lambda_q1, lambda_k1, lambda_q2, lambda_k2, diff_subln, sinks,
           w_out, norm_pre_mix, norm_post_mix, norm_pre_mlp, norm_post_mlp, w_mlp_up, w_mlp_down,
           w_ple_proj, w_ple_gate):
    b, s, d = x.shape
    depth = w_in.shape[0]
    t = min(256, s)
    topk = min(TOPK_MAX, s // 4)
    tabs = _rope_tables(positions)
    row = lambda v: v.reshape(1, -1).astype(F32)

    for i in range(depth):
        lam_init = 0.8 - 0.6 * math.exp(-0.3 * i)
        w_main, w_idx = _prep_w_in(w_in[i])
        bf_row = jnp.zeros((1, LANES), F32).at[0, MISC_F:MISC_F + 4].set(b_forget[i])
        qk, vt, iq, ik, iw = _proj_call(x, row(norm_pre_mix[i]), w_main, w_idx, tabs, bf_row, tm=t)

        o_a = _dsa_call(qk, vt, iq, ik, iw, t=t, topk=topk)
        o_b = _fox_call(qk, vt, t=t)
        lam = (jnp.exp(jnp.sum(lambda_q1[i] * lambda_k1[i])) - jnp.exp(jnp.sum(lambda_q2[i] * lambda_k2[i]))
               + lam_init)
        lam2 = jnp.stack([lam, jnp.asarray(1.0 - lam_init, F32)]).astype(F32)
        gs_full = jnp.broadcast_to(jnp.concatenate([diff_subln[i], diff_subln[i]]).astype(F32)[:, None], (LANES, t))
        o_c = _diff_call(lam2, qk, vt, gs_full, t=t)
        o_d = _swa_call(sinks[i].astype(F32), qk, vt, t=t)

        x = _post_call([o_a, o_b, o_c, o_d, x, p[i]],
                       [w_out[i].astype(BF16), row(norm_post_mix[i]), row(norm_pre_mlp[i]),
                        w_mlp_up[i].astype(BF16), w_mlp_down[i].astype(BF16), row(norm_post_mlp[i]),
                        w_ple_gate[i].astype(BF16), w_ple_proj[i].astype(BF16)], tm=t)
    return x
```

```python
import functools
import math

import jax
import jax.numpy as jnp
from jax import lax
from jax.experimental import pallas as pl
from jax.experimental.pallas import tpu as pltpu

F32, BF16, I32 = jnp.float32, jnp.bfloat16, jnp.int32

D_MODEL = 1024
HEAD_DIM = 64
DIFF_DIM = 32
IDX_HEADS = 8
IDX_DIM = 64
TOPK_MAX = 256
WINDOW = 128
D_FF = 4 * D_MODEL
D_PLE = 256
ROPE_THETA = 10000.0
EPS = 1e-6
NEG_INF = -1e30
LANES = 128
SUBLANES = 8
GROUP = 256
INT_MIN = -2147483648
FIELD_BITS = 15
FIELD_GUARDS = -2147450880

_SEC = {}
_o = 0
for _name, _w in (("a_q", 256), ("a_k", 256), ("a_v", 256), ("iq", 512), ("ik", 64), ("iw", 8),
                  ("b_q", 256), ("b_k", 256), ("b_v", 256), ("b_f", 4),
                  ("c_q", 256), ("c_k", 256), ("c_v", 256), ("d_q", 256), ("d_k", 128), ("d_v", 128)):
    _SEC[_name] = (_o, _w)
    _o += _w
D_IN = _o

QK_GROUPS = ("a_q", "a_k", "b_q", "b_k", "c_q", "c_k", "d_q", "d_k")
V_GROUPS = ("a_v", "b_v", "c_v", "d_v")
MAIN_W = GROUP * (len(QK_GROUPS) + len(V_GROUPS))
V_BASE = GROUP * len(QK_GROUPS)
QK_BLK = {"a_q": 0, "a_k": 1, "b_q": 2, "b_k": 3, "c_q": 5, "c_k": 6, "d_q": 7, "d_k": 8}
QK_W = GROUP * 9
V_ROW_BLK = {n: i for i, n in enumerate(V_GROUPS)}
ROPE64_GROUPS = ("a_q", "a_k", "d_q", "d_k")
ROPE32_GROUPS = ("c_q", "c_k")
LOG2E = math.log2(math.e)
Q_SCALE = {"a_q": HEAD_DIM ** -0.5 * LOG2E, "b_q": HEAD_DIM ** -0.5 * LOG2E, "c_q": DIFF_DIM ** -0.5 * LOG2E,
           "d_q": HEAD_DIM ** -0.5 * LOG2E}
IDX_W = 512 + LANES
MISC_IW = 64
MISC_F = 72
IDX_CAT = 256

CHUNK_TILES = 4
CHUNK_TILES_DIFF = 2
VMEM_LIMIT = 56 * 1024 * 1024
RESIDENT = pl.Buffered(1)


def _cparams(sem):
    return pltpu.CompilerParams(dimension_semantics=sem, vmem_limit_bytes=VMEM_LIMIT)


def _rms(x, g):
    return x * lax.rsqrt(jnp.mean(x * x, axis=-1, keepdims=True) + EPS) * g


def _dot_nt(a, b):
    return lax.dot_general(a, b, (((1,), (1,)), ((), ())), preferred_element_type=F32)


def _dot(a, b):
    return jnp.dot(a, b, preferred_element_type=F32)


def _transpose_bf16(a):
    return a.astype(F32).T.astype(BF16)


def _bf16_part(x):
    return x.astype(BF16).astype(F32)


def _split2(x):
    hi = _bf16_part(x)
    return hi, _bf16_part(x - hi)


def _split3(x):
    hi = _bf16_part(x)
    mid = _bf16_part(x - hi)
    return hi, mid, _bf16_part(x - hi - mid)


def _rope_chunk(xc, cos, sin_signed, half, lane):
    fwd = pltpu.roll(xc, LANES - half, axis=1)
    bwd = pltpu.roll(xc, half, axis=1)
    partner = jnp.where((lane % (2 * half)) < half, fwd, bwd)
    return xc * cos + partner * sin_signed


def _proj_kernel(x_ref, g_ref, wm_ref, wi_ref, c64_ref, s64_ref, c32_ref, s32_ref, bf_ref,
                 qk_ref, vt_ref, iq_ref, ik_ref, iw_ref, carry_ref, *, tm):
    t = pl.program_id(1)
    x = x_ref[0]
    h = _rms(x, g_ref[...])
    pm = jnp.dot(h.astype(BF16), wm_ref[...], preferred_element_type=F32)
    h_hi, h_lo = _split2(h)
    pi = jnp.dot(jnp.concatenate([h_hi, h_lo, h_hi], axis=1).astype(BF16), wi_ref[...],
                 preferred_element_type=F32)
    lane = lax.broadcasted_iota(I32, (1, LANES), 1)
    lo_half = lane < HEAD_DIM
    c64, s64, c32, s32 = c64_ref[0], s64_ref[0], c32_ref[0], s32_ref[0]

    misc = pi[:, 512:512 + LANES]
    z = misc + bf_ref[...]
    logf = jnp.minimum(z, 0.0) - jnp.log1p(jnp.exp(-jnp.abs(z)))
    logf = jnp.where((lane >= MISC_F) & (lane < MISC_F + 4), logf, 0.0)
    tri = (lax.broadcasted_iota(I32, (tm, tm), 1) <= lax.broadcasted_iota(I32, (tm, tm), 0)).astype(BF16)

    @pl.when(t == 0)
    def _():
        carry_ref[...] = jnp.zeros_like(carry_ref)

    pieces = jnp.dot(tri, jnp.concatenate(_split3(logf), axis=1).astype(BF16), preferred_element_type=F32)
    cum = pieces[:, 0:LANES] + pieces[:, LANES:2 * LANES] + pieces[:, 2 * LANES:3 * LANES] + carry_ref[...]
    carry_ref[...] = cum[tm - 1:tm, :]
    nhi, nmid, nlo = _split3(-LOG2E * cum)
    gate_bias = jnp.where(lane < 4, pltpu.roll(nhi, LANES - MISC_F, axis=1),
                          jnp.where(lane < 8, pltpu.roll(nmid, LANES - MISC_F + 4, axis=1),
                                    jnp.where(lane < 12, pltpu.roll(nlo, LANES - MISC_F + 8, axis=1),
                                              jnp.zeros_like(nlo))))

    for gi, name in enumerate(QK_GROUPS):
        for c in range(GROUP // LANES):
            lo = gi * GROUP + c * LANES
            v = pm[:, lo:lo + LANES]
            if name in ROPE64_GROUPS:
                v = _rope_chunk(v, c64, s64, HEAD_DIM // 2, lane)
            elif name in ROPE32_GROUPS:
                v = _rope_chunk(v, c32, s32, DIFF_DIM // 2, lane)
            if name in Q_SCALE:
                v = v * Q_SCALE[name]
            if name == "b_k":
                out = (QK_BLK[name] + c) * GROUP
                qk_ref[0, :, out:out + LANES] = v.astype(BF16)
                qk_ref[0, :, out + LANES:out + GROUP] = gate_bias.astype(BF16)
            else:
                out = QK_BLK[name] * GROUP + c * LANES
                qk_ref[0, :, out:out + LANES] = v.astype(BF16)

    vt_ref[0, 0] = pm[:, V_BASE:V_BASE + len(V_GROUPS) * GROUP].T.astype(BF16)

    def cat_q(q):
        hi, lo = _split2(q)
        return jnp.where(lo_half, hi, pltpu.roll(lo, HEAD_DIM, axis=1)), hi

    for c in range(512 // LANES):
        v = _rope_chunk(pi[:, c * LANES:(c + 1) * LANES], c64, s64, IDX_DIM // 2, lane)
        for r, q in enumerate((jnp.where(lo_half, v, 0.0), jnp.where(lo_half, pltpu.roll(v, HEAD_DIM, axis=1), 0.0))):
            a, b2 = cat_q(q)
            iq_ref[0, 2 * c + r, :, 0:LANES] = a.astype(BF16)
            iq_ref[0, 2 * c + r, :, LANES:IDX_CAT] = b2.astype(BF16)
    ik = jnp.where(lo_half, _rope_chunk(misc, c64, s64, IDX_DIM // 2, lane), 0.0)
    khi, klo = _split2(ik)
    ik_ref[0, :, 0:LANES] = jnp.where(lo_half, khi, pltpu.roll(khi, HEAD_DIM, axis=1)).astype(BF16)
    ik_ref[0, :, LANES:IDX_CAT] = klo.astype(BF16)
    iw_t = (misc * (IDX_HEADS ** -0.5 * IDX_DIM ** -0.5)).T
    iw_ref[0, 0] = iw_t[MISC_IW:MISC_IW + IDX_HEADS, :]


def _proj_call(x, g, w_main, w_idx, tabs, bf_row, *, tm):
    b, s, d = x.shape
    tok = lambda w: pl.BlockSpec((1, tm, w), lambda bi, ti: (bi, ti, 0))
    full = lambda a: pl.BlockSpec(a.shape, lambda bi, ti: (0,) * a.ndim)
    return pl.pallas_call(
        functools.partial(_proj_kernel, tm=tm),
        grid=(b, s // tm),
        in_specs=[tok(d), full(g), full(w_main), full(w_idx), tok(LANES), tok(LANES), tok(LANES), tok(LANES),
                  full(bf_row)],
        out_specs=[tok(QK_W),
                   pl.BlockSpec((1, 1, len(V_GROUPS) * GROUP, tm), lambda bi, ti: (bi, ti, 0, 0)),
                   pl.BlockSpec((1, IDX_HEADS, tm, IDX_CAT), lambda bi, ti: (bi, 0, ti, 0)),
                   tok(IDX_CAT),
                   pl.BlockSpec((1, 1, IDX_HEADS, tm), lambda bi, ti: (bi, ti, 0, 0))],
        out_shape=[jax.ShapeDtypeStruct((b, s, QK_W), BF16),
                   jax.ShapeDtypeStruct((b, s // tm, len(V_GROUPS) * GROUP, tm), BF16),
                   jax.ShapeDtypeStruct((b, IDX_HEADS, s, IDX_CAT), BF16),
                   jax.ShapeDtypeStruct((b, s, IDX_CAT), BF16),
                   jax.ShapeDtypeStruct((b, s // tm, IDX_HEADS, tm), F32)],
        scratch_shapes=[pltpu.VMEM((1, LANES), F32)],
        compiler_params=_cparams(("arbitrary", "arbitrary")),
        name="proj",
    )(x, g, w_main, w_idx, *tabs, bf_row)


def _softmax_step(s, m_ref, l_ref, idx):
    m_old = m_ref[idx]
    m_new = jnp.maximum(m_old, jnp.max(s, axis=0, keepdims=True))
    alpha = jnp.exp2(m_old - m_new)
    p = jnp.exp2(s - m_new)
    l_ref[idx] = alpha * l_ref[idx] + jnp.sum(p, axis=0, keepdims=True)
    m_ref[idx] = m_new
    return p, alpha


def _init_state(m_ref, l_ref, acc_ref):
    m_ref[...] = jnp.full(m_ref.shape, NEG_INF, F32)
    l_ref[...] = jnp.zeros(l_ref.shape, F32)
    acc_ref[...] = jnp.zeros(acc_ref.shape, F32)


def _causal_chunk(w, t):
    return lax.broadcasted_iota(I32, (w * t, t), 0) <= lax.broadcasted_iota(I32, (w * t, t), 1) + (w - 1) * t


def _head_mask(r):
    lane = lax.broadcasted_iota(I32, (1, LANES), 1)
    return (lane >= HEAD_DIM) if r else (lane < HEAD_DIM)


def _key_chunks(n, chunk_fn, width):
    def _body(c, carry):
        chunk_fn(width * c, width, False)
        return carry

    n_full = n // width
    lax.fori_loop(0, n_full, _body, 0)
    rest = n - n_full * width
    for k in range(width):
        @pl.when(rest == k)
        def _(k=k):
            chunk_fn(n - k, k + 1, True)


def _pipelined_key_chunks(n, scores_fn, rest_fn, width):
    n_full = n // width
    rest = n - n_full * width
    odd = n_full & 1

    @pl.when(odd == 1)
    def _():
        scores_fn(0, 1)
        rest_fn(0, width, False, 1)

    scores_fn(odd * width, 0)

    def _body(pair, carry):
        j0 = (odd + 2 * pair) * width
        scores_fn(j0 + width, 1)
        rest_fn(j0, width, False, 0)
        scores_fn(j0 + 2 * width, 0)
        rest_fn(j0 + width, width, False, 1)
        return carry

    lax.fori_loop(0, (n_full - odd) >> 1, _body, 0)
    for k in range(width):
        @pl.when(rest == k)
        def _(k=k):
            rest_fn(n - k, k + 1, True, 0)


def _accumulate(acc_ref, a, r, alpha, vt_tiles, p, t):
    rows = slice(r * HEAD_DIM, (r + 1) * HEAD_DIM)
    pb = p.astype(BF16)
    pv = jnp.dot(vt_tiles[0], pb[0:t], preferred_element_type=F32)
    for i in range(1, len(vt_tiles)):
        pv = pv + jnp.dot(vt_tiles[i], pb[i * t:(i + 1) * t], preferred_element_type=F32)
    acc_ref[a, rows, :] = acc_ref[a, rows, :] * alpha + pv


def _normalised_t(acc_ref, a, l_ref, idx0, idx1):
    return jnp.concatenate([acc_ref[a, 0:HEAD_DIM, :] / l_ref[idx0],
                            acc_ref[a, HEAD_DIM:LANES, :] / l_ref[idx1]], axis=0)


def _dsa_kernel(iq_ref, iw_ref, ik_ref, q_ref, k_ref, vt_ref, o_ref,
                key_ref, pack_ref, m_ref, l_ref, acc_ref, carry_ref, iqt_ref, qzt_ref, s_ref, *, t, topk):
    n = pl.program_id(1)
    for h in range(IDX_HEADS):
        iqt_ref[h] = _transpose_bf16(iq_ref[0, h])
    for h in range(4):
        qzt_ref[h] = _transpose_bf16(jnp.where(_head_mask(h % 2), q_ref[0, :, (h // 2) * LANES:(h // 2 + 1) * LANES],
                                               jnp.zeros((), BF16)))

    def score_chunk(j0, w, diag):
        start = pl.multiple_of(j0 * t, t)
        ik = ik_ref[0, pl.ds(start, w * t), :]
        sc = jnp.zeros((w * t, t), F32)
        for h in range(IDX_HEADS):
            d = _dot(ik, iqt_ref[h])
            sc = sc + iw_ref[0, 0, h:h + 1, :] * jnp.maximum(d, 0.0)
        if diag:
            sc = jnp.where(_causal_chunk(w, t), sc, NEG_INF)
        sc = jnp.where(sc == 0.0, 0.0, sc)
        bits = lax.bitcast_convert_type(sc, I32)
        key = bits ^ ((bits >> 31) & jnp.int32(0x7FFFFFFF))
        top = lax.shift_right_logical(key ^ jnp.int32(INT_MIN), 32 - FIELD_BITS)
        for i in range(w):
            key_ref[j0 + i] = key[i * t:(i + 1) * t]
            pack_ref[j0 + i] = (jnp.left_shift(top[i * t:i * t + t // 2], 16) | top[i * t + t // 2:(i + 1) * t]
                                | jnp.int32(FIELD_GUARDS))

    _key_chunks(n, score_chunk, CHUNK_TILES)

    def tree_sum(parts):
        while len(parts) > 1:
            parts = [parts[i] + parts[i + 1] for i in range(0, len(parts), 2)]
        return parts[0]

    def count(pred):
        def body(j, acc):
            c = jnp.where(pred(key_ref[j]), 1.0, 0.0)
            return acc + tree_sum([c[r * SUBLANES:(r + 1) * SUBLANES, :] for r in range(t // SUBLANES)])
        acc = lax.fori_loop(0, n + 1, body, jnp.zeros((SUBLANES, t), F32))
        return jnp.sum(acc, axis=0, keepdims=True)

    def count_top(cand):
        both = jnp.left_shift(cand, 16) | cand

        def body(j, acc):
            z = pack_ref[j] - both
            c = lax.shift_right_logical(z, FIELD_BITS) & jnp.int32(0x00010001)
            return acc + tree_sum([c[r * SUBLANES:(r + 1) * SUBLANES, :] for r in range(t // 2 // SUBLANES)])
        acc = lax.fori_loop(0, n + 1, body, jnp.zeros((SUBLANES, t), I32))
        per_lane = lax.shift_right_logical(acc, 16) + (acc & jnp.int32(0xFFFF))
        return jnp.sum(per_lane.astype(F32), axis=0, keepdims=True)

    def bisect(n_bits, low_bit, count_fn, carry):
        def body(i, carry):
            u, n_ge = carry
            cand = u | jnp.left_shift(jnp.int32(1), low_bit + n_bits - 1 - i)
            cnt = count_fn(cand)
            keep = cnt >= float(topk)
            return jnp.where(keep, cand, u), jnp.where(keep, cnt, n_ge)
        return lax.fori_loop(0, n_bits, body, carry)

    in_scope = ((n + 1) * t).astype(F32)
    top, n_ge = bisect(FIELD_BITS, 0, count_top, (jnp.zeros((1, t), I32), jnp.zeros((1, t), F32) + in_scope))
    u, n_ge = bisect(32 - FIELD_BITS, 0, lambda cand: count(lambda kt: kt >= (cand ^ jnp.int32(INT_MIN))),
                     (jnp.left_shift(top, 32 - FIELD_BITS), n_ge))
    thr = u ^ jnp.int32(INT_MIN)
    has_ties = jnp.max(n_ge) > float(topk)

    _init_state(m_ref, l_ref, acc_ref)

    def attend(ties):
        if ties:
            carry_ref[...] = jnp.zeros(carry_ref.shape, F32)
            need = float(topk) - count(lambda kt: kt > thr)
            lower = (lax.broadcasted_iota(I32, (t, t), 1) <= lax.broadcasted_iota(I32, (t, t), 0)).astype(BF16)

        def scores(j0, buf):
            start = pl.multiple_of(j0 * t, t)
            ks = k_ref[0, pl.ds(start, CHUNK_TILES * t), :]
            for h in range(4):
                s_ref[buf, h] = _dot(ks[:, (h // 2) * LANES:(h // 2 + 1) * LANES], qzt_ref[h])

        def rest(j0, w, diag, buf):
            if ties:
                parts = []
                for i in range(w):
                    kt = key_ref[j0 + i]
                    eq = kt == thr
                    incl = jnp.dot(lower, jnp.where(eq, 1.0, 0.0).astype(BF16),
                                   preferred_element_type=F32) + carry_ref[...]
                    carry_ref[...] = incl[t - 1:t, :]
                    parts.append(jnp.where((kt > thr) | (eq & (incl <= need)), 1.0, 0.0))
                sel = (parts[0] if w == 1 else jnp.concatenate(parts, axis=0)) > 0.5
            else:
                kt = key_ref[j0] if w == 1 else jnp.concatenate([key_ref[j0 + i] for i in range(w)], axis=0)
                sel = kt >= thr
            if diag:
                sel = sel & _causal_chunk(w, t)
            for h in range(4):
                g, r = divmod(h, 2)
                p, alpha = _softmax_step(jnp.where(sel, s_ref[buf, h, 0:w * t, :], -jnp.inf), m_ref, l_ref, h)
                _accumulate(acc_ref, g, r, alpha,
                            [vt_ref[0, j0 + i, h * HEAD_DIM:(h + 1) * HEAD_DIM, :] for i in range(w)], p, t)

        _pipelined_key_chunks(n, scores, rest, CHUNK_TILES)

    @pl.when(has_ties)
    def _():
        attend(True)

    @pl.when(jnp.logical_not(has_ties))
    def _():
        attend(False)

    for g in range(2):
        o_ref[0, :, g * LANES:(g + 1) * LANES] = _normalised_t(acc_ref, g, l_ref, 2 * g, 2 * g + 1).T.astype(BF16)


def _dsa_call(qk, vt, iq, ik, iw, *, t, topk):
    b, s, _ = qk.shape
    nt = s // t
    return pl.pallas_call(
        functools.partial(_dsa_kernel, t=t, topk=topk),
        grid=(b, nt),
        in_specs=[pl.BlockSpec((1, IDX_HEADS, t, IDX_CAT), lambda bi, ni: (bi, 0, ni, 0)),
                  pl.BlockSpec((1, 1, IDX_HEADS, t), lambda bi, ni: (bi, ni, 0, 0)),
                  pl.BlockSpec((1, s, IDX_CAT), lambda bi, ni: (bi, 0, 0), pipeline_mode=RESIDENT),
                  pl.BlockSpec((1, t, GROUP), lambda bi, ni: (bi, ni, QK_BLK["a_q"])),
                  pl.BlockSpec((1, s, GROUP), lambda bi, ni: (bi, 0, QK_BLK["a_k"]), pipeline_mode=RESIDENT),
                  pl.BlockSpec((1, nt, GROUP, t), lambda bi, ni: (bi, 0, V_ROW_BLK["a_v"], 0), pipeline_mode=RESIDENT)],
        out_specs=pl.BlockSpec((1, t, GROUP), lambda bi, ni: (bi, ni, 0)),
        out_shape=jax.ShapeDtypeStruct((b, s, GROUP), BF16),
        scratch_shapes=[pltpu.VMEM((nt, t, t), I32), pltpu.VMEM((nt, t // 2, t), I32),
                        pltpu.VMEM((4, 1, t), F32), pltpu.VMEM((4, 1, t), F32),
                        pltpu.VMEM((2, LANES, t), F32), pltpu.VMEM((1, t), F32),
                        pltpu.VMEM((IDX_HEADS, IDX_CAT, t), BF16), pltpu.VMEM((4, LANES, t), BF16),
                        pltpu.VMEM((2, 4, CHUNK_TILES * t, t), F32)],
        compiler_params=_cparams(("arbitrary", "arbitrary")),
        name="dsa",
    )(iq, iw, ik, qk, qk, vt)


def _fox_kernel(q_ref, k0_ref, k1_ref, vt_ref, o_ref, m_ref, l_ref, acc_ref, qxt_ref, s_ref, *, t):
    n = pl.program_id(1)
    _init_state(m_ref, l_ref, acc_ref)
    lane = lax.broadcasted_iota(I32, (1, LANES), 1)
    for h in range(4):
        qz = jnp.where(_head_mask(h % 2), q_ref[0, :, (h // 2) * LANES:(h // 2 + 1) * LANES], jnp.zeros((), BF16))
        ones = jnp.where((lane == h) | (lane == 4 + h) | (lane == 8 + h), 1.0, 0.0).astype(BF16)
        qxt_ref[h] = _transpose_bf16(jnp.concatenate([qz, jnp.broadcast_to(ones, (t, LANES))], axis=1))
    k_refs = (k0_ref, k1_ref)

    def scores(j0, buf):
        start = pl.multiple_of(j0 * t, t)
        for h in range(4):
            s_ref[buf, h] = _dot(k_refs[h // 2][0, pl.ds(start, CHUNK_TILES * t), :], qxt_ref[h])

    def rest(j0, w, diag, buf):
        for h in range(4):
            g, r = divmod(h, 2)
            s = s_ref[buf, h, 0:w * t, :]
            if diag:
                s = jnp.where(_causal_chunk(w, t), s, -jnp.inf)
            p, alpha = _softmax_step(s, m_ref, l_ref, h)
            _accumulate(acc_ref, g, r, alpha,
                        [vt_ref[0, j0 + i, h * HEAD_DIM:(h + 1) * HEAD_DIM, :] for i in range(w)], p, t)

    _pipelined_key_chunks(n, scores, rest, CHUNK_TILES)
    for g in range(2):
        o_ref[0, :, g * LANES:(g + 1) * LANES] = _normalised_t(acc_ref, g, l_ref, 2 * g, 2 * g + 1).T.astype(BF16)


def _fox_call(qk, vt, *, t):
    b, s, _ = qk.shape
    nt = s // t
    return pl.pallas_call(
        functools.partial(_fox_kernel, t=t),
        grid=(b, nt),
        in_specs=[pl.BlockSpec((1, t, GROUP), lambda bi, ni: (bi, ni, QK_BLK["b_q"])),
                  pl.BlockSpec((1, s, GROUP), lambda bi, ni: (bi, 0, QK_BLK["b_k"]), pipeline_mode=RESIDENT),
                  pl.BlockSpec((1, s, GROUP), lambda bi, ni: (bi, 0, QK_BLK["b_k"] + 1), pipeline_mode=RESIDENT),
                  pl.BlockSpec((1, nt, GROUP, t), lambda bi, ni: (bi, 0, V_ROW_BLK["b_v"], 0), pipeline_mode=RESIDENT)],
        out_specs=pl.BlockSpec((1, t, GROUP), lambda bi, ni: (bi, ni, 0)),
        out_shape=jax.ShapeDtypeStruct((b, s, GROUP), BF16),
        scratch_shapes=[pltpu.VMEM((4, 1, t), F32), pltpu.VMEM((4, 1, t), F32), pltpu.VMEM((2, LANES, t), F32),
                        pltpu.VMEM((4, GROUP, t), BF16), pltpu.VMEM((2, 4, CHUNK_TILES * t, t), F32)],
        compiler_params=_cparams(("arbitrary", "arbitrary")),
        name="fox",
    )(qk, qk, qk, vt)


def _diff_kernel(lam_ref, q_ref, k_ref, vt_ref, gs_ref, o_ref, m_ref, l_ref, acc_ref, qzt_ref, *, t):
    n = pl.program_id(1)
    _init_state(m_ref, l_ref, acc_ref)
    lane = lax.broadcasted_iota(I32, (1, LANES), 1)
    for h in range(4):
        for mm in range(2):
            lo = (h % 2) * HEAD_DIM + mm * DIFF_DIM
            qzt_ref[2 * h + mm] = _transpose_bf16(
                jnp.where((lane >= lo) & (lane < lo + DIFF_DIM),
                          q_ref[0, :, (h // 2) * LANES:(h // 2 + 1) * LANES], jnp.zeros((), BF16)))

    def chunk(j0, w, diag):
        start = pl.multiple_of(j0 * t, t)
        ks = k_ref[0, pl.ds(start, w * t), :]
        scores = {(h, mm): _dot(ks[:, (h // 2) * LANES:(h // 2 + 1) * LANES], qzt_ref[2 * h + mm])
                  for h in range(4) for mm in range(2)}
        for h in range(4):
            g, r = divmod(h, 2)
            vts = [vt_ref[0, j0 + i, h * HEAD_DIM:(h + 1) * HEAD_DIM, :] for i in range(w)]
            for mm in range(2):
                s = jnp.where(_causal_chunk(w, t), scores[h, mm], -jnp.inf) if diag else scores[h, mm]
                p, alpha = _softmax_step(s, m_ref, l_ref, 2 * h + mm)
                _accumulate(acc_ref, 2 * mm + g, r, alpha, vts, p, t)

    _key_chunks(n, chunk, CHUNK_TILES_DIFF)
    lam = lam_ref[0]
    out_scale = lam_ref[1]
    for g in range(2):
        o1 = _normalised_t(acc_ref, g, l_ref, 4 * g, 4 * g + 2)
        o2 = _normalised_t(acc_ref, 2 + g, l_ref, 4 * g + 1, 4 * g + 3)
        o = o1 - lam * o2
        sq = o * o
        ms = jnp.concatenate(
            [jnp.broadcast_to(jnp.mean(sq[r * HEAD_DIM:(r + 1) * HEAD_DIM], axis=0, keepdims=True), (HEAD_DIM, t))
             for r in range(2)], axis=0)
        y = o * lax.rsqrt(ms + EPS) * gs_ref[...] * out_scale
        o_ref[0, :, g * LANES:(g + 1) * LANES] = y.T.astype(BF16)


def _diff_call(lam2, qk, vt, gs_full, *, t):
    b, s, _ = qk.shape
    nt = s // t
    return pl.pallas_call(
        functools.partial(_diff_kernel, t=t),
        grid=(b, nt),
        in_specs=[pl.BlockSpec(memory_space=pltpu.SMEM),
                  pl.BlockSpec((1, t, GROUP), lambda bi, ni: (bi, ni, QK_BLK["c_q"])),
                  pl.BlockSpec((1, s, GROUP), lambda bi, ni: (bi, 0, QK_BLK["c_k"]), pipeline_mode=RESIDENT),
                  pl.BlockSpec((1, nt, GROUP, t), lambda bi, ni: (bi, 0, V_ROW_BLK["c_v"], 0), pipeline_mode=RESIDENT),
                  pl.BlockSpec((LANES, t), lambda bi, ni: (0, 0))],
        out_specs=pl.BlockSpec((1, t, GROUP), lambda bi, ni: (bi, ni, 0)),
        out_shape=jax.ShapeDtypeStruct((b, s, GROUP), BF16),
        scratch_shapes=[pltpu.VMEM((8, 1, t), F32), pltpu.VMEM((8, 1, t), F32), pltpu.VMEM((4, LANES, t), F32),
                        pltpu.VMEM((8, LANES, t), BF16)],
        compiler_params=_cparams(("arbitrary", "arbitrary")),
        name="diff",
    )(lam2, qk, qk, vt, gs_full)


def _swa_kernel(sink_ref, q_ref, kp_ref, kc_ref, vtp_ref, vtc_ref, o_ref, *, t):
    n = pl.program_id(1)
    prow = lax.broadcasted_iota(I32, (WINDOW, t), 0)
    pcol = lax.broadcasted_iota(I32, (WINDOW, t), 1)
    mask_prev = (prow > pcol) & (pcol + jnp.where(n > 0, 0, t) < WINDOW)
    crow = lax.broadcasted_iota(I32, (t, t), 0)
    ccol = lax.broadcasted_iota(I32, (t, t), 1)
    mask_cur = (crow <= ccol) & (crow > ccol - WINDOW)
    vtp = vtp_ref[0, 0][:, t - WINDOW:t]
    vtc = vtc_ref[0, 0]
    for g in range(2):
        outs = []
        for r in range(2):
            h = 2 * g + r
            qz = jnp.where(_head_mask(r), q_ref[0, :, g * LANES:(g + 1) * LANES], jnp.zeros((), BF16))
            sp = jnp.where(mask_prev, _dot_nt(kp_ref[0, :, g * LANES:(g + 1) * LANES], qz), -jnp.inf)
            sc = jnp.where(mask_cur, _dot_nt(kc_ref[0, :, g * LANES:(g + 1) * LANES], qz), -jnp.inf)
            sink = sink_ref[h] * LOG2E
            m = jnp.maximum(jnp.maximum(jnp.max(sp, axis=0, keepdims=True),
                                        jnp.max(sc, axis=0, keepdims=True)), sink)
            pp = jnp.exp2(sp - m)
            pc = jnp.exp2(sc - m)
            den = jnp.sum(pp, axis=0, keepdims=True) + jnp.sum(pc, axis=0, keepdims=True) + jnp.exp2(sink - m)
            rows = slice(h * HEAD_DIM, (h + 1) * HEAD_DIM)
            o = (jnp.dot(vtp[rows, :], pp.astype(BF16), preferred_element_type=F32)
                 + jnp.dot(vtc[rows, :], pc.astype(BF16), preferred_element_type=F32))
            outs.append(o / den)
        o_ref[0, :, g * LANES:(g + 1) * LANES] = jnp.concatenate(outs, axis=0).T.astype(BF16)


def _swa_call(sinks, qk, vt, *, t):
    b, s, _ = qk.shape
    nt = s // t
    per = t // WINDOW
    return pl.pallas_call(
        functools.partial(_swa_kernel, t=t),
        grid=(b, nt),
        in_specs=[pl.BlockSpec(memory_space=pltpu.SMEM),
                  pl.BlockSpec((1, t, GROUP), lambda bi, ni: (bi, ni, QK_BLK["d_q"])),
                  pl.BlockSpec((1, WINDOW, GROUP), lambda bi, ni: (bi, jnp.maximum(ni * per - 1, 0), QK_BLK["d_k"])),
                  pl.BlockSpec((1, t, GROUP), lambda bi, ni: (bi, ni, QK_BLK["d_k"])),
                  pl.BlockSpec((1, 1, GROUP, t), lambda bi, ni: (bi, jnp.maximum(ni - 1, 0), V_ROW_BLK["d_v"], 0)),
                  pl.BlockSpec((1, 1, GROUP, t), lambda bi, ni: (bi, ni, V_ROW_BLK["d_v"], 0))],
        out_specs=pl.BlockSpec((1, t, GROUP), lambda bi, ni: (bi, ni, 0)),
        out_shape=jax.ShapeDtypeStruct((b, s, GROUP), BF16),
        compiler_params=_cparams(("arbitrary", "arbitrary")),
        name="swa",
    )(sinks, qk, qk, qk, vt, vt)


def _post_kernel(oa_ref, ob_ref, oc_ref, od_ref, x_ref, p_ref, wo_ref, gmix_ref, gpre_ref, wu_ref, wd_ref, gpost_ref,
                 wg_ref, wp_ref, y_ref, *, chunk):
    acc = jnp.zeros(x_ref.shape[1:], F32)
    for i, o_ref in enumerate((oa_ref, ob_ref, oc_ref, od_ref)):
        acc = acc + jnp.dot(o_ref[0], wo_ref[i * GROUP:(i + 1) * GROUP, :], preferred_element_type=F32)
    x = x_ref[0] + _rms(acc, gmix_ref[...])
    h = _rms(x, gpre_ref[...]).astype(BF16)
    acc = jnp.zeros(x.shape, F32)
    for c in range(D_FF // chunk):
        u = jnp.dot(h, wu_ref[:, c * chunk:(c + 1) * chunk], preferred_element_type=F32)
        u = jnp.square(jnp.maximum(u, 0.0)).astype(BF16)
        acc = acc + jnp.dot(u, wd_ref[c * chunk:(c + 1) * chunk, :], preferred_element_type=F32)
    x = x + _rms(acc, gpost_ref[...])
    gate = jax.nn.sigmoid(jnp.dot(x.astype(BF16), wg_ref[...], preferred_element_type=F32))
    emb = jnp.dot(p_ref[0].astype(BF16), wp_ref[...], preferred_element_type=F32)
    y_ref[0] = x + gate * emb


def _post_call(tok_inputs, full_inputs, *, tm):
    b, s, d = tok_inputs[4].shape
    tok = lambda a: pl.BlockSpec((1, tm, a.shape[2]), lambda bi, ti: (bi, ti, 0))
    full = lambda a: pl.BlockSpec(a.shape, lambda bi, ti: (0,) * a.ndim, pipeline_mode=pl.Buffered(1))
    return pl.pallas_call(
        functools.partial(_post_kernel, chunk=1024),
        grid=(b, s // tm),
        in_specs=[tok(a) for a in tok_inputs] + [full(a) for a in full_inputs],
        out_specs=pl.BlockSpec((1, tm, d), lambda bi, ti: (bi, ti, 0)),
        out_shape=jax.ShapeDtypeStruct((b, s, d), F32),
        compiler_params=_cparams(("arbitrary", "arbitrary")),
        name="post",
    )(*tok_inputs, *full_inputs)


def _dup_kv(w):
    return jnp.concatenate([w[:, :HEAD_DIM], w[:, :HEAD_DIM], w[:, HEAD_DIM:], w[:, HEAD_DIM:]], axis=1)


def _prep_w_in(w):
    sec = lambda n: w[:, _SEC[n][0]:_SEC[n][0] + _SEC[n][1]]
    cols = [_dup_kv(sec(n)) if n in ("d_k", "d_v") else sec(n) for n in QK_GROUPS + V_GROUPS]
    w_main = jnp.concatenate(cols, axis=1).astype(BF16)
    pad = jnp.zeros((w.shape[0], LANES - IDX_DIM - IDX_HEADS - 4), w.dtype)
    w_idx = jnp.concatenate([sec("iq"), sec("ik"), sec("iw"), sec("b_f"), pad], axis=1)
    w_hi = w_idx.astype(BF16)
    w_lo = (w_idx - w_hi.astype(F32)).astype(BF16)
    return w_main, jnp.concatenate([w_hi, w_hi, w_lo], axis=0)


def _rope_tables(positions):
    pos = positions.astype(F32)[..., None]
    lane = jnp.arange(LANES)

    def tabs(dim):
        half = dim // 2
        inv_freq = ROPE_THETA ** (-jnp.arange(half, dtype=F32) / half)
        ang = pos * inv_freq
        sign = jnp.where((lane % dim) < half, -1.0, 1.0).astype(F32)
        reps = (1, 1, LANES // half)
        return jnp.tile(jnp.cos(ang), reps), jnp.tile(jnp.sin(ang), reps) * sign

    c64, s64 = tabs(HEAD_DIM)
    c32, s32 = tabs(DIFF_DIM)
    return c64, s64, c32, s32


def kernel(x, p, positions, w_in, b_forget, lambda_q1, lambda_k1, lambda_q2, lambda_k2, diff_subln, sinks,
           w_out, norm_pre_mix, norm_post_mix, norm_pre_mlp, norm_post_mlp, w_mlp_up, w_mlp_down,
           w_ple_proj, w_ple_gate):
    b, s, d = x.shape
    depth = w_in.shape[0]
    t = min(256, s)
    assert s % (CHUNK_TILES * t) == 0, "sequence length must be a multiple of the attention key chunk"
    topk = min(TOPK_MAX, s // 4)
    tabs = _rope_tables(positions)
    row = lambda v: v.reshape(1, -1).astype(F32)

    for i in range(depth):
        lam_init = 0.8 - 0.6 * math.exp(-0.3 * i)
        w_main, w_idx = _prep_w_in(w_in[i])
        bf_row = jnp.zeros((1, LANES), F32).at[0, MISC_F:MISC_F + 4].set(b_forget[i])
        qk, vt, iq, ik, iw = _proj_call(x, row(norm_pre_mix[i]), w_main, w_idx, tabs, bf_row, tm=t)

        o_a = _dsa_call(qk, vt, iq, ik, iw, t=t, topk=topk)
        o_b = _fox_call(qk, vt, t=t)
        lam = (jnp.exp(jnp.sum(lambda_q1[i] * lambda_k1[i])) - jnp.exp(jnp.sum(lambda_q2[i] * lambda_k2[i]))
               + lam_init)
        lam2 = jnp.stack([lam, jnp.asarray(1.0 - lam_init, F32)]).astype(F32)
        gs_full = jnp.broadcast_to(jnp.concatenate([diff_subln[i], diff_subln[i]]).astype(F32)[:, None], (LANES, t))
        o_c = _diff_call(lam2, qk, vt, gs_full, t=t)
        o_d = _swa_call(sinks[i].astype(F32), qk, vt, t=t)

        x = _post_call([o_a, o_b, o_c, o_d, x, p[i]],
                       [w_out[i].astype(BF16), row(norm_post_mix[i]), row(norm_pre_mlp[i]),
                        w_mlp_up[i].astype(BF16), w_mlp_down[i].astype(BF16), row(norm_post_mlp[i]),
                        w_ple_gate[i].astype(BF16), w_ple_proj[i].astype(BF16)], tm=t)
    return x
```

```python
import functools
import math

import jax
import jax.numpy as jnp
from jax import lax
from jax.experimental import pallas as pl
from jax.experimental.pallas import tpu as pltpu

F32, BF16, I32 = jnp.float32, jnp.bfloat16, jnp.int32

D_MODEL = 1024
HEAD_DIM = 64
DIFF_DIM = 32
IDX_HEADS = 8
IDX_DIM = 64
TOPK_MAX = 256
WINDOW = 128
D_FF = 4 * D_MODEL
D_PLE = 256
ROPE_THETA = 10000.0
EPS = 1e-6
NEG_INF = -1e30
LANES = 128
SUBLANES = 8
GROUP = 256
INT_MIN = -2147483648
FIELD_BITS = 15
FIELD_GUARDS = -2147450880

_SEC = {}
_o = 0
for _name, _w in (("a_q", 256), ("a_k", 256), ("a_v", 256), ("iq", 512), ("ik", 64), ("iw", 8),
                  ("b_q", 256), ("b_k", 256), ("b_v", 256), ("b_f", 4),
                  ("c_q", 256), ("c_k", 256), ("c_v", 256), ("d_q", 256), ("d_k", 128), ("d_v", 128)):
    _SEC[_name] = (_o, _w)
    _o += _w
D_IN = _o

QK_GROUPS = ("a_q", "a_k", "b_q", "b_k", "c_q", "c_k", "d_q", "d_k")
V_GROUPS = ("a_v", "b_v", "c_v", "d_v")
MAIN_W = GROUP * (len(QK_GROUPS) + len(V_GROUPS))
V_BASE = GROUP * len(QK_GROUPS)
QK_BLK = {"a_q": 0, "a_k": 1, "b_q": 2, "b_k": 3, "c_q": 5, "c_k": 6, "d_q": 7, "d_k": 8}
QK_W = GROUP * 9
V_ROW_BLK = {n: i for i, n in enumerate(V_GROUPS)}
ROPE64_GROUPS = ("a_q", "a_k", "d_q", "d_k")
ROPE32_GROUPS = ("c_q", "c_k")
LOG2E = math.log2(math.e)
Q_SCALE = {"a_q": HEAD_DIM ** -0.5 * LOG2E, "b_q": HEAD_DIM ** -0.5 * LOG2E, "c_q": DIFF_DIM ** -0.5 * LOG2E,
           "d_q": HEAD_DIM ** -0.5 * LOG2E}
IDX_W = 512 + LANES
MISC_IW = 64
MISC_F = 72
IDX_CAT = 256

CHUNK_TILES = 4
CHUNK_TILES_DIFF = 2
VMEM_LIMIT = 56 * 1024 * 1024
RESIDENT = pl.Buffered(1)


def _cparams(sem):
    return pltpu.CompilerParams(dimension_semantics=sem, vmem_limit_bytes=VMEM_LIMIT)


def _rms(x, g):
    return x * lax.rsqrt(jnp.mean(x * x, axis=-1, keepdims=True) + EPS) * g


def _dot_nt(a, b):
    return lax.dot_general(a, b, (((1,), (1,)), ((), ())), preferred_element_type=F32)


def _dot(a, b):
    return jnp.dot(a, b, preferred_element_type=F32)


def _transpose_bf16(a):
    return a.astype(F32).T.astype(BF16)


def _bf16_part(x):
    return x.astype(BF16).astype(F32)


def _split2(x):
    hi = _bf16_part(x)
    return hi, _bf16_part(x - hi)


def _split3(x):
    hi = _bf16_part(x)
    mid = _bf16_part(x - hi)
    return hi, mid, _bf16_part(x - hi - mid)


def _rope_chunk(xc, cos, sin_signed, half, lane):
    fwd = pltpu.roll(xc, LANES - half, axis=1)
    bwd = pltpu.roll(xc, half, axis=1)
    partner = jnp.where((lane % (2 * half)) < half, fwd, bwd)
    return xc * cos + partner * sin_signed


def _proj_kernel(x_ref, g_ref, wm_ref, wi_ref, c64_ref, s64_ref, c32_ref, s32_ref, bf_ref,
                 qk_ref, vt_ref, iq_ref, ik_ref, iw_ref, carry_ref, *, tm):
    t = pl.program_id(1)
    x = x_ref[0]
    h = _rms(x, g_ref[...])
    pm = jnp.dot(h.astype(BF16), wm_ref[...], preferred_element_type=F32)
    h_hi, h_lo = _split2(h)
    pi = jnp.dot(jnp.concatenate([h_hi, h_lo, h_hi], axis=1).astype(BF16), wi_ref[...],
                 preferred_element_type=F32)
    lane = lax.broadcasted_iota(I32, (1, LANES), 1)
    lo_half = lane < HEAD_DIM
    c64, s64, c32, s32 = c64_ref[0], s64_ref[0], c32_ref[0], s32_ref[0]

    misc = pi[:, 512:512 + LANES]
    z = misc + bf_ref[...]
    logf = jnp.minimum(z, 0.0) - jnp.log1p(jnp.exp(-jnp.abs(z)))
    logf = jnp.where((lane >= MISC_F) & (lane < MISC_F + 4), logf, 0.0)
    tri = (lax.broadcasted_iota(I32, (tm, tm), 1) <= lax.broadcasted_iota(I32, (tm, tm), 0)).astype(BF16)

    @pl.when(t == 0)
    def _():
        carry_ref[...] = jnp.zeros_like(carry_ref)

    pieces = jnp.dot(tri, jnp.concatenate(_split3(logf), axis=1).astype(BF16), preferred_element_type=F32)
    cum = pieces[:, 0:LANES] + pieces[:, LANES:2 * LANES] + pieces[:, 2 * LANES:3 * LANES] + carry_ref[...]
    carry_ref[...] = cum[tm - 1:tm, :]
    nhi, nmid, nlo = _split3(-LOG2E * cum)
    gate_bias = jnp.where(lane < 4, pltpu.roll(nhi, LANES - MISC_F, axis=1),
                          jnp.where(lane < 8, pltpu.roll(nmid, LANES - MISC_F + 4, axis=1),
                                    jnp.where(lane < 12, pltpu.roll(nlo, LANES - MISC_F + 8, axis=1),
                                              jnp.zeros_like(nlo))))

    for gi, name in enumerate(QK_GROUPS):
        for c in range(GROUP // LANES):
            lo = gi * GROUP + c * LANES
            v = pm[:, lo:lo + LANES]
            if name in ROPE64_GROUPS:
                v = _rope_chunk(v, c64, s64, HEAD_DIM // 2, lane)
            elif name in ROPE32_GROUPS:
                v = _rope_chunk(v, c32, s32, DIFF_DIM // 2, lane)
            if name in Q_SCALE:
                v = v * Q_SCALE[name]
            if name == "b_k":
                out = (QK_BLK[name] + c) * GROUP
                qk_ref[0, :, out:out + LANES] = v.astype(BF16)
                qk_ref[0, :, out + LANES:out + GROUP] = gate_bias.astype(BF16)
            else:
                out = QK_BLK[name] * GROUP + c * LANES
                qk_ref[0, :, out:out + LANES] = v.astype(BF16)

    vt_ref[0, 0] = pm[:, V_BASE:V_BASE + len(V_GROUPS) * GROUP].T.astype(BF16)

    def cat_q(q):
        hi, lo = _split2(q)
        return jnp.where(lo_half, hi, pltpu.roll(lo, HEAD_DIM, axis=1)), hi

    for c in range(512 // LANES):
        v = _rope_chunk(pi[:, c * LANES:(c + 1) * LANES], c64, s64, IDX_DIM // 2, lane)
        for r, q in enumerate((jnp.where(lo_half, v, 0.0), jnp.where(lo_half, pltpu.roll(v, HEAD_DIM, axis=1), 0.0))):
            a, b2 = cat_q(q)
            iq_ref[0, 2 * c + r, :, 0:LANES] = a.astype(BF16)
            iq_ref[0, 2 * c + r, :, LANES:IDX_CAT] = b2.astype(BF16)
    ik = jnp.where(lo_half, _rope_chunk(misc, c64, s64, IDX_DIM // 2, lane), 0.0)
    khi, klo = _split2(ik)
    ik_ref[0, :, 0:LANES] = jnp.where(lo_half, khi, pltpu.roll(khi, HEAD_DIM, axis=1)).astype(BF16)
    ik_ref[0, :, LANES:IDX_CAT] = klo.astype(BF16)
    iw_t = (misc * (IDX_HEADS ** -0.5 * IDX_DIM ** -0.5)).T
    iw_ref[0, 0] = iw_t[MISC_IW:MISC_IW + IDX_HEADS, :]


def _proj_call(x, g, w_main, w_idx, tabs, bf_row, *, tm):
    b, s, d = x.shape
    tok = lambda w: pl.BlockSpec((1, tm, w), lambda bi, ti: (bi, ti, 0))
    full = lambda a: pl.BlockSpec(a.shape, lambda bi, ti: (0,) * a.ndim)
    return pl.pallas_call(
        functools.partial(_proj_kernel, tm=tm),
        grid=(b, s // tm),
        in_specs=[tok(d), full(g), full(w_main), full(w_idx), tok(LANES), tok(LANES), tok(LANES), tok(LANES),
                  full(bf_row)],
        out_specs=[tok(QK_W),
                   pl.BlockSpec((1, 1, len(V_GROUPS) * GROUP, tm), lambda bi, ti: (bi, ti, 0, 0)),
                   pl.BlockSpec((1, IDX_HEADS, tm, IDX_CAT), lambda bi, ti: (bi, 0, ti, 0)),
                   tok(IDX_CAT),
                   pl.BlockSpec((1, 1, IDX_HEADS, tm), lambda bi, ti: (bi, ti, 0, 0))],
        out_shape=[jax.ShapeDtypeStruct((b, s, QK_W), BF16),
                   jax.ShapeDtypeStruct((b, s // tm, len(V_GROUPS) * GROUP, tm), BF16),
                   jax.ShapeDtypeStruct((b, IDX_HEADS, s, IDX_CAT), BF16),
                   jax.ShapeDtypeStruct((b, s, IDX_CAT), BF16),
                   jax.ShapeDtypeStruct((b, s // tm, IDX_HEADS, tm), F32)],
        scratch_shapes=[pltpu.VMEM((1, LANES), F32)],
        compiler_params=_cparams(("arbitrary", "arbitrary")),
        name="proj",
    )(x, g, w_main, w_idx, *tabs, bf_row)


def _softmax_step(s, m_ref, l_ref, idx):
    m_old = m_ref[idx]
    m_new = jnp.maximum(m_old, jnp.max(s, axis=0, keepdims=True))
    alpha = jnp.exp2(m_old - m_new)
    p = jnp.exp2(s - m_new)
    l_ref[idx] = alpha * l_ref[idx] + jnp.sum(p, axis=0, keepdims=True)
    m_ref[idx] = m_new
    return p, alpha


def _init_state(m_ref, l_ref, acc_ref):
    m_ref[...] = jnp.full(m_ref.shape, NEG_INF, F32)
    l_ref[...] = jnp.zeros(l_ref.shape, F32)
    acc_ref[...] = jnp.zeros(acc_ref.shape, F32)


def _causal_chunk(w, t):
    return lax.broadcasted_iota(I32, (w * t, t), 0) <= lax.broadcasted_iota(I32, (w * t, t), 1) + (w - 1) * t


def _head_mask(r):
    lane = lax.broadcasted_iota(I32, (1, LANES), 1)
    return (lane >= HEAD_DIM) if r else (lane < HEAD_DIM)


def _key_chunks(n, chunk_fn, width):
    def _body(c, carry):
        chunk_fn(width * c, width, False)
        return carry

    n_full = n // width
    lax.fori_loop(0, n_full, _body, 0)
    rest = n - n_full * width
    for k in range(width):
        @pl.when(rest == k)
        def _(k=k):
            chunk_fn(n - k, k + 1, True)


def _pipelined_key_chunks(n, scores_fn, rest_fn, width):
    n_full = n // width
    rest = n - n_full * width
    odd = n_full & 1

    @pl.when(odd == 1)
    def _():
        scores_fn(0, 1)
        scores_fn(width, 0)
        rest_fn(0, width, False, 1)

    @pl.when(odd == 0)
    def _():
        scores_fn(0, 0)

    def _body(pair, carry):
        j0 = (odd + 2 * pair) * width
        scores_fn(j0 + width, 1)
        rest_fn(j0, width, False, 0)
        scores_fn(j0 + 2 * width, 0)
        rest_fn(j0 + width, width, False, 1)
        return carry

    lax.fori_loop(0, (n_full - odd) >> 1, _body, 0)
    for k in range(width):
        @pl.when(rest == k)
        def _(k=k):
            rest_fn(n - k, k + 1, True, 0)


def _accumulate(acc_ref, a, r, alpha, vt_tiles, p, t):
    rows = slice(r * HEAD_DIM, (r + 1) * HEAD_DIM)
    pb = p.astype(BF16)
    pv = jnp.dot(vt_tiles[0], pb[0:t], preferred_element_type=F32)
    for i in range(1, len(vt_tiles)):
        pv = pv + jnp.dot(vt_tiles[i], pb[i * t:(i + 1) * t], preferred_element_type=F32)
    acc_ref[a, rows, :] = acc_ref[a, rows, :] * alpha + pv


def _normalised_t(acc_ref, a, l_ref, idx0, idx1):
    return jnp.concatenate([acc_ref[a, 0:HEAD_DIM, :] / l_ref[idx0],
                            acc_ref[a, HEAD_DIM:LANES, :] / l_ref[idx1]], axis=0)


def _dsa_kernel(iq_ref, iw_ref, ik_ref, q_ref, k_ref, vt_ref, o_ref,
                key_ref, pack_ref, m_ref, l_ref, acc_ref, carry_ref, iqt_ref, qzt_ref, s_ref, *, t, topk):
    n = pl.program_id(1)
    for h in range(IDX_HEADS):
        iqt_ref[h] = _transpose_bf16(iq_ref[0, h])
    for h in range(4):
        qzt_ref[h] = _transpose_bf16(jnp.where(_head_mask(h % 2), q_ref[0, :, (h // 2) * LANES:(h // 2 + 1) * LANES],
                                               jnp.zeros((), BF16)))

    def score_chunk(j0, w, diag):
        start = pl.multiple_of(j0 * t, t)
        ik = ik_ref[0, pl.ds(start, w * t), :]
        sc = jnp.zeros((w * t, t), F32)
        for h in range(IDX_HEADS):
            d = _dot(ik, iqt_ref[h])
            sc = sc + iw_ref[0, 0, h:h + 1, :] * jnp.maximum(d, 0.0)
        if diag:
            sc = jnp.where(_causal_chunk(w, t), sc, NEG_INF)
        sc = jnp.where(sc == 0.0, 0.0, sc)
        bits = lax.bitcast_convert_type(sc, I32)
        key = bits ^ ((bits >> 31) & jnp.int32(0x7FFFFFFF))
        top = lax.shift_right_logical(key ^ jnp.int32(INT_MIN), 32 - FIELD_BITS)
        for i in range(w):
            key_ref[j0 + i] = key[i * t:(i + 1) * t]
            pack_ref[j0 + i] = (jnp.left_shift(top[i * t:i * t + t // 2], 16) | top[i * t + t // 2:(i + 1) * t]
                                | jnp.int32(FIELD_GUARDS))

    _key_chunks(n, score_chunk, CHUNK_TILES)

    def tree_sum(parts):
        while len(parts) > 1:
            parts = [parts[i] + parts[i + 1] for i in range(0, len(parts), 2)]
        return parts[0]

    def count(pred):
        def body(j, acc):
            c = jnp.where(pred(key_ref[j]), 1.0, 0.0)
            return acc + tree_sum([c[r * SUBLANES:(r + 1) * SUBLANES, :] for r in range(t // SUBLANES)])
        acc = lax.fori_loop(0, n + 1, body, jnp.zeros((SUBLANES, t), F32))
        return jnp.sum(acc, axis=0, keepdims=True)

    def count_top(cand):
        both = jnp.left_shift(cand, 16) | cand

        def body(j, acc):
            z = pack_ref[j] - both
            c = lax.shift_right_logical(z, FIELD_BITS) & jnp.int32(0x00010001)
            return acc + tree_sum([c[r * SUBLANES:(r + 1) * SUBLANES, :] for r in range(t // 2 // SUBLANES)])
        acc = lax.fori_loop(0, n + 1, body, jnp.zeros((SUBLANES, t), I32))
        per_lane = lax.shift_right_logical(acc, 16) + (acc & jnp.int32(0xFFFF))
        return jnp.sum(per_lane.astype(F32), axis=0, keepdims=True)

    def bisect(n_bits, low_bit, count_fn, carry):
        def body(i, carry):
            u, n_ge = carry
            cand = u | jnp.left_shift(jnp.int32(1), low_bit + n_bits - 1 - i)
            cnt = count_fn(cand)
            keep = cnt >= float(topk)
            return jnp.where(keep, cand, u), jnp.where(keep, cnt, n_ge)
        return lax.fori_loop(0, n_bits, body, carry)

    in_scope = ((n + 1) * t).astype(F32)
    top, n_ge = bisect(FIELD_BITS, 0, count_top, (jnp.zeros((1, t), I32), jnp.zeros((1, t), F32) + in_scope))
    u, n_ge = bisect(32 - FIELD_BITS, 0, lambda cand: count(lambda kt: kt >= (cand ^ jnp.int32(INT_MIN))),
                     (jnp.left_shift(top, 32 - FIELD_BITS), n_ge))
    thr = u ^ jnp.int32(INT_MIN)
    has_ties = jnp.max(n_ge) > float(topk)

    _init_state(m_ref, l_ref, acc_ref)

    def attend(ties):
        if ties:
            carry_ref[...] = jnp.zeros(carry_ref.shape, F32)
            need = float(topk) - count(lambda kt: kt > thr)
            lower = (lax.broadcasted_iota(I32, (t, t), 1) <= lax.broadcasted_iota(I32, (t, t), 0)).astype(BF16)

        def scores(j0, buf):
            start = pl.multiple_of(j0 * t, t)
            ks = k_ref[0, pl.ds(start, CHUNK_TILES * t), :]
            for h in range(4):
                s_ref[buf, h] = _dot(ks[:, (h // 2) * LANES:(h // 2 + 1) * LANES], qzt_ref[h])

        def rest(j0, w, diag, buf):
            if ties:
                parts = []
                for i in range(w):
                    kt = key_ref[j0 + i]
                    eq = kt == thr
                    incl = jnp.dot(lower, jnp.where(eq, 1.0, 0.0).astype(BF16),
                                   preferred_element_type=F32) + carry_ref[...]
                    carry_ref[...] = incl[t - 1:t, :]
                    parts.append(jnp.where((kt > thr) | (eq & (incl <= need)), 1.0, 0.0))
                sel = (parts[0] if w == 1 else jnp.concatenate(parts, axis=0)) > 0.5
            else:
                kt = key_ref[j0] if w == 1 else jnp.concatenate([key_ref[j0 + i] for i in range(w)], axis=0)
                sel = kt >= thr
            if diag:
                sel = sel & _causal_chunk(w, t)
            for h in range(4):
                g, r = divmod(h, 2)
                p, alpha = _softmax_step(jnp.where(sel, s_ref[buf, h, 0:w * t, :], -jnp.inf), m_ref, l_ref, h)
                _accumulate(acc_ref, g, r, alpha,
                            [vt_ref[0, j0 + i, h * HEAD_DIM:(h + 1) * HEAD_DIM, :] for i in range(w)], p, t)

        _pipelined_key_chunks(n, scores, rest, CHUNK_TILES)

    @pl.when(has_ties)
    def _():
        attend(True)

    @pl.when(jnp.logical_not(has_ties))
    def _():
        attend(False)

    for g in range(2):
        o_ref[0, :, g * LANES:(g + 1) * LANES] = _normalised_t(acc_ref, g, l_ref, 2 * g, 2 * g + 1).T.astype(BF16)


def _dsa_call(qk, vt, iq, ik, iw, *, t, topk):
    b, s, _ = qk.shape
    nt = s // t
    return pl.pallas_call(
        functools.partial(_dsa_kernel, t=t, topk=topk),
        grid=(b, nt),
        in_specs=[pl.BlockSpec((1, IDX_HEADS, t, IDX_CAT), lambda bi, ni: (bi, 0, ni, 0)),
                  pl.BlockSpec((1, 1, IDX_HEADS, t), lambda bi, ni: (bi, ni, 0, 0)),
                  pl.BlockSpec((1, s, IDX_CAT), lambda bi, ni: (bi, 0, 0), pipeline_mode=RESIDENT),
                  pl.BlockSpec((1, t, GROUP), lambda bi, ni: (bi, ni, QK_BLK["a_q"])),
                  pl.BlockSpec((1, s, GROUP), lambda bi, ni: (bi, 0, QK_BLK["a_k"]), pipeline_mode=RESIDENT),
                  pl.BlockSpec((1, nt, GROUP, t), lambda bi, ni: (bi, 0, V_ROW_BLK["a_v"], 0), pipeline_mode=RESIDENT)],
        out_specs=pl.BlockSpec((1, t, GROUP), lambda bi, ni: (bi, ni, 0)),
        out_shape=jax.ShapeDtypeStruct((b, s, GROUP), BF16),
        scratch_shapes=[pltpu.VMEM((nt, t, t), I32), pltpu.VMEM((nt, t // 2, t), I32),
                        pltpu.VMEM((4, 1, t), F32), pltpu.VMEM((4, 1, t), F32),
                        pltpu.VMEM((2, LANES, t), F32), pltpu.VMEM((1, t), F32),
                        pltpu.VMEM((IDX_HEADS, IDX_CAT, t), BF16), pltpu.VMEM((4, LANES, t), BF16),
                        pltpu.VMEM((2, 4, CHUNK_TILES * t, t), F32)],
        compiler_params=_cparams(("arbitrary", "arbitrary")),
        name="dsa",
    )(iq, iw, ik, qk, qk, vt)


def _fox_kernel(q_ref, k0_ref, k1_ref, vt_ref, o_ref, m_ref, l_ref, acc_ref, qxt_ref, s_ref, *, t):
    n = pl.program_id(1)
    _init_state(m_ref, l_ref, acc_ref)
    lane = lax.broadcasted_iota(I32, (1, LANES), 1)
    for h in range(4):
        qz = jnp.where(_head_mask(h % 2), q_ref[0, :, (h // 2) * LANES:(h // 2 + 1) * LANES], jnp.zeros((), BF16))
        ones = jnp.where((lane == h) | (lane == 4 + h) | (lane == 8 + h), 1.0, 0.0).astype(BF16)
        qxt_ref[h] = _transpose_bf16(jnp.concatenate([qz, jnp.broadcast_to(ones, (t, LANES))], axis=1))
    k_refs = (k0_ref, k1_ref)

    def scores(j0, buf):
        start = pl.multiple_of(j0 * t, t)
        for h in range(4):
            s_ref[buf, h] = _dot(k_refs[h // 2][0, pl.ds(start, CHUNK_TILES * t), :], qxt_ref[h])

    def rest(j0, w, diag, buf):
        for h in range(4):
            g, r = divmod(h, 2)
            s = s_ref[buf, h, 0:w * t, :]
            if diag:
                s = jnp.where(_causal_chunk(w, t), s, -jnp.inf)
            p, alpha = _softmax_step(s, m_ref, l_ref, h)
            _accumulate(acc_ref, g, r, alpha,
                        [vt_ref[0, j0 + i, h * HEAD_DIM:(h + 1) * HEAD_DIM, :] for i in range(w)], p, t)

    _pipelined_key_chunks(n, scores, rest, CHUNK_TILES)
    for g in range(2):
        o_ref[0, :, g * LANES:(g + 1) * LANES] = _normalised_t(acc_ref, g, l_ref, 2 * g, 2 * g + 1).T.astype(BF16)


def _fox_call(qk, vt, *, t):
    b, s, _ = qk.shape
    nt = s // t
    return pl.pallas_call(
        functools.partial(_fox_kernel, t=t),
        grid=(b, nt),
        in_specs=[pl.BlockSpec((1, t, GROUP), lambda bi, ni: (bi, ni, QK_BLK["b_q"])),
                  pl.BlockSpec((1, s, GROUP), lambda bi, ni: (bi, 0, QK_BLK["b_k"]), pipeline_mode=RESIDENT),
                  pl.BlockSpec((1, s, GROUP), lambda bi, ni: (bi, 0, QK_BLK["b_k"] + 1), pipeline_mode=RESIDENT),
                  pl.BlockSpec((1, nt, GROUP, t), lambda bi, ni: (bi, 0, V_ROW_BLK["b_v"], 0), pipeline_mode=RESIDENT)],
        out_specs=pl.BlockSpec((1, t, GROUP), lambda bi, ni: (bi, ni, 0)),
        out_shape=jax.ShapeDtypeStruct((b, s, GROUP), BF16),
        scratch_shapes=[pltpu.VMEM((4, 1, t), F32), pltpu.VMEM((4, 1, t), F32), pltpu.VMEM((2, LANES, t), F32),
                        pltpu.VMEM((4, GROUP, t), BF16), pltpu.VMEM((2, 4, CHUNK_TILES * t, t), F32)],
        compiler_params=_cparams(("arbitrary", "arbitrary")),
        name="fox",
    )(qk, qk, qk, vt)


def _diff_kernel(lam_ref, q_ref, k_ref, vt_ref, gs_ref, o_ref, m_ref, l_ref, acc_ref, qzt_ref, s_ref, *, t):
    n = pl.program_id(1)
    _init_state(m_ref, l_ref, acc_ref)
    lane = lax.broadcasted_iota(I32, (1, LANES), 1)
    for h in range(4):
        for mm in range(2):
            lo = (h % 2) * HEAD_DIM + mm * DIFF_DIM
            qzt_ref[2 * h + mm] = _transpose_bf16(
                jnp.where((lane >= lo) & (lane < lo + DIFF_DIM),
                          q_ref[0, :, (h // 2) * LANES:(h // 2 + 1) * LANES], jnp.zeros((), BF16)))

    def scores(j0, buf):
        start = pl.multiple_of(j0 * t, t)
        ks = k_ref[0, pl.ds(start, CHUNK_TILES_DIFF * t), :]
        for i in range(8):
            s_ref[buf, i] = _dot(ks[:, (i // 4) * LANES:(i // 4 + 1) * LANES], qzt_ref[i])

    def rest(j0, w, diag, buf):
        for h in range(4):
            g, r = divmod(h, 2)
            vts = [vt_ref[0, j0 + i, h * HEAD_DIM:(h + 1) * HEAD_DIM, :] for i in range(w)]
            for mm in range(2):
                s = s_ref[buf, 2 * h + mm, 0:w * t, :]
                if diag:
                    s = jnp.where(_causal_chunk(w, t), s, -jnp.inf)
                p, alpha = _softmax_step(s, m_ref, l_ref, 2 * h + mm)
                _accumulate(acc_ref, 2 * mm + g, r, alpha, vts, p, t)

    _pipelined_key_chunks(n, scores, rest, CHUNK_TILES_DIFF)
    lam = lam_ref[0]
    out_scale = lam_ref[1]
    for g in range(2):
        o1 = _normalised_t(acc_ref, g, l_ref, 4 * g, 4 * g + 2)
        o2 = _normalised_t(acc_ref, 2 + g, l_ref, 4 * g + 1, 4 * g + 3)
        o = o1 - lam * o2
        sq = o * o
        ms = jnp.concatenate(
            [jnp.broadcast_to(jnp.mean(sq[r * HEAD_DIM:(r + 1) * HEAD_DIM], axis=0, keepdims=True), (HEAD_DIM, t))
             for r in range(2)], axis=0)
        y = o * lax.rsqrt(ms + EPS) * gs_ref[...] * out_scale
        o_ref[0, :, g * LANES:(g + 1) * LANES] = y.T.astype(BF16)


def _diff_call(lam2, qk, vt, gs_full, *, t):
    b, s, _ = qk.shape
    nt = s // t
    return pl.pallas_call(
        functools.partial(_diff_kernel, t=t),
        grid=(b, nt),
        in_specs=[pl.BlockSpec(memory_space=pltpu.SMEM),
                  pl.BlockSpec((1, t, GROUP), lambda bi, ni: (bi, ni, QK_BLK["c_q"])),
                  pl.BlockSpec((1, s, GROUP), lambda bi, ni: (bi, 0, QK_BLK["c_k"]), pipeline_mode=RESIDENT),
                  pl.BlockSpec((1, nt, GROUP, t), lambda bi, ni: (bi, 0, V_ROW_BLK["c_v"], 0), pipeline_mode=RESIDENT),
                  pl.BlockSpec((LANES, t), lambda bi, ni: (0, 0))],
        out_specs=pl.BlockSpec((1, t, GROUP), lambda bi, ni: (bi, ni, 0)),
        out_shape=jax.ShapeDtypeStruct((b, s, GROUP), BF16),
        scratch_shapes=[pltpu.VMEM((8, 1, t), F32), pltpu.VMEM((8, 1, t), F32), pltpu.VMEM((4, LANES, t), F32),
                        pltpu.VMEM((8, LANES, t), BF16), pltpu.VMEM((2, 8, CHUNK_TILES_DIFF * t, t), F32)],
        compiler_params=_cparams(("arbitrary", "arbitrary")),
        name="diff",
    )(lam2, qk, qk, vt, gs_full)


def _swa_kernel(sink_ref, q_ref, kp_ref, kc_ref, vtp_ref, vtc_ref, o_ref, *, t):
    n = pl.program_id(1)
    prow = lax.broadcasted_iota(I32, (WINDOW, t), 0)
    pcol = lax.broadcasted_iota(I32, (WINDOW, t), 1)
    mask_prev = (prow > pcol) & (pcol + jnp.where(n > 0, 0, t) < WINDOW)
    crow = lax.broadcasted_iota(I32, (t, t), 0)
    ccol = lax.broadcasted_iota(I32, (t, t), 1)
    mask_cur = (crow <= ccol) & (crow > ccol - WINDOW)
    vtp = vtp_ref[0, 0][:, t - WINDOW:t]
    vtc = vtc_ref[0, 0]
    for g in range(2):
        outs = []
        for r in range(2):
            h = 2 * g + r
            qz = jnp.where(_head_mask(r), q_ref[0, :, g * LANES:(g + 1) * LANES], jnp.zeros((), BF16))
            sp = jnp.where(mask_prev, _dot_nt(kp_ref[0, :, g * LANES:(g + 1) * LANES], qz), -jnp.inf)
            sc = jnp.where(mask_cur, _dot_nt(kc_ref[0, :, g * LANES:(g + 1) * LANES], qz), -jnp.inf)
            sink = sink_ref[h] * LOG2E
            m = jnp.maximum(jnp.maximum(jnp.max(sp, axis=0, keepdims=True),
                                        jnp.max(sc, axis=0, keepdims=True)), sink)
            pp = jnp.exp2(sp - m)
            pc = jnp.exp2(sc - m)
            den = jnp.sum(pp, axis=0, keepdims=True) + jnp.sum(pc, axis=0, keepdims=True) + jnp.exp2(sink - m)
            rows = slice(h * HEAD_DIM, (h + 1) * HEAD_DIM)
            o = (jnp.dot(vtp[rows, :], pp.astype(BF16), preferred_element_type=F32)
                 + jnp.dot(vtc[rows, :], pc.astype(BF16), preferred_element_type=F32))
            outs.append(o / den)
        o_ref[0, :, g * LANES:(g + 1) * LANES] = jnp.concatenate(outs, axis=0).T.astype(BF16)


def _swa_call(sinks, qk, vt, *, t):
    b, s, _ = qk.shape
    nt = s // t
    per = t // WINDOW
    return pl.pallas_call(
        functools.partial(_swa_kernel, t=t),
        grid=(b, nt),
        in_specs=[pl.BlockSpec(memory_space=pltpu.SMEM),
                  pl.BlockSpec((1, t, GROUP), lambda bi, ni: (bi, ni, QK_BLK["d_q"])),
                  pl.BlockSpec((1, WINDOW, GROUP), lambda bi, ni: (bi, jnp.maximum(ni * per - 1, 0), QK_BLK["d_k"])),
                  pl.BlockSpec((1, t, GROUP), lambda bi, ni: (bi, ni, QK_BLK["d_k"])),
                  pl.BlockSpec((1, 1, GROUP, t), lambda bi, ni: (bi, jnp.maximum(ni - 1, 0), V_ROW_BLK["d_v"], 0)),
                  pl.BlockSpec((1, 1, GROUP, t), lambda bi, ni: (bi, ni, V_ROW_BLK["d_v"], 0))],
        out_specs=pl.BlockSpec((1, t, GROUP), lambda bi, ni: (bi, ni, 0)),
        out_shape=jax.ShapeDtypeStruct((b, s, GROUP), BF16),
        compiler_params=_cparams(("arbitrary", "arbitrary")),
        name="swa",
    )(sinks, qk, qk, qk, vt, vt)


def _post_kernel(oa_ref, ob_ref, oc_ref, od_ref, x_ref, p_ref, wo_ref, gmix_ref, gpre_ref, wu_ref, wd_ref, gpost_ref,
                 wg_ref, wp_ref, y_ref, *, chunk):
    acc = jnp.zeros(x_ref.shape[1:], F32)
    for i, o_ref in enumerate((oa_ref, ob_ref, oc_ref, od_ref)):
        acc = acc + jnp.dot(o_ref[0], wo_ref[i * GROUP:(i + 1) * GROUP, :], preferred_element_type=F32)
    x = x_ref[0] + _rms(acc, gmix_ref[...])
    h = _rms(x, gpre_ref[...]).astype(BF16)
    acc = jnp.zeros(x.shape, F32)
    for c in range(D_FF // chunk):
        u = jnp.dot(h, wu_ref[:, c * chunk:(c + 1) * chunk], preferred_element_type=F32)
        u = jnp.square(jnp.maximum(u, 0.0)).astype(BF16)
        acc = acc + jnp.dot(u, wd_ref[c * chunk:(c + 1) * chunk, :], preferred_element_type=F32)
    x = x + _rms(acc, gpost_ref[...])
    gate = jax.nn.sigmoid(jnp.dot(x.astype(BF16), wg_ref[...], preferred_element_type=F32))
    emb = jnp.dot(p_ref[0].astype(BF16), wp_ref[...], preferred_element_type=F32)
    y_ref[0] = x + gate * emb


def _post_call(tok_inputs, full_inputs, *, tm):
    b, s, d = tok_inputs[4].shape
    tok = lambda a: pl.BlockSpec((1, tm, a.shape[2]), lambda bi, ti: (bi, ti, 0))
    full = lambda a: pl.BlockSpec(a.shape, lambda bi, ti: (0,) * a.ndim, pipeline_mode=pl.Buffered(1))
    return pl.pallas_call(
        functools.partial(_post_kernel, chunk=1024),
        grid=(b, s // tm),
        in_specs=[tok(a) for a in tok_inputs] + [full(a) for a in full_inputs],
        out_specs=pl.BlockSpec((1, tm, d), lambda bi, ti: (bi, ti, 0)),
        out_shape=jax.ShapeDtypeStruct((b, s, d), F32),
        compiler_params=_cparams(("arbitrary", "arbitrary")),
        name="post",
    )(*tok_inputs, *full_inputs)


def _dup_kv(w):
    return jnp.concatenate([w[:, :HEAD_DIM], w[:, :HEAD_DIM], w[:, HEAD_DIM:], w[:, HEAD_DIM:]], axis=1)


def _prep_w_in(w):
    sec = lambda n: w[:, _SEC[n][0]:_SEC[n][0] + _SEC[n][1]]
    cols = [_dup_kv(sec(n)) if n in ("d_k", "d_v") else sec(n) for n in QK_GROUPS + V_GROUPS]
    w_main = jnp.concatenate(cols, axis=1).astype(BF16)
    pad = jnp.zeros((w.shape[0], LANES - IDX_DIM - IDX_HEADS - 4), w.dtype)
    w_idx = jnp.concatenate([sec("iq"), sec("ik"), sec("iw"), sec("b_f"), pad], axis=1)
    w_hi = w_idx.astype(BF16)
    w_lo = (w_idx - w_hi.astype(F32)).astype(BF16)
    return w_main, jnp.concatenate([w_hi, w_hi, w_lo], axis=0)


def _rope_tables(positions):
    pos = positions.astype(F32)[..., None]
    lane = jnp.arange(LANES)

    def tabs(dim):
        half = dim // 2
        inv_freq = ROPE_THETA ** (-jnp.arange(half, dtype=F32) / half)
        ang = pos * inv_freq
        sign = jnp.where((lane % dim) < half, -1.0, 1.0).astype(F32)
        reps = (1, 1, LANES // half)
        return jnp.tile(jnp.cos(ang), reps), jnp.tile(jnp.sin(ang), reps) * sign

    c64, s64 = tabs(HEAD_DIM)
    c32, s32 = tabs(DIFF_DIM)
    return c64, s64, c32, s32


def kernel(x, p, positions, w_in, b_forget, lambda_q1, lambda_k1, lambda_q2, lambda_k2, diff_subln, sinks,
           w_out, norm_pre_mix, norm_post_mix, norm_pre_mlp, norm_post_mlp, w_mlp_up, w_mlp_down,
           w_ple_proj, w_ple_gate):
    b, s, d = x.shape
    depth = w_in.shape[0]
    t = min(256, s)
    assert s % (CHUNK_TILES * t) == 0, "sequence length must be a multiple of the attention key chunk"
    topk = min(TOPK_MAX, s // 4)
    tabs = _rope_tables(positions)
    row = lambda v: v.reshape(1, -1).astype(F32)

    for i in range(depth):
        lam_init = 0.8 - 0.6 * math.exp(-0.3 * i)
        w_main, w_idx = _prep_w_in(w_in[i])
        bf_row = jnp.zeros((1, LANES), F32).at[0, MISC_F:MISC_F + 4].set(b_forget[i])
        qk, vt, iq, ik, iw = _proj_call(x, row(norm_pre_mix[i]), w_main, w_idx, tabs, bf_row, tm=t)

        o_a = _dsa_call(qk, vt, iq, ik, iw, t=t, topk=topk)
        o_b = _fox_call(qk, vt, t=t)
        lam = (jnp.exp(jnp.sum(lambda_q1[i] * lambda_k1[i])) - jnp.exp(jnp.sum(lambda_q2[i] * lambda_k2[i]))
               + lam_init)
        lam2 = jnp.stack([lam, jnp.asarray(1.0 - lam_init, F32)]).astype(F32)
        gs_full = jnp.broadcast_to(jnp.concatenate([diff_subln[i], diff_subln[i]]).astype(F32)[:, None], (LANES, t))
        o_c = _diff_call(lam2, qk, vt, gs_full, t=t)
        o_d = _swa_call(sinks[i].astype(F32), qk, vt, t=t)

        x = _post_call([o_a, o_b, o_c, o_d, x, p[i]],
                       [w_out[i].astype(BF16), row(norm_post_mix[i]), row(norm_pre_mlp[i]),
                        w_mlp_up[i].astype(BF16), w_mlp_down[i].astype(BF16), row(norm_post_mlp[i]),
                        w_ple_gate[i].astype(BF16), w_ple_proj[i].astype(BF16)], tm=t)
    return x
```

```python
import functools
import math

import jax
import jax.numpy as jnp
from jax import lax
from jax.experimental import pallas as pl
from jax.experimental.pallas import tpu as pltpu

F32, BF16, I32 = jnp.float32, jnp.bfloat16, jnp.int32

D_MODEL = 1024
HEAD_DIM = 64
DIFF_DIM = 32
IDX_HEADS = 8
IDX_DIM = 64
TOPK_MAX = 256
WINDOW = 128
D_FF = 4 * D_MODEL
D_PLE = 256
ROPE_THETA = 10000.0
EPS = 1e-6
NEG_INF = -1e30
LANES = 128
SUBLANES = 8
GROUP = 256
INT_MIN = -2147483648
FIELD_BITS = 15
FIELD_GUARDS = -2147450880
LOW_BITS = 10

_SEC = {}
_o = 0
for _name, _w in (("a_q", 256), ("a_k", 256), ("a_v", 256), ("iq", 512), ("ik", 64), ("iw", 8),
                  ("b_q", 256), ("b_k", 256), ("b_v", 256), ("b_f", 4),
                  ("c_q", 256), ("c_k", 256), ("c_v", 256), ("d_q", 256), ("d_k", 128), ("d_v", 128)):
    _SEC[_name] = (_o, _w)
    _o += _w
D_IN = _o

QK_GROUPS = ("a_q", "a_k", "b_q", "b_k", "c_q", "c_k", "d_q", "d_k")
V_GROUPS = ("a_v", "b_v", "c_v", "d_v")
MAIN_W = GROUP * (len(QK_GROUPS) + len(V_GROUPS))
V_BASE = GROUP * len(QK_GROUPS)
QK_BLK = {"a_q": 0, "a_k": 1, "b_q": 2, "b_k": 3, "c_q": 5, "c_k": 6, "d_q": 7, "d_k": 8}
QK_W = GROUP * 9
V_ROW_BLK = {n: i for i, n in enumerate(V_GROUPS)}
ROPE64_GROUPS = ("a_q", "a_k", "d_q", "d_k")
ROPE32_GROUPS = ("c_q", "c_k")
LOG2E = math.log2(math.e)
Q_SCALE = {"a_q": HEAD_DIM ** -0.5 * LOG2E, "b_q": HEAD_DIM ** -0.5 * LOG2E, "c_q": DIFF_DIM ** -0.5 * LOG2E,
           "d_q": HEAD_DIM ** -0.5 * LOG2E}
IDX_W = 512 + LANES
MISC_IW = 64
MISC_F = 72
IDX_CAT = 256

CHUNK_TILES = 4
CHUNK_TILES_DIFF = 2
VMEM_LIMIT = 56 * 1024 * 1024
RESIDENT = pl.Buffered(1)


def _cparams(sem):
    return pltpu.CompilerParams(dimension_semantics=sem, vmem_limit_bytes=VMEM_LIMIT)


def _rms(x, g):
    return x * lax.rsqrt(jnp.mean(x * x, axis=-1, keepdims=True) + EPS) * g


def _dot_nt(a, b):
    return lax.dot_general(a, b, (((1,), (1,)), ((), ())), preferred_element_type=F32)


def _dot(a, b):
    return jnp.dot(a, b, preferred_element_type=F32)


def _transpose_bf16(a):
    return a.astype(F32).T.astype(BF16)


def _bf16_part(x):
    return x.astype(BF16).astype(F32)


def _split2(x):
    hi = _bf16_part(x)
    return hi, _bf16_part(x - hi)


def _split3(x):
    hi = _bf16_part(x)
    mid = _bf16_part(x - hi)
    return hi, mid, _bf16_part(x - hi - mid)


def _rope_chunk(xc, cos, sin_signed, half, lane):
    fwd = pltpu.roll(xc, LANES - half, axis=1)
    bwd = pltpu.roll(xc, half, axis=1)
    partner = jnp.where((lane % (2 * half)) < half, fwd, bwd)
    return xc * cos + partner * sin_signed


def _proj_kernel(x_ref, g_ref, wm_ref, wi_ref, c64_ref, s64_ref, c32_ref, s32_ref, bf_ref,
                 qk_ref, vt_ref, iq_ref, ik_ref, iw_ref, carry_ref, *, tm):
    t = pl.program_id(1)
    x = x_ref[0]
    h = _rms(x, g_ref[...])
    pm = jnp.dot(h.astype(BF16), wm_ref[...], preferred_element_type=F32)
    h_hi, h_lo = _split2(h)
    pi = jnp.dot(jnp.concatenate([h_hi, h_lo, h_hi], axis=1).astype(BF16), wi_ref[...],
                 preferred_element_type=F32)
    lane = lax.broadcasted_iota(I32, (1, LANES), 1)
    lo_half = lane < HEAD_DIM
    c64, s64, c32, s32 = c64_ref[0], s64_ref[0], c32_ref[0], s32_ref[0]

    misc = pi[:, 512:512 + LANES]
    z = misc + bf_ref[...]
    logf = jnp.minimum(z, 0.0) - jnp.log1p(jnp.exp(-jnp.abs(z)))
    logf = jnp.where((lane >= MISC_F) & (lane < MISC_F + 4), logf, 0.0)
    tri = (lax.broadcasted_iota(I32, (tm, tm), 1) <= lax.broadcasted_iota(I32, (tm, tm), 0)).astype(BF16)

    @pl.when(t == 0)
    def _():
        carry_ref[...] = jnp.zeros_like(carry_ref)

    pieces = jnp.dot(tri, jnp.concatenate(_split3(logf), axis=1).astype(BF16), preferred_element_type=F32)
    cum = pieces[:, 0:LANES] + pieces[:, LANES:2 * LANES] + pieces[:, 2 * LANES:3 * LANES] + carry_ref[...]
    carry_ref[...] = cum[tm - 1:tm, :]
    nhi, nmid, nlo = _split3(-LOG2E * cum)
    gate_bias = jnp.where(lane < 4, pltpu.roll(nhi, LANES - MISC_F, axis=1),
                          jnp.where(lane < 8, pltpu.roll(nmid, LANES - MISC_F + 4, axis=1),
                                    jnp.where(lane < 12, pltpu.roll(nlo, LANES - MISC_F + 8, axis=1),
                                              jnp.zeros_like(nlo))))

    for gi, name in enumerate(QK_GROUPS):
        for c in range(GROUP // LANES):
            lo = gi * GROUP + c * LANES
            v = pm[:, lo:lo + LANES]
            if name in ROPE64_GROUPS:
                v = _rope_chunk(v, c64, s64, HEAD_DIM // 2, lane)
            elif name in ROPE32_GROUPS:
                v = _rope_chunk(v, c32, s32, DIFF_DIM // 2, lane)
            if name in Q_SCALE:
                v = v * Q_SCALE[name]
            if name == "b_k":
                out = (QK_BLK[name] + c) * GROUP
                qk_ref[0, :, out:out + LANES] = v.astype(BF16)
                qk_ref[0, :, out + LANES:out + GROUP] = gate_bias.astype(BF16)
            else:
                out = QK_BLK[name] * GROUP + c * LANES
                qk_ref[0, :, out:out + LANES] = v.astype(BF16)

    vt_ref[0, 0] = pm[:, V_BASE:V_BASE + len(V_GROUPS) * GROUP].T.astype(BF16)

    def cat_q(q):
        hi, lo = _split2(q)
        return jnp.where(lo_half, hi, pltpu.roll(lo, HEAD_DIM, axis=1)), hi

    for c in range(512 // LANES):
        v = _rope_chunk(pi[:, c * LANES:(c + 1) * LANES], c64, s64, IDX_DIM // 2, lane)
        for r, q in enumerate((jnp.where(lo_half, v, 0.0), jnp.where(lo_half, pltpu.roll(v, HEAD_DIM, axis=1), 0.0))):
            a, b2 = cat_q(q)
            iq_ref[0, 2 * c + r, :, 0:LANES] = a.astype(BF16)
            iq_ref[0, 2 * c + r, :, LANES:IDX_CAT] = b2.astype(BF16)
    ik = jnp.where(lo_half, _rope_chunk(misc, c64, s64, IDX_DIM // 2, lane), 0.0)
    khi, klo = _split2(ik)
    ik_ref[0, :, 0:LANES] = jnp.where(lo_half, khi, pltpu.roll(khi, HEAD_DIM, axis=1)).astype(BF16)
    ik_ref[0, :, LANES:IDX_CAT] = klo.astype(BF16)
    iw_t = (misc * (IDX_HEADS ** -0.5 * IDX_DIM ** -0.5)).T
    iw_ref[0, 0] = iw_t[MISC_IW:MISC_IW + IDX_HEADS, :]


def _proj_call(x, g, w_main, w_idx, tabs, bf_row, *, tm):
    b, s, d = x.shape
    tok = lambda w: pl.BlockSpec((1, tm, w), lambda bi, ti: (bi, ti, 0))
    full = lambda a: pl.BlockSpec(a.shape, lambda bi, ti: (0,) * a.ndim)
    return pl.pallas_call(
        functools.partial(_proj_kernel, tm=tm),
        grid=(b, s // tm),
        in_specs=[tok(d), full(g), full(w_main), full(w_idx), tok(LANES), tok(LANES), tok(LANES), tok(LANES),
                  full(bf_row)],
        out_specs=[tok(QK_W),
                   pl.BlockSpec((1, 1, len(V_GROUPS) * GROUP, tm), lambda bi, ti: (bi, ti, 0, 0)),
                   pl.BlockSpec((1, IDX_HEADS, tm, IDX_CAT), lambda bi, ti: (bi, 0, ti, 0)),
                   tok(IDX_CAT),
                   pl.BlockSpec((1, 1, IDX_HEADS, tm), lambda bi, ti: (bi, ti, 0, 0))],
        out_shape=[jax.ShapeDtypeStruct((b, s, QK_W), BF16),
                   jax.ShapeDtypeStruct((b, s // tm, len(V_GROUPS) * GROUP, tm), BF16),
                   jax.ShapeDtypeStruct((b, IDX_HEADS, s, IDX_CAT), BF16),
                   jax.ShapeDtypeStruct((b, s, IDX_CAT), BF16),
                   jax.ShapeDtypeStruct((b, s // tm, IDX_HEADS, tm), F32)],
        scratch_shapes=[pltpu.VMEM((1, LANES), F32)],
        compiler_params=_cparams(("arbitrary", "arbitrary")),
        name="proj",
    )(x, g, w_main, w_idx, *tabs, bf_row)


def _softmax_step(s, m_ref, l_ref, idx):
    m_old = m_ref[idx]
    m_new = jnp.maximum(m_old, jnp.max(s, axis=0, keepdims=True))
    alpha = jnp.exp2(m_old - m_new)
    p = jnp.exp2(s - m_new)
    l_ref[idx] = alpha * l_ref[idx] + jnp.sum(p, axis=0, keepdims=True)
    m_ref[idx] = m_new
    return p, alpha


def _init_state(m_ref, l_ref, acc_ref):
    m_ref[...] = jnp.full(m_ref.shape, NEG_INF, F32)
    l_ref[...] = jnp.zeros(l_ref.shape, F32)
    acc_ref[...] = jnp.zeros(acc_ref.shape, F32)


def _causal_chunk(w, t):
    return lax.broadcasted_iota(I32, (w * t, t), 0) <= lax.broadcasted_iota(I32, (w * t, t), 1) + (w - 1) * t


def _head_mask(r):
    lane = lax.broadcasted_iota(I32, (1, LANES), 1)
    return (lane >= HEAD_DIM) if r else (lane < HEAD_DIM)


def _key_chunks(n, chunk_fn, width):
    def _body(c, carry):
        chunk_fn(width * c, width, False)
        return carry

    n_full = n // width
    lax.fori_loop(0, n_full, _body, 0)
    rest = n - n_full * width
    for k in range(width):
        @pl.when(rest == k)
        def _(k=k):
            chunk_fn(n - k, k + 1, True)


def _pipelined_key_chunks(n, scores_fn, rest_fn, width):
    n_full = n // width
    rest = n - n_full * width
    odd = n_full & 1

    @pl.when(odd == 1)
    def _():
        scores_fn(0, 1)
        scores_fn(width, 0)
        rest_fn(0, width, False, 1)

    @pl.when(odd == 0)
    def _():
        scores_fn(0, 0)

    def _body(pair, carry):
        j0 = (odd + 2 * pair) * width
        scores_fn(j0 + width, 1)
        rest_fn(j0, width, False, 0)
        scores_fn(j0 + 2 * width, 0)
        rest_fn(j0 + width, width, False, 1)
        return carry

    lax.fori_loop(0, (n_full - odd) >> 1, _body, 0)
    for k in range(width):
        @pl.when(rest == k)
        def _(k=k):
            rest_fn(n - k, k + 1, True, 0)


def _accumulate(acc_ref, a, r, alpha, vt_tiles, p, t):
    rows = slice(r * HEAD_DIM, (r + 1) * HEAD_DIM)
    pb = p.astype(BF16)
    pv = jnp.dot(vt_tiles[0], pb[0:t], preferred_element_type=F32)
    for i in range(1, len(vt_tiles)):
        pv = pv + jnp.dot(vt_tiles[i], pb[i * t:(i + 1) * t], preferred_element_type=F32)
    acc_ref[a, rows, :] = acc_ref[a, rows, :] * alpha + pv


def _normalised_t(acc_ref, a, l_ref, idx0, idx1):
    return jnp.concatenate([acc_ref[a, 0:HEAD_DIM, :] / l_ref[idx0],
                            acc_ref[a, HEAD_DIM:LANES, :] / l_ref[idx1]], axis=0)


def _dsa_kernel(iq_ref, iw_ref, ik_ref, q_ref, k_ref, vt_ref, o_ref,
                key_ref, pack_ref, m_ref, l_ref, acc_ref, carry_ref, iqt_ref, qzt_ref, s_ref, thr_ref, nge_ref, *, t, topk):
    n = pl.program_id(1)
    for h in range(IDX_HEADS):
        iqt_ref[h] = _transpose_bf16(iq_ref[0, h])
    for h in range(4):
        qzt_ref[h] = _transpose_bf16(jnp.where(_head_mask(h % 2), q_ref[0, :, (h // 2) * LANES:(h // 2 + 1) * LANES],
                                               jnp.zeros((), BF16)))

    def score_chunk(j0, w, diag):
        start = pl.multiple_of(j0 * t, t)
        ik = ik_ref[0, pl.ds(start, w * t), :]
        sc = jnp.zeros((w * t, t), F32)
        for h in range(IDX_HEADS):
            d = _dot(ik, iqt_ref[h])
            sc = sc + iw_ref[0, 0, h:h + 1, :] * jnp.maximum(d, 0.0)
        if diag:
            sc = jnp.where(_causal_chunk(w, t), sc, NEG_INF)
        sc = jnp.where(sc == 0.0, 0.0, sc)
        bits = lax.bitcast_convert_type(sc, I32)
        key = bits ^ ((bits >> 31) & jnp.int32(0x7FFFFFFF))
        top = lax.shift_right_logical(key ^ jnp.int32(INT_MIN), 32 - FIELD_BITS)
        for i in range(w):
            key_ref[j0 + i] = key[i * t:(i + 1) * t]
            pack_ref[j0 + i] = (jnp.left_shift(top[i * t:i * t + t // 2], 16) | top[i * t + t // 2:(i + 1) * t]
                                | jnp.int32(FIELD_GUARDS))

    _key_chunks(n, score_chunk, CHUNK_TILES)

    def tree_sum(parts):
        while len(parts) > 1:
            parts = [parts[i] + parts[i + 1] for i in range(0, len(parts), 2)]
        return parts[0]

    def count(pred):
        def body(j, acc):
            c = jnp.where(pred(key_ref[j]), 1.0, 0.0)
            return acc + tree_sum([c[r * SUBLANES:(r + 1) * SUBLANES, :] for r in range(t // SUBLANES)])
        acc = lax.fori_loop(0, n + 1, body, jnp.zeros((SUBLANES, t), F32))
        return jnp.sum(acc, axis=0, keepdims=True)

    def count_top(cand):
        both = jnp.left_shift(cand, 16) | cand

        def body(j, acc):
            z = pack_ref[j] - both
            c = lax.shift_right_logical(z, FIELD_BITS) & jnp.int32(0x00010001)
            return acc + tree_sum([c[r * SUBLANES:(r + 1) * SUBLANES, :] for r in range(t // 2 // SUBLANES)])
        acc = lax.fori_loop(0, n + 1, body, jnp.zeros((SUBLANES, t), I32))
        per_lane = lax.shift_right_logical(acc, 16) + (acc & jnp.int32(0xFFFF))
        return jnp.sum(per_lane.astype(F32), axis=0, keepdims=True)

    def bisect(n_bits, low_bit, count_fn, carry):
        def body(i, carry):
            u, n_ge, n_rej = carry
            cand = u | jnp.left_shift(jnp.int32(1), low_bit + n_bits - 1 - i)
            cnt = count_fn(cand)
            keep = cnt >= float(topk)
            return jnp.where(keep, cand, u), jnp.where(keep, cnt, n_ge), jnp.where(keep, n_rej, cnt)
        return lax.fori_loop(0, n_bits, body, carry)

    def count_full(cand):
        return count(lambda kt: kt >= (cand ^ jnp.int32(INT_MIN)))

    in_scope = ((n + 1) * t).astype(F32)
    zero_row = jnp.zeros((1, t), F32)
    top, n_ge, n_rej = bisect(FIELD_BITS, 0, count_top, (jnp.zeros((1, t), I32), zero_row + in_scope, zero_row))
    u, n_ge, n_rej = bisect(32 - FIELD_BITS - LOW_BITS, LOW_BITS, count_full,
                            (jnp.left_shift(top, 32 - FIELD_BITS), n_ge, n_rej))

    prefix = u ^ jnp.int32(INT_MIN)
    low_mask = jnp.int32((1 << LOW_BITS) - 1)
    big = jnp.int32(1 << (LOW_BITS + 1))

    def tree(parts, op):
        while len(parts) > 1:
            parts = [op(parts[i], parts[i + 1]) for i in range(0, len(parts), 2)]
        return parts[0]

    def band_body(j, carry):
        mx, mn, sm = carry
        kt = key_ref[j]
        inside = lax.shift_right_logical(kt ^ prefix, LOW_BITS) == 0
        low = kt & low_mask
        rows = lambda a: [a[r * SUBLANES:(r + 1) * SUBLANES, :] for r in range(t // SUBLANES)]
        return (jnp.maximum(mx, tree(rows(jnp.where(inside, low, -1)), jnp.maximum)),
                jnp.minimum(mn, tree(rows(jnp.where(inside, low, big)), jnp.minimum)),
                sm + tree(rows(jnp.where(inside, low, 0)), jnp.add))

    mx, mn, sm = lax.fori_loop(0, n + 1, band_body, (jnp.full((SUBLANES, t), -1, I32),
                                                       jnp.full((SUBLANES, t), 1 << (LOW_BITS + 1), I32),
                                                       jnp.zeros((SUBLANES, t), I32)))
    mx = jnp.max(mx.astype(F32), axis=0, keepdims=True)
    mn = jnp.min(mn.astype(F32), axis=0, keepdims=True)
    sm = jnp.sum(sm.astype(F32), axis=0, keepdims=True)
    short = n_ge < float(topk)
    in_band = n_ge - n_rej
    need = float(topk) - n_rej
    second = jnp.where(in_band > 2.5, sm - mx - mn, mn)
    thr_low = jnp.where(need < 1.5, mx, jnp.where(need < 2.5, second, mn))
    kept = (1.0 + jnp.where((in_band > 1.5) & (second >= thr_low), 1.0, 0.0)
            + jnp.where((in_band > 2.5) & (mn >= thr_low), 1.0, 0.0))
    resolved = jnp.min(jnp.where(short | (in_band < 3.5), 1.0, 0.0)) > 0.5

    @pl.when(resolved)
    def _():
        thr_ref[...] = jnp.where(short, prefix, prefix | thr_low.astype(I32))
        nge_ref[...] = jnp.where(short, n_ge, n_rej + kept)

    @pl.when(jnp.logical_not(resolved))
    def _():
        u_all, n_ge_all, _ = bisect(LOW_BITS, 0, count_full, (u, n_ge, n_rej))
        thr_ref[...] = u_all ^ jnp.int32(INT_MIN)
        nge_ref[...] = n_ge_all

    thr = thr_ref[...]
    has_ties = jnp.max(nge_ref[...]) > float(topk)

    _init_state(m_ref, l_ref, acc_ref)

    def attend(ties):
        if ties:
            carry_ref[...] = jnp.zeros(carry_ref.shape, F32)
            need = float(topk) - count(lambda kt: kt > thr)
            lower = (lax.broadcasted_iota(I32, (t, t), 1) <= lax.broadcasted_iota(I32, (t, t), 0)).astype(BF16)

        def scores(j0, buf):
            start = pl.multiple_of(j0 * t, t)
            ks = k_ref[0, pl.ds(start, CHUNK_TILES * t), :]
            for h in range(4):
                s_ref[buf, h] = _dot(ks[:, (h // 2) * LANES:(h // 2 + 1) * LANES], qzt_ref[h])

        def rest(j0, w, diag, buf):
            if ties:
                parts = []
                for i in range(w):
                    kt = key_ref[j0 + i]
                    eq = kt == thr
                    incl = jnp.dot(lower, jnp.where(eq, 1.0, 0.0).astype(BF16),
                                   preferred_element_type=F32) + carry_ref[...]
                    carry_ref[...] = incl[t - 1:t, :]
                    parts.append(jnp.where((kt > thr) | (eq & (incl <= need)), 1.0, 0.0))
                sel = (parts[0] if w == 1 else jnp.concatenate(parts, axis=0)) > 0.5
            else:
                kt = key_ref[j0] if w == 1 else jnp.concatenate([key_ref[j0 + i] for i in range(w)], axis=0)
                sel = kt >= thr
            if diag:
                sel = sel & _causal_chunk(w, t)
            for h in range(4):
                g, r = divmod(h, 2)
                p, alpha = _softmax_step(jnp.where(sel, s_ref[buf, h, 0:w * t, :], -jnp.inf), m_ref, l_ref, h)
                _accumulate(acc_ref, g, r, alpha,
                            [vt_ref[0, j0 + i, h * HEAD_DIM:(h + 1) * HEAD_DIM, :] for i in range(w)], p, t)

        _pipelined_key_chunks(n, scores, rest, CHUNK_TILES)

    @pl.when(has_ties)
    def _():
        attend(True)

    @pl.when(jnp.logical_not(has_ties))
    def _():
        attend(False)

    for g in range(2):
        o_ref[0, :, g * LANES:(g + 1) * LANES] = _normalised_t(acc_ref, g, l_ref, 2 * g, 2 * g + 1).T.astype(BF16)


def _dsa_call(qk, vt, iq, ik, iw, *, t, topk):
    b, s, _ = qk.shape
    nt = s // t
    return pl.pallas_call(
        functools.partial(_dsa_kernel, t=t, topk=topk),
        grid=(b, nt),
        in_specs=[pl.BlockSpec((1, IDX_HEADS, t, IDX_CAT), lambda bi, ni: (bi, 0, ni, 0)),
                  pl.BlockSpec((1, 1, IDX_HEADS, t), lambda bi, ni: (bi, ni, 0, 0)),
                  pl.BlockSpec((1, s, IDX_CAT), lambda bi, ni: (bi, 0, 0), pipeline_mode=RESIDENT),
                  pl.BlockSpec((1, t, GROUP), lambda bi, ni: (bi, ni, QK_BLK["a_q"])),
                  pl.BlockSpec((1, s, GROUP), lambda bi, ni: (bi, 0, QK_BLK["a_k"]), pipeline_mode=RESIDENT),
                  pl.BlockSpec((1, nt, GROUP, t), lambda bi, ni: (bi, 0, V_ROW_BLK["a_v"], 0), pipeline_mode=RESIDENT)],
        out_specs=pl.BlockSpec((1, t, GROUP), lambda bi, ni: (bi, ni, 0)),
        out_shape=jax.ShapeDtypeStruct((b, s, GROUP), BF16),
        scratch_shapes=[pltpu.VMEM((nt, t, t), I32), pltpu.VMEM((nt, t // 2, t), I32),
                        pltpu.VMEM((4, 1, t), F32), pltpu.VMEM((4, 1, t), F32),
                        pltpu.VMEM((2, LANES, t), F32), pltpu.VMEM((1, t), F32),
                        pltpu.VMEM((IDX_HEADS, IDX_CAT, t), BF16), pltpu.VMEM((4, LANES, t), BF16),
                        pltpu.VMEM((2, 4, CHUNK_TILES * t, t), F32),
                        pltpu.VMEM((1, t), I32), pltpu.VMEM((1, t), F32)],
        compiler_params=_cparams(("arbitrary", "arbitrary")),
        name="dsa",
    )(iq, iw, ik, qk, qk, vt)


def _fox_kernel(q_ref, k0_ref, k1_ref, vt_ref, o_ref, m_ref, l_ref, acc_ref, qxt_ref, s_ref, *, t):
    n = pl.program_id(1)
    _init_state(m_ref, l_ref, acc_ref)
    lane = lax.broadcasted_iota(I32, (1, LANES), 1)
    for h in range(4):
        qz = jnp.where(_head_mask(h % 2), q_ref[0, :, (h // 2) * LANES:(h // 2 + 1) * LANES], jnp.zeros((), BF16))
        ones = jnp.where((lane == h) | (lane == 4 + h) | (lane == 8 + h), 1.0, 0.0).astype(BF16)
        qxt_ref[h] = _transpose_bf16(jnp.concatenate([qz, jnp.broadcast_to(ones, (t, LANES))], axis=1))
    k_refs = (k0_ref, k1_ref)

    def scores(j0, buf):
        start = pl.multiple_of(j0 * t, t)
        for h in range(4):
            s_ref[buf, h] = _dot(k_refs[h // 2][0, pl.ds(start, CHUNK_TILES * t), :], qxt_ref[h])

    def rest(j0, w, diag, buf):
        for h in range(4):
            g, r = divmod(h, 2)
            s = s_ref[buf, h, 0:w * t, :]
            if diag:
                s = jnp.where(_causal_chunk(w, t), s, -jnp.inf)
            p, alpha = _softmax_step(s, m_ref, l_ref, h)
            _accumulate(acc_ref, g, r, alpha,
                        [vt_ref[0, j0 + i, h * HEAD_DIM:(h + 1) * HEAD_DIM, :] for i in range(w)], p, t)

    _pipelined_key_chunks(n, scores, rest, CHUNK_TILES)
    for g in range(2):
        o_ref[0, :, g * LANES:(g + 1) * LANES] = _normalised_t(acc_ref, g, l_ref, 2 * g, 2 * g + 1).T.astype(BF16)


def _fox_call(qk, vt, *, t):
    b, s, _ = qk.shape
    nt = s // t
    return pl.pallas_call(
        functools.partial(_fox_kernel, t=t),
        grid=(b, nt),
        in_specs=[pl.BlockSpec((1, t, GROUP), lambda bi, ni: (bi, ni, QK_BLK["b_q"])),
                  pl.BlockSpec((1, s, GROUP), lambda bi, ni: (bi, 0, QK_BLK["b_k"]), pipeline_mode=RESIDENT),
                  pl.BlockSpec((1, s, GROUP), lambda bi, ni: (bi, 0, QK_BLK["b_k"] + 1), pipeline_mode=RESIDENT),
                  pl.BlockSpec((1, nt, GROUP, t), lambda bi, ni: (bi, 0, V_ROW_BLK["b_v"], 0), pipeline_mode=RESIDENT)],
        out_specs=pl.BlockSpec((1, t, GROUP), lambda bi, ni: (bi, ni, 0)),
        out_shape=jax.ShapeDtypeStruct((b, s, GROUP), BF16),
        scratch_shapes=[pltpu.VMEM((4, 1, t), F32), pltpu.VMEM((4, 1, t), F32), pltpu.VMEM((2, LANES, t), F32),
                        pltpu.VMEM((4, GROUP, t), BF16), pltpu.VMEM((2, 4, CHUNK_TILES * t, t), F32)],
        compiler_params=_cparams(("arbitrary", "arbitrary")),
        name="fox",
    )(qk, qk, qk, vt)


def _diff_kernel(lam_ref, q_ref, k_ref, vt_ref, gs_ref, o_ref, m_ref, l_ref, acc_ref, qzt_ref, s_ref, *, t):
    n = pl.program_id(1)
    _init_state(m_ref, l_ref, acc_ref)
    lane = lax.broadcasted_iota(I32, (1, LANES), 1)
    for h in range(4):
        for mm in range(2):
            lo = (h % 2) * HEAD_DIM + mm * DIFF_DIM
            qzt_ref[2 * h + mm] = _transpose_bf16(
                jnp.where((lane >= lo) & (lane < lo + DIFF_DIM),
                          q_ref[0, :, (h // 2) * LANES:(h // 2 + 1) * LANES], jnp.zeros((), BF16)))

    def scores(j0, buf):
        start = pl.multiple_of(j0 * t, t)
        ks = k_ref[0, pl.ds(start, CHUNK_TILES_DIFF * t), :]
        for i in range(8):
            s_ref[buf, i] = _dot(ks[:, (i // 4) * LANES:(i // 4 + 1) * LANES], qzt_ref[i])

    def rest(j0, w, diag, buf):
        for h in range(4):
            g, r = divmod(h, 2)
            vts = [vt_ref[0, j0 + i, h * HEAD_DIM:(h + 1) * HEAD_DIM, :] for i in range(w)]
            for mm in range(2):
                s = s_ref[buf, 2 * h + mm, 0:w * t, :]
                if diag:
                    s = jnp.where(_causal_chunk(w, t), s, -jnp.inf)
                p, alpha = _softmax_step(s, m_ref, l_ref, 2 * h + mm)
                _accumulate(acc_ref, 2 * mm + g, r, alpha, vts, p, t)

    _pipelined_key_chunks(n, scores, rest, CHUNK_TILES_DIFF)
    lam = lam_ref[0]
    out_scale = lam_ref[1]
    for g in range(2):
        o1 = _normalised_t(acc_ref, g, l_ref, 4 * g, 4 * g + 2)
        o2 = _normalised_t(acc_ref, 2 + g, l_ref, 4 * g + 1, 4 * g + 3)
        o = o1 - lam * o2
        sq = o * o
        ms = jnp.concatenate(
            [jnp.broadcast_to(jnp.mean(sq[r * HEAD_DIM:(r + 1) * HEAD_DIM], axis=0, keepdims=True), (HEAD_DIM, t))
             for r in range(2)], axis=0)
        y = o * lax.rsqrt(ms + EPS) * gs_ref[...] * out_scale
        o_ref[0, :, g * LANES:(g + 1) * LANES] = y.T.astype(BF16)


def _diff_call(lam2, qk, vt, gs_full, *, t):
    b, s, _ = qk.shape
    nt = s // t
    return pl.pallas_call(
        functools.partial(_diff_kernel, t=t),
        grid=(b, nt),
        in_specs=[pl.BlockSpec(memory_space=pltpu.SMEM),
                  pl.BlockSpec((1, t, GROUP), lambda bi, ni: (bi, ni, QK_BLK["c_q"])),
                  pl.BlockSpec((1, s, GROUP), lambda bi, ni: (bi, 0, QK_BLK["c_k"]), pipeline_mode=RESIDENT),
                  pl.BlockSpec((1, nt, GROUP, t), lambda bi, ni: (bi, 0, V_ROW_BLK["c_v"], 0), pipeline_mode=RESIDENT),
                  pl.BlockSpec((LANES, t), lambda bi, ni: (0, 0))],
        out_specs=pl.BlockSpec((1, t, GROUP), lambda bi, ni: (bi, ni, 0)),
        out_shape=jax.ShapeDtypeStruct((b, s, GROUP), BF16),
        scratch_shapes=[pltpu.VMEM((8, 1, t), F32), pltpu.VMEM((8, 1, t), F32), pltpu.VMEM((4, LANES, t), F32),
                        pltpu.VMEM((8, LANES, t), BF16), pltpu.VMEM((2, 8, CHUNK_TILES_DIFF * t, t), F32)],
        compiler_params=_cparams(("arbitrary", "arbitrary")),
        name="diff",
    )(lam2, qk, qk, vt, gs_full)


def _swa_kernel(sink_ref, q_ref, kp_ref, kc_ref, vtp_ref, vtc_ref, o_ref, *, t):
    n = pl.program_id(1)
    prow = lax.broadcasted_iota(I32, (WINDOW, t), 0)
    pcol = lax.broadcasted_iota(I32, (WINDOW, t), 1)
    mask_prev = (prow > pcol) & (pcol + jnp.where(n > 0, 0, t) < WINDOW)
    crow = lax.broadcasted_iota(I32, (t, t), 0)
    ccol = lax.broadcasted_iota(I32, (t, t), 1)
    mask_cur = (crow <= ccol) & (crow > ccol - WINDOW)
    vtp = vtp_ref[0, 0][:, t - WINDOW:t]
    vtc = vtc_ref[0, 0]
    for g in range(2):
        outs = []
        for r in range(2):
            h = 2 * g + r
            qz = jnp.where(_head_mask(r), q_ref[0, :, g * LANES:(g + 1) * LANES], jnp.zeros((), BF16))
            sp = jnp.where(mask_prev, _dot_nt(kp_ref[0, :, g * LANES:(g + 1) * LANES], qz), -jnp.inf)
            sc = jnp.where(mask_cur, _dot_nt(kc_ref[0, :, g * LANES:(g + 1) * LANES], qz), -jnp.inf)
            sink = sink_ref[h] * LOG2E
            m = jnp.maximum(jnp.maximum(jnp.max(sp, axis=0, keepdims=True),
                                        jnp.max(sc, axis=0, keepdims=True)), sink)
            pp = jnp.exp2(sp - m)
            pc = jnp.exp2(sc - m)
            den = jnp.sum(pp, axis=0, keepdims=True) + jnp.sum(pc, axis=0, keepdims=True) + jnp.exp2(sink - m)
            rows = slice(h * HEAD_DIM, (h + 1) * HEAD_DIM)
            o = (jnp.dot(vtp[rows, :], pp.astype(BF16), preferred_element_type=F32)
                 + jnp.dot(vtc[rows, :], pc.astype(BF16), preferred_element_type=F32))
            outs.append(o / den)
        o_ref[0, :, g * LANES:(g + 1) * LANES] = jnp.concatenate(outs, axis=0).T.astype(BF16)


def _swa_call(sinks, qk, vt, *, t):
    b, s, _ = qk.shape
    nt = s // t
    per = t // WINDOW
    return pl.pallas_call(
        functools.partial(_swa_kernel, t=t),
        grid=(b, nt),
        in_specs=[pl.BlockSpec(memory_space=pltpu.SMEM),
                  pl.BlockSpec((1, t, GROUP), lambda bi, ni: (bi, ni, QK_BLK["d_q"])),
                  pl.BlockSpec((1, WINDOW, GROUP), lambda bi, ni: (bi, jnp.maximum(ni * per - 1, 0), QK_BLK["d_k"])),
                  pl.BlockSpec((1, t, GROUP), lambda bi, ni: (bi, ni, QK_BLK["d_k"])),
                  pl.BlockSpec((1, 1, GROUP, t), lambda bi, ni: (bi, jnp.maximum(ni - 1, 0), V_ROW_BLK["d_v"], 0)),
                  pl.BlockSpec((1, 1, GROUP, t), lambda bi, ni: (bi, ni, V_ROW_BLK["d_v"], 0))],
        out_specs=pl.BlockSpec((1, t, GROUP), lambda bi, ni: (bi, ni, 0)),
        out_shape=jax.ShapeDtypeStruct((b, s, GROUP), BF16),
        compiler_params=_cparams(("arbitrary", "arbitrary")),
        name="swa",
    )(sinks, qk, qk, qk, vt, vt)


def _post_kernel(oa_ref, ob_ref, oc_ref, od_ref, x_ref, p_ref, wo_ref, gmix_ref, gpre_ref, wu_ref, wd_ref, gpost_ref,
                 wg_ref, wp_ref, y_ref, *, chunk):
    acc = jnp.zeros(x_ref.shape[1:], F32)
    for i, o_ref in enumerate((oa_ref, ob_ref, oc_ref, od_ref)):
        acc = acc + jnp.dot(o_ref[0], wo_ref[i * GROUP:(i + 1) * GROUP, :], preferred_element_type=F32)
    x = x_ref[0] + _rms(acc, gmix_ref[...])
    h = _rms(x, gpre_ref[...]).astype(BF16)
    acc = jnp.zeros(x.shape, F32)
    for c in range(D_FF // chunk):
        u = jnp.dot(h, wu_ref[:, c * chunk:(c + 1) * chunk], preferred_element_type=F32)
        u = jnp.square(jnp.maximum(u, 0.0)).astype(BF16)
        acc = acc + jnp.dot(u, wd_ref[c * chunk:(c + 1) * chunk, :], preferred_element_type=F32)
    x = x + _rms(acc, gpost_ref[...])
    gate = jax.nn.sigmoid(jnp.dot(x.astype(BF16), wg_ref[...], preferred_element_type=F32))
    emb = jnp.dot(p_ref[0].astype(BF16), wp_ref[...], preferred_element_type=F32)
    y_ref[0] = x + gate * emb


def _post_call(tok_inputs, full_inputs, *, tm):
    b, s, d = tok_inputs[4].shape
    tok = lambda a: pl.BlockSpec((1, tm, a.shape[2]), lambda bi, ti: (bi, ti, 0))
    full = lambda a: pl.BlockSpec(a.shape, lambda bi, ti: (0,) * a.ndim, pipeline_mode=pl.Buffered(1))
    return pl.pallas_call(
        functools.partial(_post_kernel, chunk=1024),
        grid=(b, s // tm),
        in_specs=[tok(a) for a in tok_inputs] + [full(a) for a in full_inputs],
        out_specs=pl.BlockSpec((1, tm, d), lambda bi, ti: (bi, ti, 0)),
        out_shape=jax.ShapeDtypeStruct((b, s, d), F32),
        compiler_params=_cparams(("arbitrary", "arbitrary")),
        name="post",
    )(*tok_inputs, *full_inputs)


def _dup_kv(w):
    return jnp.concatenate([w[:, :HEAD_DIM], w[:, :HEAD_DIM], w[:, HEAD_DIM:], w[:, HEAD_DIM:]], axis=1)


def _prep_w_in(w):
    sec = lambda n: w[:, _SEC[n][0]:_SEC[n][0] + _SEC[n][1]]
    cols = [_dup_kv(sec(n)) if n in ("d_k", "d_v") else sec(n) for n in QK_GROUPS + V_GROUPS]
    w_main = jnp.concatenate(cols, axis=1).astype(BF16)
    pad = jnp.zeros((w.shape[0], LANES - IDX_DIM - IDX_HEADS - 4), w.dtype)
    w_idx = jnp.concatenate([sec("iq"), sec("ik"), sec("iw"), sec("b_f"), pad], axis=1)
    w_hi = w_idx.astype(BF16)
    w_lo = (w_idx - w_hi.astype(F32)).astype(BF16)
    return w_main, jnp.concatenate([w_hi, w_hi, w_lo], axis=0)


def _rope_tables(positions):
    pos = positions.astype(F32)[..., None]
    lane = jnp.arange(LANES)

    def tabs(dim):
        half = dim // 2
        inv_freq = ROPE_THETA ** (-jnp.arange(half, dtype=F32) / half)
        ang = pos * inv_freq
        sign = jnp.where((lane % dim) < half, -1.0, 1.0).astype(F32)
        reps = (1, 1, LANES // half)
        return jnp.tile(jnp.cos(ang), reps), jnp.tile(jnp.sin(ang), reps) * sign

    c64, s64 = tabs(HEAD_DIM)
    c32, s32 = tabs(DIFF_DIM)
    return c64, s64, c32, s32


def kernel(x, p, positions, w_in, b_forget, lambda_q1, lambda_k1, lambda_q2, lambda_k2, diff_subln, sinks,
           w_out, norm_pre_mix, norm_post_mix, norm_pre_mlp, norm_post_mlp, w_mlp_up, w_mlp_down,
           w_ple_proj, w_ple_gate):
    b, s, d = x.shape
    depth = w_in.shape[0]
    t = min(256, s)
    assert s % (CHUNK_TILES * t) == 0, "sequence length must be a multiple of the attention key chunk"
    topk = min(TOPK_MAX, s // 4)
    tabs = _rope_tables(positions)
    row = lambda v: v.reshape(1, -1).astype(F32)

    for i in range(depth):
        lam_init = 0.8 - 0.6 * math.exp(-0.3 * i)
        w_main, w_idx = _prep_w_in(w_in[i])
        bf_row = jnp.zeros((1, LANES), F32).at[0, MISC_F:MISC_F + 4].set(b_forget[i])
        qk, vt, iq, ik, iw = _proj_call(x, row(norm_pre_mix[i]), w_main, w_idx, tabs, bf_row, tm=t)

        o_a = _dsa_call(qk, vt, iq, ik, iw, t=t, topk=topk)
        o_b = _fox_call(qk, vt, t=t)
        lam = (jnp.exp(jnp.sum(lambda_q1[i] * lambda_k1[i])) - jnp.exp(jnp.sum(lambda_q2[i] * lambda_k2[i]))
               + lam_init)
        lam2 = jnp.stack([lam, jnp.asarray(1.0 - lam_init, F32)]).astype(F32)
        gs_full = jnp.broadcast_to(jnp.concatenate([diff_subln[i], diff_subln[i]]).astype(F32)[:, None], (LANES, t))
        o_c = _diff_call(lam2, qk, vt, gs_full, t=t)
        o_d = _swa_call(sinks[i].astype(F32), qk, vt, t=t)

        x = _post_call([o_a, o_b, o_c, o_d, x, p[i]],
                       [w_out[i].astype(BF16), row(norm_post_mix[i]), row(norm_pre_mlp[i]),
                        w_mlp_up[i].astype(BF16), w_mlp_down[i].astype(BF16), row(norm_post_mlp[i]),
                        w_ple_gate[i].astype(BF16), w_ple_proj[i].astype(BF16)], tm=t)
    return x
```

```python
import functools
import math

import jax
import jax.numpy as jnp
from jax import lax
from jax.experimental import pallas as pl
from jax.experimental.pallas import tpu as pltpu

F32, BF16, I32 = jnp.float32, jnp.bfloat16, jnp.int32

D_MODEL = 1024
HEAD_DIM = 64
DIFF_DIM = 32
IDX_HEADS = 8
IDX_DIM = 64
TOPK_MAX = 256
WINDOW = 128
D_FF = 4 * D_MODEL
D_PLE = 256
ROPE_THETA = 10000.0
EPS = 1e-6
NEG_INF = -1e30
LANES = 128
SUBLANES = 8
GROUP = 256
INT_MIN = -2147483648
FIELD_BITS = 15
FIELD_GUARDS = -2147450880
LOW_BITS = 9

_SEC = {}
_o = 0
for _name, _w in (("a_q", 256), ("a_k", 256), ("a_v", 256), ("iq", 512), ("ik", 64), ("iw", 8),
                  ("b_q", 256), ("b_k", 256), ("b_v", 256), ("b_f", 4),
                  ("c_q", 256), ("c_k", 256), ("c_v", 256), ("d_q", 256), ("d_k", 128), ("d_v", 128)):
    _SEC[_name] = (_o, _w)
    _o += _w
D_IN = _o

QK_GROUPS = ("a_q", "a_k", "b_q", "b_k", "c_q", "c_k", "d_q", "d_k")
V_GROUPS = ("a_v", "b_v", "c_v", "d_v")
MAIN_W = GROUP * (len(QK_GROUPS) + len(V_GROUPS))
V_BASE = GROUP * len(QK_GROUPS)
QK_BLK = {"a_q": 0, "a_k": 1, "b_q": 2, "b_k": 3, "c_q": 5, "c_k": 6, "d_q": 7, "d_k": 8}
QK_W = GROUP * 9
V_ROW_BLK = {n: i for i, n in enumerate(V_GROUPS)}
ROPE64_GROUPS = ("a_q", "a_k", "d_q", "d_k")
ROPE32_GROUPS = ("c_q", "c_k")
LOG2E = math.log2(math.e)
Q_SCALE = {"a_q": HEAD_DIM ** -0.5 * LOG2E, "b_q": HEAD_DIM ** -0.5 * LOG2E, "c_q": DIFF_DIM ** -0.5 * LOG2E,
           "d_q": HEAD_DIM ** -0.5 * LOG2E}
IDX_W = 512 + LANES
MISC_IW = 64
MISC_F = 72
IDX_CAT = 256

CHUNK_TILES = 4
CHUNK_TILES_DIFF = 2
IDX_CHUNK_TILES = 2
assert CHUNK_TILES == 2 * IDX_CHUNK_TILES
VMEM_LIMIT = 56 * 1024 * 1024
RESIDENT = pl.Buffered(1)


def _cparams(sem):
    return pltpu.CompilerParams(dimension_semantics=sem, vmem_limit_bytes=VMEM_LIMIT)


def _rms(x, g):
    return x * lax.rsqrt(jnp.mean(x * x, axis=-1, keepdims=True) + EPS) * g


def _dot_nt(a, b):
    return lax.dot_general(a, b, (((1,), (1,)), ((), ())), preferred_element_type=F32)


def _dot(a, b):
    return jnp.dot(a, b, preferred_element_type=F32)


def _transpose_bf16(a):
    return a.astype(F32).T.astype(BF16)


def _bf16_part(x):
    return x.astype(BF16).astype(F32)


def _split2(x):
    hi = _bf16_part(x)
    return hi, _bf16_part(x - hi)


def _split3(x):
    hi = _bf16_part(x)
    mid = _bf16_part(x - hi)
    return hi, mid, _bf16_part(x - hi - mid)


def _rope_chunk(xc, cos, sin_signed, half, lane):
    fwd = pltpu.roll(xc, LANES - half, axis=1)
    bwd = pltpu.roll(xc, half, axis=1)
    partner = jnp.where((lane % (2 * half)) < half, fwd, bwd)
    return xc * cos + partner * sin_signed


def _proj_kernel(x_ref, g_ref, wm_ref, wi_ref, c64_ref, s64_ref, c32_ref, s32_ref, bf_ref,
                 qk_ref, vt_ref, iq_ref, ik_ref, iw_ref, carry_ref, *, tm):
    t = pl.program_id(1)
    x = x_ref[0]
    h = _rms(x, g_ref[...])
    pm = jnp.dot(h.astype(BF16), wm_ref[...], preferred_element_type=F32)
    h_hi, h_lo = _split2(h)
    pi = jnp.dot(jnp.concatenate([h_hi, h_lo, h_hi], axis=1).astype(BF16), wi_ref[...],
                 preferred_element_type=F32)
    lane = lax.broadcasted_iota(I32, (1, LANES), 1)
    lo_half = lane < HEAD_DIM
    c64, s64, c32, s32 = c64_ref[0], s64_ref[0], c32_ref[0], s32_ref[0]

    misc = pi[:, 512:512 + LANES]
    z = misc + bf_ref[...]
    logf = jnp.minimum(z, 0.0) - jnp.log1p(jnp.exp(-jnp.abs(z)))
    logf = jnp.where((lane >= MISC_F) & (lane < MISC_F + 4), logf, 0.0)
    tri = (lax.broadcasted_iota(I32, (tm, tm), 1) <= lax.broadcasted_iota(I32, (tm, tm), 0)).astype(BF16)

    @pl.when(t == 0)
    def _():
        carry_ref[...] = jnp.zeros_like(carry_ref)

    pieces = jnp.dot(tri, jnp.concatenate(_split3(logf), axis=1).astype(BF16), preferred_element_type=F32)
    cum = pieces[:, 0:LANES] + pieces[:, LANES:2 * LANES] + pieces[:, 2 * LANES:3 * LANES] + carry_ref[...]
    carry_ref[...] = cum[tm - 1:tm, :]
    nhi, nmid, nlo = _split3(-LOG2E * cum)
    gate_bias = jnp.where(lane < 4, pltpu.roll(nhi, LANES - MISC_F, axis=1),
                          jnp.where(lane < 8, pltpu.roll(nmid, LANES - MISC_F + 4, axis=1),
                                    jnp.where(lane < 12, pltpu.roll(nlo, LANES - MISC_F + 8, axis=1),
                                              jnp.zeros_like(nlo))))

    for gi, name in enumerate(QK_GROUPS):
        for c in range(GROUP // LANES):
            lo = gi * GROUP + c * LANES
            v = pm[:, lo:lo + LANES]
            if name in ROPE64_GROUPS:
                v = _rope_chunk(v, c64, s64, HEAD_DIM // 2, lane)
            elif name in ROPE32_GROUPS:
                v = _rope_chunk(v, c32, s32, DIFF_DIM // 2, lane)
            if name in Q_SCALE:
                v = v * Q_SCALE[name]
            if name == "b_k":
                out = (QK_BLK[name] + c) * GROUP
                qk_ref[0, :, out:out + LANES] = v.astype(BF16)
                qk_ref[0, :, out + LANES:out + GROUP] = gate_bias.astype(BF16)
            else:
                out = QK_BLK[name] * GROUP + c * LANES
                qk_ref[0, :, out:out + LANES] = v.astype(BF16)

    vt_ref[0, 0] = pm[:, V_BASE:V_BASE + len(V_GROUPS) * GROUP].T.astype(BF16)

    def cat_q(q):
        hi, lo = _split2(q)
        return jnp.where(lo_half, hi, pltpu.roll(lo, HEAD_DIM, axis=1)), hi

    for c in range(512 // LANES):
        v = _rope_chunk(pi[:, c * LANES:(c + 1) * LANES], c64, s64, IDX_DIM // 2, lane)
        for r, q in enumerate((jnp.where(lo_half, v, 0.0), jnp.where(lo_half, pltpu.roll(v, HEAD_DIM, axis=1), 0.0))):
            a, b2 = cat_q(q)
            iq_ref[0, 2 * c + r, :, 0:LANES] = a.astype(BF16)
            iq_ref[0, 2 * c + r, :, LANES:IDX_CAT] = b2.astype(BF16)
    ik = jnp.where(lo_half, _rope_chunk(misc, c64, s64, IDX_DIM // 2, lane), 0.0)
    khi, klo = _split2(ik)
    ik_ref[0, :, 0:LANES] = jnp.where(lo_half, khi, pltpu.roll(khi, HEAD_DIM, axis=1)).astype(BF16)
    ik_ref[0, :, LANES:IDX_CAT] = klo.astype(BF16)
    iw_t = (misc * (IDX_HEADS ** -0.5 * IDX_DIM ** -0.5)).T
    iw_ref[0, 0] = iw_t[MISC_IW:MISC_IW + IDX_HEADS, :]


def _proj_call(x, g, w_main, w_idx, tabs, bf_row, *, tm):
    b, s, d = x.shape
    tok = lambda w: pl.BlockSpec((1, tm, w), lambda bi, ti: (bi, ti, 0))
    full = lambda a: pl.BlockSpec(a.shape, lambda bi, ti: (0,) * a.ndim)
    return pl.pallas_call(
        functools.partial(_proj_kernel, tm=tm),
        grid=(b, s // tm),
        in_specs=[tok(d), full(g), full(w_main), full(w_idx), tok(LANES), tok(LANES), tok(LANES), tok(LANES),
                  full(bf_row)],
        out_specs=[tok(QK_W),
                   pl.BlockSpec((1, 1, len(V_GROUPS) * GROUP, tm), lambda bi, ti: (bi, ti, 0, 0)),
                   pl.BlockSpec((1, IDX_HEADS, tm, IDX_CAT), lambda bi, ti: (bi, 0, ti, 0)),
                   tok(IDX_CAT),
                   pl.BlockSpec((1, 1, IDX_HEADS, tm), lambda bi, ti: (bi, ti, 0, 0))],
        out_shape=[jax.ShapeDtypeStruct((b, s, QK_W), BF16),
                   jax.ShapeDtypeStruct((b, s // tm, len(V_GROUPS) * GROUP, tm), BF16),
                   jax.ShapeDtypeStruct((b, IDX_HEADS, s, IDX_CAT), BF16),
                   jax.ShapeDtypeStruct((b, s, IDX_CAT), BF16),
                   jax.ShapeDtypeStruct((b, s // tm, IDX_HEADS, tm), F32)],
        scratch_shapes=[pltpu.VMEM((1, LANES), F32)],
        compiler_params=_cparams(("arbitrary", "arbitrary")),
        name="proj",
    )(x, g, w_main, w_idx, *tabs, bf_row)


def _softmax_step(s, m_ref, l_ref, idx):
    m_old = m_ref[idx]
    m_new = jnp.maximum(m_old, jnp.max(s, axis=0, keepdims=True))
    alpha = jnp.exp2(m_old - m_new)
    p = jnp.exp2(s - m_new)
    l_ref[idx] = alpha * l_ref[idx] + jnp.sum(p, axis=0, keepdims=True)
    m_ref[idx] = m_new
    return p, alpha


def _init_state(m_ref, l_ref, acc_ref):
    m_ref[...] = jnp.full(m_ref.shape, NEG_INF, F32)
    l_ref[...] = jnp.zeros(l_ref.shape, F32)
    acc_ref[...] = jnp.zeros(acc_ref.shape, F32)


def _causal_chunk(w, t):
    return lax.broadcasted_iota(I32, (w * t, t), 0) <= lax.broadcasted_iota(I32, (w * t, t), 1) + (w - 1) * t


def _head_mask(r):
    lane = lax.broadcasted_iota(I32, (1, LANES), 1)
    return (lane >= HEAD_DIM) if r else (lane < HEAD_DIM)


def _pipelined_key_chunks(n, scores_fn, rest_fn, width):
    n_full = n // width
    rest = n - n_full * width
    odd = n_full & 1

    @pl.when(odd == 1)
    def _():
        scores_fn(0, 1)
        scores_fn(width, 0)
        rest_fn(0, width, False, 1)

    @pl.when(odd == 0)
    def _():
        scores_fn(0, 0)

    def _body(pair, carry):
        j0 = (odd + 2 * pair) * width
        scores_fn(j0 + width, 1)
        rest_fn(j0, width, False, 0)
        scores_fn(j0 + 2 * width, 0)
        rest_fn(j0 + width, width, False, 1)
        return carry

    lax.fori_loop(0, (n_full - odd) >> 1, _body, 0)
    for k in range(width):
        @pl.when(rest == k)
        def _(k=k):
            rest_fn(n - k, k + 1, True, 0)


def _accumulate(acc_ref, a, r, alpha, vt_tiles, p, t):
    rows = slice(r * HEAD_DIM, (r + 1) * HEAD_DIM)
    pb = p.astype(BF16)
    pv = jnp.dot(vt_tiles[0], pb[0:t], preferred_element_type=F32)
    for i in range(1, len(vt_tiles)):
        pv = pv + jnp.dot(vt_tiles[i], pb[i * t:(i + 1) * t], preferred_element_type=F32)
    acc_ref[a, rows, :] = acc_ref[a, rows, :] * alpha + pv


def _normalised_t(acc_ref, a, l_ref, idx0, idx1):
    return jnp.concatenate([acc_ref[a, 0:HEAD_DIM, :] / l_ref[idx0],
                            acc_ref[a, HEAD_DIM:LANES, :] / l_ref[idx1]], axis=0)


def _dsa_kernel(iq_ref, iw_ref, ik_ref, q_ref, k_ref, vt_ref, o_ref,
                key_ref, pack_ref, m_ref, l_ref, acc_ref, carry_ref, iqt_ref, qzt_ref, s_ref, thr_ref, nge_ref, *, t, topk):
    n = pl.program_id(1)
    for h in range(IDX_HEADS):
        iqt_ref[h] = _transpose_bf16(iq_ref[0, h])
    for h in range(4):
        qzt_ref[h] = _transpose_bf16(jnp.where(_head_mask(h % 2), q_ref[0, :, (h // 2) * LANES:(h // 2 + 1) * LANES],
                                               jnp.zeros((), BF16)))

    def index_dots(j0, buf):
        start = pl.multiple_of(j0 * t, t)
        ik = ik_ref[0, pl.ds(start, IDX_CHUNK_TILES * t), :]
        for h in range(IDX_HEADS):
            s_ref[buf, h] = _dot(ik, iqt_ref[h])

    def index_keys(j0, w, diag, buf):
        sc = jnp.zeros((w * t, t), F32)
        for h in range(IDX_HEADS):
            sc = sc + iw_ref[0, 0, h:h + 1, :] * jnp.maximum(s_ref[buf, h, 0:w * t, :], 0.0)
        if diag:
            sc = jnp.where(_causal_chunk(w, t), sc, NEG_INF)
        sc = jnp.where(sc == 0.0, 0.0, sc)
        bits = lax.bitcast_convert_type(sc, I32)
        key = bits ^ ((bits >> 31) & jnp.int32(0x7FFFFFFF))
        top = lax.shift_right_logical(key ^ jnp.int32(INT_MIN), 32 - FIELD_BITS)
        for i in range(w):
            key_ref[j0 + i] = key[i * t:(i + 1) * t]
            pack_ref[j0 + i] = (jnp.left_shift(top[i * t:i * t + t // 2], 16) | top[i * t + t // 2:(i + 1) * t]
                                | jnp.int32(FIELD_GUARDS))

    _pipelined_key_chunks(n, index_dots, index_keys, IDX_CHUNK_TILES)

    def tree_sum(parts):
        while len(parts) > 1:
            parts = [parts[i] + parts[i + 1] for i in range(0, len(parts), 2)]
        return parts[0]

    def count(pred):
        def body(j, acc):
            c = jnp.where(pred(key_ref[j]), 1.0, 0.0)
            return acc + tree_sum([c[r * SUBLANES:(r + 1) * SUBLANES, :] for r in range(t // SUBLANES)])
        acc = lax.fori_loop(0, n + 1, body, jnp.zeros((SUBLANES, t), F32))
        return jnp.sum(acc, axis=0, keepdims=True)

    def count_top(cand):
        both = jnp.left_shift(cand, 16) | cand

        def body(j, acc):
            z = pack_ref[j] - both
            c = lax.shift_right_logical(z, FIELD_BITS) & jnp.int32(0x00010001)
            return acc + tree_sum([c[r * SUBLANES:(r + 1) * SUBLANES, :] for r in range(t // 2 // SUBLANES)])
        acc = lax.fori_loop(0, n + 1, body, jnp.zeros((SUBLANES, t), I32))
        per_lane = lax.shift_right_logical(acc, 16) + (acc & jnp.int32(0xFFFF))
        return jnp.sum(per_lane.astype(F32), axis=0, keepdims=True)

    def bisect(n_bits, low_bit, count_fn, carry):
        def body(i, carry):
            u, n_ge, n_rej = carry
            cand = u | jnp.left_shift(jnp.int32(1), low_bit + n_bits - 1 - i)
            cnt = count_fn(cand)
            keep = cnt >= float(topk)
            return jnp.where(keep, cand, u), jnp.where(keep, cnt, n_ge), jnp.where(keep, n_rej, cnt)
        return lax.fori_loop(0, n_bits, body, carry)

    def count_full(cand):
        return count(lambda kt: kt >= (cand ^ jnp.int32(INT_MIN)))

    in_scope = ((n + 1) * t).astype(F32)
    zero_row = jnp.zeros((1, t), F32)
    top, n_ge, n_rej = bisect(FIELD_BITS, 0, count_top, (jnp.zeros((1, t), I32), zero_row + in_scope, zero_row))
    u, n_ge, n_rej = bisect(32 - FIELD_BITS - LOW_BITS, LOW_BITS, count_full,
                            (jnp.left_shift(top, 32 - FIELD_BITS), n_ge, n_rej))

    prefix = u ^ jnp.int32(INT_MIN)
    low_mask = jnp.int32((1 << LOW_BITS) - 1)
    big = jnp.int32(1 << (LOW_BITS + 1))

    def tree(parts, op):
        while len(parts) > 1:
            parts = [op(parts[i], parts[i + 1]) for i in range(0, len(parts), 2)]
        return parts[0]

    def band_body(j, carry):
        mx, mn, sm = carry
        kt = key_ref[j]
        inside = lax.shift_right_logical(kt ^ prefix, LOW_BITS) == 0
        low = kt & low_mask
        rows = lambda a: [a[r * SUBLANES:(r + 1) * SUBLANES, :] for r in range(t // SUBLANES)]
        return (jnp.maximum(mx, tree(rows(jnp.where(inside, low, -1)), jnp.maximum)),
                jnp.minimum(mn, tree(rows(jnp.where(inside, low, big)), jnp.minimum)),
                sm + tree(rows(jnp.where(inside, low, 0)), jnp.add))

    mx, mn, sm = lax.fori_loop(0, n + 1, band_body, (jnp.full((SUBLANES, t), -1, I32),
                                                       jnp.full((SUBLANES, t), 1 << (LOW_BITS + 1), I32),
                                                       jnp.zeros((SUBLANES, t), I32)))
    mx = jnp.max(mx.astype(F32), axis=0, keepdims=True)
    mn = jnp.min(mn.astype(F32), axis=0, keepdims=True)
    sm = jnp.sum(sm.astype(F32), axis=0, keepdims=True)
    short = n_ge < float(topk)
    in_band = n_ge - n_rej
    need = float(topk) - n_rej
    second = jnp.where(in_band > 2.5, sm - mx - mn, mn)
    thr_low = jnp.where(need < 1.5, mx, jnp.where(need < 2.5, second, mn))
    kept = (1.0 + jnp.where((in_band > 1.5) & (second >= thr_low), 1.0, 0.0)
            + jnp.where((in_band > 2.5) & (mn >= thr_low), 1.0, 0.0))
    resolved = jnp.min(jnp.where(short | (in_band < 3.5), 1.0, 0.0)) > 0.5

    @pl.when(resolved)
    def _():
        thr_ref[...] = jnp.where(short, prefix, prefix | thr_low.astype(I32))
        nge_ref[...] = jnp.where(short, n_ge, n_rej + kept)

    @pl.when(jnp.logical_not(resolved))
    def _():
        u_all, n_ge_all, _ = bisect(LOW_BITS, 0, count_full, (u, n_ge, n_rej))
        thr_ref[...] = u_all ^ jnp.int32(INT_MIN)
        nge_ref[...] = n_ge_all

    thr = thr_ref[...]
    has_ties = jnp.max(nge_ref[...]) > float(topk)

    _init_state(m_ref, l_ref, acc_ref)

    half_rows = IDX_CHUNK_TILES * t

    def attend(ties):
        if ties:
            carry_ref[...] = jnp.zeros(carry_ref.shape, F32)
            need = float(topk) - count(lambda kt: kt > thr)
            lower = (lax.broadcasted_iota(I32, (t, t), 1) <= lax.broadcasted_iota(I32, (t, t), 0)).astype(BF16)

        def scores(j0, buf):
            start = pl.multiple_of(j0 * t, t)
            ks = k_ref[0, pl.ds(start, CHUNK_TILES * t), :]
            for h in range(4):
                s = _dot(ks[:, (h // 2) * LANES:(h // 2 + 1) * LANES], qzt_ref[h])
                s_ref[buf, 2 * h] = s[0:half_rows]
                s_ref[buf, 2 * h + 1] = s[half_rows:2 * half_rows]

        def rest(j0, w, diag, buf):
            if ties:
                parts = []
                for i in range(w):
                    kt = key_ref[j0 + i]
                    eq = kt == thr
                    incl = jnp.dot(lower, jnp.where(eq, 1.0, 0.0).astype(BF16),
                                   preferred_element_type=F32) + carry_ref[...]
                    carry_ref[...] = incl[t - 1:t, :]
                    parts.append(jnp.where((kt > thr) | (eq & (incl <= need)), 1.0, 0.0))
                sel = (parts[0] if w == 1 else jnp.concatenate(parts, axis=0)) > 0.5
            else:
                kt = key_ref[j0] if w == 1 else jnp.concatenate([key_ref[j0 + i] for i in range(w)], axis=0)
                sel = kt >= thr
            if diag:
                sel = sel & _causal_chunk(w, t)
            for h in range(4):
                g, r = divmod(h, 2)
                if w * t <= half_rows:
                    s = s_ref[buf, 2 * h, 0:w * t, :]
                else:
                    s = jnp.concatenate([s_ref[buf, 2 * h], s_ref[buf, 2 * h + 1, 0:w * t - half_rows, :]], axis=0)
                p, alpha = _softmax_step(jnp.where(sel, s, -jnp.inf), m_ref, l_ref, h)
                _accumulate(acc_ref, g, r, alpha,
                            [vt_ref[0, j0 + i, h * HEAD_DIM:(h + 1) * HEAD_DIM, :] for i in range(w)], p, t)

        _pipelined_key_chunks(n, scores, rest, CHUNK_TILES)

    @pl.when(has_ties)
    def _():
        attend(True)

    @pl.when(jnp.logical_not(has_ties))
    def _():
        attend(False)

    for g in range(2):
        o_ref[0, :, g * LANES:(g + 1) * LANES] = _normalised_t(acc_ref, g, l_ref, 2 * g, 2 * g + 1).T.astype(BF16)


def _dsa_call(qk, vt, iq, ik, iw, *, t, topk):
    b, s, _ = qk.shape
    nt = s // t
    return pl.pallas_call(
        functools.partial(_dsa_kernel, t=t, topk=topk),
        grid=(b, nt),
        in_specs=[pl.BlockSpec((1, IDX_HEADS, t, IDX_CAT), lambda bi, ni: (bi, 0, ni, 0)),
                  pl.BlockSpec((1, 1, IDX_HEADS, t), lambda bi, ni: (bi, ni, 0, 0)),
                  pl.BlockSpec((1, s, IDX_CAT), lambda bi, ni: (bi, 0, 0), pipeline_mode=RESIDENT),
                  pl.BlockSpec((1, t, GROUP), lambda bi, ni: (bi, ni, QK_BLK["a_q"])),
                  pl.BlockSpec((1, s, GROUP), lambda bi, ni: (bi, 0, QK_BLK["a_k"]), pipeline_mode=RESIDENT),
                  pl.BlockSpec((1, nt, GROUP, t), lambda bi, ni: (bi, 0, V_ROW_BLK["a_v"], 0), pipeline_mode=RESIDENT)],
        out_specs=pl.BlockSpec((1, t, GROUP), lambda bi, ni: (bi, ni, 0)),
        out_shape=jax.ShapeDtypeStruct((b, s, GROUP), BF16),
        scratch_shapes=[pltpu.VMEM((nt, t, t), I32), pltpu.VMEM((nt, t // 2, t), I32),
                        pltpu.VMEM((4, 1, t), F32), pltpu.VMEM((4, 1, t), F32),
                        pltpu.VMEM((2, LANES, t), F32), pltpu.VMEM((1, t), F32),
                        pltpu.VMEM((IDX_HEADS, IDX_CAT, t), BF16), pltpu.VMEM((4, LANES, t), BF16),
                        pltpu.VMEM((2, IDX_HEADS, IDX_CHUNK_TILES * t, t), F32),
                        pltpu.VMEM((1, t), I32), pltpu.VMEM((1, t), F32)],
        compiler_params=_cparams(("arbitrary", "arbitrary")),
        name="dsa",
    )(iq, iw, ik, qk, qk, vt)


def _fox_kernel(q_ref, k0_ref, k1_ref, vt_ref, o_ref, m_ref, l_ref, acc_ref, qxt_ref, s_ref, *, t):
    n = pl.program_id(1)
    _init_state(m_ref, l_ref, acc_ref)
    lane = lax.broadcasted_iota(I32, (1, LANES), 1)
    for h in range(4):
        qz = jnp.where(_head_mask(h % 2), q_ref[0, :, (h // 2) * LANES:(h // 2 + 1) * LANES], jnp.zeros((), BF16))
        ones = jnp.where((lane == h) | (lane == 4 + h) | (lane == 8 + h), 1.0, 0.0).astype(BF16)
        qxt_ref[h] = _transpose_bf16(jnp.concatenate([qz, jnp.broadcast_to(ones, (t, LANES))], axis=1))
    k_refs = (k0_ref, k1_ref)

    def scores(j0, buf):
        start = pl.multiple_of(j0 * t, t)
        for h in range(4):
            s_ref[buf, h] = _dot(k_refs[h // 2][0, pl.ds(start, CHUNK_TILES * t), :], qxt_ref[h])

    def rest(j0, w, diag, buf):
        for h in range(4):
            g, r = divmod(h, 2)
            s = s_ref[buf, h, 0:w * t, :]
            if diag:
                s = jnp.where(_causal_chunk(w, t), s, -jnp.inf)
            p, alpha = _softmax_step(s, m_ref, l_ref, h)
            _accumulate(acc_ref, g, r, alpha,
                        [vt_ref[0, j0 + i, h * HEAD_DIM:(h + 1) * HEAD_DIM, :] for i in range(w)], p, t)

    _pipelined_key_chunks(n, scores, rest, CHUNK_TILES)
    for g in range(2):
        o_ref[0, :, g * LANES:(g + 1) * LANES] = _normalised_t(acc_ref, g, l_ref, 2 * g, 2 * g + 1).T.astype(BF16)


def _fox_call(qk, vt, *, t):
    b, s, _ = qk.shape
    nt = s // t
    return pl.pallas_call(
        functools.partial(_fox_kernel, t=t),
        grid=(b, nt),
        in_specs=[pl.BlockSpec((1, t, GROUP), lambda bi, ni: (bi, ni, QK_BLK["b_q"])),
                  pl.BlockSpec((1, s, GROUP), lambda bi, ni: (bi, 0, QK_BLK["b_k"]), pipeline_mode=RESIDENT),
                  pl.BlockSpec((1, s, GROUP), lambda bi, ni: (bi, 0, QK_BLK["b_k"] + 1), pipeline_mode=RESIDENT),
                  pl.BlockSpec((1, nt, GROUP, t), lambda bi, ni: (bi, 0, V_ROW_BLK["b_v"], 0), pipeline_mode=RESIDENT)],
        out_specs=pl.BlockSpec((1, t, GROUP), lambda bi, ni: (bi, ni, 0)),
        out_shape=jax.ShapeDtypeStruct((b, s, GROUP), BF16),
        scratch_shapes=[pltpu.VMEM((4, 1, t), F32), pltpu.VMEM((4, 1, t), F32), pltpu.VMEM((2, LANES, t), F32),
                        pltpu.VMEM((4, GROUP, t), BF16), pltpu.VMEM((2, 4, CHUNK_TILES * t, t), F32)],
        compiler_params=_cparams(("arbitrary", "arbitrary")),
        name="fox",
    )(qk, qk, qk, vt)


def _diff_kernel(lam_ref, q_ref, k_ref, vt_ref, gs_ref, o_ref, m_ref, l_ref, acc_ref, qzt_ref, s_ref, *, t):
    n = pl.program_id(1)
    _init_state(m_ref, l_ref, acc_ref)
    lane = lax.broadcasted_iota(I32, (1, LANES), 1)
    for h in range(4):
        for mm in range(2):
            lo = (h % 2) * HEAD_DIM + mm * DIFF_DIM
            qzt_ref[2 * h + mm] = _transpose_bf16(
                jnp.where((lane >= lo) & (lane < lo + DIFF_DIM),
                          q_ref[0, :, (h // 2) * LANES:(h // 2 + 1) * LANES], jnp.zeros((), BF16)))

    def scores(j0, buf):
        start = pl.multiple_of(j0 * t, t)
        ks = k_ref[0, pl.ds(start, CHUNK_TILES_DIFF * t), :]
        for i in range(8):
            s_ref[buf, i] = _dot(ks[:, (i // 4) * LANES:(i // 4 + 1) * LANES], qzt_ref[i])

    def rest(j0, w, diag, buf):
        for h in range(4):
            g, r = divmod(h, 2)
            vts = [vt_ref[0, j0 + i, h * HEAD_DIM:(h + 1) * HEAD_DIM, :] for i in range(w)]
            for mm in range(2):
                s = s_ref[buf, 2 * h + mm, 0:w * t, :]
                if diag:
                    s = jnp.where(_causal_chunk(w, t), s, -jnp.inf)
                p, alpha = _softmax_step(s, m_ref, l_ref, 2 * h + mm)
                _accumulate(acc_ref, 2 * mm + g, r, alpha, vts, p, t)

    _pipelined_key_chunks(n, scores, rest, CHUNK_TILES_DIFF)
    lam = lam_ref[0]
    out_scale = lam_ref[1]
    for g in range(2):
        o1 = _normalised_t(acc_ref, g, l_ref, 4 * g, 4 * g + 2)
        o2 = _normalised_t(acc_ref, 2 + g, l_ref, 4 * g + 1, 4 * g + 3)
        o = o1 - lam * o2
        sq = o * o
        ms = jnp.concatenate(
            [jnp.broadcast_to(jnp.mean(sq[r * HEAD_DIM:(r + 1) * HEAD_DIM], axis=0, keepdims=True), (HEAD_DIM, t))
             for r in range(2)], axis=0)
        y = o * lax.rsqrt(ms + EPS) * gs_ref[...] * out_scale
        o_ref[0, :, g * LANES:(g + 1) * LANES] = y.T.astype(BF16)


def _diff_call(lam2, qk, vt, gs_full, *, t):
    b, s, _ = qk.shape
    nt = s // t
    return pl.pallas_call(
        functools.partial(_diff_kernel, t=t),
        grid=(b, nt),
        in_specs=[pl.BlockSpec(memory_space=pltpu.SMEM),
                  pl.BlockSpec((1, t, GROUP), lambda bi, ni: (bi, ni, QK_BLK["c_q"])),
                  pl.BlockSpec((1, s, GROUP), lambda bi, ni: (bi, 0, QK_BLK["c_k"]), pipeline_mode=RESIDENT),
                  pl.BlockSpec((1, nt, GROUP, t), lambda bi, ni: (bi, 0, V_ROW_BLK["c_v"], 0), pipeline_mode=RESIDENT),
                  pl.BlockSpec((LANES, t), lambda bi, ni: (0, 0))],
        out_specs=pl.BlockSpec((1, t, GROUP), lambda bi, ni: (bi, ni, 0)),
        out_shape=jax.ShapeDtypeStruct((b, s, GROUP), BF16),
        scratch_shapes=[pltpu.VMEM((8, 1, t), F32), pltpu.VMEM((8, 1, t), F32), pltpu.VMEM((4, LANES, t), F32),
                        pltpu.VMEM((8, LANES, t), BF16), pltpu.VMEM((2, 8, CHUNK_TILES_DIFF * t, t), F32)],
        compiler_params=_cparams(("arbitrary", "arbitrary")),
        name="diff",
    )(lam2, qk, qk, vt, gs_full)


def _swa_kernel(sink_ref, q_ref, kp_ref, kc_ref, vtp_ref, vtc_ref, o_ref, *, t):
    n = pl.program_id(1)
    prow = lax.broadcasted_iota(I32, (WINDOW, t), 0)
    pcol = lax.broadcasted_iota(I32, (WINDOW, t), 1)
    mask_prev = (prow > pcol) & (pcol + jnp.where(n > 0, 0, t) < WINDOW)
    crow = lax.broadcasted_iota(I32, (t, t), 0)
    ccol = lax.broadcasted_iota(I32, (t, t), 1)
    mask_cur = (crow <= ccol) & (crow > ccol - WINDOW)
    vtp = vtp_ref[0, 0][:, t - WINDOW:t]
    vtc = vtc_ref[0, 0]
    for g in range(2):
        outs = []
        for r in range(2):
            h = 2 * g + r
            qz = jnp.where(_head_mask(r), q_ref[0, :, g * LANES:(g + 1) * LANES], jnp.zeros((), BF16))
            sp = jnp.where(mask_prev, _dot_nt(kp_ref[0, :, g * LANES:(g + 1) * LANES], qz), -jnp.inf)
            sc = jnp.where(mask_cur, _dot_nt(kc_ref[0, :, g * LANES:(g + 1) * LANES], qz), -jnp.inf)
            sink = sink_ref[h] * LOG2E
            m = jnp.maximum(jnp.maximum(jnp.max(sp, axis=0, keepdims=True),
                                        jnp.max(sc, axis=0, keepdims=True)), sink)
            pp = jnp.exp2(sp - m)
            pc = jnp.exp2(sc - m)
            den = jnp.sum(pp, axis=0, keepdims=True) + jnp.sum(pc, axis=0, keepdims=True) + jnp.exp2(sink - m)
            rows = slice(h * HEAD_DIM, (h + 1) * HEAD_DIM)
            o = (jnp.dot(vtp[rows, :], pp.astype(BF16), preferred_element_type=F32)
                 + jnp.dot(vtc[rows, :], pc.astype(BF16), preferred_element_type=F32))
            outs.append(o / den)
        o_ref[0, :, g * LANES:(g + 1) * LANES] = jnp.concatenate(outs, axis=0).T.astype(BF16)


def _swa_call(sinks, qk, vt, *, t):
    b, s, _ = qk.shape
    nt = s // t
    per = t // WINDOW
    return pl.pallas_call(
        functools.partial(_swa_kernel, t=t),
        grid=(b, nt),
        in_specs=[pl.BlockSpec(memory_space=pltpu.SMEM),
                  pl.BlockSpec((1, t, GROUP), lambda bi, ni: (bi, ni, QK_BLK["d_q"])),
                  pl.BlockSpec((1, WINDOW, GROUP), lambda bi, ni: (bi, jnp.maximum(ni * per - 1, 0), QK_BLK["d_k"])),
                  pl.BlockSpec((1, t, GROUP), lambda bi, ni: (bi, ni, QK_BLK["d_k"])),
                  pl.BlockSpec((1, 1, GROUP, t), lambda bi, ni: (bi, jnp.maximum(ni - 1, 0), V_ROW_BLK["d_v"], 0)),
                  pl.BlockSpec((1, 1, GROUP, t), lambda bi, ni: (bi, ni, V_ROW_BLK["d_v"], 0))],
        out_specs=pl.BlockSpec((1, t, GROUP), lambda bi, ni: (bi, ni, 0)),
        out_shape=jax.ShapeDtypeStruct((b, s, GROUP), BF16),
        compiler_params=_cparams(("arbitrary", "arbitrary")),
        name="swa",
    )(sinks, qk, qk, qk, vt, vt)


def _post_kernel(oa_ref, ob_ref, oc_ref, od_ref, x_ref, p_ref, wo_ref, gmix_ref, gpre_ref, wu_ref, wd_ref, gpost_ref,
                 wg_ref, wp_ref, y_ref, *, chunk):
    acc = jnp.zeros(x_ref.shape[1:], F32)
    for i, o_ref in enumerate((oa_ref, ob_ref, oc_ref, od_ref)):
        acc = acc + jnp.dot(o_ref[0], wo_ref[i * GROUP:(i + 1) * GROUP, :], preferred_element_type=F32)
    x = x_ref[0] + _rms(acc, gmix_ref[...])
    h = _rms(x, gpre_ref[...]).astype(BF16)
    acc = jnp.zeros(x.shape, F32)
    for c in range(D_FF // chunk):
        u = jnp.dot(h, wu_ref[:, c * chunk:(c + 1) * chunk], preferred_element_type=F32)
        u = jnp.square(jnp.maximum(u, 0.0)).astype(BF16)
        acc = acc + jnp.dot(u, wd_ref[c * chunk:(c + 1) * chunk, :], preferred_element_type=F32)
    x = x + _rms(acc, gpost_ref[...])
    gate = jax.nn.sigmoid(jnp.dot(x.astype(BF16), wg_ref[...], preferred_element_type=F32))
    emb = jnp.dot(p_ref[0].astype(BF16), wp_ref[...], preferred_element_type=F32)
    y_ref[0] = x + gate * emb


def _post_call(tok_inputs, full_inputs, *, tm):
    b, s, d = tok_inputs[4].shape
    tok = lambda a: pl.BlockSpec((1, tm, a.shape[2]), lambda bi, ti: (bi, ti, 0))
    full = lambda a: pl.BlockSpec(a.shape, lambda bi, ti: (0,) * a.ndim, pipeline_mode=pl.Buffered(1))
    return pl.pallas_call(
        functools.partial(_post_kernel, chunk=1024),
        grid=(b, s // tm),
        in_specs=[tok(a) for a in tok_inputs] + [full(a) for a in full_inputs],
        out_specs=pl.BlockSpec((1, tm, d), lambda bi, ti: (bi, ti, 0)),
        out_shape=jax.ShapeDtypeStruct((b, s, d), F32),
        compiler_params=_cparams(("arbitrary", "arbitrary")),
        name="post",
    )(*tok_inputs, *full_inputs)


def _dup_kv(w):
    return jnp.concatenate([w[:, :HEAD_DIM], w[:, :HEAD_DIM], w[:, HEAD_DIM:], w[:, HEAD_DIM:]], axis=1)


def _prep_w_in(w):
    sec = lambda n: w[:, _SEC[n][0]:_SEC[n][0] + _SEC[n][1]]
    cols = [_dup_kv(sec(n)) if n in ("d_k", "d_v") else sec(n) for n in QK_GROUPS + V_GROUPS]
    w_main = jnp.concatenate(cols, axis=1).astype(BF16)
    pad = jnp.zeros((w.shape[0], LANES - IDX_DIM - IDX_HEADS - 4), w.dtype)
    w_idx = jnp.concatenate([sec("iq"), sec("ik"), sec("iw"), sec("b_f"), pad], axis=1)
    w_hi = w_idx.astype(BF16)
    w_lo = (w_idx - w_hi.astype(F32)).astype(BF16)
    return w_main, jnp.concatenate([w_hi, w_hi, w_lo], axis=0)


def _rope_tables(positions):
    pos = positions.astype(F32)[..., None]
    lane = jnp.arange(LANES)

    def tabs(dim):
        half = dim // 2
        inv_freq = ROPE_THETA ** (-jnp.arange(half, dtype=F32) / half)
        ang = pos * inv_freq
        sign = jnp.where((lane % dim) < half, -1.0, 1.0).astype(F32)
        reps = (1, 1, LANES // half)
        return jnp.tile(jnp.cos(ang), reps), jnp.tile(jnp.sin(ang), reps) * sign

    c64, s64 = tabs(HEAD_DIM)
    c32, s32 = tabs(DIFF_DIM)
    return c64, s64, c32, s32


def kernel(x, p, positions, w_in, b_forget, lambda_q1, lambda_k1, lambda_q2, lambda_k2, diff_subln, sinks,
           w_out, norm_pre_mix, norm_post_mix, norm_pre_mlp, norm_post_mlp, w_mlp_up, w_mlp_down,
           w_ple_proj, w_ple_gate):
    b, s, d = x.shape
    depth = w_in.shape[0]
    t = min(256, s)
    assert s % (CHUNK_TILES * t) == 0, "sequence length must be a multiple of the attention key chunk"
    topk = min(TOPK_MAX, s // 4)
    tabs = _rope_tables(positions)
    row = lambda v: v.reshape(1, -1).astype(F32)

    for i in range(depth):
        lam_init = 0.8 - 0.6 * math.exp(-0.3 * i)
        w_main, w_idx = _prep_w_in(w_in[i])
        bf_row = jnp.zeros((1, LANES), F32).at[0, MISC_F:MISC_F + 4].set(b_forget[i])
        qk, vt, iq, ik, iw = _proj_call(x, row(norm_pre_mix[i]), w_main, w_idx, tabs, bf_row, tm=t)

        o_a = _dsa_call(qk, vt, iq, ik, iw, t=t, topk=topk)
        o_b = _fox_call(qk, vt, t=t)
        lam = (jnp.exp(jnp.sum(lambda_q1[i] * lambda_k1[i])) - jnp.exp(jnp.sum(lambda_q2[i] * lambda_k2[i]))
               + lam_init)
        lam2 = jnp.stack([lam, jnp.asarray(1.0 - lam_init, F32)]).astype(F32)
        gs_full = jnp.broadcast_to(jnp.concatenate([diff_subln[i], diff_subln[i]]).astype(F32)[:, None], (LANES, t))
        o_c = _diff_call(lam2, qk, vt, gs_full, t=t)
        o_d = _swa_call(sinks[i].astype(F32), qk, vt, t=t)

        x = _post_call([o_a, o_b, o_c, o_d, x, p[i]],
                       [w_out[i].astype(BF16), row(norm_post_mix[i]), row(norm_pre_mlp[i]),
                        w_mlp_up[i].astype(BF16), w_mlp_down[i].astype(BF16), row(norm_post_mlp[i]),
                        w_ple_gate[i].astype(BF16), w_ple_proj[i].astype(BF16)], tm=t)
    return x
```

```python
import functools
import math

import jax
import jax.numpy as jnp
from jax import lax
from jax.experimental import pallas as pl
from jax.experimental.pallas import tpu as pltpu

F32, BF16, I32 = jnp.float32, jnp.bfloat16, jnp.int32

D_MODEL = 1024
HEAD_DIM = 64
DIFF_DIM = 32
IDX_HEADS = 8
IDX_DIM = 64
TOPK_MAX = 256
WINDOW = 128
D_FF = 4 * D_MODEL
D_PLE = 256
ROPE_THETA = 10000.0
EPS = 1e-6
NEG_INF = -1e30
LANES = 128
SUBLANES = 8
GROUP = 256
INT_MIN = -2147483648
FIELD_BITS = 15
FIELD_GUARDS = -2147450880
LOW_BITS = 9

_SEC = {}
_o = 0
for _name, _w in (("a_q", 256), ("a_k", 256), ("a_v", 256), ("iq", 512), ("ik", 64), ("iw", 8),
                  ("b_q", 256), ("b_k", 256), ("b_v", 256), ("b_f", 4),
                  ("c_q", 256), ("c_k", 256), ("c_v", 256), ("d_q", 256), ("d_k", 128), ("d_v", 128)):
    _SEC[_name] = (_o, _w)
    _o += _w
D_IN = _o

QK_GROUPS = ("a_q", "a_k", "b_q", "b_k", "c_q", "c_k", "d_q", "d_k")
V_GROUPS = ("a_v", "b_v", "c_v", "d_v")
MAIN_W = GROUP * (len(QK_GROUPS) + len(V_GROUPS))
V_BASE = GROUP * len(QK_GROUPS)
QK_BLK = {"a_q": 0, "a_k": 1, "b_q": 2, "b_k": 3, "c_q": 5, "c_k": 6, "d_q": 7, "d_k": 8}
QK_W = GROUP * 9
V_ROW_BLK = {n: i for i, n in enumerate(V_GROUPS)}
ROPE64_GROUPS = ("a_q", "a_k", "d_q", "d_k")
ROPE32_GROUPS = ("c_q", "c_k")
LOG2E = math.log2(math.e)
Q_SCALE = {"a_q": HEAD_DIM ** -0.5 * LOG2E, "b_q": HEAD_DIM ** -0.5 * LOG2E, "c_q": DIFF_DIM ** -0.5 * LOG2E,
           "d_q": HEAD_DIM ** -0.5 * LOG2E}
IDX_W = 512 + LANES
MISC_IW = 64
MISC_F = 72
IDX_CAT = 256

CHUNK_TILES = 4
CHUNK_TILES_DIFF = 2
VMEM_LIMIT = 56 * 1024 * 1024
RESIDENT = pl.Buffered(1)


def _cparams(sem):
    return pltpu.CompilerParams(dimension_semantics=sem, vmem_limit_bytes=VMEM_LIMIT)


def _rms(x, g):
    return x * lax.rsqrt(jnp.mean(x * x, axis=-1, keepdims=True) + EPS) * g


def _dot_nt(a, b):
    return lax.dot_general(a, b, (((1,), (1,)), ((), ())), preferred_element_type=F32)


def _dot(a, b):
    return jnp.dot(a, b, preferred_element_type=F32)


def _transpose_bf16(a):
    return a.astype(F32).T.astype(BF16)


def _bf16_part(x):
    return x.astype(BF16).astype(F32)


def _split2(x):
    hi = _bf16_part(x)
    return hi, _bf16_part(x - hi)


def _split3(x):
    hi = _bf16_part(x)
    mid = _bf16_part(x - hi)
    return hi, mid, _bf16_part(x - hi - mid)


def _rope_chunk(xc, cos, sin_signed, half, lane):
    fwd = pltpu.roll(xc, LANES - half, axis=1)
    bwd = pltpu.roll(xc, half, axis=1)
    partner = jnp.where((lane % (2 * half)) < half, fwd, bwd)
    return xc * cos + partner * sin_signed


def _proj_kernel(x_ref, g_ref, wm_ref, wi_ref, c64_ref, s64_ref, c32_ref, s32_ref, bf_ref,
                 qk_ref, vt_ref, iq_ref, ik_ref, iw_ref, carry_ref, *, tm):
    t = pl.program_id(1)
    x = x_ref[0]
    h = _rms(x, g_ref[...])
    pm = jnp.dot(h.astype(BF16), wm_ref[...], preferred_element_type=F32)
    h_hi, h_lo = _split2(h)
    pi = jnp.dot(jnp.concatenate([h_hi, h_lo, h_hi], axis=1).astype(BF16), wi_ref[...],
                 preferred_element_type=F32)
    lane = lax.broadcasted_iota(I32, (1, LANES), 1)
    lo_half = lane < HEAD_DIM
    c64, s64, c32, s32 = c64_ref[0], s64_ref[0], c32_ref[0], s32_ref[0]

    misc = pi[:, 512:512 + LANES]
    z = misc + bf_ref[...]
    logf = jnp.minimum(z, 0.0) - jnp.log1p(jnp.exp(-jnp.abs(z)))
    logf = jnp.where((lane >= MISC_F) & (lane < MISC_F + 4), logf, 0.0)
    tri = (lax.broadcasted_iota(I32, (tm, tm), 1) <= lax.broadcasted_iota(I32, (tm, tm), 0)).astype(BF16)

    @pl.when(t == 0)
    def _():
        carry_ref[...] = jnp.zeros_like(carry_ref)

    pieces = jnp.dot(tri, jnp.concatenate(_split3(logf), axis=1).astype(BF16), preferred_element_type=F32)
    cum = pieces[:, 0:LANES] + pieces[:, LANES:2 * LANES] + pieces[:, 2 * LANES:3 * LANES] + carry_ref[...]
    carry_ref[...] = cum[tm - 1:tm, :]
    nhi, nmid, nlo = _split3(-LOG2E * cum)
    gate_bias = jnp.where(lane < 4, pltpu.roll(nhi, LANES - MISC_F, axis=1),
                          jnp.where(lane < 8, pltpu.roll(nmid, LANES - MISC_F + 4, axis=1),
                                    jnp.where(lane < 12, pltpu.roll(nlo, LANES - MISC_F + 8, axis=1),
                                              jnp.zeros_like(nlo))))

    for gi, name in enumerate(QK_GROUPS):
        for c in range(GROUP // LANES):
            lo = gi * GROUP + c * LANES
            v = pm[:, lo:lo + LANES]
            if name in ROPE64_GROUPS:
                v = _rope_chunk(v, c64, s64, HEAD_DIM // 2, lane)
            elif name in ROPE32_GROUPS:
                v = _rope_chunk(v, c32, s32, DIFF_DIM // 2, lane)
            if name in Q_SCALE:
                v = v * Q_SCALE[name]
            if name == "b_k":
                out = (QK_BLK[name] + c) * GROUP
                qk_ref[0, :, out:out + LANES] = v.astype(BF16)
                qk_ref[0, :, out + LANES:out + GROUP] = gate_bias.astype(BF16)
            else:
                out = QK_BLK[name] * GROUP + c * LANES
                qk_ref[0, :, out:out + LANES] = v.astype(BF16)

    vt_ref[0, 0] = pm[:, V_BASE:V_BASE + len(V_GROUPS) * GROUP].T.astype(BF16)

    def cat_q(q):
        hi, lo = _split2(q)
        return jnp.where(lo_half, hi, pltpu.roll(lo, HEAD_DIM, axis=1)), hi

    for c in range(512 // LANES):
        v = _rope_chunk(pi[:, c * LANES:(c + 1) * LANES], c64, s64, IDX_DIM // 2, lane)
        for r, q in enumerate((jnp.where(lo_half, v, 0.0), jnp.where(lo_half, pltpu.roll(v, HEAD_DIM, axis=1), 0.0))):
            a, b2 = cat_q(q)
            iq_ref[0, 2 * c + r, :, 0:LANES] = a.astype(BF16)
            iq_ref[0, 2 * c + r, :, LANES:IDX_CAT] = b2.astype(BF16)
    ik = jnp.where(lo_half, _rope_chunk(misc, c64, s64, IDX_DIM // 2, lane), 0.0)
    khi, klo = _split2(ik)
    ik_ref[0, :, 0:LANES] = jnp.where(lo_half, khi, pltpu.roll(khi, HEAD_DIM, axis=1)).astype(BF16)
    ik_ref[0, :, LANES:IDX_CAT] = klo.astype(BF16)
    iw_t = (misc * (IDX_HEADS ** -0.5 * IDX_DIM ** -0.5)).T
    iw_ref[0, 0] = iw_t[MISC_IW:MISC_IW + IDX_HEADS, :]


def _proj_call(x, g, w_main, w_idx, tabs, bf_row, *, tm):
    b, s, d = x.shape
    tok = lambda w: pl.BlockSpec((1, tm, w), lambda bi, ti: (bi, ti, 0))
    full = lambda a: pl.BlockSpec(a.shape, lambda bi, ti: (0,) * a.ndim)
    return pl.pallas_call(
        functools.partial(_proj_kernel, tm=tm),
        grid=(b, s // tm),
        in_specs=[tok(d), full(g), full(w_main), full(w_idx), tok(LANES), tok(LANES), tok(LANES), tok(LANES),
                  full(bf_row)],
        out_specs=[tok(QK_W),
                   pl.BlockSpec((1, 1, len(V_GROUPS) * GROUP, tm), lambda bi, ti: (bi, ti, 0, 0)),
                   pl.BlockSpec((1, IDX_HEADS, tm, IDX_CAT), lambda bi, ti: (bi, 0, ti, 0)),
                   tok(IDX_CAT),
                   pl.BlockSpec((1, 1, IDX_HEADS, tm), lambda bi, ti: (bi, ti, 0, 0))],
        out_shape=[jax.ShapeDtypeStruct((b, s, QK_W), BF16),
                   jax.ShapeDtypeStruct((b, s // tm, len(V_GROUPS) * GROUP, tm), BF16),
                   jax.ShapeDtypeStruct((b, IDX_HEADS, s, IDX_CAT), BF16),
                   jax.ShapeDtypeStruct((b, s, IDX_CAT), BF16),
                   jax.ShapeDtypeStruct((b, s // tm, IDX_HEADS, tm), F32)],
        scratch_shapes=[pltpu.VMEM((1, LANES), F32)],
        compiler_params=_cparams(("arbitrary", "arbitrary")),
        name="proj",
    )(x, g, w_main, w_idx, *tabs, bf_row)


def _softmax_step(s, m_ref, l_ref, idx):
    m_old = m_ref[idx]
    m_new = jnp.maximum(m_old, jnp.max(s, axis=0, keepdims=True))
    alpha = jnp.exp2(m_old - m_new)
    p = jnp.exp2(s - m_new)
    l_ref[idx] = alpha * l_ref[idx] + jnp.sum(p, axis=0, keepdims=True)
    m_ref[idx] = m_new
    return p, alpha


def _init_state(m_ref, l_ref, acc_ref):
    m_ref[...] = jnp.full(m_ref.shape, NEG_INF, F32)
    l_ref[...] = jnp.zeros(l_ref.shape, F32)
    acc_ref[...] = jnp.zeros(acc_ref.shape, F32)


def _causal_chunk(w, t):
    return lax.broadcasted_iota(I32, (w * t, t), 0) <= lax.broadcasted_iota(I32, (w * t, t), 1) + (w - 1) * t


def _head_mask(r):
    lane = lax.broadcasted_iota(I32, (1, LANES), 1)
    return (lane >= HEAD_DIM) if r else (lane < HEAD_DIM)


def _key_chunks(n, chunk_fn, width):
    def _body(c, carry):
        chunk_fn(width * c, width, False)
        return carry

    n_full = n // width
    lax.fori_loop(0, n_full, _body, 0)
    rest = n - n_full * width
    for k in range(width):
        @pl.when(rest == k)
        def _(k=k):
            chunk_fn(n - k, k + 1, True)


def _pipelined_key_chunks(n, scores_fn, rest_fn, width):
    n_full = n // width
    rest = n - n_full * width
    odd = n_full & 1

    @pl.when(odd == 1)
    def _():
        scores_fn(0, 1)
        scores_fn(width, 0)
        rest_fn(0, width, False, 1)

    @pl.when(odd == 0)
    def _():
        scores_fn(0, 0)

    def _body(pair, carry):
        j0 = (odd + 2 * pair) * width
        scores_fn(j0 + width, 1)
        rest_fn(j0, width, False, 0)
        scores_fn(j0 + 2 * width, 0)
        rest_fn(j0 + width, width, False, 1)
        return carry

    lax.fori_loop(0, (n_full - odd) >> 1, _body, 0)
    for k in range(width):
        @pl.when(rest == k)
        def _(k=k):
            rest_fn(n - k, k + 1, True, 0)


def _accumulate(acc_ref, a, r, alpha, vt_tiles, p, t):
    rows = slice(r * HEAD_DIM, (r + 1) * HEAD_DIM)
    pb = p.astype(BF16)
    pv = jnp.dot(vt_tiles[0], pb[0:t], preferred_element_type=F32)
    for i in range(1, len(vt_tiles)):
        pv = pv + jnp.dot(vt_tiles[i], pb[i * t:(i + 1) * t], preferred_element_type=F32)
    acc_ref[a, rows, :] = acc_ref[a, rows, :] * alpha + pv


def _normalised_t(acc_ref, a, l_ref, idx0, idx1):
    return jnp.concatenate([acc_ref[a, 0:HEAD_DIM, :] / l_ref[idx0],
                            acc_ref[a, HEAD_DIM:LANES, :] / l_ref[idx1]], axis=0)


def _dsa_kernel(iq_ref, iw_ref, ik_ref, q_ref, k_ref, vt_ref, o_ref,
                key_ref, pack_ref, m_ref, l_ref, acc_ref, carry_ref, iqt_ref, qzt_ref, s_ref, thr_ref, nge_ref, *, t, topk):
    n = pl.program_id(1)
    for h in range(IDX_HEADS):
        iqt_ref[h] = _transpose_bf16(iq_ref[0, h])
    for h in range(4):
        qzt_ref[h] = _transpose_bf16(jnp.where(_head_mask(h % 2), q_ref[0, :, (h // 2) * LANES:(h // 2 + 1) * LANES],
                                               jnp.zeros((), BF16)))

    def score_chunk(j0, w, diag):
        start = pl.multiple_of(j0 * t, t)
        ik = ik_ref[0, pl.ds(start, w * t), :]
        sc = jnp.zeros((w * t, t), F32)
        for h in range(IDX_HEADS):
            d = _dot(ik, iqt_ref[h])
            sc = sc + iw_ref[0, 0, h:h + 1, :] * jnp.maximum(d, 0.0)
        if diag:
            sc = jnp.where(_causal_chunk(w, t), sc, NEG_INF)
        sc = jnp.where(sc == 0.0, 0.0, sc)
        bits = lax.bitcast_convert_type(sc, I32)
        key = bits ^ ((bits >> 31) & jnp.int32(0x7FFFFFFF))
        top = lax.shift_right_logical(key ^ jnp.int32(INT_MIN), 32 - FIELD_BITS)
        for i in range(w):
            key_ref[j0 + i] = key[i * t:(i + 1) * t]
            pack_ref[j0 + i] = (jnp.left_shift(top[i * t:i * t + t // 2], 16) | top[i * t + t // 2:(i + 1) * t]
                                | jnp.int32(FIELD_GUARDS))

    _key_chunks(n, score_chunk, CHUNK_TILES)

    def tree_sum(parts):
        while len(parts) > 1:
            parts = [parts[i] + parts[i + 1] for i in range(0, len(parts), 2)]
        return parts[0]

    def count(pred):
        def body(j, acc):
            c = jnp.where(pred(key_ref[j]), 1.0, 0.0)
            return acc + tree_sum([c[r * SUBLANES:(r + 1) * SUBLANES, :] for r in range(t // SUBLANES)])
        acc = lax.fori_loop(0, n + 1, body, jnp.zeros((SUBLANES, t), F32))
        return jnp.sum(acc, axis=0, keepdims=True)

    def count_top(cand):
        both = jnp.left_shift(cand, 16) | cand

        def body(j, acc):
            z = pack_ref[j] - both
            c = lax.shift_right_logical(z, FIELD_BITS) & jnp.int32(0x00010001)
            return acc + tree_sum([c[r * SUBLANES:(r + 1) * SUBLANES, :] for r in range(t // 2 // SUBLANES)])
        acc = lax.fori_loop(0, n + 1, body, jnp.zeros((SUBLANES, t), I32))
        per_lane = lax.shift_right_logical(acc, 16) + (acc & jnp.int32(0xFFFF))
        return jnp.sum(per_lane.astype(F32), axis=0, keepdims=True)

    def bisect(n_bits, low_bit, count_fn, carry):
        def body(i, carry):
            u, n_ge, n_rej = carry
            cand = u | jnp.left_shift(jnp.int32(1), low_bit + n_bits - 1 - i)
            cnt = count_fn(cand)
            keep = cnt >= float(topk)
            return jnp.where(keep, cand, u), jnp.where(keep, cnt, n_ge), jnp.where(keep, n_rej, cnt)
        return lax.fori_loop(0, n_bits, body, carry)

    def count_full(cand):
        return count(lambda kt: kt >= (cand ^ jnp.int32(INT_MIN)))

    in_scope = ((n + 1) * t).astype(F32)
    zero_row = jnp.zeros((1, t), F32)
    top, n_ge, n_rej = bisect(FIELD_BITS, 0, count_top, (jnp.zeros((1, t), I32), zero_row + in_scope, zero_row))
    u, n_ge, n_rej = bisect(32 - FIELD_BITS - LOW_BITS, LOW_BITS, count_full,
                            (jnp.left_shift(top, 32 - FIELD_BITS), n_ge, n_rej))

    prefix = u ^ jnp.int32(INT_MIN)
    low_mask = jnp.int32((1 << LOW_BITS) - 1)
    big = jnp.int32(1 << (LOW_BITS + 1))

    def tree(parts, op):
        while len(parts) > 1:
            parts = [op(parts[i], parts[i + 1]) for i in range(0, len(parts), 2)]
        return parts[0]

    def band_body(j, carry):
        mx, mn, sm = carry
        kt = key_ref[j]
        inside = lax.shift_right_logical(kt ^ prefix, LOW_BITS) == 0
        low = kt & low_mask
        rows = lambda a: [a[r * SUBLANES:(r + 1) * SUBLANES, :] for r in range(t // SUBLANES)]
        return (jnp.maximum(mx, tree(rows(jnp.where(inside, low, -1)), jnp.maximum)),
                jnp.minimum(mn, tree(rows(jnp.where(inside, low, big)), jnp.minimum)),
                sm + tree(rows(jnp.where(inside, low, 0)), jnp.add))

    mx, mn, sm = lax.fori_loop(0, n + 1, band_body, (jnp.full((SUBLANES, t), -1, I32),
                                                       jnp.full((SUBLANES, t), 1 << (LOW_BITS + 1), I32),
                                                       jnp.zeros((SUBLANES, t), I32)))
    mx = jnp.max(mx.astype(F32), axis=0, keepdims=True)
    mn = jnp.min(mn.astype(F32), axis=0, keepdims=True)
    sm = jnp.sum(sm.astype(F32), axis=0, keepdims=True)
    short = n_ge < float(topk)
    in_band = n_ge - n_rej
    need = float(topk) - n_rej
    second = jnp.where(in_band > 2.5, sm - mx - mn, mn)
    thr_low = jnp.where(need < 1.5, mx, jnp.where(need < 2.5, second, mn))
    kept = (1.0 + jnp.where((in_band > 1.5) & (second >= thr_low), 1.0, 0.0)
            + jnp.where((in_band > 2.5) & (mn >= thr_low), 1.0, 0.0))
    resolved = jnp.min(jnp.where(short | (in_band < 3.5), 1.0, 0.0)) > 0.5

    @pl.when(resolved)
    def _():
        thr_ref[...] = jnp.where(short, prefix, prefix | thr_low.astype(I32))
        nge_ref[...] = jnp.where(short, n_ge, n_rej + kept)

    @pl.when(jnp.logical_not(resolved))
    def _():
        u_all, n_ge_all, _ = bisect(LOW_BITS, 0, count_full, (u, n_ge, n_rej))
        thr_ref[...] = u_all ^ jnp.int32(INT_MIN)
        nge_ref[...] = n_ge_all

    thr = thr_ref[...]
    has_ties = jnp.max(nge_ref[...]) > float(topk)

    _init_state(m_ref, l_ref, acc_ref)

    def attend(ties):
        if ties:
            carry_ref[...] = jnp.zeros(carry_ref.shape, F32)
            need = float(topk) - count(lambda kt: kt > thr)
            lower = (lax.broadcasted_iota(I32, (t, t), 1) <= lax.broadcasted_iota(I32, (t, t), 0)).astype(BF16)

        def scores(j0, buf):
            start = pl.multiple_of(j0 * t, t)
            ks = k_ref[0, pl.ds(start, CHUNK_TILES * t), :]
            for h in range(4):
                s_ref[buf, h] = _dot(ks[:, (h // 2) * LANES:(h // 2 + 1) * LANES], qzt_ref[h])

        def rest(j0, w, diag, buf):
            if ties:
                parts = []
                for i in range(w):
                    kt = key_ref[j0 + i]
                    eq = kt == thr
                    incl = jnp.dot(lower, jnp.where(eq, 1.0, 0.0).astype(BF16),
                                   preferred_element_type=F32) + carry_ref[...]
                    carry_ref[...] = incl[t - 1:t, :]
                    parts.append(jnp.where((kt > thr) | (eq & (incl <= need)), 1.0, 0.0))
                sel = (parts[0] if w == 1 else jnp.concatenate(parts, axis=0)) > 0.5
            else:
                kt = key_ref[j0] if w == 1 else jnp.concatenate([key_ref[j0 + i] for i in range(w)], axis=0)
                sel = kt >= thr
            if diag:
                sel = sel & _causal_chunk(w, t)
            for h in range(4):
                g, r = divmod(h, 2)
                p, alpha = _softmax_step(jnp.where(sel, s_ref[buf, h, 0:w * t, :], -jnp.inf), m_ref, l_ref, h)
                _accumulate(acc_ref, g, r, alpha,
                            [vt_ref[0, j0 + i, h * HEAD_DIM:(h + 1) * HEAD_DIM, :] for i in range(w)], p, t)

        _pipelined_key_chunks(n, scores, rest, CHUNK_TILES)

    @pl.when(has_ties)
    def _():
        attend(True)

    @pl.when(jnp.logical_not(has_ties))
    def _():
        attend(False)

    for g in range(2):
        o_ref[0, :, g * LANES:(g + 1) * LANES] = _normalised_t(acc_ref, g, l_ref, 2 * g, 2 * g + 1).T.astype(BF16)


def _dsa_call(qk, vt, iq, ik, iw, *, t, topk):
    b, s, _ = qk.shape
    nt = s // t
    return pl.pallas_call(
        functools.partial(_dsa_kernel, t=t, topk=topk),
        grid=(b, nt),
        in_specs=[pl.BlockSpec((1, IDX_HEADS, t, IDX_CAT), lambda bi, ni: (bi, 0, ni, 0)),
                  pl.BlockSpec((1, 1, IDX_HEADS, t), lambda bi, ni: (bi, ni, 0, 0)),
                  pl.BlockSpec((1, s, IDX_CAT), lambda bi, ni: (bi, 0, 0), pipeline_mode=RESIDENT),
                  pl.BlockSpec((1, t, GROUP), lambda bi, ni: (bi, ni, QK_BLK["a_q"])),
                  pl.BlockSpec((1, s, GROUP), lambda bi, ni: (bi, 0, QK_BLK["a_k"]), pipeline_mode=RESIDENT),
                  pl.BlockSpec((1, nt, GROUP, t), lambda bi, ni: (bi, 0, V_ROW_BLK["a_v"], 0), pipeline_mode=RESIDENT)],
        out_specs=pl.BlockSpec((1, t, GROUP), lambda bi, ni: (bi, ni, 0)),
        out_shape=jax.ShapeDtypeStruct((b, s, GROUP), BF16),
        scratch_shapes=[pltpu.VMEM((nt, t, t), I32), pltpu.VMEM((nt, t // 2, t), I32),
                        pltpu.VMEM((4, 1, t), F32), pltpu.VMEM((4, 1, t), F32),
                        pltpu.VMEM((2, LANES, t), F32), pltpu.VMEM((1, t), F32),
                        pltpu.VMEM((IDX_HEADS, IDX_CAT, t), BF16), pltpu.VMEM((4, LANES, t), BF16),
                        pltpu.VMEM((2, 4, CHUNK_TILES * t, t), F32),
                        pltpu.VMEM((1, t), I32), pltpu.VMEM((1, t), F32)],
        compiler_params=_cparams(("arbitrary", "arbitrary")),
        name="dsa",
    )(iq, iw, ik, qk, qk, vt)


def _fox_kernel(q_ref, k0_ref, k1_ref, vt_ref, o_ref, m_ref, l_ref, acc_ref, qxt_ref, s_ref, *, t):
    n = pl.program_id(1)
    _init_state(m_ref, l_ref, acc_ref)
    lane = lax.broadcasted_iota(I32, (1, LANES), 1)
    for h in range(4):
        qz = jnp.where(_head_mask(h % 2), q_ref[0, :, (h // 2) * LANES:(h // 2 + 1) * LANES], jnp.zeros((), BF16))
        ones = jnp.where((lane == h) | (lane == 4 + h) | (lane == 8 + h), 1.0, 0.0).astype(BF16)
        qxt_ref[h] = _transpose_bf16(jnp.concatenate([qz, jnp.broadcast_to(ones, (t, LANES))], axis=1))
    k_refs = (k0_ref, k1_ref)

    def scores(j0, buf):
        start = pl.multiple_of(j0 * t, t)
        for h in range(4):
            s_ref[buf, h] = _dot(k_refs[h // 2][0, pl.ds(start, CHUNK_TILES * t), :], qxt_ref[h])

    def rest(j0, w, diag, buf):
        for h in range(4):
            g, r = divmod(h, 2)
            s = s_ref[buf, h, 0:w * t, :]
            if diag:
                s = jnp.where(_causal_chunk(w, t), s, -jnp.inf)
            p, alpha = _softmax_step(s, m_ref, l_ref, h)
            _accumulate(acc_ref, g, r, alpha,
                        [vt_ref[0, j0 + i, h * HEAD_DIM:(h + 1) * HEAD_DIM, :] for i in range(w)], p, t)

    _pipelined_key_chunks(n, scores, rest, CHUNK_TILES)
    for g in range(2):
        o_ref[0, :, g * LANES:(g + 1) * LANES] = _normalised_t(acc_ref, g, l_ref, 2 * g, 2 * g + 1).T.astype(BF16)


def _fox_call(qk, vt, *, t):
    b, s, _ = qk.shape
    nt = s // t
    return pl.pallas_call(
        functools.partial(_fox_kernel, t=t),
        grid=(b, nt),
        in_specs=[pl.BlockSpec((1, t, GROUP), lambda bi, ni: (bi, ni, QK_BLK["b_q"])),
                  pl.BlockSpec((1, s, GROUP), lambda bi, ni: (bi, 0, QK_BLK["b_k"]), pipeline_mode=RESIDENT),
                  pl.BlockSpec((1, s, GROUP), lambda bi, ni: (bi, 0, QK_BLK["b_k"] + 1), pipeline_mode=RESIDENT),
                  pl.BlockSpec((1, nt, GROUP, t), lambda bi, ni: (bi, 0, V_ROW_BLK["b_v"], 0), pipeline_mode=RESIDENT)],
        out_specs=pl.BlockSpec((1, t, GROUP), lambda bi, ni: (bi, ni, 0)),
        out_shape=jax.ShapeDtypeStruct((b, s, GROUP), BF16),
        scratch_shapes=[pltpu.VMEM((4, 1, t), F32), pltpu.VMEM((4, 1, t), F32), pltpu.VMEM((2, LANES, t), F32),
                        pltpu.VMEM((4, GROUP, t), BF16), pltpu.VMEM((2, 4, CHUNK_TILES * t, t), F32)],
        compiler_params=_cparams(("arbitrary", "arbitrary")),
        name="fox",
    )(qk, qk, qk, vt)


def _diff_kernel(lam_ref, q_ref, k_ref, vt_ref, gs_ref, o_ref, m_ref, l_ref, acc_ref, qzt_ref, s_ref, *, t):
    n = pl.program_id(1)
    _init_state(m_ref, l_ref, acc_ref)
    lane = lax.broadcasted_iota(I32, (1, LANES), 1)
    for h in range(4):
        for mm in range(2):
            lo = (h % 2) * HEAD_DIM + mm * DIFF_DIM
            qzt_ref[2 * h + mm] = _transpose_bf16(
                jnp.where((lane >= lo) & (lane < lo + DIFF_DIM),
                          q_ref[0, :, (h // 2) * LANES:(h // 2 + 1) * LANES], jnp.zeros((), BF16)))

    def scores(j0, buf):
        start = pl.multiple_of(j0 * t, t)
        ks = k_ref[0, pl.ds(start, CHUNK_TILES_DIFF * t), :]
        for i in range(8):
            s_ref[buf, i] = _dot(ks[:, (i // 4) * LANES:(i // 4 + 1) * LANES], qzt_ref[i])

    def rest(j0, w, diag, buf):
        for h in range(4):
            g, r = divmod(h, 2)
            vts = [vt_ref[0, j0 + i, h * HEAD_DIM:(h + 1) * HEAD_DIM, :] for i in range(w)]
            for mm in range(2):
                s = s_ref[buf, 2 * h + mm, 0:w * t, :]
                if diag:
                    s = jnp.where(_causal_chunk(w, t), s, -jnp.inf)
                p, alpha = _softmax_step(s, m_ref, l_ref, 2 * h + mm)
                _accumulate(acc_ref, 2 * mm + g, r, alpha, vts, p, t)

    _pipelined_key_chunks(n, scores, rest, CHUNK_TILES_DIFF)
    lam = lam_ref[0]
    out_scale = lam_ref[1]
    for g in range(2):
        o1 = _normalised_t(acc_ref, g, l_ref, 4 * g, 4 * g + 2)
        o2 = _normalised_t(acc_ref, 2 + g, l_ref, 4 * g + 1, 4 * g + 3)
        o = o1 - lam * o2
        sq = o * o
        ms = jnp.concatenate(
            [jnp.broadcast_to(jnp.mean(sq[r * HEAD_DIM:(r + 1) * HEAD_DIM], axis=0, keepdims=True), (HEAD_DIM, t))
             for r in range(2)], axis=0)
        y = o * lax.rsqrt(ms + EPS) * gs_ref[...] * out_scale
        o_ref[0, :, g * LANES:(g + 1) * LANES] = y.T.astype(BF16)


def _diff_call(lam2, qk, vt, gs_full, *, t):
    b, s, _ = qk.shape
    nt = s // t
    return pl.pallas_call(
        functools.partial(_diff_kernel, t=t),
        grid=(b, nt),
        in_specs=[pl.BlockSpec(memory_space=pltpu.SMEM),
                  pl.BlockSpec((1, t, GROUP), lambda bi, ni: (bi, ni, QK_BLK["c_q"])),
                  pl.BlockSpec((1, s, GROUP), lambda bi, ni: (bi, 0, QK_BLK["c_k"]), pipeline_mode=RESIDENT),
                  pl.BlockSpec((1, nt, GROUP, t), lambda bi, ni: (bi, 0, V_ROW_BLK["c_v"], 0), pipeline_mode=RESIDENT),
                  pl.BlockSpec((LANES, t), lambda bi, ni: (0, 0))],
        out_specs=pl.BlockSpec((1, t, GROUP), lambda bi, ni: (bi, ni, 0)),
        out_shape=jax.ShapeDtypeStruct((b, s, GROUP), BF16),
        scratch_shapes=[pltpu.VMEM((8, 1, t), F32), pltpu.VMEM((8, 1, t), F32), pltpu.VMEM((4, LANES, t), F32),
                        pltpu.VMEM((8, LANES, t), BF16), pltpu.VMEM((2, 8, CHUNK_TILES_DIFF * t, t), F32)],
        compiler_params=_cparams(("arbitrary", "arbitrary")),
        name="diff",
    )(lam2, qk, qk, vt, gs_full)


def _swa_kernel(sink_ref, q_ref, kp_ref, kc_ref, vtp_ref, vtc_ref, o_ref, *, t):
    n = pl.program_id(1)
    prow = lax.broadcasted_iota(I32, (WINDOW, t), 0)
    pcol = lax.broadcasted_iota(I32, (WINDOW, t), 1)
    mask_prev = (prow > pcol) & (pcol + jnp.where(n > 0, 0, t) < WINDOW)
    crow = lax.broadcasted_iota(I32, (t, t), 0)
    ccol = lax.broadcasted_iota(I32, (t, t), 1)
    mask_cur = (crow <= ccol) & (crow > ccol - WINDOW)
    vtp = vtp_ref[0, 0][:, t - WINDOW:t]
    vtc = vtc_ref[0, 0]
    for g in range(2):
        outs = []
        for r in range(2):
            h = 2 * g + r
            qz = jnp.where(_head_mask(r), q_ref[0, :, g * LANES:(g + 1) * LANES], jnp.zeros((), BF16))
            sp = jnp.where(mask_prev, _dot_nt(kp_ref[0, :, g * LANES:(g + 1) * LANES], qz), -jnp.inf)
            sc = jnp.where(mask_cur, _dot_nt(kc_ref[0, :, g * LANES:(g + 1) * LANES], qz), -jnp.inf)
            sink = sink_ref[h] * LOG2E
            m = jnp.maximum(jnp.maximum(jnp.max(sp, axis=0, keepdims=True),
                                        jnp.max(sc, axis=0, keepdims=True)), sink)
            pp = jnp.exp2(sp - m)
            pc = jnp.exp2(sc - m)
            den = jnp.sum(pp, axis=0, keepdims=True) + jnp.sum(pc, axis=0, keepdims=True) + jnp.exp2(sink - m)
            rows = slice(h * HEAD_DIM, (h + 1) * HEAD_DIM)
            o = (jnp.dot(vtp[rows, :], pp.astype(BF16), preferred_element_type=F32)
                 + jnp.dot(vtc[rows, :], pc.astype(BF16), preferred_element_type=F32))
            outs.append(o / den)
        o_ref[0, :, g * LANES:(g + 1) * LANES] = jnp.concatenate(outs, axis=0).T.astype(BF16)


def _swa_call(sinks, qk, vt, *, t):
    b, s, _ = qk.shape
    nt = s // t
    per = t // WINDOW
    return pl.pallas_call(
        functools.partial(_swa_kernel, t=t),
        grid=(b, nt),
        in_specs=[pl.BlockSpec(memory_space=pltpu.SMEM),
                  pl.BlockSpec((1, t, GROUP), lambda bi, ni: (bi, ni, QK_BLK["d_q"])),
                  pl.BlockSpec((1, WINDOW, GROUP), lambda bi, ni: (bi, jnp.maximum(ni * per - 1, 0), QK_BLK["d_k"])),
                  pl.BlockSpec((1, t, GROUP), lambda bi, ni: (bi, ni, QK_BLK["d_k"])),
                  pl.BlockSpec((1, 1, GROUP, t), lambda bi, ni: (bi, jnp.maximum(ni - 1, 0), V_ROW_BLK["d_v"], 0)),
                  pl.BlockSpec((1, 1, GROUP, t), lambda bi, ni: (bi, ni, V_ROW_BLK["d_v"], 0))],
        out_specs=pl.BlockSpec((1, t, GROUP), lambda bi, ni: (bi, ni, 0)),
        out_shape=jax.ShapeDtypeStruct((b, s, GROUP), BF16),
        compiler_params=_cparams(("arbitrary", "arbitrary")),
        name="swa",
    )(sinks, qk, qk, qk, vt, vt)


def _post_kernel(oa_ref, ob_ref, oc_ref, od_ref, x_ref, p_ref, wo_ref, gmix_ref, gpre_ref, wu_ref, wd_ref, gpost_ref,
                 wg_ref, wp_ref, y_ref, *, chunk):
    acc = jnp.zeros(x_ref.shape[1:], F32)
    for i, o_ref in enumerate((oa_ref, ob_ref, oc_ref, od_ref)):
        acc = acc + jnp.dot(o_ref[0], wo_ref[i * GROUP:(i + 1) * GROUP, :], preferred_element_type=F32)
    x = x_ref[0] + _rms(acc, gmix_ref[...])
    h = _rms(x, gpre_ref[...]).astype(BF16)
    acc = jnp.zeros(x.shape, F32)
    for c in range(D_FF // chunk):
        u = jnp.dot(h, wu_ref[:, c * chunk:(c + 1) * chunk], preferred_element_type=F32)
        u = jnp.square(jnp.maximum(u, 0.0)).astype(BF16)
        acc = acc + jnp.dot(u, wd_ref[c * chunk:(c + 1) * chunk, :], preferred_element_type=F32)
    x = x + _rms(acc, gpost_ref[...])
    gate = jax.nn.sigmoid(jnp.dot(x.astype(BF16), wg_ref[...], preferred_element_type=F32))
    emb = jnp.dot(p_ref[0].astype(BF16), wp_ref[...], preferred_element_type=F32)
    y_ref[0] = x + gate * emb


def _post_call(tok_inputs, full_inputs, *, tm):
    b, s, d = tok_inputs[4].shape
    tok = lambda a: pl.BlockSpec((1, tm, a.shape[2]), lambda bi, ti: (bi, ti, 0))
    full = lambda a: pl.BlockSpec(a.shape, lambda bi, ti: (0,) * a.ndim, pipeline_mode=pl.Buffered(1))
    return pl.pallas_call(
        functools.partial(_post_kernel, chunk=1024),
        grid=(b, s // tm),
        in_specs=[tok(a) for a in tok_inputs] + [full(a) for a in full_inputs],
        out_specs=pl.BlockSpec((1, tm, d), lambda bi, ti: (bi, ti, 0)),
        out_shape=jax.ShapeDtypeStruct((b, s, d), F32),
        compiler_params=_cparams(("arbitrary", "arbitrary")),
        name="post",
    )(*tok_inputs, *full_inputs)


def _dup_kv(w):
    return jnp.concatenate([w[:, :HEAD_DIM], w[:, :HEAD_DIM], w[:, HEAD_DIM:], w[:, HEAD_DIM:]], axis=1)


def _prep_w_in(w):
    sec = lambda n: w[:, _SEC[n][0]:_SEC[n][0] + _SEC[n][1]]
    cols = [_dup_kv(sec(n)) if n in ("d_k", "d_v") else sec(n) for n in QK_GROUPS + V_GROUPS]
    w_main = jnp.concatenate(cols, axis=1).astype(BF16)
    pad = jnp.zeros((w.shape[0], LANES - IDX_DIM - IDX_HEADS - 4), w.dtype)
    w_idx = jnp.concatenate([sec("iq"), sec("ik"), sec("iw"), sec("b_f"), pad], axis=1)
    w_hi = w_idx.astype(BF16)
    w_lo = (w_idx - w_hi.astype(F32)).astype(BF16)
    return w_main, jnp.concatenate([w_hi, w_hi, w_lo], axis=0)


def _rope_tables(positions):
    pos = positions.astype(F32)[..., None]
    lane = jnp.arange(LANES)

    def tabs(dim):
        half = dim // 2
        inv_freq = ROPE_THETA ** (-jnp.arange(half, dtype=F32) / half)
        ang = pos * inv_freq
        sign = jnp.where((lane % dim) < half, -1.0, 1.0).astype(F32)
        reps = (1, 1, LANES // half)
        return jnp.tile(jnp.cos(ang), reps), jnp.tile(jnp.sin(ang), reps) * sign

    c64, s64 = tabs(HEAD_DIM)
    c32, s32 = tabs(DIFF_DIM)
    return c64, s64, c32, s32


def kernel(x, p, positions, w_in, b_forget, lambda_q1, lambda_k1, lambda_q2, lambda_k2, diff_subln, sinks,
           w_out, norm_pre_mix, norm_post_mix, norm_pre_mlp, norm_post_mlp, w_mlp_up, w_mlp_down,
           w_ple_proj, w_ple_gate):
    b, s, d = x.shape
    depth = w_in.shape[0]
    t = min(256, s)
    assert s % (CHUNK_TILES * t) == 0, "sequence length must be a multiple of the attention key chunk"
    topk = min(TOPK_MAX, s // 4)
    tabs = _rope_tables(positions)
    row = lambda v: v.reshape(1, -1).astype(F32)

    for i in range(depth):
        lam_init = 0.8 - 0.6 * math.exp(-0.3 * i)
        w_main, w_idx = _prep_w_in(w_in[i])
        bf_row = jnp.zeros((1, LANES), F32).at[0, MISC_F:MISC_F + 4].set(b_forget[i])
        qk, vt, iq, ik, iw = _proj_call(x, row(norm_pre_mix[i]), w_main, w_idx, tabs, bf_row, tm=t)

        o_a = _dsa_call(qk, vt, iq, ik, iw, t=t, topk=topk)
        o_b = _fox_call(qk, vt, t=t)
        lam = (jnp.exp(jnp.sum(lambda_q1[i] * lambda_k1[i])) - jnp.exp(jnp.sum(lambda_q2[i] * lambda_k2[i]))
               + lam_init)
        lam2 = jnp.stack([lam, jnp.asarray(1.0 - lam_init, F32)]).astype(F32)
        gs_full = jnp.broadcast_to(jnp.concatenate([diff_subln[i], diff_subln[i]]).astype(F32)[:, None], (LANES, t))
        o_c = _diff_call(lam2, qk, vt, gs_full, t=t)
        o_d = _swa_call(sinks[i].astype(F32), qk, vt, t=t)

        x = _post_call([o_a, o_b, o_c, o_d, x, p[i]],
                       [w_out[i].astype(BF16), row(norm_post_mix[i]), row(norm_pre_mlp[i]),
                        w_mlp_up[i].astype(BF16), w_mlp_down[i].astype(BF16), row(norm_post_mlp[i]),
                        w_ple_gate[i].astype(BF16), w_ple_proj[i].astype(BF16)], tm=t)
    return x
```

```python
import functools
import math

import jax
import jax.numpy as jnp
from jax import lax
from jax.experimental import pallas as pl
from jax.experimental.pallas import tpu as pltpu

F32, BF16, I32 = jnp.float32, jnp.bfloat16, jnp.int32

D_MODEL = 1024
HEAD_DIM = 64
DIFF_DIM = 32
IDX_HEADS = 8
IDX_DIM = 64
TOPK_MAX = 256
WINDOW = 128
D_FF = 4 * D_MODEL
D_PLE = 256
ROPE_THETA = 10000.0
EPS = 1e-6
NEG_INF = -1e30
LANES = 128
SUBLANES = 8
GROUP = 256
INT_MIN = -2147483648
FIELD_BITS = 15
FIELD_GUARDS = -2147450880
LOW_BITS = 10

_SEC = {}
_o = 0
for _name, _w in (("a_q", 256), ("a_k", 256), ("a_v", 256), ("iq", 512), ("ik", 64), ("iw", 8),
                  ("b_q", 256), ("b_k", 256), ("b_v", 256), ("b_f", 4),
                  ("c_q", 256), ("c_k", 256), ("c_v", 256), ("d_q", 256), ("d_k", 128), ("d_v", 128)):
    _SEC[_name] = (_o, _w)
    _o += _w
D_IN = _o

QK_GROUPS = ("a_q", "a_k", "b_q", "b_k", "c_q", "c_k", "d_q", "d_k")
V_GROUPS = ("a_v", "b_v", "c_v", "d_v")
MAIN_W = GROUP * (len(QK_GROUPS) + len(V_GROUPS))
V_BASE = GROUP * len(QK_GROUPS)
QK_BLK = {"a_q": 0, "a_k": 1, "b_q": 2, "b_k": 3, "c_q": 5, "c_k": 6, "d_q": 7, "d_k": 8}
QK_W = GROUP * 9
V_ROW_BLK = {n: i for i, n in enumerate(V_GROUPS)}
ROPE64_GROUPS = ("a_q", "a_k", "d_q", "d_k")
ROPE32_GROUPS = ("c_q", "c_k")
LOG2E = math.log2(math.e)
Q_SCALE = {"a_q": HEAD_DIM ** -0.5 * LOG2E, "b_q": HEAD_DIM ** -0.5 * LOG2E, "c_q": DIFF_DIM ** -0.5 * LOG2E,
           "d_q": HEAD_DIM ** -0.5 * LOG2E}
IDX_W = 512 + LANES
MISC_IW = 64
MISC_F = 72
IDX_CAT = 256

CHUNK_TILES = 4
CHUNK_TILES_DIFF = 4
VMEM_LIMIT = 56 * 1024 * 1024
RESIDENT = pl.Buffered(1)


def _cparams(sem):
    return pltpu.CompilerParams(dimension_semantics=sem, vmem_limit_bytes=VMEM_LIMIT)


def _rms(x, g):
    return x * lax.rsqrt(jnp.mean(x * x, axis=-1, keepdims=True) + EPS) * g


def _dot_nt(a, b):
    return lax.dot_general(a, b, (((1,), (1,)), ((), ())), preferred_element_type=F32)


def _dot(a, b):
    return jnp.dot(a, b, preferred_element_type=F32)


def _transpose_bf16(a):
    return a.astype(F32).T.astype(BF16)


def _bf16_part(x):
    return x.astype(BF16).astype(F32)


def _split2(x):
    hi = _bf16_part(x)
    return hi, _bf16_part(x - hi)


def _split3(x):
    hi = _bf16_part(x)
    mid = _bf16_part(x - hi)
    return hi, mid, _bf16_part(x - hi - mid)


def _rope_chunk(xc, cos, sin_signed, half, lane):
    fwd = pltpu.roll(xc, LANES - half, axis=1)
    bwd = pltpu.roll(xc, half, axis=1)
    partner = jnp.where((lane % (2 * half)) < half, fwd, bwd)
    return xc * cos + partner * sin_signed


def _proj_kernel(x_ref, g_ref, wm_ref, wi_ref, c64_ref, s64_ref, c32_ref, s32_ref, bf_ref,
                 qk_ref, vt_ref, iq_ref, ik_ref, iw_ref, carry_ref, *, tm):
    t = pl.program_id(1)
    x = x_ref[0]
    h = _rms(x, g_ref[...])
    pm = jnp.dot(h.astype(BF16), wm_ref[...], preferred_element_type=F32)
    h_hi, h_lo = _split2(h)
    pi = jnp.dot(jnp.concatenate([h_hi, h_lo, h_hi], axis=1).astype(BF16), wi_ref[...],
                 preferred_element_type=F32)
    lane = lax.broadcasted_iota(I32, (1, LANES), 1)
    lo_half = lane < HEAD_DIM
    c64, s64, c32, s32 = c64_ref[0], s64_ref[0], c32_ref[0], s32_ref[0]

    misc = pi[:, 512:512 + LANES]
    z = misc + bf_ref[...]
    logf = jnp.minimum(z, 0.0) - jnp.log1p(jnp.exp(-jnp.abs(z)))
    logf = jnp.where((lane >= MISC_F) & (lane < MISC_F + 4), logf, 0.0)
    tri = (lax.broadcasted_iota(I32, (tm, tm), 1) <= lax.broadcasted_iota(I32, (tm, tm), 0)).astype(BF16)

    @pl.when(t == 0)
    def _():
        carry_ref[...] = jnp.zeros_like(carry_ref)

    pieces = jnp.dot(tri, jnp.concatenate(_split3(logf), axis=1).astype(BF16), preferred_element_type=F32)
    cum = pieces[:, 0:LANES] + pieces[:, LANES:2 * LANES] + pieces[:, 2 * LANES:3 * LANES] + carry_ref[...]
    carry_ref[...] = cum[tm - 1:tm, :]
    nhi, nmid, nlo = _split3(-LOG2E * cum)
    gate_bias = jnp.where(lane < 4, pltpu.roll(nhi, LANES - MISC_F, axis=1),
                          jnp.where(lane < 8, pltpu.roll(nmid, LANES - MISC_F + 4, axis=1),
                                    jnp.where(lane < 12, pltpu.roll(nlo, LANES - MISC_F + 8, axis=1),
                                              jnp.zeros_like(nlo))))

    for gi, name in enumerate(QK_GROUPS):
        for c in range(GROUP // LANES):
            lo = gi * GROUP + c * LANES
            v = pm[:, lo:lo + LANES]
            if name in ROPE64_GROUPS:
                v = _rope_chunk(v, c64, s64, HEAD_DIM // 2, lane)
            elif name in ROPE32_GROUPS:
                v = _rope_chunk(v, c32, s32, DIFF_DIM // 2, lane)
            if name in Q_SCALE:
                v = v * Q_SCALE[name]
            if name == "b_k":
                out = (QK_BLK[name] + c) * GROUP
                qk_ref[0, :, out:out + LANES] = v.astype(BF16)
                qk_ref[0, :, out + LANES:out + GROUP] = gate_bias.astype(BF16)
            else:
                out = QK_BLK[name] * GROUP + c * LANES
                qk_ref[0, :, out:out + LANES] = v.astype(BF16)

    vt_ref[0, 0] = pm[:, V_BASE:V_BASE + len(V_GROUPS) * GROUP].T.astype(BF16)

    def cat_q(q):
        hi, lo = _split2(q)
        return jnp.where(lo_half, hi, pltpu.roll(lo, HEAD_DIM, axis=1)), hi

    for c in range(512 // LANES):
        v = _rope_chunk(pi[:, c * LANES:(c + 1) * LANES], c64, s64, IDX_DIM // 2, lane)
        for r, q in enumerate((jnp.where(lo_half, v, 0.0), jnp.where(lo_half, pltpu.roll(v, HEAD_DIM, axis=1), 0.0))):
            a, b2 = cat_q(q)
            iq_ref[0, 2 * c + r, :, 0:LANES] = a.astype(BF16)
            iq_ref[0, 2 * c + r, :, LANES:IDX_CAT] = b2.astype(BF16)
    ik = jnp.where(lo_half, _rope_chunk(misc, c64, s64, IDX_DIM // 2, lane), 0.0)
    khi, klo = _split2(ik)
    ik_ref[0, :, 0:LANES] = jnp.where(lo_half, khi, pltpu.roll(khi, HEAD_DIM, axis=1)).astype(BF16)
    ik_ref[0, :, LANES:IDX_CAT] = klo.astype(BF16)
    iw_t = (misc * (IDX_HEADS ** -0.5 * IDX_DIM ** -0.5)).T
    iw_ref[0, 0] = iw_t[MISC_IW:MISC_IW + IDX_HEADS, :]


def _proj_call(x, g, w_main, w_idx, tabs, bf_row, *, tm):
    b, s, d = x.shape
    tok = lambda w: pl.BlockSpec((1, tm, w), lambda bi, ti: (bi, ti, 0))
    full = lambda a: pl.BlockSpec(a.shape, lambda bi, ti: (0,) * a.ndim)
    return pl.pallas_call(
        functools.partial(_proj_kernel, tm=tm),
        grid=(b, s // tm),
        in_specs=[tok(d), full(g), full(w_main), full(w_idx), tok(LANES), tok(LANES), tok(LANES), tok(LANES),
                  full(bf_row)],
        out_specs=[tok(QK_W),
                   pl.BlockSpec((1, 1, len(V_GROUPS) * GROUP, tm), lambda bi, ti: (bi, ti, 0, 0)),
                   pl.BlockSpec((1, IDX_HEADS, tm, IDX_CAT), lambda bi, ti: (bi, 0, ti, 0)),
                   tok(IDX_CAT),
                   pl.BlockSpec((1, 1, IDX_HEADS, tm), lambda bi, ti: (bi, ti, 0, 0))],
        out_shape=[jax.ShapeDtypeStruct((b, s, QK_W), BF16),
                   jax.ShapeDtypeStruct((b, s // tm, len(V_GROUPS) * GROUP, tm), BF16),
                   jax.ShapeDtypeStruct((b, IDX_HEADS, s, IDX_CAT), BF16),
                   jax.ShapeDtypeStruct((b, s, IDX_CAT), BF16),
                   jax.ShapeDtypeStruct((b, s // tm, IDX_HEADS, tm), F32)],
        scratch_shapes=[pltpu.VMEM((1, LANES), F32)],
        compiler_params=_cparams(("arbitrary", "arbitrary")),
        name="proj",
    )(x, g, w_main, w_idx, *tabs, bf_row)


def _softmax_step(s, m_ref, l_ref, idx):
    m_old = m_ref[idx]
    m_new = jnp.maximum(m_old, jnp.max(s, axis=0, keepdims=True))
    alpha = jnp.exp2(m_old - m_new)
    p = jnp.exp2(s - m_new)
    l_ref[idx] = alpha * l_ref[idx] + jnp.sum(p, axis=0, keepdims=True)
    m_ref[idx] = m_new
    return p, alpha


def _init_state(m_ref, l_ref, acc_ref):
    m_ref[...] = jnp.full(m_ref.shape, NEG_INF, F32)
    l_ref[...] = jnp.zeros(l_ref.shape, F32)
    acc_ref[...] = jnp.zeros(acc_ref.shape, F32)


def _causal_chunk(w, t):
    return lax.broadcasted_iota(I32, (w * t, t), 0) <= lax.broadcasted_iota(I32, (w * t, t), 1) + (w - 1) * t


def _head_mask(r):
    lane = lax.broadcasted_iota(I32, (1, LANES), 1)
    return (lane >= HEAD_DIM) if r else (lane < HEAD_DIM)


def _key_chunks(n, chunk_fn, width):
    def _body(c, carry):
        chunk_fn(width * c, width, False)
        return carry

    n_full = n // width
    lax.fori_loop(0, n_full, _body, 0)
    rest = n - n_full * width
    for k in range(width):
        @pl.when(rest == k)
        def _(k=k):
            chunk_fn(n - k, k + 1, True)


def _pipelined_key_chunks(n, scores_fn, rest_fn, width):
    n_full = n // width
    rest = n - n_full * width
    odd = n_full & 1

    @pl.when(odd == 1)
    def _():
        scores_fn(0, 1)
        scores_fn(width, 0)
        rest_fn(0, width, False, 1)

    @pl.when(odd == 0)
    def _():
        scores_fn(0, 0)

    def _body(pair, carry):
        j0 = (odd + 2 * pair) * width
        scores_fn(j0 + width, 1)
        rest_fn(j0, width, False, 0)
        scores_fn(j0 + 2 * width, 0)
        rest_fn(j0 + width, width, False, 1)
        return carry

    lax.fori_loop(0, (n_full - odd) >> 1, _body, 0)
    for k in range(width):
        @pl.when(rest == k)
        def _(k=k):
            rest_fn(n - k, k + 1, True, 0)


def _accumulate(acc_ref, a, r, alpha, vt_tiles, p, t):
    rows = slice(r * HEAD_DIM, (r + 1) * HEAD_DIM)
    pb = p.astype(BF16)
    pv = jnp.dot(vt_tiles[0], pb[0:t], preferred_element_type=F32)
    for i in range(1, len(vt_tiles)):
        pv = pv + jnp.dot(vt_tiles[i], pb[i * t:(i + 1) * t], preferred_element_type=F32)
    acc_ref[a, rows, :] = acc_ref[a, rows, :] * alpha + pv


def _normalised_t(acc_ref, a, l_ref, idx0, idx1):
    return jnp.concatenate([acc_ref[a, 0:HEAD_DIM, :] / l_ref[idx0],
                            acc_ref[a, HEAD_DIM:LANES, :] / l_ref[idx1]], axis=0)


def _dsa_kernel(iq_ref, iw_ref, ik_ref, q_ref, k_ref, vt_ref, o_ref,
                key_ref, pack_ref, m_ref, l_ref, acc_ref, carry_ref, iqt_ref, qzt_ref, s_ref, thr_ref, nge_ref, *, t, topk):
    n = pl.program_id(1)
    for h in range(IDX_HEADS):
        iqt_ref[h] = _transpose_bf16(iq_ref[0, h])
    for h in range(4):
        qzt_ref[h] = _transpose_bf16(jnp.where(_head_mask(h % 2), q_ref[0, :, (h // 2) * LANES:(h // 2 + 1) * LANES],
                                               jnp.zeros((), BF16)))

    def score_chunk(j0, w, diag):
        start = pl.multiple_of(j0 * t, t)
        ik = ik_ref[0, pl.ds(start, w * t), :]
        sc = jnp.zeros((w * t, t), F32)
        for h in range(IDX_HEADS):
            d = _dot(ik, iqt_ref[h])
            sc = sc + iw_ref[0, 0, h:h + 1, :] * jnp.maximum(d, 0.0)
        if diag:
            sc = jnp.where(_causal_chunk(w, t), sc, NEG_INF)
        sc = jnp.where(sc == 0.0, 0.0, sc)
        bits = lax.bitcast_convert_type(sc, I32)
        key = bits ^ ((bits >> 31) & jnp.int32(0x7FFFFFFF))
        top = lax.shift_right_logical(key ^ jnp.int32(INT_MIN), 32 - FIELD_BITS)
        for i in range(w):
            key_ref[j0 + i] = key[i * t:(i + 1) * t]
            pack_ref[j0 + i] = (jnp.left_shift(top[i * t:i * t + t // 2], 16) | top[i * t + t // 2:(i + 1) * t]
                                | jnp.int32(FIELD_GUARDS))

    _key_chunks(n, score_chunk, CHUNK_TILES)

    def tree_sum(parts):
        while len(parts) > 1:
            parts = [parts[i] + parts[i + 1] for i in range(0, len(parts), 2)]
        return parts[0]

    def count(pred):
        def body(j, acc):
            c = jnp.where(pred(key_ref[j]), 1.0, 0.0)
            return acc + tree_sum([c[r * SUBLANES:(r + 1) * SUBLANES, :] for r in range(t // SUBLANES)])
        acc = lax.fori_loop(0, n + 1, body, jnp.zeros((SUBLANES, t), F32))
        return jnp.sum(acc, axis=0, keepdims=True)

    def count_top(cand):
        both = jnp.left_shift(cand, 16) | cand

        def body(j, acc):
            z = pack_ref[j] - both
            c = lax.shift_right_logical(z, FIELD_BITS) & jnp.int32(0x00010001)
            return acc + tree_sum([c[r * SUBLANES:(r + 1) * SUBLANES, :] for r in range(t // 2 // SUBLANES)])
        acc = lax.fori_loop(0, n + 1, body, jnp.zeros((SUBLANES, t), I32))
        per_lane = lax.shift_right_logical(acc, 16) + (acc & jnp.int32(0xFFFF))
        return jnp.sum(per_lane.astype(F32), axis=0, keepdims=True)

    def bisect(n_bits, low_bit, count_fn, carry):
        def body(i, carry):
            u, n_ge, n_rej = carry
            cand = u | jnp.left_shift(jnp.int32(1), low_bit + n_bits - 1 - i)
            cnt = count_fn(cand)
            keep = cnt >= float(topk)
            return jnp.where(keep, cand, u), jnp.where(keep, cnt, n_ge), jnp.where(keep, n_rej, cnt)
        return lax.fori_loop(0, n_bits, body, carry)

    def count_full(cand):
        return count(lambda kt: kt >= (cand ^ jnp.int32(INT_MIN)))

    in_scope = ((n + 1) * t).astype(F32)
    zero_row = jnp.zeros((1, t), F32)
    top, n_ge, n_rej = bisect(FIELD_BITS, 0, count_top, (jnp.zeros((1, t), I32), zero_row + in_scope, zero_row))
    u, n_ge, n_rej = bisect(32 - FIELD_BITS - LOW_BITS, LOW_BITS, count_full,
                            (jnp.left_shift(top, 32 - FIELD_BITS), n_ge, n_rej))

    prefix = u ^ jnp.int32(INT_MIN)
    low_mask = jnp.int32((1 << LOW_BITS) - 1)
    big = jnp.int32(1 << (LOW_BITS + 1))

    def tree(parts, op):
        while len(parts) > 1:
            parts = [op(parts[i], parts[i + 1]) for i in range(0, len(parts), 2)]
        return parts[0]

    def band_body(j, carry):
        mx, mn, sm = carry
        kt = key_ref[j]
        inside = lax.shift_right_logical(kt ^ prefix, LOW_BITS) == 0
        low = kt & low_mask
        rows = lambda a: [a[r * SUBLANES:(r + 1) * SUBLANES, :] for r in range(t // SUBLANES)]
        return (jnp.maximum(mx, tree(rows(jnp.where(inside, low, -1)), jnp.maximum)),
                jnp.minimum(mn, tree(rows(jnp.where(inside, low, big)), jnp.minimum)),
                sm + tree(rows(jnp.where(inside, low, 0)), jnp.add))

    mx, mn, sm = lax.fori_loop(0, n + 1, band_body, (jnp.full((SUBLANES, t), -1, I32),
                                                       jnp.full((SUBLANES, t), 1 << (LOW_BITS + 1), I32),
                                                       jnp.zeros((SUBLANES, t), I32)))
    mx = jnp.max(mx.astype(F32), axis=0, keepdims=True)
    mn = jnp.min(mn.astype(F32), axis=0, keepdims=True)
    sm = jnp.sum(sm.astype(F32), axis=0, keepdims=True)
    short = n_ge < float(topk)
    in_band = n_ge - n_rej
    need = float(topk) - n_rej
    second = jnp.where(in_band > 2.5, sm - mx - mn, mn)
    thr_low = jnp.where(need < 1.5, mx, jnp.where(need < 2.5, second, mn))
    kept = (1.0 + jnp.where((in_band > 1.5) & (second >= thr_low), 1.0, 0.0)
            + jnp.where((in_band > 2.5) & (mn >= thr_low), 1.0, 0.0))
    resolved = jnp.min(jnp.where(short | (in_band < 3.5), 1.0, 0.0)) > 0.5

    @pl.when(resolved)
    def _():
        thr_ref[...] = jnp.where(short, prefix, prefix | thr_low.astype(I32))
        nge_ref[...] = jnp.where(short, n_ge, n_rej + kept)

    @pl.when(jnp.logical_not(resolved))
    def _():
        u_all, n_ge_all, _ = bisect(LOW_BITS, 0, count_full, (u, n_ge, n_rej))
        thr_ref[...] = u_all ^ jnp.int32(INT_MIN)
        nge_ref[...] = n_ge_all

    thr = thr_ref[...]
    has_ties = jnp.max(nge_ref[...]) > float(topk)

    _init_state(m_ref, l_ref, acc_ref)

    def attend(ties):
        if ties:
            carry_ref[...] = jnp.zeros(carry_ref.shape, F32)
            need = float(topk) - count(lambda kt: kt > thr)
            lower = (lax.broadcasted_iota(I32, (t, t), 1) <= lax.broadcasted_iota(I32, (t, t), 0)).astype(BF16)

        def scores(j0, buf):
            start = pl.multiple_of(j0 * t, t)
            ks = k_ref[0, pl.ds(start, CHUNK_TILES * t), :]
            for h in range(4):
                s_ref[buf, h] = _dot(ks[:, (h // 2) * LANES:(h // 2 + 1) * LANES], qzt_ref[h])

        def rest(j0, w, diag, buf):
            if ties:
                parts = []
                for i in range(w):
                    kt = key_ref[j0 + i]
                    eq = kt == thr
                    incl = jnp.dot(lower, jnp.where(eq, 1.0, 0.0).astype(BF16),
                                   preferred_element_type=F32) + carry_ref[...]
                    carry_ref[...] = incl[t - 1:t, :]
                    parts.append(jnp.where((kt > thr) | (eq & (incl <= need)), 1.0, 0.0))
                sel = (parts[0] if w == 1 else jnp.concatenate(parts, axis=0)) > 0.5
            else:
                kt = key_ref[j0] if w == 1 else jnp.concatenate([key_ref[j0 + i] for i in range(w)], axis=0)
                sel = kt >= thr
            if diag:
                sel = sel & _causal_chunk(w, t)
            for h in range(4):
                g, r = divmod(h, 2)
                p, alpha = _softmax_step(jnp.where(sel, s_ref[buf, h, 0:w * t, :], -jnp.inf), m_ref, l_ref, h)
                _accumulate(acc_ref, g, r, alpha,
                            [vt_ref[0, j0 + i, h * HEAD_DIM:(h + 1) * HEAD_DIM, :] for i in range(w)], p, t)

        _pipelined_key_chunks(n, scores, rest, CHUNK_TILES)

    @pl.when(has_ties)
    def _():
        attend(True)

    @pl.when(jnp.logical_not(has_ties))
    def _():
        attend(False)

    for g in range(2):
        o_ref[0, :, g * LANES:(g + 1) * LANES] = _normalised_t(acc_ref, g, l_ref, 2 * g, 2 * g + 1).T.astype(BF16)


def _dsa_call(qk, vt, iq, ik, iw, *, t, topk):
    b, s, _ = qk.shape
    nt = s // t
    return pl.pallas_call(
        functools.partial(_dsa_kernel, t=t, topk=topk),
        grid=(b, nt),
        in_specs=[pl.BlockSpec((1, IDX_HEADS, t, IDX_CAT), lambda bi, ni: (bi, 0, ni, 0)),
                  pl.BlockSpec((1, 1, IDX_HEADS, t), lambda bi, ni: (bi, ni, 0, 0)),
                  pl.BlockSpec((1, s, IDX_CAT), lambda bi, ni: (bi, 0, 0), pipeline_mode=RESIDENT),
                  pl.BlockSpec((1, t, GROUP), lambda bi, ni: (bi, ni, QK_BLK["a_q"])),
                  pl.BlockSpec((1, s, GROUP), lambda bi, ni: (bi, 0, QK_BLK["a_k"]), pipeline_mode=RESIDENT),
                  pl.BlockSpec((1, nt, GROUP, t), lambda bi, ni: (bi, 0, V_ROW_BLK["a_v"], 0), pipeline_mode=RESIDENT)],
        out_specs=pl.BlockSpec((1, t, GROUP), lambda bi, ni: (bi, ni, 0)),
        out_shape=jax.ShapeDtypeStruct((b, s, GROUP), BF16),
        scratch_shapes=[pltpu.VMEM((nt, t, t), I32), pltpu.VMEM((nt, t // 2, t), I32),
                        pltpu.VMEM((4, 1, t), F32), pltpu.VMEM((4, 1, t), F32),
                        pltpu.VMEM((2, LANES, t), F32), pltpu.VMEM((1, t), F32),
                        pltpu.VMEM((IDX_HEADS, IDX_CAT, t), BF16), pltpu.VMEM((4, LANES, t), BF16),
                        pltpu.VMEM((2, 4, CHUNK_TILES * t, t), F32),
                        pltpu.VMEM((1, t), I32), pltpu.VMEM((1, t), F32)],
        compiler_params=_cparams(("arbitrary", "arbitrary")),
        name="dsa",
    )(iq, iw, ik, qk, qk, vt)


def _fox_kernel(q_ref, k0_ref, k1_ref, vt_ref, o_ref, m_ref, l_ref, acc_ref, qxt_ref, s_ref, *, t):
    n = pl.program_id(1)
    _init_state(m_ref, l_ref, acc_ref)
    lane = lax.broadcasted_iota(I32, (1, LANES), 1)
    for h in range(4):
        qz = jnp.where(_head_mask(h % 2), q_ref[0, :, (h // 2) * LANES:(h // 2 + 1) * LANES], jnp.zeros((), BF16))
        ones = jnp.where((lane == h) | (lane == 4 + h) | (lane == 8 + h), 1.0, 0.0).astype(BF16)
        qxt_ref[h] = _transpose_bf16(jnp.concatenate([qz, jnp.broadcast_to(ones, (t, LANES))], axis=1))
    k_refs = (k0_ref, k1_ref)

    def scores(j0, buf):
        start = pl.multiple_of(j0 * t, t)
        for h in range(4):
            s_ref[buf, h] = _dot(k_refs[h // 2][0, pl.ds(start, CHUNK_TILES * t), :], qxt_ref[h])

    def rest(j0, w, diag, buf):
        for h in range(4):
            g, r = divmod(h, 2)
            s = s_ref[buf, h, 0:w * t, :]
            if diag:
                s = jnp.where(_causal_chunk(w, t), s, -jnp.inf)
            p, alpha = _softmax_step(s, m_ref, l_ref, h)
            _accumulate(acc_ref, g, r, alpha,
                        [vt_ref[0, j0 + i, h * HEAD_DIM:(h + 1) * HEAD_DIM, :] for i in range(w)], p, t)

    _pipelined_key_chunks(n, scores, rest, CHUNK_TILES)
    for g in range(2):
        o_ref[0, :, g * LANES:(g + 1) * LANES] = _normalised_t(acc_ref, g, l_ref, 2 * g, 2 * g + 1).T.astype(BF16)


def _fox_call(qk, vt, *, t):
    b, s, _ = qk.shape
    nt = s // t
    return pl.pallas_call(
        functools.partial(_fox_kernel, t=t),
        grid=(b, nt),
        in_specs=[pl.BlockSpec((1, t, GROUP), lambda bi, ni: (bi, ni, QK_BLK["b_q"])),
                  pl.BlockSpec((1, s, GROUP), lambda bi, ni: (bi, 0, QK_BLK["b_k"]), pipeline_mode=RESIDENT),
                  pl.BlockSpec((1, s, GROUP), lambda bi, ni: (bi, 0, QK_BLK["b_k"] + 1), pipeline_mode=RESIDENT),
                  pl.BlockSpec((1, nt, GROUP, t), lambda bi, ni: (bi, 0, V_ROW_BLK["b_v"], 0), pipeline_mode=RESIDENT)],
        out_specs=pl.BlockSpec((1, t, GROUP), lambda bi, ni: (bi, ni, 0)),
        out_shape=jax.ShapeDtypeStruct((b, s, GROUP), BF16),
        scratch_shapes=[pltpu.VMEM((4, 1, t), F32), pltpu.VMEM((4, 1, t), F32), pltpu.VMEM((2, LANES, t), F32),
                        pltpu.VMEM((4, GROUP, t), BF16), pltpu.VMEM((2, 4, CHUNK_TILES * t, t), F32)],
        compiler_params=_cparams(("arbitrary", "arbitrary")),
        name="fox",
    )(qk, qk, qk, vt)


def _diff_kernel(lam_ref, q_ref, k_ref, vt_ref, gs_ref, o_ref, m_ref, l_ref, acc_ref, qzt_ref, s_ref, *, t):
    n = pl.program_id(1)
    _init_state(m_ref, l_ref, acc_ref)
    lane = lax.broadcasted_iota(I32, (1, LANES), 1)
    for h in range(4):
        for mm in range(2):
            lo = (h % 2) * HEAD_DIM + mm * DIFF_DIM
            qzt_ref[2 * h + mm] = _transpose_bf16(
                jnp.where((lane >= lo) & (lane < lo + DIFF_DIM),
                          q_ref[0, :, (h // 2) * LANES:(h // 2 + 1) * LANES], jnp.zeros((), BF16)))

    def scores(j0, buf):
        start = pl.multiple_of(j0 * t, t)
        ks = k_ref[0, pl.ds(start, CHUNK_TILES_DIFF * t), :]
        for i in range(8):
            s_ref[buf, i] = _dot(ks[:, (i // 4) * LANES:(i // 4 + 1) * LANES], qzt_ref[i])

    def rest(j0, w, diag, buf):
        for h in range(4):
            g, r = divmod(h, 2)
            vts = [vt_ref[0, j0 + i, h * HEAD_DIM:(h + 1) * HEAD_DIM, :] for i in range(w)]
            for mm in range(2):
                s = s_ref[buf, 2 * h + mm, 0:w * t, :]
                if diag:
                    s = jnp.where(_causal_chunk(w, t), s, -jnp.inf)
                p, alpha = _softmax_step(s, m_ref, l_ref, 2 * h + mm)
                _accumulate(acc_ref, 2 * mm + g, r, alpha, vts, p, t)

    _pipelined_key_chunks(n, scores, rest, CHUNK_TILES_DIFF)
    lam = lam_ref[0]
    out_scale = lam_ref[1]
    for g in range(2):
        o1 = _normalised_t(acc_ref, g, l_ref, 4 * g, 4 * g + 2)
        o2 = _normalised_t(acc_ref, 2 + g, l_ref, 4 * g + 1, 4 * g + 3)
        o = o1 - lam * o2
        sq = o * o
        ms = jnp.concatenate(
            [jnp.broadcast_to(jnp.mean(sq[r * HEAD_DIM:(r + 1) * HEAD_DIM], axis=0, keepdims=True), (HEAD_DIM, t))
             for r in range(2)], axis=0)
        y = o * lax.rsqrt(ms + EPS) * gs_ref[...] * out_scale
        o_ref[0, :, g * LANES:(g + 1) * LANES] = y.T.astype(BF16)


def _diff_call(lam2, qk, vt, gs_full, *, t):
    b, s, _ = qk.shape
    nt = s // t
    return pl.pallas_call(
        functools.partial(_diff_kernel, t=t),
        grid=(b, nt),
        in_specs=[pl.BlockSpec(memory_space=pltpu.SMEM),
                  pl.BlockSpec((1, t, GROUP), lambda bi, ni: (bi, ni, QK_BLK["c_q"])),
                  pl.BlockSpec((1, s, GROUP), lambda bi, ni: (bi, 0, QK_BLK["c_k"]), pipeline_mode=RESIDENT),
                  pl.BlockSpec((1, nt, GROUP, t), lambda bi, ni: (bi, 0, V_ROW_BLK["c_v"], 0), pipeline_mode=RESIDENT),
                  pl.BlockSpec((LANES, t), lambda bi, ni: (0, 0))],
        out_specs=pl.BlockSpec((1, t, GROUP), lambda bi, ni: (bi, ni, 0)),
        out_shape=jax.ShapeDtypeStruct((b, s, GROUP), BF16),
        scratch_shapes=[pltpu.VMEM((8, 1, t), F32), pltpu.VMEM((8, 1, t), F32), pltpu.VMEM((4, LANES, t), F32),
                        pltpu.VMEM((8, LANES, t), BF16), pltpu.VMEM((2, 8, CHUNK_TILES_DIFF * t, t), F32)],
        compiler_params=_cparams(("arbitrary", "arbitrary")),
        name="diff",
    )(lam2, qk, qk, vt, gs_full)


def _swa_kernel(sink_ref, q_ref, kp_ref, kc_ref, vtp_ref, vtc_ref, o_ref, *, t):
    n = pl.program_id(1)
    prow = lax.broadcasted_iota(I32, (WINDOW, t), 0)
    pcol = lax.broadcasted_iota(I32, (WINDOW, t), 1)
    mask_prev = (prow > pcol) & (pcol + jnp.where(n > 0, 0, t) < WINDOW)
    crow = lax.broadcasted_iota(I32, (t, t), 0)
    ccol = lax.broadcasted_iota(I32, (t, t), 1)
    mask_cur = (crow <= ccol) & (crow > ccol - WINDOW)
    vtp = vtp_ref[0, 0][:, t - WINDOW:t]
    vtc = vtc_ref[0, 0]
    for g in range(2):
        outs = []
        for r in range(2):
            h = 2 * g + r
            qz = jnp.where(_head_mask(r), q_ref[0, :, g * LANES:(g + 1) * LANES], jnp.zeros((), BF16))
            sp = jnp.where(mask_prev, _dot_nt(kp_ref[0, :, g * LANES:(g + 1) * LANES], qz), -jnp.inf)
            sc = jnp.where(mask_cur, _dot_nt(kc_ref[0, :, g * LANES:(g + 1) * LANES], qz), -jnp.inf)
            sink = sink_ref[h] * LOG2E
            m = jnp.maximum(jnp.maximum(jnp.max(sp, axis=0, keepdims=True),
                                        jnp.max(sc, axis=0, keepdims=True)), sink)
            pp = jnp.exp2(sp - m)
            pc = jnp.exp2(sc - m)
            den = jnp.sum(pp, axis=0, keepdims=True) + jnp.sum(pc, axis=0, keepdims=True) + jnp.exp2(sink - m)
            rows = slice(h * HEAD_DIM, (h + 1) * HEAD_DIM)
            o = (jnp.dot(vtp[rows, :], pp.astype(BF16), preferred_element_type=F32)
                 + jnp.dot(vtc[rows, :], pc.astype(BF16), preferred_element_type=F32))
            outs.append(o / den)
        o_ref[0, :, g * LANES:(g + 1) * LANES] = jnp.concatenate(outs, axis=0).T.astype(BF16)


def _swa_call(sinks, qk, vt, *, t):
    b, s, _ = qk.shape
    nt = s // t
    per = t // WINDOW
    return pl.pallas_call(
        functools.partial(_swa_kernel, t=t),
        grid=(b, nt),
        in_specs=[pl.BlockSpec(memory_space=pltpu.SMEM),
                  pl.BlockSpec((1, t, GROUP), lambda bi, ni: (bi, ni, QK_BLK["d_q"])),
                  pl.BlockSpec((1, WINDOW, GROUP), lambda bi, ni: (bi, jnp.maximum(ni * per - 1, 0), QK_BLK["d_k"])),
                  pl.BlockSpec((1, t, GROUP), lambda bi, ni: (bi, ni, QK_BLK["d_k"])),
                  pl.BlockSpec((1, 1, GROUP, t), lambda bi, ni: (bi, jnp.maximum(ni - 1, 0), V_ROW_BLK["d_v"], 0)),
                  pl.BlockSpec((1, 1, GROUP, t), lambda bi, ni: (bi, ni, V_ROW_BLK["d_v"], 0))],
        out_specs=pl.BlockSpec((1, t, GROUP), lambda bi, ni: (bi, ni, 0)),
        out_shape=jax.ShapeDtypeStruct((b, s, GROUP), BF16),
        compiler_params=_cparams(("arbitrary", "arbitrary")),
        name="swa",
    )(sinks, qk, qk, qk, vt, vt)


def _post_kernel(oa_ref, ob_ref, oc_ref, od_ref, x_ref, p_ref, wo_ref, gmix_ref, gpre_ref, wu_ref, wd_ref, gpost_ref,
                 wg_ref, wp_ref, y_ref, *, chunk):
    acc = jnp.zeros(x_ref.shape[1:], F32)
    for i, o_ref in enumerate((oa_ref, ob_ref, oc_ref, od_ref)):
        acc = acc + jnp.dot(o_ref[0], wo_ref[i * GROUP:(i + 1) * GROUP, :], preferred_element_type=F32)
    x = x_ref[0] + _rms(acc, gmix_ref[...])
    h = _rms(x, gpre_ref[...]).astype(BF16)
    acc = jnp.zeros(x.shape, F32)
    for c in range(D_FF // chunk):
        u = jnp.dot(h, wu_ref[:, c * chunk:(c + 1) * chunk], preferred_element_type=F32)
        u = jnp.square(jnp.maximum(u, 0.0)).astype(BF16)
        acc = acc + jnp.dot(u, wd_ref[c * chunk:(c + 1) * chunk, :], preferred_element_type=F32)
    x = x + _rms(acc, gpost_ref[...])
    gate = jax.nn.sigmoid(jnp.dot(x.astype(BF16), wg_ref[...], preferred_element_type=F32))
    emb = jnp.dot(p_ref[0].astype(BF16), wp_ref[...], preferred_element_type=F32)
    y_ref[0] = x + gate * emb


def _post_call(tok_inputs, full_inputs, *, tm):
    b, s, d = tok_inputs[4].shape
    tok = lambda a: pl.BlockSpec((1, tm, a.shape[2]), lambda bi, ti: (bi, ti, 0))
    full = lambda a: pl.BlockSpec(a.shape, lambda bi, ti: (0,) * a.ndim, pipeline_mode=pl.Buffered(1))
    return pl.pallas_call(
        functools.partial(_post_kernel, chunk=1024),
        grid=(b, s // tm),
        in_specs=[tok(a) for a in tok_inputs] + [full(a) for a in full_inputs],
        out_specs=pl.BlockSpec((1, tm, d), lambda bi, ti: (bi, ti, 0)),
        out_shape=jax.ShapeDtypeStruct((b, s, d), F32),
        compiler_params=_cparams(("arbitrary", "arbitrary")),
        name="post",
    )(*tok_inputs, *full_inputs)


def _dup_kv(w):
    return jnp.concatenate([w[:, :HEAD_DIM], w[:, :HEAD_DIM], w[:, HEAD_DIM:], w[:, HEAD_DIM:]], axis=1)


def _prep_w_in(w):
    sec = lambda n: w[:, _SEC[n][0]:_SEC[n][0] + _SEC[n][1]]
    cols = [_dup_kv(sec(n)) if n in ("d_k", "d_v") else sec(n) for n in QK_GROUPS + V_GROUPS]
    w_main = jnp.concatenate(cols, axis=1).astype(BF16)
    pad = jnp.zeros((w.shape[0], LANES - IDX_DIM - IDX_HEADS - 4), w.dtype)
    w_idx = jnp.concatenate([sec("iq"), sec("ik"), sec("iw"), sec("b_f"), pad], axis=1)
    w_hi = w_idx.astype(BF16)
    w_lo = (w_idx - w_hi.astype(F32)).astype(BF16)
    return w_main, jnp.concatenate([w_hi, w_hi, w_lo], axis=0)


def _rope_tables(positions):
    pos = positions.astype(F32)[..., None]
    lane = jnp.arange(LANES)

    def tabs(dim):
        half = dim // 2
        inv_freq = ROPE_THETA ** (-jnp.arange(half, dtype=F32) / half)
        ang = pos * inv_freq
        sign = jnp.where((lane % dim) < half, -1.0, 1.0).astype(F32)
        reps = (1, 1, LANES // half)
        return jnp.tile(jnp.cos(ang), reps), jnp.tile(jnp.sin(ang), reps) * sign

    c64, s64 = tabs(HEAD_DIM)
    c32, s32 = tabs(DIFF_DIM)
    return c64, s64, c32, s32


def kernel(x, p, positions, w_in, b_forget, lambda_q1, lambda_k1, lambda_q2, lambda_k2, diff_subln, sinks,
           w_out, norm_pre_mix, norm_post_mix, norm_pre_mlp, norm_post_mlp, w_mlp_up, w_mlp_down,
           w_ple_proj, w_ple_gate):
    b, s, d = x.shape
    depth = w_in.shape[0]
    t = min(256, s)
    assert s % (CHUNK_TILES * t) == 0, "sequence length must be a multiple of the attention key chunk"
    topk = min(TOPK_MAX, s // 4)
    tabs = _rope_tables(positions)
    row = lambda v: v.reshape(1, -1).astype(F32)

    for i in range(depth):
        lam_init = 0.8 - 0.6 * math.exp(-0.3 * i)
        w_main, w_idx = _prep_w_in(w_in[i])
        bf_row = jnp.zeros((1, LANES), F32).at[0, MISC_F:MISC_F + 4].set(b_forget[i])
        qk, vt, iq, ik, iw = _proj_call(x, row(norm_pre_mix[i]), w_main, w_idx, tabs, bf_row, tm=t)

        o_a = _dsa_call(qk, vt, iq, ik, iw, t=t, topk=topk)
        o_b = _fox_call(qk, vt, t=t)
        lam = (jnp.exp(jnp.sum(lambda_q1[i] * lambda_k1[i])) - jnp.exp(jnp.sum(lambda_q2[i] * lambda_k2[i]))
               + lam_init)
        lam2 = jnp.stack([lam, jnp.asarray(1.0 - lam_init, F32)]).astype(F32)
        gs_full = jnp.broadcast_to(jnp.concatenate([diff_subln[i], diff_subln[i]]).astype(F32)[:, None], (LANES, t))
        o_c = _diff_call(lam2, qk, vt, gs_full, t=t)
        o_d = _swa_call(sinks[i].astype(F32), qk, vt, t=t)

        x = _post_call([o_a, o_b, o_c, o_d, x, p[i]],
                       [w_out[i].astype(BF16), row(norm_post_mix[i]), row(norm_pre_mlp[i]),
                        w_mlp_up[i].astype(BF16), w_mlp_down[i].astype(BF16), row(norm_post_mlp[i]),
                        w_ple_gate[i].astype(BF16), w_ple_proj[i].astype(BF16)], tm=t)
    return x
```

```python
import functools
import math

import jax
import jax.numpy as jnp
from jax import lax
from jax.experimental import pallas as pl
from jax.experimental.pallas import tpu as pltpu

F32, BF16, I32 = jnp.float32, jnp.bfloat16, jnp.int32

D_MODEL = 1024
HEAD_DIM = 64
DIFF_DIM = 32
IDX_HEADS = 8
IDX_DIM = 64
TOPK_MAX = 256
WINDOW = 128
D_FF = 4 * D_MODEL
ROPE_THETA = 10000.0
EPS = 1e-6
NEG_INF = -1e30
LANES = 128
SUBLANES = 8
GROUP = 256
INT_MIN = -2147483648
FIELD_BITS = 15
FIELD_GUARDS = -2147450880
LOW_BITS = 10

_SEC = {}
_o = 0
for _name, _w in (("a_q", 256), ("a_k", 256), ("a_v", 256), ("iq", 512), ("ik", 64), ("iw", 8),
                  ("b_q", 256), ("b_k", 256), ("b_v", 256), ("b_f", 4),
                  ("c_q", 256), ("c_k", 256), ("c_v", 256), ("d_q", 256), ("d_k", 128), ("d_v", 128)):
    _SEC[_name] = (_o, _w)
    _o += _w

QK_GROUPS = ("a_q", "a_k", "b_q", "b_k", "c_q", "c_k", "d_q", "d_k")
V_GROUPS = ("a_v", "b_v", "c_v", "d_v")
MAIN_W = GROUP * (len(QK_GROUPS) + len(V_GROUPS))
V_BASE = GROUP * len(QK_GROUPS)
QK_BLK = {"a_q": 0, "a_k": 1, "b_q": 2, "b_k": 3, "c_q": 5, "c_k": 6, "d_q": 7, "d_k": 8}
QK_W = GROUP * 9
V_ROW_BLK = {n: i for i, n in enumerate(V_GROUPS)}
ROPE64_GROUPS = ("a_q", "a_k", "d_q", "d_k")
ROPE32_GROUPS = ("c_q", "c_k")
LOG2E = math.log2(math.e)
Q_SCALE = {"a_q": HEAD_DIM ** -0.5 * LOG2E, "b_q": HEAD_DIM ** -0.5 * LOG2E, "c_q": DIFF_DIM ** -0.5 * LOG2E,
           "d_q": HEAD_DIM ** -0.5 * LOG2E}
IDX_W = 512 + LANES
MISC_IW = 64
MISC_F = 72
IDX_CAT = 256

CHUNK_TILES = 4
CHUNK_TILES_DIFF = 2
VMEM_LIMIT = 56 * 1024 * 1024
RESIDENT = pl.Buffered(1)


def _cparams(sem):
    return pltpu.CompilerParams(dimension_semantics=sem, vmem_limit_bytes=VMEM_LIMIT)


def _rms(x, g):
    return x * lax.rsqrt(jnp.mean(x * x, axis=-1, keepdims=True) + EPS) * g


def _dot(a, b):
    return jnp.dot(a, b, preferred_element_type=F32)


def _transpose_bf16(a):
    return a.astype(F32).T.astype(BF16)


def _bf16_part(x):
    return x.astype(BF16).astype(F32)


def _split2(x):
    hi = _bf16_part(x)
    return hi, _bf16_part(x - hi)


def _split3(x):
    hi = _bf16_part(x)
    mid = _bf16_part(x - hi)
    return hi, mid, _bf16_part(x - hi - mid)


def _rope_chunk(xc, cos, sin_signed, half, lane):
    fwd = pltpu.roll(xc, LANES - half, axis=1)
    bwd = pltpu.roll(xc, half, axis=1)
    partner = jnp.where((lane % (2 * half)) < half, fwd, bwd)
    return xc * cos + partner * sin_signed


def _proj_kernel(x_ref, g_ref, wm_ref, wi_ref, c64_ref, s64_ref, c32_ref, s32_ref, bf_ref,
                 qk_ref, vt_ref, iq_ref, ik_ref, iw_ref, carry_ref, *, tm):
    t = pl.program_id(1)
    x = x_ref[0]
    h = _rms(x, g_ref[...])
    pm = jnp.dot(h.astype(BF16), wm_ref[...], preferred_element_type=F32)
    h_hi, h_lo = _split2(h)
    pi = jnp.dot(jnp.concatenate([h_hi, h_lo, h_hi], axis=1).astype(BF16), wi_ref[...],
                 preferred_element_type=F32)
    lane = lax.broadcasted_iota(I32, (1, LANES), 1)
    lo_half = lane < HEAD_DIM
    c64, s64, c32, s32 = c64_ref[0], s64_ref[0], c32_ref[0], s32_ref[0]

    misc = pi[:, 512:512 + LANES]
    z = misc + bf_ref[...]
    logf = jnp.minimum(z, 0.0) - jnp.log1p(jnp.exp(-jnp.abs(z)))
    logf = jnp.where((lane >= MISC_F) & (lane < MISC_F + 4), logf, 0.0)
    tri = (lax.broadcasted_iota(I32, (tm, tm), 1) <= lax.broadcasted_iota(I32, (tm, tm), 0)).astype(BF16)

    @pl.when(t == 0)
    def _():
        carry_ref[...] = jnp.zeros_like(carry_ref)

    pieces = jnp.dot(tri, jnp.concatenate(_split3(logf), axis=1).astype(BF16), preferred_element_type=F32)
    cum = pieces[:, 0:LANES] + pieces[:, LANES:2 * LANES] + pieces[:, 2 * LANES:3 * LANES] + carry_ref[...]
    carry_ref[...] = cum[tm - 1:tm, :]
    nhi, nmid, nlo = _split3(-LOG2E * cum)
    gate_bias = jnp.where(lane < 4, pltpu.roll(nhi, LANES - MISC_F, axis=1),
                          jnp.where(lane < 8, pltpu.roll(nmid, LANES - MISC_F + 4, axis=1),
                                    jnp.where(lane < 12, pltpu.roll(nlo, LANES - MISC_F + 8, axis=1),
                                              jnp.zeros_like(nlo))))

    for gi, name in enumerate(QK_GROUPS):
        for c in range(GROUP // LANES):
            lo = gi * GROUP + c * LANES
            v = pm[:, lo:lo + LANES]
            if name in ROPE64_GROUPS:
                v = _rope_chunk(v, c64, s64, HEAD_DIM // 2, lane)
            elif name in ROPE32_GROUPS:
                v = _rope_chunk(v, c32, s32, DIFF_DIM // 2, lane)
            if name in Q_SCALE:
                v = v * Q_SCALE[name]
            if name == "b_k":
                out = (QK_BLK[name] + c) * GROUP
                qk_ref[0, :, out:out + LANES] = v.astype(BF16)
                qk_ref[0, :, out + LANES:out + GROUP] = gate_bias.astype(BF16)
            else:
                out = QK_BLK[name] * GROUP + c * LANES
                qk_ref[0, :, out:out + LANES] = v.astype(BF16)

    vt_ref[0, 0] = pm[:, V_BASE:V_BASE + len(V_GROUPS) * GROUP].T.astype(BF16)

    def cat_q(q):
        hi, lo = _split2(q)
        return jnp.where(lo_half, hi, pltpu.roll(lo, HEAD_DIM, axis=1)), hi

    for c in range(512 // LANES):
        v = _rope_chunk(pi[:, c * LANES:(c + 1) * LANES], c64, s64, IDX_DIM // 2, lane)
        for r, q in enumerate((jnp.where(lo_half, v, 0.0), jnp.where(lo_half, pltpu.roll(v, HEAD_DIM, axis=1), 0.0))):
            a, b2 = cat_q(q)
            iq_ref[0, 2 * c + r, :, 0:LANES] = a.astype(BF16)
            iq_ref[0, 2 * c + r, :, LANES:IDX_CAT] = b2.astype(BF16)
    ik = jnp.where(lo_half, _rope_chunk(misc, c64, s64, IDX_DIM // 2, lane), 0.0)
    khi, klo = _split2(ik)
    ik_ref[0, :, 0:LANES] = jnp.where(lo_half, khi, pltpu.roll(khi, HEAD_DIM, axis=1)).astype(BF16)
    ik_ref[0, :, LANES:IDX_CAT] = klo.astype(BF16)
    iw_t = (misc * (IDX_HEADS ** -0.5 * IDX_DIM ** -0.5)).T
    iw_ref[0, 0] = iw_t[MISC_IW:MISC_IW + IDX_HEADS, :]


def _proj_call(x, g, w_main, w_idx, tabs, bf_row, *, tm):
    b, s, d = x.shape
    tok = lambda w: pl.BlockSpec((1, tm, w), lambda bi, ti: (bi, ti, 0))
    full = lambda a: pl.BlockSpec(a.shape, lambda bi, ti: (0,) * a.ndim)
    return pl.pallas_call(
        functools.partial(_proj_kernel, tm=tm),
        grid=(b, s // tm),
        in_specs=[tok(d), full(g), full(w_main), full(w_idx), tok(LANES), tok(LANES), tok(LANES), tok(LANES),
                  full(bf_row)],
        out_specs=[tok(QK_W),
                   pl.BlockSpec((1, 1, len(V_GROUPS) * GROUP, tm), lambda bi, ti: (bi, ti, 0, 0)),
                   pl.BlockSpec((1, IDX_HEADS, tm, IDX_CAT), lambda bi, ti: (bi, 0, ti, 0)),
                   tok(IDX_CAT),
                   pl.BlockSpec((1, 1, IDX_HEADS, tm), lambda bi, ti: (bi, ti, 0, 0))],
        out_shape=[jax.ShapeDtypeStruct((b, s, QK_W), BF16),
                   jax.ShapeDtypeStruct((b, s // tm, len(V_GROUPS) * GROUP, tm), BF16),
                   jax.ShapeDtypeStruct((b, IDX_HEADS, s, IDX_CAT), BF16),
                   jax.ShapeDtypeStruct((b, s, IDX_CAT), BF16),
                   jax.ShapeDtypeStruct((b, s // tm, IDX_HEADS, tm), F32)],
        scratch_shapes=[pltpu.VMEM((1, LANES), F32)],
        compiler_params=_cparams(("arbitrary", "arbitrary")),
        name="proj",
    )(x, g, w_main, w_idx, *tabs, bf_row)


def _softmax_step(s, m_ref, l_ref, idx):
    m_old = m_ref[idx]
    m_new = jnp.maximum(m_old, jnp.max(s, axis=0, keepdims=True))
    alpha = jnp.exp2(m_old - m_new)
    p = jnp.exp2(s - m_new)
    l_ref[idx] = alpha * l_ref[idx] + jnp.sum(p, axis=0, keepdims=True)
    m_ref[idx] = m_new
    return p, alpha


def _init_state(m_ref, l_ref, acc_ref):
    m_ref[...] = jnp.full(m_ref.shape, NEG_INF, F32)
    l_ref[...] = jnp.zeros(l_ref.shape, F32)
    acc_ref[...] = jnp.zeros(acc_ref.shape, F32)


def _causal_chunk(w, t):
    return lax.broadcasted_iota(I32, (w * t, t), 0) <= lax.broadcasted_iota(I32, (w * t, t), 1) + (w - 1) * t


def _head_mask(r):
    lane = lax.broadcasted_iota(I32, (1, LANES), 1)
    return (lane >= HEAD_DIM) if r else (lane < HEAD_DIM)


def _key_chunks(n, chunk_fn, width):
    def _body(c, carry):
        chunk_fn(width * c, width, False)
        return carry

    n_full = n // width
    lax.fori_loop(0, n_full, _body, 0)
    rest = n - n_full * width
    for k in range(width):
        @pl.when(rest == k)
        def _(k=k):
            chunk_fn(n - k, k + 1, True)


def _pipelined_key_chunks(n, scores_fn, rest_fn, width):
    n_full = n // width
    rest = n - n_full * width
    odd = n_full & 1

    @pl.when(odd == 1)
    def _():
        scores_fn(0, 1)
        scores_fn(width, 0)
        rest_fn(0, width, False, 1)

    @pl.when(odd == 0)
    def _():
        scores_fn(0, 0)

    def _body(pair, carry):
        j0 = (odd + 2 * pair) * width
        scores_fn(j0 + width, 1)
        rest_fn(j0, width, False, 0)
        scores_fn(j0 + 2 * width, 0)
        rest_fn(j0 + width, width, False, 1)
        return carry

    lax.fori_loop(0, (n_full - odd) >> 1, _body, 0)
    for k in range(width):
        @pl.when(rest == k)
        def _(k=k):
            rest_fn(n - k, k + 1, True, 0)


def _accumulate(acc_ref, a, r, alpha, vt_tiles, p, t):
    rows = slice(r * HEAD_DIM, (r + 1) * HEAD_DIM)
    pb = p.astype(BF16)
    pv = jnp.dot(vt_tiles[0], pb[0:t], preferred_element_type=F32)
    for i in range(1, len(vt_tiles)):
        pv = pv + jnp.dot(vt_tiles[i], pb[i * t:(i + 1) * t], preferred_element_type=F32)
    acc_ref[a, rows, :] = acc_ref[a, rows, :] * alpha + pv


def _normalised_t(acc_ref, a, l_ref, idx0, idx1):
    return jnp.concatenate([acc_ref[a, 0:HEAD_DIM, :] / l_ref[idx0],
                            acc_ref[a, HEAD_DIM:LANES, :] / l_ref[idx1]], axis=0)


def _dsa_kernel(iq_ref, iw_ref, ik_ref, q_ref, k_ref, vt_ref, o_ref,
                key_ref, pack_ref, m_ref, l_ref, acc_ref, carry_ref, iqt_ref, qzt_ref, s_ref, thr_ref, nge_ref, *, t, topk):
    n = pl.program_id(1)
    for h in range(IDX_HEADS):
        iqt_ref[h] = _transpose_bf16(iq_ref[0, h])
    for h in range(4):
        qzt_ref[h] = _transpose_bf16(jnp.where(_head_mask(h % 2), q_ref[0, :, (h // 2) * LANES:(h // 2 + 1) * LANES],
                                               jnp.zeros((), BF16)))

    def score_chunk(j0, w, diag):
        start = pl.multiple_of(j0 * t, t)
        ik = ik_ref[0, pl.ds(start, w * t), :]
        sc = jnp.zeros((w * t, t), F32)
        for h in range(IDX_HEADS):
            d = _dot(ik, iqt_ref[h])
            sc = sc + iw_ref[0, 0, h:h + 1, :] * jnp.maximum(d, 0.0)
        if diag:
            sc = jnp.where(_causal_chunk(w, t), sc, NEG_INF)
        sc = jnp.where(sc == 0.0, 0.0, sc)
        bits = lax.bitcast_convert_type(sc, I32)
        key = bits ^ ((bits >> 31) & jnp.int32(0x7FFFFFFF))
        top = lax.shift_right_logical(key ^ jnp.int32(INT_MIN), 32 - FIELD_BITS)
        for i in range(w):
            key_ref[j0 + i] = key[i * t:(i + 1) * t]
            pack_ref[j0 + i] = (jnp.left_shift(top[i * t:i * t + t // 2], 16) | top[i * t + t // 2:(i + 1) * t]
                                | jnp.int32(FIELD_GUARDS))

    _key_chunks(n, score_chunk, CHUNK_TILES)

    def tree_sum(parts):
        while len(parts) > 1:
            parts = [parts[i] + parts[i + 1] for i in range(0, len(parts), 2)]
        return parts[0]

    def count(pred):
        def body(j, acc):
            c = jnp.where(pred(key_ref[j]), 1.0, 0.0)
            return acc + tree_sum([c[r * SUBLANES:(r + 1) * SUBLANES, :] for r in range(t // SUBLANES)])
        acc = lax.fori_loop(0, n + 1, body, jnp.zeros((SUBLANES, t), F32))
        return jnp.sum(acc, axis=0, keepdims=True)

    def count_top(cand):
        both = jnp.left_shift(cand, 16) | cand

        def body(j, acc):
            z = pack_ref[j] - both
            c = lax.shift_right_logical(z, FIELD_BITS) & jnp.int32(0x00010001)
            return acc + tree_sum([c[r * SUBLANES:(r + 1) * SUBLANES, :] for r in range(t // 2 // SUBLANES)])
        acc = lax.fori_loop(0, n + 1, body, jnp.zeros((SUBLANES, t), I32))
        per_lane = lax.shift_right_logical(acc, 16) + (acc & jnp.int32(0xFFFF))
        return jnp.sum(per_lane.astype(F32), axis=0, keepdims=True)

    def bisect(n_bits, low_bit, count_fn, carry):
        def body(i, carry):
            u, n_ge, n_rej = carry
            cand = u | jnp.left_shift(jnp.int32(1), low_bit + n_bits - 1 - i)
            cnt = count_fn(cand)
            keep = cnt >= float(topk)
            return jnp.where(keep, cand, u), jnp.where(keep, cnt, n_ge), jnp.where(keep, n_rej, cnt)
        return lax.fori_loop(0, n_bits, body, carry)

    def count_full(cand):
        return count(lambda kt: kt >= (cand ^ jnp.int32(INT_MIN)))

    in_scope = ((n + 1) * t).astype(F32)
    zero_row = jnp.zeros((1, t), F32)
    top, n_ge, n_rej = bisect(FIELD_BITS, 0, count_top, (jnp.zeros((1, t), I32), zero_row + in_scope, zero_row))
    u, n_ge, n_rej = bisect(32 - FIELD_BITS - LOW_BITS, LOW_BITS, count_full,
                            (jnp.left_shift(top, 32 - FIELD_BITS), n_ge, n_rej))

    prefix = u ^ jnp.int32(INT_MIN)
    low_mask = jnp.int32((1 << LOW_BITS) - 1)
    big = jnp.int32(1 << (LOW_BITS + 1))

    def tree(parts, op):
        while len(parts) > 1:
            parts = [op(parts[i], parts[i + 1]) for i in range(0, len(parts), 2)]
        return parts[0]

    def band_body(j, carry):
        mx, mn, sm = carry
        kt = key_ref[j]
        inside = lax.shift_right_logical(kt ^ prefix, LOW_BITS) == 0
        low = kt & low_mask
        rows = lambda a: [a[r * SUBLANES:(r + 1) * SUBLANES, :] for r in range(t // SUBLANES)]
        low_or_zero = rows(jnp.where(inside, low, 0))
        return (jnp.maximum(mx, tree(low_or_zero, jnp.maximum)),
                jnp.minimum(mn, tree(rows(jnp.where(inside, low, big)), jnp.minimum)),
                sm + tree(low_or_zero, jnp.add))

    mx, mn, sm = lax.fori_loop(0, n + 1, band_body, (jnp.zeros((SUBLANES, t), I32),
                                                       jnp.full((SUBLANES, t), 1 << (LOW_BITS + 1), I32),
                                                       jnp.zeros((SUBLANES, t), I32)))
    mx = jnp.max(mx.astype(F32), axis=0, keepdims=True)
    mn = jnp.min(mn.astype(F32), axis=0, keepdims=True)
    sm = jnp.sum(sm.astype(F32), axis=0, keepdims=True)
    short = n_ge < float(topk)
    in_band = n_ge - n_rej
    need = float(topk) - n_rej
    second = jnp.where(in_band > 2.5, sm - mx - mn, mn)
    thr_low = jnp.where(need < 1.5, mx, jnp.where(need < 2.5, second, mn))
    kept = (1.0 + jnp.where((in_band > 1.5) & (second >= thr_low), 1.0, 0.0)
            + jnp.where((in_band > 2.5) & (mn >= thr_low), 1.0, 0.0))
    resolved = jnp.min(jnp.where(short | (in_band < 3.5), 1.0, 0.0)) > 0.5

    @pl.when(resolved)
    def _():
        thr_ref[...] = jnp.where(short, prefix, prefix | thr_low.astype(I32))
        nge_ref[...] = jnp.where(short, n_ge, n_rej + kept)

    @pl.when(jnp.logical_not(resolved))
    def _():
        u_all, n_ge_all, _ = bisect(LOW_BITS, 0, count_full, (u, n_ge, n_rej))
        thr_ref[...] = u_all ^ jnp.int32(INT_MIN)
        nge_ref[...] = n_ge_all

    thr = thr_ref[...]
    has_ties = jnp.max(nge_ref[...]) > float(topk)

    _init_state(m_ref, l_ref, acc_ref)

    def attend(ties):
        if ties:
            carry_ref[...] = jnp.zeros(carry_ref.shape, F32)
            need = float(topk) - count(lambda kt: kt > thr)
            lower = (lax.broadcasted_iota(I32, (t, t), 1) <= lax.broadcasted_iota(I32, (t, t), 0)).astype(BF16)

        def scores(j0, buf):
            start = pl.multiple_of(j0 * t, t)
            ks = k_ref[0, pl.ds(start, CHUNK_TILES * t), :]
            for h in range(4):
                s_ref[buf, h] = _dot(ks[:, (h // 2) * LANES:(h // 2 + 1) * LANES], qzt_ref[h])

        def rest(j0, w, diag, buf):
            if ties:
                parts = []
                for i in range(w):
                    kt = key_ref[j0 + i]
                    eq = kt == thr
                    incl = jnp.dot(lower, jnp.where(eq, 1.0, 0.0).astype(BF16),
                                   preferred_element_type=F32) + carry_ref[...]
                    carry_ref[...] = incl[t - 1:t, :]
                    parts.append(jnp.where((kt > thr) | (eq & (incl <= need)), 1.0, 0.0))
                sel = (parts[0] if w == 1 else jnp.concatenate(parts, axis=0)) > 0.5
            else:
                kt = key_ref[j0] if w == 1 else jnp.concatenate([key_ref[j0 + i] for i in range(w)], axis=0)
                sel = kt >= thr
            if diag:
                sel = sel & _causal_chunk(w, t)
            for h in range(4):
                g, r = divmod(h, 2)
                p, alpha = _softmax_step(jnp.where(sel, s_ref[buf, h, 0:w * t, :], -jnp.inf), m_ref, l_ref, h)
                _accumulate(acc_ref, g, r, alpha,
                            [vt_ref[0, j0 + i, h * HEAD_DIM:(h + 1) * HEAD_DIM, :] for i in range(w)], p, t)

        _pipelined_key_chunks(n, scores, rest, CHUNK_TILES)

    @pl.when(has_ties)
    def _():
        attend(True)

    @pl.when(jnp.logical_not(has_ties))
    def _():
        attend(False)

    for g in range(2):
        o_ref[0, :, g * LANES:(g + 1) * LANES] = _normalised_t(acc_ref, g, l_ref, 2 * g, 2 * g + 1).T.astype(BF16)


def _dsa_call(qk, vt, iq, ik, iw, *, t, topk):
    b, s, _ = qk.shape
    nt = s // t
    return pl.pallas_call(
        functools.partial(_dsa_kernel, t=t, topk=topk),
        grid=(b, nt),
        in_specs=[pl.BlockSpec((1, IDX_HEADS, t, IDX_CAT), lambda bi, ni: (bi, 0, ni, 0)),
                  pl.BlockSpec((1, 1, IDX_HEADS, t), lambda bi, ni: (bi, ni, 0, 0)),
                  pl.BlockSpec((1, s, IDX_CAT), lambda bi, ni: (bi, 0, 0), pipeline_mode=RESIDENT),
                  pl.BlockSpec((1, t, GROUP), lambda bi, ni: (bi, ni, QK_BLK["a_q"])),
                  pl.BlockSpec((1, s, GROUP), lambda bi, ni: (bi, 0, QK_BLK["a_k"]), pipeline_mode=RESIDENT),
                  pl.BlockSpec((1, nt, GROUP, t), lambda bi, ni: (bi, 0, V_ROW_BLK["a_v"], 0), pipeline_mode=RESIDENT)],
        out_specs=pl.BlockSpec((1, t, GROUP), lambda bi, ni: (bi, ni, 0)),
        out_shape=jax.ShapeDtypeStruct((b, s, GROUP), BF16),
        scratch_shapes=[pltpu.VMEM((nt, t, t), I32), pltpu.VMEM((nt, t // 2, t), I32),
                        pltpu.VMEM((4, 1, t), F32), pltpu.VMEM((4, 1, t), F32),
                        pltpu.VMEM((2, LANES, t), F32), pltpu.VMEM((1, t), F32),
                        pltpu.VMEM((IDX_HEADS, IDX_CAT, t), BF16), pltpu.VMEM((4, LANES, t), BF16),
                        pltpu.VMEM((2, 4, CHUNK_TILES * t, t), F32),
                        pltpu.VMEM((1, t), I32), pltpu.VMEM((1, t), F32)],
        compiler_params=_cparams(("arbitrary", "arbitrary")),
        name="dsa",
    )(iq, iw, ik, qk, qk, vt)


def _fox_kernel(q_ref, k0_ref, k1_ref, vt_ref, o_ref, m_ref, l_ref, acc_ref, qxt_ref, s_ref, *, t):
    n = pl.program_id(1)
    _init_state(m_ref, l_ref, acc_ref)
    lane = lax.broadcasted_iota(I32, (1, LANES), 1)
    for h in range(4):
        qz = jnp.where(_head_mask(h % 2), q_ref[0, :, (h // 2) * LANES:(h // 2 + 1) * LANES], jnp.zeros((), BF16))
        ones = jnp.where((lane == h) | (lane == 4 + h) | (lane == 8 + h), 1.0, 0.0).astype(BF16)
        qxt_ref[h] = _transpose_bf16(jnp.concatenate([qz, jnp.broadcast_to(ones, (t, LANES))], axis=1))
    k_refs = (k0_ref, k1_ref)

    def scores(j0, buf):
        start = pl.multiple_of(j0 * t, t)
        for h in range(4):
            s_ref[buf, h] = _dot(k_refs[h // 2][0, pl.ds(start, CHUNK_TILES * t), :], qxt_ref[h])

    def rest(j0, w, diag, buf):
        for h in range(4):
            g, r = divmod(h, 2)
            s = s_ref[buf, h, 0:w * t, :]
            if diag:
                s = jnp.where(_causal_chunk(w, t), s, -jnp.inf)
            p, alpha = _softmax_step(s, m_ref, l_ref, h)
            _accumulate(acc_ref, g, r, alpha,
                        [vt_ref[0, j0 + i, h * HEAD_DIM:(h + 1) * HEAD_DIM, :] for i in range(w)], p, t)

    _pipelined_key_chunks(n, scores, rest, CHUNK_TILES)
    for g in range(2):
        o_ref[0, :, g * LANES:(g + 1) * LANES] = _normalised_t(acc_ref, g, l_ref, 2 * g, 2 * g + 1).T.astype(BF16)


def _fox_call(qk, vt, *, t):
    b, s, _ = qk.shape
    nt = s // t
    return pl.pallas_call(
        functools.partial(_fox_kernel, t=t),
        grid=(b, nt),
        in_specs=[pl.BlockSpec((1, t, GROUP), lambda bi, ni: (bi, ni, QK_BLK["b_q"])),
                  pl.BlockSpec((1, s, GROUP), lambda bi, ni: (bi, 0, QK_BLK["b_k"]), pipeline_mode=RESIDENT),
                  pl.BlockSpec((1, s, GROUP), lambda bi, ni: (bi, 0, QK_BLK["b_k"] + 1), pipeline_mode=RESIDENT),
                  pl.BlockSpec((1, nt, GROUP, t), lambda bi, ni: (bi, 0, V_ROW_BLK["b_v"], 0), pipeline_mode=RESIDENT)],
        out_specs=pl.BlockSpec((1, t, GROUP), lambda bi, ni: (bi, ni, 0)),
        out_shape=jax.ShapeDtypeStruct((b, s, GROUP), BF16),
        scratch_shapes=[pltpu.VMEM((4, 1, t), F32), pltpu.VMEM((4, 1, t), F32), pltpu.VMEM((2, LANES, t), F32),
                        pltpu.VMEM((4, GROUP, t), BF16), pltpu.VMEM((2, 4, CHUNK_TILES * t, t), F32)],
        compiler_params=_cparams(("arbitrary", "arbitrary")),
        name="fox",
    )(qk, qk, qk, vt)


def _diff_kernel(lam_ref, q_ref, k_ref, vt_ref, gs_ref, o_ref, m_ref, l_ref, acc_ref, qzt_ref, s_ref, *, t):
    n = pl.program_id(1)
    _init_state(m_ref, l_ref, acc_ref)
    lane = lax.broadcasted_iota(I32, (1, LANES), 1)
    for h in range(4):
        for mm in range(2):
            lo = (h % 2) * HEAD_DIM + mm * DIFF_DIM
            qzt_ref[2 * h + mm] = _transpose_bf16(
                jnp.where((lane >= lo) & (lane < lo + DIFF_DIM),
                          q_ref[0, :, (h // 2) * LANES:(h // 2 + 1) * LANES], jnp.zeros((), BF16)))

    def scores(j0, buf):
        start = pl.multiple_of(j0 * t, t)
        ks = k_ref[0, pl.ds(start, CHUNK_TILES_DIFF * t), :]
        for i in range(8):
            s_ref[buf, i] = _dot(ks[:, (i // 4) * LANES:(i // 4 + 1) * LANES], qzt_ref[i])

    def rest(j0, w, diag, buf):
        for h in range(4):
            g, r = divmod(h, 2)
            vts = [vt_ref[0, j0 + i, h * HEAD_DIM:(h + 1) * HEAD_DIM, :] for i in range(w)]
            for mm in range(2):
                s = s_ref[buf, 2 * h + mm, 0:w * t, :]
                if diag:
                    s = jnp.where(_causal_chunk(w, t), s, -jnp.inf)
                p, alpha = _softmax_step(s, m_ref, l_ref, 2 * h + mm)
                _accumulate(acc_ref, 2 * mm + g, r, alpha, vts, p, t)

    _pipelined_key_chunks(n, scores, rest, CHUNK_TILES_DIFF)
    lam = lam_ref[0]
    out_scale = lam_ref[1]
    for g in range(2):
        o1 = _normalised_t(acc_ref, g, l_ref, 4 * g, 4 * g + 2)
        o2 = _normalised_t(acc_ref, 2 + g, l_ref, 4 * g + 1, 4 * g + 3)
        o = o1 - lam * o2
        sq = o * o
        ms = jnp.concatenate(
            [jnp.broadcast_to(jnp.mean(sq[r * HEAD_DIM:(r + 1) * HEAD_DIM], axis=0, keepdims=True), (HEAD_DIM, t))
             for r in range(2)], axis=0)
        y = o * lax.rsqrt(ms + EPS) * gs_ref[...] * out_scale
        o_ref[0, :, g * LANES:(g + 1) * LANES] = y.T.astype(BF16)


def _diff_call(lam2, qk, vt, gs_full, *, t):
    b, s, _ = qk.shape
    nt = s // t
    return pl.pallas_call(
        functools.partial(_diff_kernel, t=t),
        grid=(b, nt),
        in_specs=[pl.BlockSpec(memory_space=pltpu.SMEM),
                  pl.BlockSpec((1, t, GROUP), lambda bi, ni: (bi, ni, QK_BLK["c_q"])),
                  pl.BlockSpec((1, s, GROUP), lambda bi, ni: (bi, 0, QK_BLK["c_k"]), pipeline_mode=RESIDENT),
                  pl.BlockSpec((1, nt, GROUP, t), lambda bi, ni: (bi, 0, V_ROW_BLK["c_v"], 0), pipeline_mode=RESIDENT),
                  pl.BlockSpec((LANES, t), lambda bi, ni: (0, 0))],
        out_specs=pl.BlockSpec((1, t, GROUP), lambda bi, ni: (bi, ni, 0)),
        out_shape=jax.ShapeDtypeStruct((b, s, GROUP), BF16),
        scratch_shapes=[pltpu.VMEM((8, 1, t), F32), pltpu.VMEM((8, 1, t), F32), pltpu.VMEM((4, LANES, t), F32),
                        pltpu.VMEM((8, LANES, t), BF16), pltpu.VMEM((2, 8, CHUNK_TILES_DIFF * t, t), F32)],
        compiler_params=_cparams(("arbitrary", "arbitrary")),
        name="diff",
    )(lam2, qk, qk, vt, gs_full)


def _swa_kernel(sink_ref, q_ref, kp_ref, kc_ref, vtp_ref, vtc_ref, o_ref, *, t):
    n = pl.program_id(1)
    prow = lax.broadcasted_iota(I32, (WINDOW, t), 0)
    pcol = lax.broadcasted_iota(I32, (WINDOW, t), 1)
    mask_prev = (prow > pcol) & (pcol + jnp.where(n > 0, 0, t) < WINDOW)
    crow = lax.broadcasted_iota(I32, (t, t), 0)
    ccol = lax.broadcasted_iota(I32, (t, t), 1)
    mask_cur = (crow <= ccol) & (crow > ccol - WINDOW)
    vtp = vtp_ref[0, 0][:, t - WINDOW:t]
    vtc = vtc_ref[0, 0]
    for g in range(2):
        outs = []
        for r in range(2):
            h = 2 * g + r
            qzt = _transpose_bf16(jnp.where(_head_mask(r), q_ref[0, :, g * LANES:(g + 1) * LANES],
                                            jnp.zeros((), BF16)))
            sp = jnp.where(mask_prev, _dot(kp_ref[0, :, g * LANES:(g + 1) * LANES], qzt), -jnp.inf)
            sc = jnp.where(mask_cur, _dot(kc_ref[0, :, g * LANES:(g + 1) * LANES], qzt), -jnp.inf)
            sink = sink_ref[h] * LOG2E
            m = jnp.maximum(jnp.maximum(jnp.max(sp, axis=0, keepdims=True),
                                        jnp.max(sc, axis=0, keepdims=True)), sink)
            pp = jnp.exp2(sp - m)
            pc = jnp.exp2(sc - m)
            den = jnp.sum(pp, axis=0, keepdims=True) + jnp.sum(pc, axis=0, keepdims=True) + jnp.exp2(sink - m)
            rows = slice(h * HEAD_DIM, (h + 1) * HEAD_DIM)
            o = (jnp.dot(vtp[rows, :], pp.astype(BF16), preferred_element_type=F32)
                 + jnp.dot(vtc[rows, :], pc.astype(BF16), preferred_element_type=F32))
            outs.append(o / den)
        o_ref[0, :, g * LANES:(g + 1) * LANES] = jnp.concatenate(outs, axis=0).T.astype(BF16)


def _swa_call(sinks, qk, vt, *, t):
    b, s, _ = qk.shape
    nt = s // t
    per = t // WINDOW
    return pl.pallas_call(
        functools.partial(_swa_kernel, t=t),
        grid=(b, nt),
        in_specs=[pl.BlockSpec(memory_space=pltpu.SMEM),
                  pl.BlockSpec((1, t, GROUP), lambda bi, ni: (bi, ni, QK_BLK["d_q"])),
                  pl.BlockSpec((1, WINDOW, GROUP), lambda bi, ni: (bi, jnp.maximum(ni * per - 1, 0), QK_BLK["d_k"])),
                  pl.BlockSpec((1, t, GROUP), lambda bi, ni: (bi, ni, QK_BLK["d_k"])),
                  pl.BlockSpec((1, 1, GROUP, t), lambda bi, ni: (bi, jnp.maximum(ni - 1, 0), V_ROW_BLK["d_v"], 0)),
                  pl.BlockSpec((1, 1, GROUP, t), lambda bi, ni: (bi, ni, V_ROW_BLK["d_v"], 0))],
        out_specs=pl.BlockSpec((1, t, GROUP), lambda bi, ni: (bi, ni, 0)),
        out_shape=jax.ShapeDtypeStruct((b, s, GROUP), BF16),
        compiler_params=_cparams(("arbitrary", "arbitrary")),
        name="swa",
    )(sinks, qk, qk, qk, vt, vt)


def _post_kernel(oa_ref, ob_ref, oc_ref, od_ref, x_ref, p_ref, wo_ref, gmix_ref, gpre_ref, wu_ref, wd_ref, gpost_ref,
                 wg_ref, wp_ref, y_ref, *, chunk):
    acc = jnp.zeros(x_ref.shape[1:], F32)
    for i, o_ref in enumerate((oa_ref, ob_ref, oc_ref, od_ref)):
        acc = acc + jnp.dot(o_ref[0], wo_ref[i * GROUP:(i + 1) * GROUP, :], preferred_element_type=F32)
    x = x_ref[0] + _rms(acc, gmix_ref[...])
    h = _rms(x, gpre_ref[...]).astype(BF16)
    acc = jnp.zeros(x.shape, F32)
    for c in range(D_FF // chunk):
        u = jnp.dot(h, wu_ref[:, c * chunk:(c + 1) * chunk], preferred_element_type=F32)
        u = jnp.square(jnp.maximum(u, 0.0)).astype(BF16)
        acc = acc + jnp.dot(u, wd_ref[c * chunk:(c + 1) * chunk, :], preferred_element_type=F32)
    x = x + _rms(acc, gpost_ref[...])
    gate = jax.nn.sigmoid(jnp.dot(x.astype(BF16), wg_ref[...], preferred_element_type=F32))
    emb = jnp.dot(p_ref[0].astype(BF16), wp_ref[...], preferred_element_type=F32)
    y_ref[0] = x + gate * emb


def _post_call(tok_inputs, full_inputs, *, tm):
    b, s, d = tok_inputs[4].shape
    tok = lambda a: pl.BlockSpec((1, tm, a.shape[2]), lambda bi, ti: (bi, ti, 0))
    full = lambda a: pl.BlockSpec(a.shape, lambda bi, ti: (0,) * a.ndim, pipeline_mode=pl.Buffered(1))
    return pl.pallas_call(
        functools.partial(_post_kernel, chunk=1024),
        grid=(b, s // tm),
        in_specs=[tok(a) for a in tok_inputs] + [full(a) for a in full_inputs],
        out_specs=pl.BlockSpec((1, tm, d), lambda bi, ti: (bi, ti, 0)),
        out_shape=jax.ShapeDtypeStruct((b, s, d), F32),
        compiler_params=_cparams(("arbitrary", "arbitrary")),
        name="post",
    )(*tok_inputs, *full_inputs)


def _dup_kv(w):
    return jnp.concatenate([w[:, :HEAD_DIM], w[:, :HEAD_DIM], w[:, HEAD_DIM:], w[:, HEAD_DIM:]], axis=1)


def _prep_w_in(w):
    sec = lambda n: w[:, _SEC[n][0]:_SEC[n][0] + _SEC[n][1]]
    cols = [_dup_kv(sec(n)) if n in ("d_k", "d_v") else sec(n) for n in QK_GROUPS + V_GROUPS]
    w_main = jnp.concatenate(cols, axis=1).astype(BF16)
    pad = jnp.zeros((w.shape[0], LANES - IDX_DIM - IDX_HEADS - 4), w.dtype)
    w_idx = jnp.concatenate([sec("iq"), sec("ik"), sec("iw"), sec("b_f"), pad], axis=1)
    w_hi = w_idx.astype(BF16)
    w_lo = (w_idx - w_hi.astype(F32)).astype(BF16)
    return w_main, jnp.concatenate([w_hi, w_hi, w_lo], axis=0)


def _rope_tables(positions):
    pos = positions.astype(F32)[..., None]
    lane = jnp.arange(LANES)

    def tabs(dim):
        half = dim // 2
        inv_freq = ROPE_THETA ** (-jnp.arange(half, dtype=F32) / half)
        ang = pos * inv_freq
        sign = jnp.where((lane % dim) < half, -1.0, 1.0).astype(F32)
        reps = (1, 1, LANES // half)
        return jnp.tile(jnp.cos(ang), reps), jnp.tile(jnp.sin(ang), reps) * sign

    c64, s64 = tabs(HEAD_DIM)
    c32, s32 = tabs(DIFF_DIM)
    return c64, s64, c32, s32


def kernel(x, p, positions, w_in, b_forget, lambda_q1, lambda_k1, lambda_q2, lambda_k2, diff_subln, sinks,
           w_out, norm_pre_mix, norm_post_mix, norm_pre_mlp, norm_post_mlp, w_mlp_up, w_mlp_down,
           w_ple_proj, w_ple_gate):
    b, s, d = x.shape
    depth = w_in.shape[0]
    t = min(256, s)
    assert s % (CHUNK_TILES * t) == 0, "sequence length must be a multiple of the attention key chunk"
    topk = min(TOPK_MAX, s // 4)
    tabs = _rope_tables(positions)
    row = lambda v: v.reshape(1, -1).astype(F32)

    for i in range(depth):
        lam_init = 0.8 - 0.6 * math.exp(-0.3 * i)
        w_main, w_idx = _prep_w_in(w_in[i])
        bf_row = jnp.zeros((1, LANES), F32).at[0, MISC_F:MISC_F + 4].set(b_forget[i])
        qk, vt, iq, ik, iw = _proj_call(x, row(norm_pre_mix[i]), w_main, w_idx, tabs, bf_row, tm=t)

        o_a = _dsa_call(qk, vt, iq, ik, iw, t=t, topk=topk)
        o_b = _fox_call(qk, vt, t=t)
        lam = (jnp.exp(jnp.sum(lambda_q1[i] * lambda_k1[i])) - jnp.exp(jnp.sum(lambda_q2[i] * lambda_k2[i]))
               + lam_init)
        lam2 = jnp.stack([lam, jnp.asarray(1.0 - lam_init, F32)]).astype(F32)
        gs_full = jnp.broadcast_to(jnp.concatenate([diff_subln[i], diff_subln[i]]).astype(F32)[:, None], (LANES, t))
        o_c = _diff_call(lam2, qk, vt, gs_full, t=t)
        o_d = _swa_call(sinks[i].astype(F32), qk, vt, t=t)

        x = _post_call([o_a, o_b, o_c, o_d, x, p[i]],
                       [w_out[i].astype(BF16), row(norm_post_mix[i]), row(norm_pre_mlp[i]),
                        w_mlp_up[i].astype(BF16), w_mlp_down[i].astype(BF16), row(norm_post_mlp[i]),
                        w_ple_gate[i].astype(BF16), w_ple_proj[i].astype(BF16)], tm=t)
    return x
```

```python
import functools
import math

import jax
import jax.numpy as jnp
from jax import lax
from jax.experimental import pallas as pl
from jax.experimental.pallas import tpu as pltpu

F32, BF16, I32 = jnp.float32, jnp.bfloat16, jnp.int32

D_MODEL = 1024
HEAD_DIM = 64
DIFF_DIM = 32
IDX_HEADS = 8
IDX_DIM = 64
TOPK_MAX = 256
WINDOW = 128
D_FF = 4 * D_MODEL
ROPE_THETA = 10000.0
EPS = 1e-6
NEG_INF = -1e30
LANES = 128
SUBLANES = 8
GROUP = 256
INT_MIN = -2147483648
FIELD_BITS = 15
FIELD_GUARDS = -2147450880
LOW_BITS = 10
BYTE_BITS = 7
BYTE_GUARDS = -2139062144

_SEC = {}
_o = 0
for _name, _w in (("a_q", 256), ("a_k", 256), ("a_v", 256), ("iq", 512), ("ik", 64), ("iw", 8),
                  ("b_q", 256), ("b_k", 256), ("b_v", 256), ("b_f", 4),
                  ("c_q", 256), ("c_k", 256), ("c_v", 256), ("d_q", 256), ("d_k", 128), ("d_v", 128)):
    _SEC[_name] = (_o, _w)
    _o += _w

QK_GROUPS = ("a_q", "a_k", "b_q", "b_k", "c_q", "c_k", "d_q", "d_k")
V_GROUPS = ("a_v", "b_v", "c_v", "d_v")
MAIN_W = GROUP * (len(QK_GROUPS) + len(V_GROUPS))
V_BASE = GROUP * len(QK_GROUPS)
QK_BLK = {"a_q": 0, "a_k": 1, "b_q": 2, "b_k": 3, "c_q": 5, "c_k": 6, "d_q": 7, "d_k": 8}
QK_W = GROUP * 9
V_ROW_BLK = {n: i for i, n in enumerate(V_GROUPS)}
ROPE64_GROUPS = ("a_q", "a_k", "d_q", "d_k")
ROPE32_GROUPS = ("c_q", "c_k")
LOG2E = math.log2(math.e)
Q_SCALE = {"a_q": HEAD_DIM ** -0.5 * LOG2E, "b_q": HEAD_DIM ** -0.5 * LOG2E, "c_q": DIFF_DIM ** -0.5 * LOG2E,
           "d_q": HEAD_DIM ** -0.5 * LOG2E}
IDX_W = 512 + LANES
MISC_IW = 64
MISC_F = 72
IDX_CAT = 256

CHUNK_TILES = 4
CHUNK_TILES_DIFF = 2
VMEM_LIMIT = 56 * 1024 * 1024
RESIDENT = pl.Buffered(1)


def _cparams(sem):
    return pltpu.CompilerParams(dimension_semantics=sem, vmem_limit_bytes=VMEM_LIMIT)


def _rms(x, g):
    return x * lax.rsqrt(jnp.mean(x * x, axis=-1, keepdims=True) + EPS) * g


def _dot(a, b):
    return jnp.dot(a, b, preferred_element_type=F32)


def _transpose_bf16(a):
    return a.astype(F32).T.astype(BF16)


def _bf16_part(x):
    return x.astype(BF16).astype(F32)


def _split2(x):
    hi = _bf16_part(x)
    return hi, _bf16_part(x - hi)


def _split3(x):
    hi = _bf16_part(x)
    mid = _bf16_part(x - hi)
    return hi, mid, _bf16_part(x - hi - mid)


def _rope_chunk(xc, cos, sin_signed, half, lane):
    fwd = pltpu.roll(xc, LANES - half, axis=1)
    bwd = pltpu.roll(xc, half, axis=1)
    partner = jnp.where((lane % (2 * half)) < half, fwd, bwd)
    return xc * cos + partner * sin_signed


def _proj_kernel(x_ref, g_ref, wm_ref, wi_ref, c64_ref, s64_ref, c32_ref, s32_ref, bf_ref,
                 qk_ref, vt_ref, iq_ref, ik_ref, iw_ref, carry_ref, *, tm):
    t = pl.program_id(1)
    x = x_ref[0]
    h = _rms(x, g_ref[...])
    pm = jnp.dot(h.astype(BF16), wm_ref[...], preferred_element_type=F32)
    h_hi, h_lo = _split2(h)
    pi = jnp.dot(jnp.concatenate([h_hi, h_lo, h_hi], axis=1).astype(BF16), wi_ref[...],
                 preferred_element_type=F32)
    lane = lax.broadcasted_iota(I32, (1, LANES), 1)
    lo_half = lane < HEAD_DIM
    c64, s64, c32, s32 = c64_ref[0], s64_ref[0], c32_ref[0], s32_ref[0]

    misc = pi[:, 512:512 + LANES]
    z = misc + bf_ref[...]
    logf = jnp.minimum(z, 0.0) - jnp.log1p(jnp.exp(-jnp.abs(z)))
    logf = jnp.where((lane >= MISC_F) & (lane < MISC_F + 4), logf, 0.0)
    tri = (lax.broadcasted_iota(I32, (tm, tm), 1) <= lax.broadcasted_iota(I32, (tm, tm), 0)).astype(BF16)

    @pl.when(t == 0)
    def _():
        carry_ref[...] = jnp.zeros_like(carry_ref)

    pieces = jnp.dot(tri, jnp.concatenate(_split3(logf), axis=1).astype(BF16), preferred_element_type=F32)
    cum = pieces[:, 0:LANES] + pieces[:, LANES:2 * LANES] + pieces[:, 2 * LANES:3 * LANES] + carry_ref[...]
    carry_ref[...] = cum[tm - 1:tm, :]
    nhi, nmid, nlo = _split3(-LOG2E * cum)
    gate_bias = jnp.where(lane < 4, pltpu.roll(nhi, LANES - MISC_F, axis=1),
                          jnp.where(lane < 8, pltpu.roll(nmid, LANES - MISC_F + 4, axis=1),
                                    jnp.where(lane < 12, pltpu.roll(nlo, LANES - MISC_F + 8, axis=1),
                                              jnp.zeros_like(nlo))))

    for gi, name in enumerate(QK_GROUPS):
        for c in range(GROUP // LANES):
            lo = gi * GROUP + c * LANES
            v = pm[:, lo:lo + LANES]
            if name in ROPE64_GROUPS:
                v = _rope_chunk(v, c64, s64, HEAD_DIM // 2, lane)
            elif name in ROPE32_GROUPS:
                v = _rope_chunk(v, c32, s32, DIFF_DIM // 2, lane)
            if name in Q_SCALE:
                v = v * Q_SCALE[name]
            if name == "b_k":
                out = (QK_BLK[name] + c) * GROUP
                qk_ref[0, :, out:out + LANES] = v.astype(BF16)
                qk_ref[0, :, out + LANES:out + GROUP] = gate_bias.astype(BF16)
            else:
                out = QK_BLK[name] * GROUP + c * LANES
                qk_ref[0, :, out:out + LANES] = v.astype(BF16)

    vt_ref[0, 0] = pm[:, V_BASE:V_BASE + len(V_GROUPS) * GROUP].T.astype(BF16)

    def cat_q(q):
        hi, lo = _split2(q)
        return jnp.where(lo_half, hi, pltpu.roll(lo, HEAD_DIM, axis=1)), hi

    for c in range(512 // LANES):
        v = _rope_chunk(pi[:, c * LANES:(c + 1) * LANES], c64, s64, IDX_DIM // 2, lane)
        for r, q in enumerate((jnp.where(lo_half, v, 0.0), jnp.where(lo_half, pltpu.roll(v, HEAD_DIM, axis=1), 0.0))):
            a, b2 = cat_q(q)
            iq_ref[0, 2 * c + r, :, 0:LANES] = a.astype(BF16)
            iq_ref[0, 2 * c + r, :, LANES:IDX_CAT] = b2.astype(BF16)
    ik = jnp.where(lo_half, _rope_chunk(misc, c64, s64, IDX_DIM // 2, lane), 0.0)
    khi, klo = _split2(ik)
    ik_ref[0, :, 0:LANES] = jnp.where(lo_half, khi, pltpu.roll(khi, HEAD_DIM, axis=1)).astype(BF16)
    ik_ref[0, :, LANES:IDX_CAT] = klo.astype(BF16)
    iw_t = (misc * (IDX_HEADS ** -0.5 * IDX_DIM ** -0.5)).T
    iw_ref[0, 0] = iw_t[MISC_IW:MISC_IW + IDX_HEADS, :]


def _proj_call(x, g, w_main, w_idx, tabs, bf_row, *, tm):
    b, s, d = x.shape
    tok = lambda w: pl.BlockSpec((1, tm, w), lambda bi, ti: (bi, ti, 0))
    full = lambda a: pl.BlockSpec(a.shape, lambda bi, ti: (0,) * a.ndim)
    return pl.pallas_call(
        functools.partial(_proj_kernel, tm=tm),
        grid=(b, s // tm),
        in_specs=[tok(d), full(g), full(w_main), full(w_idx), tok(LANES), tok(LANES), tok(LANES), tok(LANES),
                  full(bf_row)],
        out_specs=[tok(QK_W),
                   pl.BlockSpec((1, 1, len(V_GROUPS) * GROUP, tm), lambda bi, ti: (bi, ti, 0, 0)),
                   pl.BlockSpec((1, IDX_HEADS, tm, IDX_CAT), lambda bi, ti: (bi, 0, ti, 0)),
                   tok(IDX_CAT),
                   pl.BlockSpec((1, 1, IDX_HEADS, tm), lambda bi, ti: (bi, ti, 0, 0))],
        out_shape=[jax.ShapeDtypeStruct((b, s, QK_W), BF16),
                   jax.ShapeDtypeStruct((b, s // tm, len(V_GROUPS) * GROUP, tm), BF16),
                   jax.ShapeDtypeStruct((b, IDX_HEADS, s, IDX_CAT), BF16),
                   jax.ShapeDtypeStruct((b, s, IDX_CAT), BF16),
                   jax.ShapeDtypeStruct((b, s // tm, IDX_HEADS, tm), F32)],
        scratch_shapes=[pltpu.VMEM((1, LANES), F32)],
        compiler_params=_cparams(("arbitrary", "arbitrary")),
        name="proj",
    )(x, g, w_main, w_idx, *tabs, bf_row)


def _softmax_step(s, m_ref, l_ref, idx):
    m_old = m_ref[idx]
    m_new = jnp.maximum(m_old, jnp.max(s, axis=0, keepdims=True))
    alpha = jnp.exp2(m_old - m_new)
    p = jnp.exp2(s - m_new)
    l_ref[idx] = alpha * l_ref[idx] + jnp.sum(p, axis=0, keepdims=True)
    m_ref[idx] = m_new
    return p, alpha


def _init_state(m_ref, l_ref, acc_ref):
    m_ref[...] = jnp.full(m_ref.shape, NEG_INF, F32)
    l_ref[...] = jnp.zeros(l_ref.shape, F32)
    acc_ref[...] = jnp.zeros(acc_ref.shape, F32)


def _causal_chunk(w, t):
    return lax.broadcasted_iota(I32, (w * t, t), 0) <= lax.broadcasted_iota(I32, (w * t, t), 1) + (w - 1) * t


def _head_mask(r):
    lane = lax.broadcasted_iota(I32, (1, LANES), 1)
    return (lane >= HEAD_DIM) if r else (lane < HEAD_DIM)


def _key_chunks(n, chunk_fn, width):
    def _body(c, carry):
        chunk_fn(width * c, width, False)
        return carry

    n_full = n // width
    lax.fori_loop(0, n_full, _body, 0)
    rest = n - n_full * width
    for k in range(width):
        @pl.when(rest == k)
        def _(k=k):
            chunk_fn(n - k, k + 1, True)


def _pipelined_key_chunks(n, scores_fn, rest_fn, width):
    n_full = n // width
    rest = n - n_full * width
    odd = n_full & 1

    @pl.when(odd == 1)
    def _():
        scores_fn(0, 1)
        scores_fn(width, 0)
        rest_fn(0, width, False, 1)

    @pl.when(odd == 0)
    def _():
        scores_fn(0, 0)

    def _body(pair, carry):
        j0 = (odd + 2 * pair) * width
        scores_fn(j0 + width, 1)
        rest_fn(j0, width, False, 0)
        scores_fn(j0 + 2 * width, 0)
        rest_fn(j0 + width, width, False, 1)
        return carry

    lax.fori_loop(0, (n_full - odd) >> 1, _body, 0)
    for k in range(width):
        @pl.when(rest == k)
        def _(k=k):
            rest_fn(n - k, k + 1, True, 0)


def _accumulate(acc_ref, a, r, alpha, vt_tiles, p, t):
    rows = slice(r * HEAD_DIM, (r + 1) * HEAD_DIM)
    pb = p.astype(BF16)
    pv = jnp.dot(vt_tiles[0], pb[0:t], preferred_element_type=F32)
    for i in range(1, len(vt_tiles)):
        pv = pv + jnp.dot(vt_tiles[i], pb[i * t:(i + 1) * t], preferred_element_type=F32)
    acc_ref[a, rows, :] = acc_ref[a, rows, :] * alpha + pv


def _normalised_t(acc_ref, a, l_ref, idx0, idx1):
    return jnp.concatenate([acc_ref[a, 0:HEAD_DIM, :] / l_ref[idx0],
                            acc_ref[a, HEAD_DIM:LANES, :] / l_ref[idx1]], axis=0)


def _dsa_kernel(iq_ref, iw_ref, ik_ref, q_ref, k_ref, vt_ref, o_ref,
                key_ref, pack_ref, pack4_ref, m_ref, l_ref, acc_ref, carry_ref, iqt_ref, qzt_ref, s_ref, thr_ref, nge_ref,
                *, t, topk):
    n = pl.program_id(1)
    for h in range(IDX_HEADS):
        iqt_ref[h] = _transpose_bf16(iq_ref[0, h])
    for h in range(4):
        qzt_ref[h] = _transpose_bf16(jnp.where(_head_mask(h % 2), q_ref[0, :, (h // 2) * LANES:(h // 2 + 1) * LANES],
                                               jnp.zeros((), BF16)))

    def score_chunk(j0, w, diag):
        start = pl.multiple_of(j0 * t, t)
        ik = ik_ref[0, pl.ds(start, w * t), :]
        sc = jnp.zeros((w * t, t), F32)
        for h in range(IDX_HEADS):
            d = _dot(ik, iqt_ref[h])
            sc = sc + iw_ref[0, 0, h:h + 1, :] * jnp.maximum(d, 0.0)
        if diag:
            sc = jnp.where(_causal_chunk(w, t), sc, NEG_INF)
        sc = jnp.where(sc == 0.0, 0.0, sc)
        bits = lax.bitcast_convert_type(sc, I32)
        key = bits ^ ((bits >> 31) & jnp.int32(0x7FFFFFFF))
        top = lax.shift_right_logical(key ^ jnp.int32(INT_MIN), 32 - FIELD_BITS)
        for i in range(w):
            key_ref[j0 + i] = key[i * t:(i + 1) * t]
            pack_ref[j0 + i] = (jnp.left_shift(top[i * t:i * t + t // 2], 16) | top[i * t + t // 2:(i + 1) * t]
                                | jnp.int32(FIELD_GUARDS))
            lead = [lax.shift_right_logical(top[i * t + q * (t // 4):i * t + (q + 1) * (t // 4)], FIELD_BITS - BYTE_BITS)
                    for q in range(4)]
            pack4_ref[j0 + i] = (jnp.left_shift(lead[0], 24) | jnp.left_shift(lead[1], 16) | jnp.left_shift(lead[2], 8)
                                 | lead[3] | jnp.int32(BYTE_GUARDS))

    _key_chunks(n, score_chunk, CHUNK_TILES)

    def tree_sum(parts):
        while len(parts) > 1:
            parts = [parts[i] + parts[i + 1] for i in range(0, len(parts), 2)]
        return parts[0]

    def count(pred):
        def body(j, acc):
            c = jnp.where(pred(key_ref[j]), 1.0, 0.0)
            return acc + tree_sum([c[r * SUBLANES:(r + 1) * SUBLANES, :] for r in range(t // SUBLANES)])
        acc = lax.fori_loop(0, n + 1, body, jnp.zeros((SUBLANES, t), F32))
        return jnp.sum(acc, axis=0, keepdims=True)

    def count_top(cand):
        both = jnp.left_shift(cand, 16) | cand

        def body(j, acc):
            z = pack_ref[j] - both
            c = lax.shift_right_logical(z, FIELD_BITS) & jnp.int32(0x00010001)
            return acc + tree_sum([c[r * SUBLANES:(r + 1) * SUBLANES, :] for r in range(t // 2 // SUBLANES)])
        acc = lax.fori_loop(0, n + 1, body, jnp.zeros((SUBLANES, t), I32))
        per_lane = lax.shift_right_logical(acc, 16) + (acc & jnp.int32(0xFFFF))
        return jnp.sum(per_lane.astype(F32), axis=0, keepdims=True)

    def count_lead(cand):
        four = cand * jnp.int32(0x01010101)
        pairs = jnp.int32(0x00FF00FF)

        def body(j, acc):
            even, odd = acc
            z = pack4_ref[j] - four
            c = lax.shift_right_logical(z, BYTE_BITS) & jnp.int32(0x01010101)
            c = tree_sum([c[r * SUBLANES:(r + 1) * SUBLANES, :] for r in range(t // 4 // SUBLANES)])
            return even + (c & pairs), odd + (lax.shift_right_logical(c, 8) & pairs)
        zeros = jnp.zeros((SUBLANES, t), I32)
        even, odd = lax.fori_loop(0, n + 1, body, (zeros, zeros))
        both = even + odd
        per_lane = lax.shift_right_logical(both, 16) + (both & jnp.int32(0xFFFF))
        return jnp.sum(per_lane.astype(F32), axis=0, keepdims=True)

    def bisect(n_bits, low_bit, count_fn, carry):
        def body(i, carry):
            u, n_ge, n_rej = carry
            cand = u | jnp.left_shift(jnp.int32(1), low_bit + n_bits - 1 - i)
            cnt = count_fn(cand)
            keep = cnt >= float(topk)
            return jnp.where(keep, cand, u), jnp.where(keep, cnt, n_ge), jnp.where(keep, n_rej, cnt)
        return lax.fori_loop(0, n_bits, body, carry)

    def count_full(cand):
        return count(lambda kt: kt >= (cand ^ jnp.int32(INT_MIN)))

    in_scope = ((n + 1) * t).astype(F32)
    zero_row = jnp.zeros((1, t), F32)
    lead, n_ge, n_rej = bisect(BYTE_BITS, 0, count_lead, (jnp.zeros((1, t), I32), zero_row + in_scope, zero_row))
    top, n_ge, n_rej = bisect(FIELD_BITS - BYTE_BITS, 0, count_top,
                              (jnp.left_shift(lead, FIELD_BITS - BYTE_BITS), n_ge, n_rej))
    u, n_ge, n_rej = bisect(32 - FIELD_BITS - LOW_BITS, LOW_BITS, count_full,
                            (jnp.left_shift(top, 32 - FIELD_BITS), n_ge, n_rej))

    prefix = u ^ jnp.int32(INT_MIN)
    low_mask = jnp.int32((1 << LOW_BITS) - 1)
    big = jnp.int32(1 << (LOW_BITS + 1))

    def tree(parts, op):
        while len(parts) > 1:
            parts = [op(parts[i], parts[i + 1]) for i in range(0, len(parts), 2)]
        return parts[0]

    def band_body(j, carry):
        mx, mn, sm = carry
        kt = key_ref[j]
        inside = lax.shift_right_logical(kt ^ prefix, LOW_BITS) == 0
        low = kt & low_mask
        rows = lambda a: [a[r * SUBLANES:(r + 1) * SUBLANES, :] for r in range(t // SUBLANES)]
        low_or_zero = rows(jnp.where(inside, low, 0))
        return (jnp.maximum(mx, tree(low_or_zero, jnp.maximum)),
                jnp.minimum(mn, tree(rows(jnp.where(inside, low, big)), jnp.minimum)),
                sm + tree(low_or_zero, jnp.add))

    mx, mn, sm = lax.fori_loop(0, n + 1, band_body, (jnp.zeros((SUBLANES, t), I32),
                                                       jnp.full((SUBLANES, t), 1 << (LOW_BITS + 1), I32),
                                                       jnp.zeros((SUBLANES, t), I32)))
    mx = jnp.max(mx.astype(F32), axis=0, keepdims=True)
    mn = jnp.min(mn.astype(F32), axis=0, keepdims=True)
    sm = jnp.sum(sm.astype(F32), axis=0, keepdims=True)
    short = n_ge < float(topk)
    in_band = n_ge - n_rej
    need = float(topk) - n_rej
    second = jnp.where(in_band > 2.5, sm - mx - mn, mn)
    thr_low = jnp.where(need < 1.5, mx, jnp.where(need < 2.5, second, mn))
    kept = (1.0 + jnp.where((in_band > 1.5) & (second >= thr_low), 1.0, 0.0)
            + jnp.where((in_band > 2.5) & (mn >= thr_low), 1.0, 0.0))
    resolved = jnp.min(jnp.where(short | (in_band < 3.5), 1.0, 0.0)) > 0.5

    @pl.when(resolved)
    def _():
        thr_ref[...] = jnp.where(short, prefix, prefix | thr_low.astype(I32))
        nge_ref[...] = jnp.where(short, n_ge, n_rej + kept)

    @pl.when(jnp.logical_not(resolved))
    def _():
        u_all, n_ge_all, _ = bisect(LOW_BITS, 0, count_full, (u, n_ge, n_rej))
        thr_ref[...] = u_all ^ jnp.int32(INT_MIN)
        nge_ref[...] = n_ge_all

    thr = thr_ref[...]
    has_ties = jnp.max(nge_ref[...]) > float(topk)

    _init_state(m_ref, l_ref, acc_ref)

    def attend(ties):
        if ties:
            carry_ref[...] = jnp.zeros(carry_ref.shape, F32)
            need = float(topk) - count(lambda kt: kt > thr)
            lower = (lax.broadcasted_iota(I32, (t, t), 1) <= lax.broadcasted_iota(I32, (t, t), 0)).astype(BF16)

        def scores(j0, buf):
            start = pl.multiple_of(j0 * t, t)
            ks = k_ref[0, pl.ds(start, CHUNK_TILES * t), :]
            for h in range(4):
                s_ref[buf, h] = _dot(ks[:, (h // 2) * LANES:(h // 2 + 1) * LANES], qzt_ref[h])

        def rest(j0, w, diag, buf):
            if ties:
                parts = []
                for i in range(w):
                    kt = key_ref[j0 + i]
                    eq = kt == thr
                    incl = jnp.dot(lower, jnp.where(eq, 1.0, 0.0).astype(BF16),
                                   preferred_element_type=F32) + carry_ref[...]
                    carry_ref[...] = incl[t - 1:t, :]
                    parts.append(jnp.where((kt > thr) | (eq & (incl <= need)), 1.0, 0.0))
                sel = (parts[0] if w == 1 else jnp.concatenate(parts, axis=0)) > 0.5
            else:
                kt = key_ref[j0] if w == 1 else jnp.concatenate([key_ref[j0 + i] for i in range(w)], axis=0)
                sel = kt >= thr
            if diag:
                sel = sel & _causal_chunk(w, t)
            for h in range(4):
                g, r = divmod(h, 2)
                p, alpha = _softmax_step(jnp.where(sel, s_ref[buf, h, 0:w * t, :], -jnp.inf), m_ref, l_ref, h)
                _accumulate(acc_ref, g, r, alpha,
                            [vt_ref[0, j0 + i, h * HEAD_DIM:(h + 1) * HEAD_DIM, :] for i in range(w)], p, t)

        _pipelined_key_chunks(n, scores, rest, CHUNK_TILES)

    @pl.when(has_ties)
    def _():
        attend(True)

    @pl.when(jnp.logical_not(has_ties))
    def _():
        attend(False)

    for g in range(2):
        o_ref[0, :, g * LANES:(g + 1) * LANES] = _normalised_t(acc_ref, g, l_ref, 2 * g, 2 * g + 1).T.astype(BF16)


def _dsa_call(qk, vt, iq, ik, iw, *, t, topk):
    b, s, _ = qk.shape
    nt = s // t
    return pl.pallas_call(
        functools.partial(_dsa_kernel, t=t, topk=topk),
        grid=(b, nt),
        in_specs=[pl.BlockSpec((1, IDX_HEADS, t, IDX_CAT), lambda bi, ni: (bi, 0, ni, 0)),
                  pl.BlockSpec((1, 1, IDX_HEADS, t), lambda bi, ni: (bi, ni, 0, 0)),
                  pl.BlockSpec((1, s, IDX_CAT), lambda bi, ni: (bi, 0, 0), pipeline_mode=RESIDENT),
                  pl.BlockSpec((1, t, GROUP), lambda bi, ni: (bi, ni, QK_BLK["a_q"])),
                  pl.BlockSpec((1, s, GROUP), lambda bi, ni: (bi, 0, QK_BLK["a_k"]), pipeline_mode=RESIDENT),
                  pl.BlockSpec((1, nt, GROUP, t), lambda bi, ni: (bi, 0, V_ROW_BLK["a_v"], 0), pipeline_mode=RESIDENT)],
        out_specs=pl.BlockSpec((1, t, GROUP), lambda bi, ni: (bi, ni, 0)),
        out_shape=jax.ShapeDtypeStruct((b, s, GROUP), BF16),
        scratch_shapes=[pltpu.VMEM((nt, t, t), I32), pltpu.VMEM((nt, t // 2, t), I32), pltpu.VMEM((nt, t // 4, t), I32),
                        pltpu.VMEM((4, 1, t), F32), pltpu.VMEM((4, 1, t), F32),
                        pltpu.VMEM((2, LANES, t), F32), pltpu.VMEM((1, t), F32),
                        pltpu.VMEM((IDX_HEADS, IDX_CAT, t), BF16), pltpu.VMEM((4, LANES, t), BF16),
                        pltpu.VMEM((2, 4, CHUNK_TILES * t, t), F32),
                        pltpu.VMEM((1, t), I32), pltpu.VMEM((1, t), F32)],
        compiler_params=_cparams(("arbitrary", "arbitrary")),
        name="dsa",
    )(iq, iw, ik, qk, qk, vt)


def _fox_kernel(q_ref, k0_ref, k1_ref, vt_ref, o_ref, m_ref, l_ref, acc_ref, qxt_ref, s_ref, *, t):
    n = pl.program_id(1)
    _init_state(m_ref, l_ref, acc_ref)
    lane = lax.broadcasted_iota(I32, (1, LANES), 1)
    for h in range(4):
        qz = jnp.where(_head_mask(h % 2), q_ref[0, :, (h // 2) * LANES:(h // 2 + 1) * LANES], jnp.zeros((), BF16))
        ones = jnp.where((lane == h) | (lane == 4 + h) | (lane == 8 + h), 1.0, 0.0).astype(BF16)
        qxt_ref[h] = _transpose_bf16(jnp.concatenate([qz, jnp.broadcast_to(ones, (t, LANES))], axis=1))
    k_refs = (k0_ref, k1_ref)

    def scores(j0, buf):
        start = pl.multiple_of(j0 * t, t)
        for h in range(4):
            s_ref[buf, h] = _dot(k_refs[h // 2][0, pl.ds(start, CHUNK_TILES * t), :], qxt_ref[h])

    def rest(j0, w, diag, buf):
        for h in range(4):
            g, r = divmod(h, 2)
            s = s_ref[buf, h, 0:w * t, :]
            if diag:
                s = jnp.where(_causal_chunk(w, t), s, -jnp.inf)
            p, alpha = _softmax_step(s, m_ref, l_ref, h)
            _accumulate(acc_ref, g, r, alpha,
                        [vt_ref[0, j0 + i, h * HEAD_DIM:(h + 1) * HEAD_DIM, :] for i in range(w)], p, t)

    _pipelined_key_chunks(n, scores, rest, CHUNK_TILES)
    for g in range(2):
        o_ref[0, :, g * LANES:(g + 1) * LANES] = _normalised_t(acc_ref, g, l_ref, 2 * g, 2 * g + 1).T.astype(BF16)


def _fox_call(qk, vt, *, t):
    b, s, _ = qk.shape
    nt = s // t
    return pl.pallas_call(
        functools.partial(_fox_kernel, t=t),
        grid=(b, nt),
        in_specs=[pl.BlockSpec((1, t, GROUP), lambda bi, ni: (bi, ni, QK_BLK["b_q"])),
                  pl.BlockSpec((1, s, GROUP), lambda bi, ni: (bi, 0, QK_BLK["b_k"]), pipeline_mode=RESIDENT),
                  pl.BlockSpec((1, s, GROUP), lambda bi, ni: (bi, 0, QK_BLK["b_k"] + 1), pipeline_mode=RESIDENT),
                  pl.BlockSpec((1, nt, GROUP, t), lambda bi, ni: (bi, 0, V_ROW_BLK["b_v"], 0), pipeline_mode=RESIDENT)],
        out_specs=pl.BlockSpec((1, t, GROUP), lambda bi, ni: (bi, ni, 0)),
        out_shape=jax.ShapeDtypeStruct((b, s, GROUP), BF16),
        scratch_shapes=[pltpu.VMEM((4, 1, t), F32), pltpu.VMEM((4, 1, t), F32), pltpu.VMEM((2, LANES, t), F32),
                        pltpu.VMEM((4, GROUP, t), BF16), pltpu.VMEM((2, 4, CHUNK_TILES * t, t), F32)],
        compiler_params=_cparams(("arbitrary", "arbitrary")),
        name="fox",
    )(qk, qk, qk, vt)


def _diff_kernel(lam_ref, q_ref, k_ref, vt_ref, gs_ref, o_ref, m_ref, l_ref, acc_ref, qzt_ref, s_ref, *, t):
    n = pl.program_id(1)
    _init_state(m_ref, l_ref, acc_ref)
    lane = lax.broadcasted_iota(I32, (1, LANES), 1)
    for h in range(4):
        for mm in range(2):
            lo = (h % 2) * HEAD_DIM + mm * DIFF_DIM
            qzt_ref[2 * h + mm] = _transpose_bf16(
                jnp.where((lane >= lo) & (lane < lo + DIFF_DIM),
                          q_ref[0, :, (h // 2) * LANES:(h // 2 + 1) * LANES], jnp.zeros((), BF16)))

    def scores(j0, buf):
        start = pl.multiple_of(j0 * t, t)
        ks = k_ref[0, pl.ds(start, CHUNK_TILES_DIFF * t), :]
        for i in range(8):
            s_ref[buf, i] = _dot(ks[:, (i // 4) * LANES:(i // 4 + 1) * LANES], qzt_ref[i])

    def rest(j0, w, diag, buf):
        for h in range(4):
            g, r = divmod(h, 2)
            vts = [vt_ref[0, j0 + i, h * HEAD_DIM:(h + 1) * HEAD_DIM, :] for i in range(w)]
            for mm in range(2):
                s = s_ref[buf, 2 * h + mm, 0:w * t, :]
                if diag:
                    s = jnp.where(_causal_chunk(w, t), s, -jnp.inf)
                p, alpha = _softmax_step(s, m_ref, l_ref, 2 * h + mm)
                _accumulate(acc_ref, 2 * mm + g, r, alpha, vts, p, t)

    _pipelined_key_chunks(n, scores, rest, CHUNK_TILES_DIFF)
    lam = lam_ref[0]
    out_scale = lam_ref[1]
    for g in range(2):
        o1 = _normalised_t(acc_ref, g, l_ref, 4 * g, 4 * g + 2)
        o2 = _normalised_t(acc_ref, 2 + g, l_ref, 4 * g + 1, 4 * g + 3)
        o = o1 - lam * o2
        sq = o * o
        ms = jnp.concatenate(
            [jnp.broadcast_to(jnp.mean(sq[r * HEAD_DIM:(r + 1) * HEAD_DIM], axis=0, keepdims=True), (HEAD_DIM, t))
             for r in range(2)], axis=0)
        y = o * lax.rsqrt(ms + EPS) * gs_ref[...] * out_scale
        o_ref[0, :, g * LANES:(g + 1) * LANES] = y.T.astype(BF16)


def _diff_call(lam2, qk, vt, gs_full, *, t):
    b, s, _ = qk.shape
    nt = s // t
    return pl.pallas_call(
        functools.partial(_diff_kernel, t=t),
        grid=(b, nt),
        in_specs=[pl.BlockSpec(memory_space=pltpu.SMEM),
                  pl.BlockSpec((1, t, GROUP), lambda bi, ni: (bi, ni, QK_BLK["c_q"])),
                  pl.BlockSpec((1, s, GROUP), lambda bi, ni: (bi, 0, QK_BLK["c_k"]), pipeline_mode=RESIDENT),
                  pl.BlockSpec((1, nt, GROUP, t), lambda bi, ni: (bi, 0, V_ROW_BLK["c_v"], 0), pipeline_mode=RESIDENT),
                  pl.BlockSpec((LANES, t), lambda bi, ni: (0, 0))],
        out_specs=pl.BlockSpec((1, t, GROUP), lambda bi, ni: (bi, ni, 0)),
        out_shape=jax.ShapeDtypeStruct((b, s, GROUP), BF16),
        scratch_shapes=[pltpu.VMEM((8, 1, t), F32), pltpu.VMEM((8, 1, t), F32), pltpu.VMEM((4, LANES, t), F32),
                        pltpu.VMEM((8, LANES, t), BF16), pltpu.VMEM((2, 8, CHUNK_TILES_DIFF * t, t), F32)],
        compiler_params=_cparams(("arbitrary", "arbitrary")),
        name="diff",
    )(lam2, qk, qk, vt, gs_full)


def _swa_kernel(sink_ref, q_ref, kp_ref, kc_ref, vtp_ref, vtc_ref, o_ref, *, t):
    n = pl.program_id(1)
    prow = lax.broadcasted_iota(I32, (WINDOW, t), 0)
    pcol = lax.broadcasted_iota(I32, (WINDOW, t), 1)
    mask_prev = (prow > pcol) & (pcol + jnp.where(n > 0, 0, t) < WINDOW)
    crow = lax.broadcasted_iota(I32, (t, t), 0)
    ccol = lax.broadcasted_iota(I32, (t, t), 1)
    mask_cur = (crow <= ccol) & (crow > ccol - WINDOW)
    vtp = vtp_ref[0, 0][:, t - WINDOW:t]
    vtc = vtc_ref[0, 0]
    for g in range(2):
        outs = []
        for r in range(2):
            h = 2 * g + r
            qzt = _transpose_bf16(jnp.where(_head_mask(r), q_ref[0, :, g * LANES:(g + 1) * LANES],
                                            jnp.zeros((), BF16)))
            sp = jnp.where(mask_prev, _dot(kp_ref[0, :, g * LANES:(g + 1) * LANES], qzt), -jnp.inf)
            sc = jnp.where(mask_cur, _dot(kc_ref[0, :, g * LANES:(g + 1) * LANES], qzt), -jnp.inf)
            sink = sink_ref[h] * LOG2E
            m = jnp.maximum(jnp.maximum(jnp.max(sp, axis=0, keepdims=True),
                                        jnp.max(sc, axis=0, keepdims=True)), sink)
            pp = jnp.exp2(sp - m)
            pc = jnp.exp2(sc - m)
            den = jnp.sum(pp, axis=0, keepdims=True) + jnp.sum(pc, axis=0, keepdims=True) + jnp.exp2(sink - m)
            rows = slice(h * HEAD_DIM, (h + 1) * HEAD_DIM)
            o = (jnp.dot(vtp[rows, :], pp.astype(BF16), preferred_element_type=F32)
                 + jnp.dot(vtc[rows, :], pc.astype(BF16), preferred_element_type=F32))
            outs.append(o / den)
        o_ref[0, :, g * LANES:(g + 1) * LANES] = jnp.concatenate(outs, axis=0).T.astype(BF16)


def _swa_call(sinks, qk, vt, *, t):
    b, s, _ = qk.shape
    nt = s // t
    per = t // WINDOW
    return pl.pallas_call(
        functools.partial(_swa_kernel, t=t),
        grid=(b, nt),
        in_specs=[pl.BlockSpec(memory_space=pltpu.SMEM),
                  pl.BlockSpec((1, t, GROUP), lambda bi, ni: (bi, ni, QK_BLK["d_q"])),
                  pl.BlockSpec((1, WINDOW, GROUP), lambda bi, ni: (bi, jnp.maximum(ni * per - 1, 0), QK_BLK["d_k"])),
                  pl.BlockSpec((1, t, GROUP), lambda bi, ni: (bi, ni, QK_BLK["d_k"])),
                  pl.BlockSpec((1, 1, GROUP, t), lambda bi, ni: (bi, jnp.maximum(ni - 1, 0), V_ROW_BLK["d_v"], 0)),
                  pl.BlockSpec((1, 1, GROUP, t), lambda bi, ni: (bi, ni, V_ROW_BLK["d_v"], 0))],
        out_specs=pl.BlockSpec((1, t, GROUP), lambda bi, ni: (bi, ni, 0)),
        out_shape=jax.ShapeDtypeStruct((b, s, GROUP), BF16),
        compiler_params=_cparams(("arbitrary", "arbitrary")),
        name="swa",
    )(sinks, qk, qk, qk, vt, vt)


def _post_kernel(oa_ref, ob_ref, oc_ref, od_ref, x_ref, p_ref, wo_ref, gmix_ref, gpre_ref, wu_ref, wd_ref, gpost_ref,
                 wg_ref, wp_ref, y_ref, *, chunk):
    acc = jnp.zeros(x_ref.shape[1:], F32)
    for i, o_ref in enumerate((oa_ref, ob_ref, oc_ref, od_ref)):
        acc = acc + jnp.dot(o_ref[0], wo_ref[i * GROUP:(i + 1) * GROUP, :], preferred_element_type=F32)
    x = x_ref[0] + _rms(acc, gmix_ref[...])
    h = _rms(x, gpre_ref[...]).astype(BF16)
    acc = jnp.zeros(x.shape, F32)
    for c in range(D_FF // chunk):
        u = jnp.dot(h, wu_ref[:, c * chunk:(c + 1) * chunk], preferred_element_type=F32)
        u = jnp.square(jnp.maximum(u, 0.0)).astype(BF16)
        acc = acc + jnp.dot(u, wd_ref[c * chunk:(c + 1) * chunk, :], preferred_element_type=F32)
    x = x + _rms(acc, gpost_ref[...])
    gate = jax.nn.sigmoid(jnp.dot(x.astype(BF16), wg_ref[...], preferred_element_type=F32))
    emb = jnp.dot(p_ref[0].astype(BF16), wp_ref[...], preferred_element_type=F32)
    y_ref[0] = x + gate * emb


def _post_call(tok_inputs, full_inputs, *, tm):
    b, s, d = tok_inputs[4].shape
    tok = lambda a: pl.BlockSpec((1, tm, a.shape[2]), lambda bi, ti: (bi, ti, 0))
    full = lambda a: pl.BlockSpec(a.shape, lambda bi, ti: (0,) * a.ndim, pipeline_mode=pl.Buffered(1))
    return pl.pallas_call(
        functools.partial(_post_kernel, chunk=1024),
        grid=(b, s // tm),
        in_specs=[tok(a) for a in tok_inputs] + [full(a) for a in full_inputs],
        out_specs=pl.BlockSpec((1, tm, d), lambda bi, ti: (bi, ti, 0)),
        out_shape=jax.ShapeDtypeStruct((b, s, d), F32),
        compiler_params=_cparams(("arbitrary", "arbitrary")),
        name="post",
    )(*tok_inputs, *full_inputs)


def _dup_kv(w):
    return jnp.concatenate([w[:, :HEAD_DIM], w[:, :HEAD_DIM], w[:, HEAD_DIM:], w[:, HEAD_DIM:]], axis=1)


def _prep_w_in(w):
    sec = lambda n: w[:, _SEC[n][0]:_SEC[n][0] + _SEC[n][1]]
    cols = [_dup_kv(sec(n)) if n in ("d_k", "d_v") else sec(n) for n in QK_GROUPS + V_GROUPS]
    w_main = jnp.concatenate(cols, axis=1).astype(BF16)
    pad = jnp.zeros((w.shape[0], LANES - IDX_DIM - IDX_HEADS - 4), w.dtype)
    w_idx = jnp.concatenate([sec("iq"), sec("ik"), sec("iw"), sec("b_f"), pad], axis=1)
    w_hi = w_idx.astype(BF16)
    w_lo = (w_idx - w_hi.astype(F32)).astype(BF16)
    return w_main, jnp.concatenate([w_hi, w_hi, w_lo], axis=0)


def _rope_tables(positions):
    pos = positions.astype(F32)[..., None]
    lane = jnp.arange(LANES)

    def tabs(dim):
        half = dim // 2
        inv_freq = ROPE_THETA ** (-jnp.arange(half, dtype=F32) / half)
        ang = pos * inv_freq
        sign = jnp.where((lane % dim) < half, -1.0, 1.0).astype(F32)
        reps = (1, 1, LANES // half)
        return jnp.tile(jnp.cos(ang), reps), jnp.tile(jnp.sin(ang), reps) * sign

    c64, s64 = tabs(HEAD_DIM)
    c32, s32 = tabs(DIFF_DIM)
    return c64, s64, c32, s32


def kernel(x, p, positions, w_in, b_forget, lambda_q1, lambda_k1, lambda_q2, lambda_k2, diff_subln, sinks,
           w_out, norm_pre_mix, norm_post_mix, norm_pre_mlp, norm_post_mlp, w_mlp_up, w_mlp_down,
           w_ple_proj, w_ple_gate):
    b, s, d = x.shape
    depth = w_in.shape[0]
    t = min(256, s)
    assert s % (CHUNK_TILES * t) == 0, "sequence length must be a multiple of the attention key chunk"
    topk = min(TOPK_MAX, s // 4)
    tabs = _rope_tables(positions)
    row = lambda v: v.reshape(1, -1).astype(F32)

    for i in range(depth):
        lam_init = 0.8 - 0.6 * math.exp(-0.3 * i)
        w_main, w_idx = _prep_w_in(w_in[i])
        bf_row = jnp.zeros((1, LANES), F32).at[0, MISC_F:MISC_F + 4].set(b_forget[i])
        qk, vt, iq, ik, iw = _proj_call(x, row(norm_pre_mix[i]), w_main, w_idx, tabs, bf_row, tm=t)

        o_a = _dsa_call(qk, vt, iq, ik, iw, t=t, topk=topk)
        o_b = _fox_call(qk, vt, t=t)
        lam = (jnp.exp(jnp.sum(lambda_q1[i] * lambda_k1[i])) - jnp.exp(jnp.sum(lambda_q2[i] * lambda_k2[i]))
               + lam_init)
        lam2 = jnp.stack([lam, jnp.asarray(1.0 - lam_init, F32)]).astype(F32)
        gs_full = jnp.broadcast_to(jnp.concatenate([diff_subln[i], diff_subln[i]]).astype(F32)[:, None], (LANES, t))
        o_c = _diff_call(lam2, qk, vt, gs_full, t=t)
        o_d = _swa_call(sinks[i].astype(F32), qk, vt, t=t)

        x = _post_call([o_a, o_b, o_c, o_d, x, p[i]],
                       [w_out[i].astype(BF16), row(norm_post_mix[i]), row(norm_pre_mlp[i]),
                        w_mlp_up[i].astype(BF16), w_mlp_down[i].astype(BF16), row(norm_post_mlp[i]),
                        w_ple_gate[i].astype(BF16), w_ple_proj[i].astype(BF16)], tm=t)
    return x
```

```python
import functools
import math

import jax
import jax.numpy as jnp
from jax import lax
from jax.experimental import pallas as pl
from jax.experimental.pallas import tpu as pltpu

F32, BF16, I32 = jnp.float32, jnp.bfloat16, jnp.int32

D_MODEL = 1024
HEAD_DIM = 64
DIFF_DIM = 32
IDX_HEADS = 8
IDX_DIM = 64
TOPK_MAX = 256
WINDOW = 128
D_FF = 4 * D_MODEL
ROPE_THETA = 10000.0
EPS = 1e-6
NEG_INF = -1e30
LANES = 128
SUBLANES = 8
GROUP = 256
INT_MIN = -2147483648
FIELD_BITS = 15
FIELD_GUARDS = -2147450880
LOW_BITS = 11
BYTE_BITS = 7
BYTE_GUARDS = -2139062144

_SEC = {}
_o = 0
for _name, _w in (("a_q", 256), ("a_k", 256), ("a_v", 256), ("iq", 512), ("ik", 64), ("iw", 8),
                  ("b_q", 256), ("b_k", 256), ("b_v", 256), ("b_f", 4),
                  ("c_q", 256), ("c_k", 256), ("c_v", 256), ("d_q", 256), ("d_k", 128), ("d_v", 128)):
    _SEC[_name] = (_o, _w)
    _o += _w

QK_GROUPS = ("a_q", "a_k", "b_q", "b_k", "c_q", "c_k", "d_q", "d_k")
V_GROUPS = ("a_v", "b_v", "c_v", "d_v")
MAIN_W = GROUP * (len(QK_GROUPS) + len(V_GROUPS))
V_BASE = GROUP * len(QK_GROUPS)
QK_BLK = {"a_q": 0, "a_k": 1, "b_q": 2, "b_k": 3, "c_q": 5, "c_k": 6, "d_q": 7, "d_k": 8}
QK_W = GROUP * 9
V_ROW_BLK = {n: i for i, n in enumerate(V_GROUPS)}
ROPE64_GROUPS = ("a_q", "a_k", "d_q", "d_k")
ROPE32_GROUPS = ("c_q", "c_k")
LOG2E = math.log2(math.e)
Q_SCALE = {"a_q": HEAD_DIM ** -0.5 * LOG2E, "b_q": HEAD_DIM ** -0.5 * LOG2E, "c_q": DIFF_DIM ** -0.5 * LOG2E,
           "d_q": HEAD_DIM ** -0.5 * LOG2E}
IDX_W = 512 + LANES
MISC_IW = 64
MISC_F = 72
IDX_CAT = 256

CHUNK_TILES = 4
CHUNK_TILES_DIFF = 2
VMEM_LIMIT = 56 * 1024 * 1024
RESIDENT = pl.Buffered(1)


def _cparams(sem):
    return pltpu.CompilerParams(dimension_semantics=sem, vmem_limit_bytes=VMEM_LIMIT)


def _rms(x, g):
    return x * lax.rsqrt(jnp.mean(x * x, axis=-1, keepdims=True) + EPS) * g


def _dot(a, b):
    return jnp.dot(a, b, preferred_element_type=F32)


def _transpose_bf16(a):
    return a.astype(F32).T.astype(BF16)


def _bf16_part(x):
    return x.astype(BF16).astype(F32)


def _split2(x):
    hi = _bf16_part(x)
    return hi, _bf16_part(x - hi)


def _split3(x):
    hi = _bf16_part(x)
    mid = _bf16_part(x - hi)
    return hi, mid, _bf16_part(x - hi - mid)


def _rope_chunk(xc, cos, sin_signed, half, lane):
    fwd = pltpu.roll(xc, LANES - half, axis=1)
    bwd = pltpu.roll(xc, half, axis=1)
    partner = jnp.where((lane % (2 * half)) < half, fwd, bwd)
    return xc * cos + partner * sin_signed


def _proj_kernel(x_ref, g_ref, wm_ref, wi_ref, c64_ref, s64_ref, c32_ref, s32_ref, bf_ref,
                 qk_ref, vt_ref, iq_ref, ik_ref, iw_ref, carry_ref, *, tm):
    t = pl.program_id(1)
    x = x_ref[0]
    h = _rms(x, g_ref[...])
    pm = jnp.dot(h.astype(BF16), wm_ref[...], preferred_element_type=F32)
    h_hi, h_lo = _split2(h)
    pi = jnp.dot(jnp.concatenate([h_hi, h_lo, h_hi], axis=1).astype(BF16), wi_ref[...],
                 preferred_element_type=F32)
    lane = lax.broadcasted_iota(I32, (1, LANES), 1)
    lo_half = lane < HEAD_DIM
    c64, s64, c32, s32 = c64_ref[0], s64_ref[0], c32_ref[0], s32_ref[0]

    misc = pi[:, 512:512 + LANES]
    z = misc + bf_ref[...]
    logf = jnp.minimum(z, 0.0) - jnp.log1p(jnp.exp(-jnp.abs(z)))
    logf = jnp.where((lane >= MISC_F) & (lane < MISC_F + 4), logf, 0.0)
    tri = (lax.broadcasted_iota(I32, (tm, tm), 1) <= lax.broadcasted_iota(I32, (tm, tm), 0)).astype(BF16)

    @pl.when(t == 0)
    def _():
        carry_ref[...] = jnp.zeros_like(carry_ref)

    pieces = jnp.dot(tri, jnp.concatenate(_split3(logf), axis=1).astype(BF16), preferred_element_type=F32)
    cum = pieces[:, 0:LANES] + pieces[:, LANES:2 * LANES] + pieces[:, 2 * LANES:3 * LANES] + carry_ref[...]
    carry_ref[...] = cum[tm - 1:tm, :]
    nhi, nmid, nlo = _split3(-LOG2E * cum)
    gate_bias = jnp.where(lane < 4, pltpu.roll(nhi, LANES - MISC_F, axis=1),
                          jnp.where(lane < 8, pltpu.roll(nmid, LANES - MISC_F + 4, axis=1),
                                    jnp.where(lane < 12, pltpu.roll(nlo, LANES - MISC_F + 8, axis=1),
                                              jnp.zeros_like(nlo))))

    for gi, name in enumerate(QK_GROUPS):
        for c in range(GROUP // LANES):
            lo = gi * GROUP + c * LANES
            v = pm[:, lo:lo + LANES]
            if name in ROPE64_GROUPS:
                v = _rope_chunk(v, c64, s64, HEAD_DIM // 2, lane)
            elif name in ROPE32_GROUPS:
                v = _rope_chunk(v, c32, s32, DIFF_DIM // 2, lane)
            if name in Q_SCALE:
                v = v * Q_SCALE[name]
            if name == "b_k":
                out = (QK_BLK[name] + c) * GROUP
                qk_ref[0, :, out:out + LANES] = v.astype(BF16)
                qk_ref[0, :, out + LANES:out + GROUP] = gate_bias.astype(BF16)
            else:
                out = QK_BLK[name] * GROUP + c * LANES
                qk_ref[0, :, out:out + LANES] = v.astype(BF16)

    vt_ref[0, 0] = pm[:, V_BASE:V_BASE + len(V_GROUPS) * GROUP].T.astype(BF16)

    def cat_q(q):
        hi, lo = _split2(q)
        return jnp.where(lo_half, hi, pltpu.roll(lo, HEAD_DIM, axis=1)), hi

    for c in range(512 // LANES):
        v = _rope_chunk(pi[:, c * LANES:(c + 1) * LANES], c64, s64, IDX_DIM // 2, lane)
        for r, q in enumerate((jnp.where(lo_half, v, 0.0), jnp.where(lo_half, pltpu.roll(v, HEAD_DIM, axis=1), 0.0))):
            a, b2 = cat_q(q)
            iq_ref[0, 2 * c + r, :, 0:LANES] = a.astype(BF16)
            iq_ref[0, 2 * c + r, :, LANES:IDX_CAT] = b2.astype(BF16)
    ik = jnp.where(lo_half, _rope_chunk(misc, c64, s64, IDX_DIM // 2, lane), 0.0)
    khi, klo = _split2(ik)
    ik_ref[0, :, 0:LANES] = jnp.where(lo_half, khi, pltpu.roll(khi, HEAD_DIM, axis=1)).astype(BF16)
    ik_ref[0, :, LANES:IDX_CAT] = klo.astype(BF16)
    iw_t = (misc * (IDX_HEADS ** -0.5 * IDX_DIM ** -0.5)).T
    iw_ref[0, 0] = iw_t[MISC_IW:MISC_IW + IDX_HEADS, :]


def _proj_call(x, g, w_main, w_idx, tabs, bf_row, *, tm):
    b, s, d = x.shape
    tok = lambda w: pl.BlockSpec((1, tm, w), lambda bi, ti: (bi, ti, 0))
    full = lambda a: pl.BlockSpec(a.shape, lambda bi, ti: (0,) * a.ndim)
    return pl.pallas_call(
        functools.partial(_proj_kernel, tm=tm),
        grid=(b, s // tm),
        in_specs=[tok(d), full(g), full(w_main), full(w_idx), tok(LANES), tok(LANES), tok(LANES), tok(LANES),
                  full(bf_row)],
        out_specs=[tok(QK_W),
                   pl.BlockSpec((1, 1, len(V_GROUPS) * GROUP, tm), lambda bi, ti: (bi, ti, 0, 0)),
                   pl.BlockSpec((1, IDX_HEADS, tm, IDX_CAT), lambda bi, ti: (bi, 0, ti, 0)),
                   tok(IDX_CAT),
                   pl.BlockSpec((1, 1, IDX_HEADS, tm), lambda bi, ti: (bi, ti, 0, 0))],
        out_shape=[jax.ShapeDtypeStruct((b, s, QK_W), BF16),
                   jax.ShapeDtypeStruct((b, s // tm, len(V_GROUPS) * GROUP, tm), BF16),
                   jax.ShapeDtypeStruct((b, IDX_HEADS, s, IDX_CAT), BF16),
                   jax.ShapeDtypeStruct((b, s, IDX_CAT), BF16),
                   jax.ShapeDtypeStruct((b, s // tm, IDX_HEADS, tm), F32)],
        scratch_shapes=[pltpu.VMEM((1, LANES), F32)],
        compiler_params=_cparams(("arbitrary", "arbitrary")),
        name="proj",
    )(x, g, w_main, w_idx, *tabs, bf_row)


def _softmax_step(s, m_ref, l_ref, idx):
    m_old = m_ref[idx]
    m_new = jnp.maximum(m_old, jnp.max(s, axis=0, keepdims=True))
    alpha = jnp.exp2(m_old - m_new)
    p = jnp.exp2(s - m_new)
    l_ref[idx] = alpha * l_ref[idx] + jnp.sum(p, axis=0, keepdims=True)
    m_ref[idx] = m_new
    return p, alpha


def _init_state(m_ref, l_ref, acc_ref):
    m_ref[...] = jnp.full(m_ref.shape, NEG_INF, F32)
    l_ref[...] = jnp.zeros(l_ref.shape, F32)
    acc_ref[...] = jnp.zeros(acc_ref.shape, F32)


def _causal_chunk(w, t):
    return lax.broadcasted_iota(I32, (w * t, t), 0) <= lax.broadcasted_iota(I32, (w * t, t), 1) + (w - 1) * t


def _head_mask(r):
    lane = lax.broadcasted_iota(I32, (1, LANES), 1)
    return (lane >= HEAD_DIM) if r else (lane < HEAD_DIM)


def _key_chunks(n, chunk_fn, width):
    def _body(c, carry):
        chunk_fn(width * c, width, False)
        return carry

    n_full = n // width
    lax.fori_loop(0, n_full, _body, 0)
    rest = n - n_full * width
    for k in range(width):
        @pl.when(rest == k)
        def _(k=k):
            chunk_fn(n - k, k + 1, True)


def _pipelined_key_chunks(n, scores_fn, rest_fn, width):
    n_full = n // width
    rest = n - n_full * width
    odd = n_full & 1

    @pl.when(odd == 1)
    def _():
        scores_fn(0, 1)
        scores_fn(width, 0)
        rest_fn(0, width, False, 1)

    @pl.when(odd == 0)
    def _():
        scores_fn(0, 0)

    def _body(pair, carry):
        j0 = (odd + 2 * pair) * width
        scores_fn(j0 + width, 1)
        rest_fn(j0, width, False, 0)
        scores_fn(j0 + 2 * width, 0)
        rest_fn(j0 + width, width, False, 1)
        return carry

    lax.fori_loop(0, (n_full - odd) >> 1, _body, 0)
    for k in range(width):
        @pl.when(rest == k)
        def _(k=k):
            rest_fn(n - k, k + 1, True, 0)


def _accumulate(acc_ref, a, r, alpha, vt_tiles, p, t):
    rows = slice(r * HEAD_DIM, (r + 1) * HEAD_DIM)
    pb = p.astype(BF16)
    pv = jnp.dot(vt_tiles[0], pb[0:t], preferred_element_type=F32)
    for i in range(1, len(vt_tiles)):
        pv = pv + jnp.dot(vt_tiles[i], pb[i * t:(i + 1) * t], preferred_element_type=F32)
    acc_ref[a, rows, :] = acc_ref[a, rows, :] * alpha + pv


def _normalised_t(acc_ref, a, l_ref, idx0, idx1):
    return jnp.concatenate([acc_ref[a, 0:HEAD_DIM, :] / l_ref[idx0],
                            acc_ref[a, HEAD_DIM:LANES, :] / l_ref[idx1]], axis=0)


def _dsa_kernel(iq_ref, iw_ref, ik_ref, q_ref, k_ref, vt_ref, o_ref,
                key_ref, pack_ref, pack4_ref, m_ref, l_ref, acc_ref, carry_ref, iqt_ref, qzt_ref, s_ref, thr_ref, nge_ref,
                *, t, topk):
    n = pl.program_id(1)
    for h in range(IDX_HEADS):
        iqt_ref[h] = _transpose_bf16(iq_ref[0, h])
    for h in range(4):
        qzt_ref[h] = _transpose_bf16(jnp.where(_head_mask(h % 2), q_ref[0, :, (h // 2) * LANES:(h // 2 + 1) * LANES],
                                               jnp.zeros((), BF16)))

    def score_chunk(j0, w, diag):
        start = pl.multiple_of(j0 * t, t)
        ik = ik_ref[0, pl.ds(start, w * t), :]
        sc = jnp.zeros((w * t, t), F32)
        for h in range(IDX_HEADS):
            d = _dot(ik, iqt_ref[h])
            sc = sc + iw_ref[0, 0, h:h + 1, :] * jnp.maximum(d, 0.0)
        if diag:
            sc = jnp.where(_causal_chunk(w, t), sc, NEG_INF)
        sc = jnp.where(sc == 0.0, 0.0, sc)
        bits = lax.bitcast_convert_type(sc, I32)
        key = bits ^ ((bits >> 31) & jnp.int32(0x7FFFFFFF))
        top = lax.shift_right_logical(key ^ jnp.int32(INT_MIN), 32 - FIELD_BITS)
        for i in range(w):
            key_ref[j0 + i] = key[i * t:(i + 1) * t]
            pack_ref[j0 + i] = (jnp.left_shift(top[i * t:i * t + t // 2], 16) | top[i * t + t // 2:(i + 1) * t]
                                | jnp.int32(FIELD_GUARDS))
            lead = [lax.shift_right_logical(top[i * t + q * (t // 4):i * t + (q + 1) * (t // 4)], FIELD_BITS - BYTE_BITS)
                    for q in range(4)]
            pack4_ref[j0 + i] = (jnp.left_shift(lead[0], 24) | jnp.left_shift(lead[1], 16) | jnp.left_shift(lead[2], 8)
                                 | lead[3] | jnp.int32(BYTE_GUARDS))

    _key_chunks(n, score_chunk, CHUNK_TILES)

    def tree_sum(parts):
        while len(parts) > 1:
            parts = [parts[i] + parts[i + 1] for i in range(0, len(parts), 2)]
        return parts[0]

    def count(pred):
        def body(j, acc):
            c = jnp.where(pred(key_ref[j]), 1.0, 0.0)
            return acc + tree_sum([c[r * SUBLANES:(r + 1) * SUBLANES, :] for r in range(t // SUBLANES)])
        acc = lax.fori_loop(0, n + 1, body, jnp.zeros((SUBLANES, t), F32))
        return jnp.sum(acc, axis=0, keepdims=True)

    def count_top(cand):
        both = jnp.left_shift(cand, 16) | cand

        def body(j, acc):
            z = pack_ref[j] - both
            c = lax.shift_right_logical(z, FIELD_BITS) & jnp.int32(0x00010001)
            return acc + tree_sum([c[r * SUBLANES:(r + 1) * SUBLANES, :] for r in range(t // 2 // SUBLANES)])
        acc = lax.fori_loop(0, n + 1, body, jnp.zeros((SUBLANES, t), I32))
        per_lane = lax.shift_right_logical(acc, 16) + (acc & jnp.int32(0xFFFF))
        return jnp.sum(per_lane.astype(F32), axis=0, keepdims=True)

    def count_lead(cand):
        four = cand * jnp.int32(0x01010101)
        pairs = jnp.int32(0x00FF00FF)

        def body(j, acc):
            even, odd = acc
            z = pack4_ref[j] - four
            c = lax.shift_right_logical(z, BYTE_BITS) & jnp.int32(0x01010101)
            c = tree_sum([c[r * SUBLANES:(r + 1) * SUBLANES, :] for r in range(t // 4 // SUBLANES)])
            return even + (c & pairs), odd + (lax.shift_right_logical(c, 8) & pairs)
        zeros = jnp.zeros((SUBLANES, t), I32)
        even, odd = lax.fori_loop(0, n + 1, body, (zeros, zeros))
        both = even + odd
        per_lane = lax.shift_right_logical(both, 16) + (both & jnp.int32(0xFFFF))
        return jnp.sum(per_lane.astype(F32), axis=0, keepdims=True)

    def bisect(n_bits, low_bit, count_fn, carry):
        def body(i, carry):
            u, n_ge, n_rej = carry
            cand = u | jnp.left_shift(jnp.int32(1), low_bit + n_bits - 1 - i)
            cnt = count_fn(cand)
            keep = cnt >= float(topk)
            return jnp.where(keep, cand, u), jnp.where(keep, cnt, n_ge), jnp.where(keep, n_rej, cnt)
        return lax.fori_loop(0, n_bits, body, carry)

    def count_full(cand):
        return count(lambda kt: kt >= (cand ^ jnp.int32(INT_MIN)))

    in_scope = ((n + 1) * t).astype(F32)
    zero_row = jnp.zeros((1, t), F32)
    lead, n_ge, n_rej = bisect(BYTE_BITS, 0, count_lead, (jnp.zeros((1, t), I32), zero_row + in_scope, zero_row))
    top, n_ge, n_rej = bisect(FIELD_BITS - BYTE_BITS, 0, count_top,
                              (jnp.left_shift(lead, FIELD_BITS - BYTE_BITS), n_ge, n_rej))
    u, n_ge, n_rej = bisect(32 - FIELD_BITS - LOW_BITS, LOW_BITS, count_full,
                            (jnp.left_shift(top, 32 - FIELD_BITS), n_ge, n_rej))

    prefix = u ^ jnp.int32(INT_MIN)
    low_mask = jnp.int32((1 << LOW_BITS) - 1)
    big = jnp.int32(1 << (LOW_BITS + 1))

    def tree(parts, op):
        while len(parts) > 1:
            parts = [op(parts[i], parts[i + 1]) for i in range(0, len(parts), 2)]
        return parts[0]

    def band_body(j, carry):
        mx, mn, sm = carry
        kt = key_ref[j]
        inside = lax.shift_right_logical(kt ^ prefix, LOW_BITS) == 0
        low = kt & low_mask
        rows = lambda a: [a[r * SUBLANES:(r + 1) * SUBLANES, :] for r in range(t // SUBLANES)]
        low_or_zero = rows(jnp.where(inside, low, 0))
        return (jnp.maximum(mx, tree(low_or_zero, jnp.maximum)),
                jnp.minimum(mn, tree(rows(jnp.where(inside, low, big)), jnp.minimum)),
                sm + tree(low_or_zero, jnp.add))

    mx, mn, sm = lax.fori_loop(0, n + 1, band_body, (jnp.zeros((SUBLANES, t), I32),
                                                       jnp.full((SUBLANES, t), 1 << (LOW_BITS + 1), I32),
                                                       jnp.zeros((SUBLANES, t), I32)))
    mx = jnp.max(mx.astype(F32), axis=0, keepdims=True)
    mn = jnp.min(mn.astype(F32), axis=0, keepdims=True)
    sm = jnp.sum(sm.astype(F32), axis=0, keepdims=True)
    short = n_ge < float(topk)
    in_band = n_ge - n_rej
    need = float(topk) - n_rej
    second = jnp.where(in_band > 2.5, sm - mx - mn, mn)
    thr_low = jnp.where(need < 1.5, mx, jnp.where(need < 2.5, second, mn))
    kept = (1.0 + jnp.where((in_band > 1.5) & (second >= thr_low), 1.0, 0.0)
            + jnp.where((in_band > 2.5) & (mn >= thr_low), 1.0, 0.0))
    resolved = jnp.min(jnp.where(short | (in_band < 3.5), 1.0, 0.0)) > 0.5

    @pl.when(resolved)
    def _():
        thr_ref[...] = jnp.where(short, prefix, prefix | thr_low.astype(I32))
        nge_ref[...] = jnp.where(short, n_ge, n_rej + kept)

    @pl.when(jnp.logical_not(resolved))
    def _():
        u_all, n_ge_all, _ = bisect(LOW_BITS, 0, count_full, (u, n_ge, n_rej))
        thr_ref[...] = u_all ^ jnp.int32(INT_MIN)
        nge_ref[...] = n_ge_all

    thr = thr_ref[...]
    has_ties = jnp.max(nge_ref[...]) > float(topk)

    _init_state(m_ref, l_ref, acc_ref)

    def attend(ties):
        if ties:
            carry_ref[...] = jnp.zeros(carry_ref.shape, F32)
            need = float(topk) - count(lambda kt: kt > thr)
            lower = (lax.broadcasted_iota(I32, (t, t), 1) <= lax.broadcasted_iota(I32, (t, t), 0)).astype(BF16)

        def scores(j0, buf):
            start = pl.multiple_of(j0 * t, t)
            ks = k_ref[0, pl.ds(start, CHUNK_TILES * t), :]
            for h in range(4):
                s_ref[buf, h] = _dot(ks[:, (h // 2) * LANES:(h // 2 + 1) * LANES], qzt_ref[h])

        def rest(j0, w, diag, buf):
            if ties:
                parts = []
                for i in range(w):
                    kt = key_ref[j0 + i]
                    eq = kt == thr
                    incl = jnp.dot(lower, jnp.where(eq, 1.0, 0.0).astype(BF16),
                                   preferred_element_type=F32) + carry_ref[...]
                    carry_ref[...] = incl[t - 1:t, :]
                    parts.append(jnp.where((kt > thr) | (eq & (incl <= need)), 1.0, 0.0))
                sel = (parts[0] if w == 1 else jnp.concatenate(parts, axis=0)) > 0.5
            else:
                kt = key_ref[j0] if w == 1 else jnp.concatenate([key_ref[j0 + i] for i in range(w)], axis=0)
                sel = kt >= thr
            if diag:
                sel = sel & _causal_chunk(w, t)
            for h in range(4):
                g, r = divmod(h, 2)
                p, alpha = _softmax_step(jnp.where(sel, s_ref[buf, h, 0:w * t, :], -jnp.inf), m_ref, l_ref, h)
                _accumulate(acc_ref, g, r, alpha,
                            [vt_ref[0, j0 + i, h * HEAD_DIM:(h + 1) * HEAD_DIM, :] for i in range(w)], p, t)

        _pipelined_key_chunks(n, scores, rest, CHUNK_TILES)

    @pl.when(has_ties)
    def _():
        attend(True)

    @pl.when(jnp.logical_not(has_ties))
    def _():
        attend(False)

    for g in range(2):
        o_ref[0, :, g * LANES:(g + 1) * LANES] = _normalised_t(acc_ref, g, l_ref, 2 * g, 2 * g + 1).T.astype(BF16)


def _dsa_call(qk, vt, iq, ik, iw, *, t, topk):
    b, s, _ = qk.shape
    nt = s // t
    return pl.pallas_call(
        functools.partial(_dsa_kernel, t=t, topk=topk),
        grid=(b, nt),
        in_specs=[pl.BlockSpec((1, IDX_HEADS, t, IDX_CAT), lambda bi, ni: (bi, 0, ni, 0)),
                  pl.BlockSpec((1, 1, IDX_HEADS, t), lambda bi, ni: (bi, ni, 0, 0)),
                  pl.BlockSpec((1, s, IDX_CAT), lambda bi, ni: (bi, 0, 0), pipeline_mode=RESIDENT),
                  pl.BlockSpec((1, t, GROUP), lambda bi, ni: (bi, ni, QK_BLK["a_q"])),
                  pl.BlockSpec((1, s, GROUP), lambda bi, ni: (bi, 0, QK_BLK["a_k"]), pipeline_mode=RESIDENT),
                  pl.BlockSpec((1, nt, GROUP, t), lambda bi, ni: (bi, 0, V_ROW_BLK["a_v"], 0), pipeline_mode=RESIDENT)],
        out_specs=pl.BlockSpec((1, t, GROUP), lambda bi, ni: (bi, ni, 0)),
        out_shape=jax.ShapeDtypeStruct((b, s, GROUP), BF16),
        scratch_shapes=[pltpu.VMEM((nt, t, t), I32), pltpu.VMEM((nt, t // 2, t), I32), pltpu.VMEM((nt, t // 4, t), I32),
                        pltpu.VMEM((4, 1, t), F32), pltpu.VMEM((4, 1, t), F32),
                        pltpu.VMEM((2, LANES, t), F32), pltpu.VMEM((1, t), F32),
                        pltpu.VMEM((IDX_HEADS, IDX_CAT, t), BF16), pltpu.VMEM((4, LANES, t), BF16),
                        pltpu.VMEM((2, 4, CHUNK_TILES * t, t), F32),
                        pltpu.VMEM((1, t), I32), pltpu.VMEM((1, t), F32)],
        compiler_params=_cparams(("arbitrary", "arbitrary")),
        name="dsa",
    )(iq, iw, ik, qk, qk, vt)


def _fox_kernel(q_ref, k0_ref, k1_ref, vt_ref, o_ref, m_ref, l_ref, acc_ref, qxt_ref, s_ref, *, t):
    n = pl.program_id(1)
    _init_state(m_ref, l_ref, acc_ref)
    lane = lax.broadcasted_iota(I32, (1, LANES), 1)
    for h in range(4):
        qz = jnp.where(_head_mask(h % 2), q_ref[0, :, (h // 2) * LANES:(h // 2 + 1) * LANES], jnp.zeros((), BF16))
        ones = jnp.where((lane == h) | (lane == 4 + h) | (lane == 8 + h), 1.0, 0.0).astype(BF16)
        qxt_ref[h] = _transpose_bf16(jnp.concatenate([qz, jnp.broadcast_to(ones, (t, LANES))], axis=1))
    k_refs = (k0_ref, k1_ref)

    def scores(j0, buf):
        start = pl.multiple_of(j0 * t, t)
        for h in range(4):
            s_ref[buf, h] = _dot(k_refs[h // 2][0, pl.ds(start, CHUNK_TILES * t), :], qxt_ref[h])

    def rest(j0, w, diag, buf):
        for h in range(4):
            g, r = divmod(h, 2)
            s = s_ref[buf, h, 0:w * t, :]
            if diag:
                s = jnp.where(_causal_chunk(w, t), s, -jnp.inf)
            p, alpha = _softmax_step(s, m_ref, l_ref, h)
            _accumulate(acc_ref, g, r, alpha,
                        [vt_ref[0, j0 + i, h * HEAD_DIM:(h + 1) * HEAD_DIM, :] for i in range(w)], p, t)

    _pipelined_key_chunks(n, scores, rest, CHUNK_TILES)
    for g in range(2):
        o_ref[0, :, g * LANES:(g + 1) * LANES] = _normalised_t(acc_ref, g, l_ref, 2 * g, 2 * g + 1).T.astype(BF16)


def _fox_call(qk, vt, *, t):
    b, s, _ = qk.shape
    nt = s // t
    return pl.pallas_call(
        functools.partial(_fox_kernel, t=t),
        grid=(b, nt),
        in_specs=[pl.BlockSpec((1, t, GROUP), lambda bi, ni: (bi, ni, QK_BLK["b_q"])),
                  pl.BlockSpec((1, s, GROUP), lambda bi, ni: (bi, 0, QK_BLK["b_k"]), pipeline_mode=RESIDENT),
                  pl.BlockSpec((1, s, GROUP), lambda bi, ni: (bi, 0, QK_BLK["b_k"] + 1), pipeline_mode=RESIDENT),
                  pl.BlockSpec((1, nt, GROUP, t), lambda bi, ni: (bi, 0, V_ROW_BLK["b_v"], 0), pipeline_mode=RESIDENT)],
        out_specs=pl.BlockSpec((1, t, GROUP), lambda bi, ni: (bi, ni, 0)),
        out_shape=jax.ShapeDtypeStruct((b, s, GROUP), BF16),
        scratch_shapes=[pltpu.VMEM((4, 1, t), F32), pltpu.VMEM((4, 1, t), F32), pltpu.VMEM((2, LANES, t), F32),
                        pltpu.VMEM((4, GROUP, t), BF16), pltpu.VMEM((2, 4, CHUNK_TILES * t, t), F32)],
        compiler_params=_cparams(("arbitrary", "arbitrary")),
        name="fox",
    )(qk, qk, qk, vt)


def _diff_kernel(lam_ref, q_ref, k_ref, vt_ref, gs_ref, o_ref, m_ref, l_ref, acc_ref, qzt_ref, s_ref, *, t):
    n = pl.program_id(1)
    _init_state(m_ref, l_ref, acc_ref)
    lane = lax.broadcasted_iota(I32, (1, LANES), 1)
    for h in range(4):
        for mm in range(2):
            lo = (h % 2) * HEAD_DIM + mm * DIFF_DIM
            qzt_ref[2 * h + mm] = _transpose_bf16(
                jnp.where((lane >= lo) & (lane < lo + DIFF_DIM),
                          q_ref[0, :, (h // 2) * LANES:(h // 2 + 1) * LANES], jnp.zeros((), BF16)))

    def scores(j0, buf):
        start = pl.multiple_of(j0 * t, t)
        ks = k_ref[0, pl.ds(start, CHUNK_TILES_DIFF * t), :]
        for i in range(8):
            s_ref[buf, i] = _dot(ks[:, (i // 4) * LANES:(i // 4 + 1) * LANES], qzt_ref[i])

    def rest(j0, w, diag, buf):
        for h in range(4):
            g, r = divmod(h, 2)
            vts = [vt_ref[0, j0 + i, h * HEAD_DIM:(h + 1) * HEAD_DIM, :] for i in range(w)]
            for mm in range(2):
                s = s_ref[buf, 2 * h + mm, 0:w * t, :]
                if diag:
                    s = jnp.where(_causal_chunk(w, t), s, -jnp.inf)
                p, alpha = _softmax_step(s, m_ref, l_ref, 2 * h + mm)
                _accumulate(acc_ref, 2 * mm + g, r, alpha, vts, p, t)

    _pipelined_key_chunks(n, scores, rest, CHUNK_TILES_DIFF)
    lam = lam_ref[0]
    out_scale = lam_ref[1]
    for g in range(2):
        o1 = _normalised_t(acc_ref, g, l_ref, 4 * g, 4 * g + 2)
        o2 = _normalised_t(acc_ref, 2 + g, l_ref, 4 * g + 1, 4 * g + 3)
        o = o1 - lam * o2
        sq = o * o
        ms = jnp.concatenate(
            [jnp.broadcast_to(jnp.mean(sq[r * HEAD_DIM:(r + 1) * HEAD_DIM], axis=0, keepdims=True), (HEAD_DIM, t))
             for r in range(2)], axis=0)
        y = o * lax.rsqrt(ms + EPS) * gs_ref[...] * out_scale
        o_ref[0, :, g * LANES:(g + 1) * LANES] = y.T.astype(BF16)


def _diff_call(lam2, qk, vt, gs_full, *, t):
    b, s, _ = qk.shape
    nt = s // t
    return pl.pallas_call(
        functools.partial(_diff_kernel, t=t),
        grid=(b, nt),
        in_specs=[pl.BlockSpec(memory_space=pltpu.SMEM),
                  pl.BlockSpec((1, t, GROUP), lambda bi, ni: (bi, ni, QK_BLK["c_q"])),
                  pl.BlockSpec((1, s, GROUP), lambda bi, ni: (bi, 0, QK_BLK["c_k"]), pipeline_mode=RESIDENT),
                  pl.BlockSpec((1, nt, GROUP, t), lambda bi, ni: (bi, 0, V_ROW_BLK["c_v"], 0), pipeline_mode=RESIDENT),
                  pl.BlockSpec((LANES, t), lambda bi, ni: (0, 0))],
        out_specs=pl.BlockSpec((1, t, GROUP), lambda bi, ni: (bi, ni, 0)),
        out_shape=jax.ShapeDtypeStruct((b, s, GROUP), BF16),
        scratch_shapes=[pltpu.VMEM((8, 1, t), F32), pltpu.VMEM((8, 1, t), F32), pltpu.VMEM((4, LANES, t), F32),
                        pltpu.VMEM((8, LANES, t), BF16), pltpu.VMEM((2, 8, CHUNK_TILES_DIFF * t, t), F32)],
        compiler_params=_cparams(("arbitrary", "arbitrary")),
        name="diff",
    )(lam2, qk, qk, vt, gs_full)


def _swa_kernel(sink_ref, q_ref, kp_ref, kc_ref, vtp_ref, vtc_ref, o_ref, *, t):
    n = pl.program_id(1)
    prow = lax.broadcasted_iota(I32, (WINDOW, t), 0)
    pcol = lax.broadcasted_iota(I32, (WINDOW, t), 1)
    mask_prev = (prow > pcol) & (pcol + jnp.where(n > 0, 0, t) < WINDOW)
    crow = lax.broadcasted_iota(I32, (t, t), 0)
    ccol = lax.broadcasted_iota(I32, (t, t), 1)
    mask_cur = (crow <= ccol) & (crow > ccol - WINDOW)
    vtp = vtp_ref[0, 0][:, t - WINDOW:t]
    vtc = vtc_ref[0, 0]
    for g in range(2):
        outs = []
        for r in range(2):
            h = 2 * g + r
            qzt = _transpose_bf16(jnp.where(_head_mask(r), q_ref[0, :, g * LANES:(g + 1) * LANES],
                                            jnp.zeros((), BF16)))
            sp = jnp.where(mask_prev, _dot(kp_ref[0, :, g * LANES:(g + 1) * LANES], qzt), -jnp.inf)
            sc = jnp.where(mask_cur, _dot(kc_ref[0, :, g * LANES:(g + 1) * LANES], qzt), -jnp.inf)
            sink = sink_ref[h] * LOG2E
            m = jnp.maximum(jnp.maximum(jnp.max(sp, axis=0, keepdims=True),
                                        jnp.max(sc, axis=0, keepdims=True)), sink)
            pp = jnp.exp2(sp - m)
            pc = jnp.exp2(sc - m)
            den = jnp.sum(pp, axis=0, keepdims=True) + jnp.sum(pc, axis=0, keepdims=True) + jnp.exp2(sink - m)
            rows = slice(h * HEAD_DIM, (h + 1) * HEAD_DIM)
            o = (jnp.dot(vtp[rows, :], pp.astype(BF16), preferred_element_type=F32)
                 + jnp.dot(vtc[rows, :], pc.astype(BF16), preferred_element_type=F32))
            outs.append(o / den)
        o_ref[0, :, g * LANES:(g + 1) * LANES] = jnp.concatenate(outs, axis=0).T.astype(BF16)


def _swa_call(sinks, qk, vt, *, t):
    b, s, _ = qk.shape
    nt = s // t
    per = t // WINDOW
    return pl.pallas_call(
        functools.partial(_swa_kernel, t=t),
        grid=(b, nt),
        in_specs=[pl.BlockSpec(memory_space=pltpu.SMEM),
                  pl.BlockSpec((1, t, GROUP), lambda bi, ni: (bi, ni, QK_BLK["d_q"])),
                  pl.BlockSpec((1, WINDOW, GROUP), lambda bi, ni: (bi, jnp.maximum(ni * per - 1, 0), QK_BLK["d_k"])),
                  pl.BlockSpec((1, t, GROUP), lambda bi, ni: (bi, ni, QK_BLK["d_k"])),
                  pl.BlockSpec((1, 1, GROUP, t), lambda bi, ni: (bi, jnp.maximum(ni - 1, 0), V_ROW_BLK["d_v"], 0)),
                  pl.BlockSpec((1, 1, GROUP, t), lambda bi, ni: (bi, ni, V_ROW_BLK["d_v"], 0))],
        out_specs=pl.BlockSpec((1, t, GROUP), lambda bi, ni: (bi, ni, 0)),
        out_shape=jax.ShapeDtypeStruct((b, s, GROUP), BF16),
        compiler_params=_cparams(("arbitrary", "arbitrary")),
        name="swa",
    )(sinks, qk, qk, qk, vt, vt)


def _post_kernel(oa_ref, ob_ref, oc_ref, od_ref, x_ref, p_ref, wo_ref, gmix_ref, gpre_ref, wu_ref, wd_ref, gpost_ref,
                 wg_ref, wp_ref, y_ref, *, chunk):
    acc = jnp.zeros(x_ref.shape[1:], F32)
    for i, o_ref in enumerate((oa_ref, ob_ref, oc_ref, od_ref)):
        acc = acc + jnp.dot(o_ref[0], wo_ref[i * GROUP:(i + 1) * GROUP, :], preferred_element_type=F32)
    x = x_ref[0] + _rms(acc, gmix_ref[...])
    h = _rms(x, gpre_ref[...]).astype(BF16)
    acc = jnp.zeros(x.shape, F32)
    for c in range(D_FF // chunk):
        u = jnp.dot(h, wu_ref[:, c * chunk:(c + 1) * chunk], preferred_element_type=F32)
        u = jnp.square(jnp.maximum(u, 0.0)).astype(BF16)
        acc = acc + jnp.dot(u, wd_ref[c * chunk:(c + 1) * chunk, :], preferred_element_type=F32)
    x = x + _rms(acc, gpost_ref[...])
    gate = jax.nn.sigmoid(jnp.dot(x.astype(BF16), wg_ref[...], preferred_element_type=F32))
    emb = jnp.dot(p_ref[0].astype(BF16), wp_ref[...], preferred_element_type=F32)
    y_ref[0] = x + gate * emb


def _post_call(tok_inputs, full_inputs, *, tm):
    b, s, d = tok_inputs[4].shape
    tok = lambda a: pl.BlockSpec((1, tm, a.shape[2]), lambda bi, ti: (bi, ti, 0))
    full = lambda a: pl.BlockSpec(a.shape, lambda bi, ti: (0,) * a.ndim, pipeline_mode=pl.Buffered(1))
    return pl.pallas_call(
        functools.partial(_post_kernel, chunk=1024),
        grid=(b, s // tm),
        in_specs=[tok(a) for a in tok_inputs] + [full(a) for a in full_inputs],
        out_specs=pl.BlockSpec((1, tm, d), lambda bi, ti: (bi, ti, 0)),
        out_shape=jax.ShapeDtypeStruct((b, s, d), F32),
        compiler_params=_cparams(("arbitrary", "arbitrary")),
        name="post",
    )(*tok_inputs, *full_inputs)


def _dup_kv(w):
    return jnp.concatenate([w[:, :HEAD_DIM], w[:, :HEAD_DIM], w[:, HEAD_DIM:], w[:, HEAD_DIM:]], axis=1)


def _prep_w_in(w):
    sec = lambda n: w[:, _SEC[n][0]:_SEC[n][0] + _SEC[n][1]]
    cols = [_dup_kv(sec(n)) if n in ("d_k", "d_v") else sec(n) for n in QK_GROUPS + V_GROUPS]
    w_main = jnp.concatenate(cols, axis=1).astype(BF16)
    pad = jnp.zeros((w.shape[0], LANES - IDX_DIM - IDX_HEADS - 4), w.dtype)
    w_idx = jnp.concatenate([sec("iq"), sec("ik"), sec("iw"), sec("b_f"), pad], axis=1)
    w_hi = w_idx.astype(BF16)
    w_lo = (w_idx - w_hi.astype(F32)).astype(BF16)
    return w_main, jnp.concatenate([w_hi, w_hi, w_lo], axis=0)


def _rope_tables(positions):
    pos = positions.astype(F32)[..., None]
    lane = jnp.arange(LANES)

    def tabs(dim):
        half = dim // 2
        inv_freq = ROPE_THETA ** (-jnp.arange(half, dtype=F32) / half)
        ang = pos * inv_freq
        sign = jnp.where((lane % dim) < half, -1.0, 1.0).astype(F32)
        reps = (1, 1, LANES // half)
        return jnp.tile(jnp.cos(ang), reps), jnp.tile(jnp.sin(ang), reps) * sign

    c64, s64 = tabs(HEAD_DIM)
    c32, s32 = tabs(DIFF_DIM)
    return c64, s64, c32, s32


def kernel(x, p, positions, w_in, b_forget, lambda_q1, lambda_k1, lambda_q2, lambda_k2, diff_subln, sinks,
           w_out, norm_pre_mix, norm_post_mix, norm_pre_mlp, norm_post_mlp, w_mlp_up, w_mlp_down,
           w_ple_proj, w_ple_gate):
    b, s, d = x.shape
    depth = w_in.shape[0]
    t = min(256, s)
    assert s % (CHUNK_TILES * t) == 0, "sequence length must be a multiple of the attention key chunk"
    topk = min(TOPK_MAX, s // 4)
    tabs = _rope_tables(positions)
    row = lambda v: v.reshape(1, -1).astype(F32)

    for i in range(depth):
        lam_init = 0.8 - 0.6 * math.exp(-0.3 * i)
        w_main, w_idx = _prep_w_in(w_in[i])
        bf_row = jnp.zeros((1, LANES), F32).at[0, MISC_F:MISC_F + 4].set(b_forget[i])
        qk, vt, iq, ik, iw = _proj_call(x, row(norm_pre_mix[i]), w_main, w_idx, tabs, bf_row, tm=t)

        o_a = _dsa_call(qk, vt, iq, ik, iw, t=t, topk=topk)
        o_b = _fox_call(qk, vt, t=t)
        lam = (jnp.exp(jnp.sum(lambda_q1[i] * lambda_k1[i])) - jnp.exp(jnp.sum(lambda_q2[i] * lambda_k2[i]))
               + lam_init)
        lam2 = jnp.stack([lam, jnp.asarray(1.0 - lam_init, F32)]).astype(F32)
        gs_full = jnp.broadcast_to(jnp.concatenate([diff_subln[i], diff_subln[i]]).astype(F32)[:, None], (LANES, t))
        o_c = _diff_call(lam2, qk, vt, gs_full, t=t)
        o_d = _swa_call(sinks[i].astype(F32), qk, vt, t=t)

        x = _post_call([o_a, o_b, o_c, o_d, x, p[i]],
                       [w_out[i].astype(BF16), row(norm_post_mix[i]), row(norm_pre_mlp[i]),
                        w_mlp_up[i].astype(BF16), w_mlp_down[i].astype(BF16), row(norm_post_mlp[i]),
                        w_ple_gate[i].astype(BF16), w_ple_proj[i].astype(BF16)], tm=t)
    return x
```

```python
import functools
import math

import jax
import jax.numpy as jnp
from jax import lax
from jax.experimental import pallas as pl
from jax.experimental.pallas import tpu as pltpu

F32, BF16, I32 = jnp.float32, jnp.bfloat16, jnp.int32

D_MODEL = 1024
HEAD_DIM = 64
DIFF_DIM = 32
IDX_HEADS = 8
IDX_DIM = 64
TOPK_MAX = 256
WINDOW = 128
D_FF = 4 * D_MODEL
ROPE_THETA = 10000.0
EPS = 1e-6
NEG_INF = -1e30
LANES = 128
SUBLANES = 8
GROUP = 256
INT_MIN = -2147483648
FIELD_BITS = 15
FIELD_GUARDS = -2147450880
LOW_BITS = 10
BYTE_BITS = 7
BYTE_GUARDS = -2139062144

_SEC = {}
_o = 0
for _name, _w in (("a_q", 256), ("a_k", 256), ("a_v", 256), ("iq", 512), ("ik", 64), ("iw", 8),
                  ("b_q", 256), ("b_k", 256), ("b_v", 256), ("b_f", 4),
                  ("c_q", 256), ("c_k", 256), ("c_v", 256), ("d_q", 256), ("d_k", 128), ("d_v", 128)):
    _SEC[_name] = (_o, _w)
    _o += _w

QK_GROUPS = ("a_q", "a_k", "b_q", "b_k", "c_q", "c_k", "d_q", "d_k")
V_GROUPS = ("a_v", "b_v", "c_v", "d_v")
MAIN_W = GROUP * (len(QK_GROUPS) + len(V_GROUPS))
V_BASE = GROUP * len(QK_GROUPS)
QK_BLK = {"a_q": 0, "a_k": 1, "b_q": 2, "b_k": 3, "c_q": 5, "c_k": 6, "d_q": 7, "d_k": 8}
QK_W = GROUP * 9
V_ROW_BLK = {n: i for i, n in enumerate(V_GROUPS)}
ROPE64_GROUPS = ("a_q", "a_k", "d_q", "d_k")
ROPE32_GROUPS = ("c_q", "c_k")
LOG2E = math.log2(math.e)
Q_SCALE = {"a_q": HEAD_DIM ** -0.5 * LOG2E, "b_q": HEAD_DIM ** -0.5 * LOG2E, "c_q": DIFF_DIM ** -0.5 * LOG2E,
           "d_q": HEAD_DIM ** -0.5 * LOG2E}
IDX_W = 512 + LANES
MISC_IW = 64
MISC_F = 72
IDX_CAT = 256

CHUNK_TILES = 4
CHUNK_TILES_DIFF = 2
POST_TILE = 512
VMEM_LIMIT = 56 * 1024 * 1024
RESIDENT = pl.Buffered(1)


def _cparams(sem):
    return pltpu.CompilerParams(dimension_semantics=sem, vmem_limit_bytes=VMEM_LIMIT)


def _rms(x, g):
    return x * lax.rsqrt(jnp.mean(x * x, axis=-1, keepdims=True) + EPS) * g


def _dot(a, b):
    return jnp.dot(a, b, preferred_element_type=F32)


def _transpose_bf16(a):
    return a.astype(F32).T.astype(BF16)


def _bf16_part(x):
    return x.astype(BF16).astype(F32)


def _split2(x):
    hi = _bf16_part(x)
    return hi, _bf16_part(x - hi)


def _split3(x):
    hi = _bf16_part(x)
    mid = _bf16_part(x - hi)
    return hi, mid, _bf16_part(x - hi - mid)


def _rope_chunk(xc, cos, sin_signed, half, lane):
    fwd = pltpu.roll(xc, LANES - half, axis=1)
    bwd = pltpu.roll(xc, half, axis=1)
    partner = jnp.where((lane % (2 * half)) < half, fwd, bwd)
    return xc * cos + partner * sin_signed


def _proj_kernel(x_ref, g_ref, wm_ref, wi_ref, c64_ref, s64_ref, c32_ref, s32_ref, bf_ref,
                 qk_ref, vt_ref, iq_ref, ik_ref, iw_ref, carry_ref, *, tm):
    t = pl.program_id(1)
    x = x_ref[0]
    h = _rms(x, g_ref[...])
    pm = jnp.dot(h.astype(BF16), wm_ref[...], preferred_element_type=F32)
    h_hi, h_lo = _split2(h)
    pi = jnp.dot(jnp.concatenate([h_hi, h_lo, h_hi], axis=1).astype(BF16), wi_ref[...],
                 preferred_element_type=F32)
    lane = lax.broadcasted_iota(I32, (1, LANES), 1)
    lo_half = lane < HEAD_DIM
    c64, s64, c32, s32 = c64_ref[0], s64_ref[0], c32_ref[0], s32_ref[0]

    misc = pi[:, 512:512 + LANES]
    z = misc + bf_ref[...]
    logf = jnp.minimum(z, 0.0) - jnp.log1p(jnp.exp(-jnp.abs(z)))
    logf = jnp.where((lane >= MISC_F) & (lane < MISC_F + 4), logf, 0.0)
    tri = (lax.broadcasted_iota(I32, (tm, tm), 1) <= lax.broadcasted_iota(I32, (tm, tm), 0)).astype(BF16)

    @pl.when(t == 0)
    def _():
        carry_ref[...] = jnp.zeros_like(carry_ref)

    pieces = jnp.dot(tri, jnp.concatenate(_split3(logf), axis=1).astype(BF16), preferred_element_type=F32)
    cum = pieces[:, 0:LANES] + pieces[:, LANES:2 * LANES] + pieces[:, 2 * LANES:3 * LANES] + carry_ref[...]
    carry_ref[...] = cum[tm - 1:tm, :]
    nhi, nmid, nlo = _split3(-LOG2E * cum)
    gate_bias = jnp.where(lane < 4, pltpu.roll(nhi, LANES - MISC_F, axis=1),
                          jnp.where(lane < 8, pltpu.roll(nmid, LANES - MISC_F + 4, axis=1),
                                    jnp.where(lane < 12, pltpu.roll(nlo, LANES - MISC_F + 8, axis=1),
                                              jnp.zeros_like(nlo))))

    for gi, name in enumerate(QK_GROUPS):
        for c in range(GROUP // LANES):
            lo = gi * GROUP + c * LANES
            v = pm[:, lo:lo + LANES]
            if name in ROPE64_GROUPS:
                v = _rope_chunk(v, c64, s64, HEAD_DIM // 2, lane)
            elif name in ROPE32_GROUPS:
                v = _rope_chunk(v, c32, s32, DIFF_DIM // 2, lane)
            if name in Q_SCALE:
                v = v * Q_SCALE[name]
            if name == "b_k":
                out = (QK_BLK[name] + c) * GROUP
                qk_ref[0, :, out:out + LANES] = v.astype(BF16)
                qk_ref[0, :, out + LANES:out + GROUP] = gate_bias.astype(BF16)
            else:
                out = QK_BLK[name] * GROUP + c * LANES
                qk_ref[0, :, out:out + LANES] = v.astype(BF16)

    vt_ref[0, 0] = pm[:, V_BASE:V_BASE + len(V_GROUPS) * GROUP].T.astype(BF16)

    def cat_q(q):
        hi, lo = _split2(q)
        return jnp.where(lo_half, hi, pltpu.roll(lo, HEAD_DIM, axis=1)), hi

    for c in range(512 // LANES):
        v = _rope_chunk(pi[:, c * LANES:(c + 1) * LANES], c64, s64, IDX_DIM // 2, lane)
        for r, q in enumerate((jnp.where(lo_half, v, 0.0), jnp.where(lo_half, pltpu.roll(v, HEAD_DIM, axis=1), 0.0))):
            a, b2 = cat_q(q)
            iq_ref[0, 2 * c + r, :, 0:LANES] = a.astype(BF16)
            iq_ref[0, 2 * c + r, :, LANES:IDX_CAT] = b2.astype(BF16)
    ik = jnp.where(lo_half, _rope_chunk(misc, c64, s64, IDX_DIM // 2, lane), 0.0)
    khi, klo = _split2(ik)
    ik_ref[0, :, 0:LANES] = jnp.where(lo_half, khi, pltpu.roll(khi, HEAD_DIM, axis=1)).astype(BF16)
    ik_ref[0, :, LANES:IDX_CAT] = klo.astype(BF16)
    iw_t = (misc * (IDX_HEADS ** -0.5 * IDX_DIM ** -0.5)).T
    iw_ref[0, 0] = iw_t[MISC_IW:MISC_IW + IDX_HEADS, :]


def _proj_call(x, g, w_main, w_idx, tabs, bf_row, *, tm):
    b, s, d = x.shape
    tok = lambda w: pl.BlockSpec((1, tm, w), lambda bi, ti: (bi, ti, 0))
    full = lambda a: pl.BlockSpec(a.shape, lambda bi, ti: (0,) * a.ndim)
    return pl.pallas_call(
        functools.partial(_proj_kernel, tm=tm),
        grid=(b, s // tm),
        in_specs=[tok(d), full(g), full(w_main), full(w_idx), tok(LANES), tok(LANES), tok(LANES), tok(LANES),
                  full(bf_row)],
        out_specs=[tok(QK_W),
                   pl.BlockSpec((1, 1, len(V_GROUPS) * GROUP, tm), lambda bi, ti: (bi, ti, 0, 0)),
                   pl.BlockSpec((1, IDX_HEADS, tm, IDX_CAT), lambda bi, ti: (bi, 0, ti, 0)),
                   tok(IDX_CAT),
                   pl.BlockSpec((1, 1, IDX_HEADS, tm), lambda bi, ti: (bi, ti, 0, 0))],
        out_shape=[jax.ShapeDtypeStruct((b, s, QK_W), BF16),
                   jax.ShapeDtypeStruct((b, s // tm, len(V_GROUPS) * GROUP, tm), BF16),
                   jax.ShapeDtypeStruct((b, IDX_HEADS, s, IDX_CAT), BF16),
                   jax.ShapeDtypeStruct((b, s, IDX_CAT), BF16),
                   jax.ShapeDtypeStruct((b, s // tm, IDX_HEADS, tm), F32)],
        scratch_shapes=[pltpu.VMEM((1, LANES), F32)],
        compiler_params=_cparams(("arbitrary", "arbitrary")),
        name="proj",
    )(x, g, w_main, w_idx, *tabs, bf_row)


def _softmax_step(s, m_ref, l_ref, idx):
    m_old = m_ref[idx]
    m_new = jnp.maximum(m_old, jnp.max(s, axis=0, keepdims=True))
    alpha = jnp.exp2(m_old - m_new)
    p = jnp.exp2(s - m_new)
    l_ref[idx] = alpha * l_ref[idx] + jnp.sum(p, axis=0, keepdims=True)
    m_ref[idx] = m_new
    return p, alpha


def _init_state(m_ref, l_ref, acc_ref):
    m_ref[...] = jnp.full(m_ref.shape, NEG_INF, F32)
    l_ref[...] = jnp.zeros(l_ref.shape, F32)
    acc_ref[...] = jnp.zeros(acc_ref.shape, F32)


def _causal_chunk(w, t):
    return lax.broadcasted_iota(I32, (w * t, t), 0) <= lax.broadcasted_iota(I32, (w * t, t), 1) + (w - 1) * t


def _head_mask(r):
    lane = lax.broadcasted_iota(I32, (1, LANES), 1)
    return (lane >= HEAD_DIM) if r else (lane < HEAD_DIM)


def _key_chunks(n, chunk_fn, width):
    def _body(c, carry):
        chunk_fn(width * c, width, False)
        return carry

    n_full = n // width
    lax.fori_loop(0, n_full, _body, 0)
    rest = n - n_full * width
    for k in range(width):
        @pl.when(rest == k)
        def _(k=k):
            chunk_fn(n - k, k + 1, True)


def _pipelined_key_chunks(n, scores_fn, rest_fn, width):
    n_full = n // width
    rest = n - n_full * width
    odd = n_full & 1

    @pl.when(odd == 1)
    def _():
        scores_fn(0, 1)
        scores_fn(width, 0)
        rest_fn(0, width, False, 1)

    @pl.when(odd == 0)
    def _():
        scores_fn(0, 0)

    def _body(pair, carry):
        j0 = (odd + 2 * pair) * width
        scores_fn(j0 + width, 1)
        rest_fn(j0, width, False, 0)
        scores_fn(j0 + 2 * width, 0)
        rest_fn(j0 + width, width, False, 1)
        return carry

    lax.fori_loop(0, (n_full - odd) >> 1, _body, 0)
    for k in range(width):
        @pl.when(rest == k)
        def _(k=k):
            rest_fn(n - k, k + 1, True, 0)


def _accumulate(acc_ref, a, r, alpha, vt_tiles, p, t):
    rows = slice(r * HEAD_DIM, (r + 1) * HEAD_DIM)
    pb = p.astype(BF16)
    pv = jnp.dot(vt_tiles[0], pb[0:t], preferred_element_type=F32)
    for i in range(1, len(vt_tiles)):
        pv = pv + jnp.dot(vt_tiles[i], pb[i * t:(i + 1) * t], preferred_element_type=F32)
    acc_ref[a, rows, :] = acc_ref[a, rows, :] * alpha + pv


def _normalised_t(acc_ref, a, l_ref, idx0, idx1):
    return jnp.concatenate([acc_ref[a, 0:HEAD_DIM, :] / l_ref[idx0],
                            acc_ref[a, HEAD_DIM:LANES, :] / l_ref[idx1]], axis=0)


def _dsa_kernel(iq_ref, iw_ref, ik_ref, q_ref, k_ref, vt_ref, o_ref,
                key_ref, pack_ref, pack4_ref, m_ref, l_ref, acc_ref, carry_ref, iqt_ref, qzt_ref, s_ref, thr_ref, nge_ref,
                *, t, topk):
    n = pl.program_id(1)
    for h in range(IDX_HEADS):
        iqt_ref[h] = _transpose_bf16(iq_ref[0, h])
    for h in range(4):
        qzt_ref[h] = _transpose_bf16(jnp.where(_head_mask(h % 2), q_ref[0, :, (h // 2) * LANES:(h // 2 + 1) * LANES],
                                               jnp.zeros((), BF16)))

    def score_chunk(j0, w, diag):
        start = pl.multiple_of(j0 * t, t)
        ik = ik_ref[0, pl.ds(start, w * t), :]
        sc = jnp.zeros((w * t, t), F32)
        for h in range(IDX_HEADS):
            d = _dot(ik, iqt_ref[h])
            sc = sc + iw_ref[0, 0, h:h + 1, :] * jnp.maximum(d, 0.0)
        if diag:
            sc = jnp.where(_causal_chunk(w, t), sc, NEG_INF)
        sc = jnp.where(sc == 0.0, 0.0, sc)
        bits = lax.bitcast_convert_type(sc, I32)
        key = bits ^ ((bits >> 31) & jnp.int32(0x7FFFFFFF))
        top = lax.shift_right_logical(key ^ jnp.int32(INT_MIN), 32 - FIELD_BITS)
        for i in range(w):
            key_ref[j0 + i] = key[i * t:(i + 1) * t]
            pack_ref[j0 + i] = (jnp.left_shift(top[i * t:i * t + t // 2], 16) | top[i * t + t // 2:(i + 1) * t]
                                | jnp.int32(FIELD_GUARDS))
            lead = [lax.shift_right_logical(top[i * t + q * (t // 4):i * t + (q + 1) * (t // 4)], FIELD_BITS - BYTE_BITS)
                    for q in range(4)]
            pack4_ref[j0 + i] = (jnp.left_shift(lead[0], 24) | jnp.left_shift(lead[1], 16) | jnp.left_shift(lead[2], 8)
                                 | lead[3] | jnp.int32(BYTE_GUARDS))

    _key_chunks(n, score_chunk, CHUNK_TILES)

    def tree_sum(parts):
        while len(parts) > 1:
            parts = [parts[i] + parts[i + 1] for i in range(0, len(parts), 2)]
        return parts[0]

    def count(pred):
        def body(j, acc):
            c = jnp.where(pred(key_ref[j]), 1.0, 0.0)
            return acc + tree_sum([c[r * SUBLANES:(r + 1) * SUBLANES, :] for r in range(t // SUBLANES)])
        acc = lax.fori_loop(0, n + 1, body, jnp.zeros((SUBLANES, t), F32))
        return jnp.sum(acc, axis=0, keepdims=True)

    def count_top(cand):
        both = jnp.left_shift(cand, 16) | cand

        def body(j, acc):
            z = pack_ref[j] - both
            c = lax.shift_right_logical(z, FIELD_BITS) & jnp.int32(0x00010001)
            return acc + tree_sum([c[r * SUBLANES:(r + 1) * SUBLANES, :] for r in range(t // 2 // SUBLANES)])
        acc = lax.fori_loop(0, n + 1, body, jnp.zeros((SUBLANES, t), I32))
        per_lane = lax.shift_right_logical(acc, 16) + (acc & jnp.int32(0xFFFF))
        return jnp.sum(per_lane.astype(F32), axis=0, keepdims=True)

    def count_lead(cand):
        four = cand * jnp.int32(0x01010101)
        pairs = jnp.int32(0x00FF00FF)

        def body(j, acc):
            even, odd = acc
            z = pack4_ref[j] - four
            c = lax.shift_right_logical(z, BYTE_BITS) & jnp.int32(0x01010101)
            c = tree_sum([c[r * SUBLANES:(r + 1) * SUBLANES, :] for r in range(t // 4 // SUBLANES)])
            return even + (c & pairs), odd + (lax.shift_right_logical(c, 8) & pairs)
        zeros = jnp.zeros((SUBLANES, t), I32)
        even, odd = lax.fori_loop(0, n + 1, body, (zeros, zeros))
        both = even + odd
        per_lane = lax.shift_right_logical(both, 16) + (both & jnp.int32(0xFFFF))
        return jnp.sum(per_lane.astype(F32), axis=0, keepdims=True)

    def bisect(n_bits, low_bit, count_fn, carry):
        def body(i, carry):
            u, n_ge, n_rej = carry
            cand = u | jnp.left_shift(jnp.int32(1), low_bit + n_bits - 1 - i)
            cnt = count_fn(cand)
            keep = cnt >= float(topk)
            return jnp.where(keep, cand, u), jnp.where(keep, cnt, n_ge), jnp.where(keep, n_rej, cnt)
        return lax.fori_loop(0, n_bits, body, carry)

    def count_full(cand):
        return count(lambda kt: kt >= (cand ^ jnp.int32(INT_MIN)))

    in_scope = ((n + 1) * t).astype(F32)
    zero_row = jnp.zeros((1, t), F32)
    lead, n_ge, n_rej = bisect(BYTE_BITS, 0, count_lead, (jnp.zeros((1, t), I32), zero_row + in_scope, zero_row))
    top, n_ge, n_rej = bisect(FIELD_BITS - BYTE_BITS, 0, count_top,
                              (jnp.left_shift(lead, FIELD_BITS - BYTE_BITS), n_ge, n_rej))
    u, n_ge, n_rej = bisect(32 - FIELD_BITS - LOW_BITS, LOW_BITS, count_full,
                            (jnp.left_shift(top, 32 - FIELD_BITS), n_ge, n_rej))

    prefix = u ^ jnp.int32(INT_MIN)
    low_mask = jnp.int32((1 << LOW_BITS) - 1)
    big = jnp.int32(1 << (LOW_BITS + 1))

    def tree(parts, op):
        while len(parts) > 1:
            parts = [op(parts[i], parts[i + 1]) for i in range(0, len(parts), 2)]
        return parts[0]

    def band_body(j, carry):
        mx, mn, sm = carry
        kt = key_ref[j]
        inside = lax.shift_right_logical(kt ^ prefix, LOW_BITS) == 0
        low = kt & low_mask
        rows = lambda a: [a[r * SUBLANES:(r + 1) * SUBLANES, :] for r in range(t // SUBLANES)]
        low_or_zero = rows(jnp.where(inside, low, 0))
        return (jnp.maximum(mx, tree(low_or_zero, jnp.maximum)),
                jnp.minimum(mn, tree(rows(jnp.where(inside, low, big)), jnp.minimum)),
                sm + tree(low_or_zero, jnp.add))

    mx, mn, sm = lax.fori_loop(0, n + 1, band_body, (jnp.zeros((SUBLANES, t), I32),
                                                       jnp.full((SUBLANES, t), 1 << (LOW_BITS + 1), I32),
                                                       jnp.zeros((SUBLANES, t), I32)))
    mx = jnp.max(mx.astype(F32), axis=0, keepdims=True)
    mn = jnp.min(mn.astype(F32), axis=0, keepdims=True)
    sm = jnp.sum(sm.astype(F32), axis=0, keepdims=True)
    short = n_ge < float(topk)
    in_band = n_ge - n_rej
    need = float(topk) - n_rej
    second = jnp.where(in_band > 2.5, sm - mx - mn, mn)
    thr_low = jnp.where(need < 1.5, mx, jnp.where(need < 2.5, second, mn))
    kept = (1.0 + jnp.where((in_band > 1.5) & (second >= thr_low), 1.0, 0.0)
            + jnp.where((in_band > 2.5) & (mn >= thr_low), 1.0, 0.0))
    resolved = jnp.min(jnp.where(short | (in_band < 3.5), 1.0, 0.0)) > 0.5

    @pl.when(resolved)
    def _():
        thr_ref[...] = jnp.where(short, prefix, prefix | thr_low.astype(I32))
        nge_ref[...] = jnp.where(short, n_ge, n_rej + kept)

    @pl.when(jnp.logical_not(resolved))
    def _():
        u_all, n_ge_all, _ = bisect(LOW_BITS, 0, count_full, (u, n_ge, n_rej))
        thr_ref[...] = u_all ^ jnp.int32(INT_MIN)
        nge_ref[...] = n_ge_all

    thr = thr_ref[...]
    has_ties = jnp.max(nge_ref[...]) > float(topk)

    _init_state(m_ref, l_ref, acc_ref)

    def attend(ties):
        if ties:
            carry_ref[...] = jnp.zeros(carry_ref.shape, F32)
            need = float(topk) - count(lambda kt: kt > thr)
            lower = (lax.broadcasted_iota(I32, (t, t), 1) <= lax.broadcasted_iota(I32, (t, t), 0)).astype(BF16)

        def scores(j0, buf):
            start = pl.multiple_of(j0 * t, t)
            ks = k_ref[0, pl.ds(start, CHUNK_TILES * t), :]
            for h in range(4):
                s_ref[buf, h] = _dot(ks[:, (h // 2) * LANES:(h // 2 + 1) * LANES], qzt_ref[h])

        def rest(j0, w, diag, buf):
            if ties:
                parts = []
                for i in range(w):
                    kt = key_ref[j0 + i]
                    eq = kt == thr
                    incl = jnp.dot(lower, jnp.where(eq, 1.0, 0.0).astype(BF16),
                                   preferred_element_type=F32) + carry_ref[...]
                    carry_ref[...] = incl[t - 1:t, :]
                    parts.append(jnp.where((kt > thr) | (eq & (incl <= need)), 1.0, 0.0))
                sel = (parts[0] if w == 1 else jnp.concatenate(parts, axis=0)) > 0.5
            else:
                kt = key_ref[j0] if w == 1 else jnp.concatenate([key_ref[j0 + i] for i in range(w)], axis=0)
                sel = kt >= thr
            if diag:
                sel = sel & _causal_chunk(w, t)
            for h in range(4):
                g, r = divmod(h, 2)
                p, alpha = _softmax_step(jnp.where(sel, s_ref[buf, h, 0:w * t, :], -jnp.inf), m_ref, l_ref, h)
                _accumulate(acc_ref, g, r, alpha,
                            [vt_ref[0, j0 + i, h * HEAD_DIM:(h + 1) * HEAD_DIM, :] for i in range(w)], p, t)

        _pipelined_key_chunks(n, scores, rest, CHUNK_TILES)

    @pl.when(has_ties)
    def _():
        attend(True)

    @pl.when(jnp.logical_not(has_ties))
    def _():
        attend(False)

    for g in range(2):
        o_ref[0, :, g * LANES:(g + 1) * LANES] = _normalised_t(acc_ref, g, l_ref, 2 * g, 2 * g + 1).T.astype(BF16)


def _dsa_call(qk, vt, iq, ik, iw, *, t, topk):
    b, s, _ = qk.shape
    nt = s // t
    return pl.pallas_call(
        functools.partial(_dsa_kernel, t=t, topk=topk),
        grid=(b, nt),
        in_specs=[pl.BlockSpec((1, IDX_HEADS, t, IDX_CAT), lambda bi, ni: (bi, 0, ni, 0)),
                  pl.BlockSpec((1, 1, IDX_HEADS, t), lambda bi, ni: (bi, ni, 0, 0)),
                  pl.BlockSpec((1, s, IDX_CAT), lambda bi, ni: (bi, 0, 0), pipeline_mode=RESIDENT),
                  pl.BlockSpec((1, t, GROUP), lambda bi, ni: (bi, ni, QK_BLK["a_q"])),
                  pl.BlockSpec((1, s, GROUP), lambda bi, ni: (bi, 0, QK_BLK["a_k"]), pipeline_mode=RESIDENT),
                  pl.BlockSpec((1, nt, GROUP, t), lambda bi, ni: (bi, 0, V_ROW_BLK["a_v"], 0), pipeline_mode=RESIDENT)],
        out_specs=pl.BlockSpec((1, t, GROUP), lambda bi, ni: (bi, ni, 0)),
        out_shape=jax.ShapeDtypeStruct((b, s, GROUP), BF16),
        scratch_shapes=[pltpu.VMEM((nt, t, t), I32), pltpu.VMEM((nt, t // 2, t), I32), pltpu.VMEM((nt, t // 4, t), I32),
                        pltpu.VMEM((4, 1, t), F32), pltpu.VMEM((4, 1, t), F32),
                        pltpu.VMEM((2, LANES, t), F32), pltpu.VMEM((1, t), F32),
                        pltpu.VMEM((IDX_HEADS, IDX_CAT, t), BF16), pltpu.VMEM((4, LANES, t), BF16),
                        pltpu.VMEM((2, 4, CHUNK_TILES * t, t), F32),
                        pltpu.VMEM((1, t), I32), pltpu.VMEM((1, t), F32)],
        compiler_params=_cparams(("arbitrary", "arbitrary")),
        name="dsa",
    )(iq, iw, ik, qk, qk, vt)


def _fox_kernel(q_ref, k0_ref, k1_ref, vt_ref, o_ref, m_ref, l_ref, acc_ref, qxt_ref, s_ref, *, t):
    n = pl.program_id(1)
    _init_state(m_ref, l_ref, acc_ref)
    lane = lax.broadcasted_iota(I32, (1, LANES), 1)
    for h in range(4):
        qz = jnp.where(_head_mask(h % 2), q_ref[0, :, (h // 2) * LANES:(h // 2 + 1) * LANES], jnp.zeros((), BF16))
        ones = jnp.where((lane == h) | (lane == 4 + h) | (lane == 8 + h), 1.0, 0.0).astype(BF16)
        qxt_ref[h] = _transpose_bf16(jnp.concatenate([qz, jnp.broadcast_to(ones, (t, LANES))], axis=1))
    k_refs = (k0_ref, k1_ref)

    def scores(j0, buf):
        start = pl.multiple_of(j0 * t, t)
        for h in range(4):
            s_ref[buf, h] = _dot(k_refs[h // 2][0, pl.ds(start, CHUNK_TILES * t), :], qxt_ref[h])

    def rest(j0, w, diag, buf):
        for h in range(4):
            g, r = divmod(h, 2)
            s = s_ref[buf, h, 0:w * t, :]
            if diag:
                s = jnp.where(_causal_chunk(w, t), s, -jnp.inf)
            p, alpha = _softmax_step(s, m_ref, l_ref, h)
            _accumulate(acc_ref, g, r, alpha,
                        [vt_ref[0, j0 + i, h * HEAD_DIM:(h + 1) * HEAD_DIM, :] for i in range(w)], p, t)

    _pipelined_key_chunks(n, scores, rest, CHUNK_TILES)
    for g in range(2):
        o_ref[0, :, g * LANES:(g + 1) * LANES] = _normalised_t(acc_ref, g, l_ref, 2 * g, 2 * g + 1).T.astype(BF16)


def _fox_call(qk, vt, *, t):
    b, s, _ = qk.shape
    nt = s // t
    return pl.pallas_call(
        functools.partial(_fox_kernel, t=t),
        grid=(b, nt),
        in_specs=[pl.BlockSpec((1, t, GROUP), lambda bi, ni: (bi, ni, QK_BLK["b_q"])),
                  pl.BlockSpec((1, s, GROUP), lambda bi, ni: (bi, 0, QK_BLK["b_k"]), pipeline_mode=RESIDENT),
                  pl.BlockSpec((1, s, GROUP), lambda bi, ni: (bi, 0, QK_BLK["b_k"] + 1), pipeline_mode=RESIDENT),
                  pl.BlockSpec((1, nt, GROUP, t), lambda bi, ni: (bi, 0, V_ROW_BLK["b_v"], 0), pipeline_mode=RESIDENT)],
        out_specs=pl.BlockSpec((1, t, GROUP), lambda bi, ni: (bi, ni, 0)),
        out_shape=jax.ShapeDtypeStruct((b, s, GROUP), BF16),
        scratch_shapes=[pltpu.VMEM((4, 1, t), F32), pltpu.VMEM((4, 1, t), F32), pltpu.VMEM((2, LANES, t), F32),
                        pltpu.VMEM((4, GROUP, t), BF16), pltpu.VMEM((2, 4, CHUNK_TILES * t, t), F32)],
        compiler_params=_cparams(("arbitrary", "arbitrary")),
        name="fox",
    )(qk, qk, qk, vt)


def _diff_kernel(lam_ref, q_ref, k_ref, vt_ref, gs_ref, o_ref, m_ref, l_ref, acc_ref, qzt_ref, s_ref, *, t):
    n = pl.program_id(1)
    _init_state(m_ref, l_ref, acc_ref)
    lane = lax.broadcasted_iota(I32, (1, LANES), 1)
    for h in range(4):
        for mm in range(2):
            lo = (h % 2) * HEAD_DIM + mm * DIFF_DIM
            qzt_ref[2 * h + mm] = _transpose_bf16(
                jnp.where((lane >= lo) & (lane < lo + DIFF_DIM),
                          q_ref[0, :, (h // 2) * LANES:(h // 2 + 1) * LANES], jnp.zeros((), BF16)))

    def scores(j0, buf):
        start = pl.multiple_of(j0 * t, t)
        ks = k_ref[0, pl.ds(start, CHUNK_TILES_DIFF * t), :]
        for i in range(8):
            s_ref[buf, i] = _dot(ks[:, (i // 4) * LANES:(i // 4 + 1) * LANES], qzt_ref[i])

    def rest(j0, w, diag, buf):
        for h in range(4):
            g, r = divmod(h, 2)
            vts = [vt_ref[0, j0 + i, h * HEAD_DIM:(h + 1) * HEAD_DIM, :] for i in range(w)]
            for mm in range(2):
                s = s_ref[buf, 2 * h + mm, 0:w * t, :]
                if diag:
                    s = jnp.where(_causal_chunk(w, t), s, -jnp.inf)
                p, alpha = _softmax_step(s, m_ref, l_ref, 2 * h + mm)
                _accumulate(acc_ref, 2 * mm + g, r, alpha, vts, p, t)

    _pipelined_key_chunks(n, scores, rest, CHUNK_TILES_DIFF)
    lam = lam_ref[0]
    out_scale = lam_ref[1]
    for g in range(2):
        o1 = _normalised_t(acc_ref, g, l_ref, 4 * g, 4 * g + 2)
        o2 = _normalised_t(acc_ref, 2 + g, l_ref, 4 * g + 1, 4 * g + 3)
        o = o1 - lam * o2
        sq = o * o
        ms = jnp.concatenate(
            [jnp.broadcast_to(jnp.mean(sq[r * HEAD_DIM:(r + 1) * HEAD_DIM], axis=0, keepdims=True), (HEAD_DIM, t))
             for r in range(2)], axis=0)
        y = o * lax.rsqrt(ms + EPS) * gs_ref[...] * out_scale
        o_ref[0, :, g * LANES:(g + 1) * LANES] = y.T.astype(BF16)


def _diff_call(lam2, qk, vt, gs_full, *, t):
    b, s, _ = qk.shape
    nt = s // t
    return pl.pallas_call(
        functools.partial(_diff_kernel, t=t),
        grid=(b, nt),
        in_specs=[pl.BlockSpec(memory_space=pltpu.SMEM),
                  pl.BlockSpec((1, t, GROUP), lambda bi, ni: (bi, ni, QK_BLK["c_q"])),
                  pl.BlockSpec((1, s, GROUP), lambda bi, ni: (bi, 0, QK_BLK["c_k"]), pipeline_mode=RESIDENT),
                  pl.BlockSpec((1, nt, GROUP, t), lambda bi, ni: (bi, 0, V_ROW_BLK["c_v"], 0), pipeline_mode=RESIDENT),
                  pl.BlockSpec((LANES, t), lambda bi, ni: (0, 0))],
        out_specs=pl.BlockSpec((1, t, GROUP), lambda bi, ni: (bi, ni, 0)),
        out_shape=jax.ShapeDtypeStruct((b, s, GROUP), BF16),
        scratch_shapes=[pltpu.VMEM((8, 1, t), F32), pltpu.VMEM((8, 1, t), F32), pltpu.VMEM((4, LANES, t), F32),
                        pltpu.VMEM((8, LANES, t), BF16), pltpu.VMEM((2, 8, CHUNK_TILES_DIFF * t, t), F32)],
        compiler_params=_cparams(("arbitrary", "arbitrary")),
        name="diff",
    )(lam2, qk, qk, vt, gs_full)


def _swa_kernel(sink_ref, q_ref, kp_ref, kc_ref, vtp_ref, vtc_ref, o_ref, *, t):
    n = pl.program_id(1)
    prow = lax.broadcasted_iota(I32, (WINDOW, t), 0)
    pcol = lax.broadcasted_iota(I32, (WINDOW, t), 1)
    mask_prev = (prow > pcol) & (pcol + jnp.where(n > 0, 0, t) < WINDOW)
    crow = lax.broadcasted_iota(I32, (t, t), 0)
    ccol = lax.broadcasted_iota(I32, (t, t), 1)
    mask_cur = (crow <= ccol) & (crow > ccol - WINDOW)
    vtp = vtp_ref[0, 0][:, t - WINDOW:t]
    vtc = vtc_ref[0, 0]
    for g in range(2):
        outs = []
        for r in range(2):
            h = 2 * g + r
            qzt = _transpose_bf16(jnp.where(_head_mask(r), q_ref[0, :, g * LANES:(g + 1) * LANES],
                                            jnp.zeros((), BF16)))
            sp = jnp.where(mask_prev, _dot(kp_ref[0, :, g * LANES:(g + 1) * LANES], qzt), -jnp.inf)
            sc = jnp.where(mask_cur, _dot(kc_ref[0, :, g * LANES:(g + 1) * LANES], qzt), -jnp.inf)
            sink = sink_ref[h] * LOG2E
            m = jnp.maximum(jnp.maximum(jnp.max(sp, axis=0, keepdims=True),
                                        jnp.max(sc, axis=0, keepdims=True)), sink)
            pp = jnp.exp2(sp - m)
            pc = jnp.exp2(sc - m)
            den = jnp.sum(pp, axis=0, keepdims=True) + jnp.sum(pc, axis=0, keepdims=True) + jnp.exp2(sink - m)
            rows = slice(h * HEAD_DIM, (h + 1) * HEAD_DIM)
            o = (jnp.dot(vtp[rows, :], pp.astype(BF16), preferred_element_type=F32)
                 + jnp.dot(vtc[rows, :], pc.astype(BF16), preferred_element_type=F32))
            outs.append(o / den)
        o_ref[0, :, g * LANES:(g + 1) * LANES] = jnp.concatenate(outs, axis=0).T.astype(BF16)


def _swa_call(sinks, qk, vt, *, t):
    b, s, _ = qk.shape
    nt = s // t
    per = t // WINDOW
    return pl.pallas_call(
        functools.partial(_swa_kernel, t=t),
        grid=(b, nt),
        in_specs=[pl.BlockSpec(memory_space=pltpu.SMEM),
                  pl.BlockSpec((1, t, GROUP), lambda bi, ni: (bi, ni, QK_BLK["d_q"])),
                  pl.BlockSpec((1, WINDOW, GROUP), lambda bi, ni: (bi, jnp.maximum(ni * per - 1, 0), QK_BLK["d_k"])),
                  pl.BlockSpec((1, t, GROUP), lambda bi, ni: (bi, ni, QK_BLK["d_k"])),
                  pl.BlockSpec((1, 1, GROUP, t), lambda bi, ni: (bi, jnp.maximum(ni - 1, 0), V_ROW_BLK["d_v"], 0)),
                  pl.BlockSpec((1, 1, GROUP, t), lambda bi, ni: (bi, ni, V_ROW_BLK["d_v"], 0))],
        out_specs=pl.BlockSpec((1, t, GROUP), lambda bi, ni: (bi, ni, 0)),
        out_shape=jax.ShapeDtypeStruct((b, s, GROUP), BF16),
        compiler_params=_cparams(("arbitrary", "arbitrary")),
        name="swa",
    )(sinks, qk, qk, qk, vt, vt)


def _post_kernel(oa_ref, ob_ref, oc_ref, od_ref, x_ref, p_ref, wo_ref, gmix_ref, gpre_ref, wu_ref, wd_ref, gpost_ref,
                 wg_ref, wp_ref, y_ref, *, chunk):
    acc = jnp.zeros(x_ref.shape[1:], F32)
    for i, o_ref in enumerate((oa_ref, ob_ref, oc_ref, od_ref)):
        acc = acc + jnp.dot(o_ref[0], wo_ref[i * GROUP:(i + 1) * GROUP, :], preferred_element_type=F32)
    x = x_ref[0] + _rms(acc, gmix_ref[...])
    h = _rms(x, gpre_ref[...]).astype(BF16)
    acc = jnp.zeros(x.shape, F32)
    for c in range(D_FF // chunk):
        u = jnp.dot(h, wu_ref[:, c * chunk:(c + 1) * chunk], preferred_element_type=F32)
        u = jnp.square(jnp.maximum(u, 0.0)).astype(BF16)
        acc = acc + jnp.dot(u, wd_ref[c * chunk:(c + 1) * chunk, :], preferred_element_type=F32)
    x = x + _rms(acc, gpost_ref[...])
    gate = jax.nn.sigmoid(jnp.dot(x.astype(BF16), wg_ref[...], preferred_element_type=F32))
    emb = jnp.dot(p_ref[0].astype(BF16), wp_ref[...], preferred_element_type=F32)
    y_ref[0] = x + gate * emb


def _post_call(tok_inputs, full_inputs, *, tm):
    b, s, d = tok_inputs[4].shape
    tok = lambda a: pl.BlockSpec((1, tm, a.shape[2]), lambda bi, ti: (bi, ti, 0))
    full = lambda a: pl.BlockSpec(a.shape, lambda bi, ti: (0,) * a.ndim, pipeline_mode=pl.Buffered(1))
    return pl.pallas_call(
        functools.partial(_post_kernel, chunk=1024),
        grid=(b, s // tm),
        in_specs=[tok(a) for a in tok_inputs] + [full(a) for a in full_inputs],
        out_specs=pl.BlockSpec((1, tm, d), lambda bi, ti: (bi, ti, 0)),
        out_shape=jax.ShapeDtypeStruct((b, s, d), F32),
        compiler_params=_cparams(("arbitrary", "arbitrary")),
        name="post",
    )(*tok_inputs, *full_inputs)


def _dup_kv(w):
    return jnp.concatenate([w[:, :HEAD_DIM], w[:, :HEAD_DIM], w[:, HEAD_DIM:], w[:, HEAD_DIM:]], axis=1)


def _prep_w_in(w):
    sec = lambda n: w[:, _SEC[n][0]:_SEC[n][0] + _SEC[n][1]]
    cols = [_dup_kv(sec(n)) if n in ("d_k", "d_v") else sec(n) for n in QK_GROUPS + V_GROUPS]
    w_main = jnp.concatenate(cols, axis=1).astype(BF16)
    pad = jnp.zeros((w.shape[0], LANES - IDX_DIM - IDX_HEADS - 4), w.dtype)
    w_idx = jnp.concatenate([sec("iq"), sec("ik"), sec("iw"), sec("b_f"), pad], axis=1)
    w_hi = w_idx.astype(BF16)
    w_lo = (w_idx - w_hi.astype(F32)).astype(BF16)
    return w_main, jnp.concatenate([w_hi, w_hi, w_lo], axis=0)


def _rope_tables(positions):
    pos = positions.astype(F32)[..., None]
    lane = jnp.arange(LANES)

    def tabs(dim):
        half = dim // 2
        inv_freq = ROPE_THETA ** (-jnp.arange(half, dtype=F32) / half)
        ang = pos * inv_freq
        sign = jnp.where((lane % dim) < half, -1.0, 1.0).astype(F32)
        reps = (1, 1, LANES // half)
        return jnp.tile(jnp.cos(ang), reps), jnp.tile(jnp.sin(ang), reps) * sign

    c64, s64 = tabs(HEAD_DIM)
    c32, s32 = tabs(DIFF_DIM)
    return c64, s64, c32, s32


def kernel(x, p, positions, w_in, b_forget, lambda_q1, lambda_k1, lambda_q2, lambda_k2, diff_subln, sinks,
           w_out, norm_pre_mix, norm_post_mix, norm_pre_mlp, norm_post_mlp, w_mlp_up, w_mlp_down,
           w_ple_proj, w_ple_gate):
    b, s, d = x.shape
    depth = w_in.shape[0]
    t = min(256, s)
    assert s % (CHUNK_TILES * t) == 0, "sequence length must be a multiple of the attention key chunk"
    topk = min(TOPK_MAX, s // 4)
    tabs = _rope_tables(positions)
    row = lambda v: v.reshape(1, -1).astype(F32)

    for i in range(depth):
        lam_init = 0.8 - 0.6 * math.exp(-0.3 * i)
        w_main, w_idx = _prep_w_in(w_in[i])
        bf_row = jnp.zeros((1, LANES), F32).at[0, MISC_F:MISC_F + 4].set(b_forget[i])
        qk, vt, iq, ik, iw = _proj_call(x, row(norm_pre_mix[i]), w_main, w_idx, tabs, bf_row, tm=t)

        o_a = _dsa_call(qk, vt, iq, ik, iw, t=t, topk=topk)
        o_b = _fox_call(qk, vt, t=t)
        lam = (jnp.exp(jnp.sum(lambda_q1[i] * lambda_k1[i])) - jnp.exp(jnp.sum(lambda_q2[i] * lambda_k2[i]))
               + lam_init)
        lam2 = jnp.stack([lam, jnp.asarray(1.0 - lam_init, F32)]).astype(F32)
        gs_full = jnp.broadcast_to(jnp.concatenate([diff_subln[i], diff_subln[i]]).astype(F32)[:, None], (LANES, t))
        o_c = _diff_call(lam2, qk, vt, gs_full, t=t)
        o_d = _swa_call(sinks[i].astype(F32), qk, vt, t=t)

        x = _post_call([o_a, o_b, o_c, o_d, x, p[i]],
                       [w_out[i].astype(BF16), row(norm_post_mix[i]), row(norm_pre_mlp[i]),
                        w_mlp_up[i].astype(BF16), w_mlp_down[i].astype(BF16), row(norm_post_mlp[i]),
                        w_ple_gate[i].astype(BF16), w_ple_proj[i].astype(BF16)], tm=min(POST_TILE, s))
    return x
```

```python
import functools
import math

import jax
import jax.numpy as jnp
from jax import lax
from jax.experimental import pallas as pl
from jax.experimental.pallas import tpu as pltpu

F32, BF16, I32 = jnp.float32, jnp.bfloat16, jnp.int32

D_MODEL = 1024
HEAD_DIM = 64
DIFF_DIM = 32
IDX_HEADS = 8
IDX_DIM = 64
TOPK_MAX = 256
WINDOW = 128
D_FF = 4 * D_MODEL
ROPE_THETA = 10000.0
EPS = 1e-6
NEG_INF = -1e30
LANES = 128
SUBLANES = 8
GROUP = 256
INT_MIN = -2147483648
FIELD_BITS = 15
FIELD_GUARDS = -2147450880
LOW_BITS = 10
BYTE_BITS = 7
BYTE_GUARDS = -2139062144

_SEC = {}
_o = 0
for _name, _w in (("a_q", 256), ("a_k", 256), ("a_v", 256), ("iq", 512), ("ik", 64), ("iw", 8),
                  ("b_q", 256), ("b_k", 256), ("b_v", 256), ("b_f", 4),
                  ("c_q", 256), ("c_k", 256), ("c_v", 256), ("d_q", 256), ("d_k", 128), ("d_v", 128)):
    _SEC[_name] = (_o, _w)
    _o += _w

QK_GROUPS = ("a_q", "a_k", "b_q", "b_k", "c_q", "c_k", "d_q", "d_k")
V_GROUPS = ("a_v", "b_v", "c_v", "d_v")
MAIN_W = GROUP * (len(QK_GROUPS) + len(V_GROUPS))
V_BASE = GROUP * len(QK_GROUPS)
QK_BLK = {"a_q": 0, "a_k": 1, "b_q": 2, "b_k": 3, "c_q": 5, "c_k": 6, "d_q": 7, "d_k": 8}
QK_W = GROUP * 9
V_ROW_BLK = {n: i for i, n in enumerate(V_GROUPS)}
ROPE64_GROUPS = ("a_q", "a_k", "d_q", "d_k")
ROPE32_GROUPS = ("c_q", "c_k")
LOG2E = math.log2(math.e)
Q_SCALE = {"a_q": HEAD_DIM ** -0.5 * LOG2E, "b_q": HEAD_DIM ** -0.5 * LOG2E, "c_q": DIFF_DIM ** -0.5 * LOG2E,
           "d_q": HEAD_DIM ** -0.5 * LOG2E}
IDX_W = 512 + LANES
MISC_IW = 64
MISC_F = 72
IDX_CAT = 256

CHUNK_TILES = 4
CHUNK_TILES_DIFF = 2
POST_TILE = 512
VMEM_LIMIT = 56 * 1024 * 1024
RESIDENT = pl.Buffered(1)


def _cparams(sem):
    return pltpu.CompilerParams(dimension_semantics=sem, vmem_limit_bytes=VMEM_LIMIT)


def _rms(x, g):
    return x * lax.rsqrt(jnp.mean(x * x, axis=-1, keepdims=True) + EPS) * g


def _dot(a, b):
    return jnp.dot(a, b, preferred_element_type=F32)


def _transpose_bf16(a):
    return a.astype(F32).T.astype(BF16)


def _bf16_part(x):
    return x.astype(BF16).astype(F32)


def _split2(x):
    hi = _bf16_part(x)
    return hi, _bf16_part(x - hi)


def _split3(x):
    hi = _bf16_part(x)
    mid = _bf16_part(x - hi)
    return hi, mid, _bf16_part(x - hi - mid)


def _rope_chunk(xc, cos, sin_signed, half, lane):
    fwd = pltpu.roll(xc, LANES - half, axis=1)
    bwd = pltpu.roll(xc, half, axis=1)
    partner = jnp.where((lane % (2 * half)) < half, fwd, bwd)
    return xc * cos + partner * sin_signed


def _proj_kernel(x_ref, g_ref, wm_ref, wi_ref, c64_ref, s64_ref, c32_ref, s32_ref, bf_ref,
                 qk_ref, vt_ref, iq_ref, ik_ref, iw_ref, carry_ref, *, tm):
    t = pl.program_id(1)
    x = x_ref[0]
    h = _rms(x, g_ref[...])
    pm = jnp.dot(h.astype(BF16), wm_ref[...], preferred_element_type=F32)
    h_hi, h_lo = _split2(h)
    pi = jnp.dot(jnp.concatenate([h_hi, h_lo, h_hi], axis=1).astype(BF16), wi_ref[...],
                 preferred_element_type=F32)
    lane = lax.broadcasted_iota(I32, (1, LANES), 1)
    lo_half = lane < HEAD_DIM
    c64, s64, c32, s32 = c64_ref[0], s64_ref[0], c32_ref[0], s32_ref[0]

    misc = pi[:, 512:512 + LANES]
    z = misc + bf_ref[...]
    logf = jnp.minimum(z, 0.0) - jnp.log1p(jnp.exp(-jnp.abs(z)))
    logf = jnp.where((lane >= MISC_F) & (lane < MISC_F + 4), logf, 0.0)
    tri = (lax.broadcasted_iota(I32, (tm, tm), 1) <= lax.broadcasted_iota(I32, (tm, tm), 0)).astype(BF16)

    @pl.when(t == 0)
    def _():
        carry_ref[...] = jnp.zeros_like(carry_ref)

    pieces = jnp.dot(tri, jnp.concatenate(_split3(logf), axis=1).astype(BF16), preferred_element_type=F32)
    cum = pieces[:, 0:LANES] + pieces[:, LANES:2 * LANES] + pieces[:, 2 * LANES:3 * LANES] + carry_ref[...]
    carry_ref[...] = cum[tm - 1:tm, :]
    nhi, nmid, nlo = _split3(-LOG2E * cum)
    gate_bias = jnp.where(lane < 4, pltpu.roll(nhi, LANES - MISC_F, axis=1),
                          jnp.where(lane < 8, pltpu.roll(nmid, LANES - MISC_F + 4, axis=1),
                                    jnp.where(lane < 12, pltpu.roll(nlo, LANES - MISC_F + 8, axis=1),
                                              jnp.zeros_like(nlo))))

    for gi, name in enumerate(QK_GROUPS):
        for c in range(GROUP // LANES):
            lo = gi * GROUP + c * LANES
            v = pm[:, lo:lo + LANES]
            if name in ROPE64_GROUPS:
                v = _rope_chunk(v, c64, s64, HEAD_DIM // 2, lane)
            elif name in ROPE32_GROUPS:
                v = _rope_chunk(v, c32, s32, DIFF_DIM // 2, lane)
            if name in Q_SCALE:
                v = v * Q_SCALE[name]
            if name == "b_k":
                out = (QK_BLK[name] + c) * GROUP
                qk_ref[0, :, out:out + LANES] = v.astype(BF16)
                qk_ref[0, :, out + LANES:out + GROUP] = gate_bias.astype(BF16)
            else:
                out = QK_BLK[name] * GROUP + c * LANES
                qk_ref[0, :, out:out + LANES] = v.astype(BF16)

    vt_ref[0, 0] = pm[:, V_BASE:V_BASE + len(V_GROUPS) * GROUP].T.astype(BF16)

    def cat_q(q):
        hi, lo = _split2(q)
        return jnp.where(lo_half, hi, pltpu.roll(lo, HEAD_DIM, axis=1)), hi

    for c in range(512 // LANES):
        v = _rope_chunk(pi[:, c * LANES:(c + 1) * LANES], c64, s64, IDX_DIM // 2, lane)
        for r, q in enumerate((jnp.where(lo_half, v, 0.0), jnp.where(lo_half, pltpu.roll(v, HEAD_DIM, axis=1), 0.0))):
            a, b2 = cat_q(q)
            iq_ref[0, 2 * c + r, 0, 0:LANES, :] = a.T.astype(BF16)
            iq_ref[0, 2 * c + r, 0, LANES:IDX_CAT, :] = b2.T.astype(BF16)
    ik = jnp.where(lo_half, _rope_chunk(misc, c64, s64, IDX_DIM // 2, lane), 0.0)
    khi, klo = _split2(ik)
    ik_ref[0, :, 0:LANES] = jnp.where(lo_half, khi, pltpu.roll(khi, HEAD_DIM, axis=1)).astype(BF16)
    ik_ref[0, :, LANES:IDX_CAT] = klo.astype(BF16)
    iw_t = (misc * (IDX_HEADS ** -0.5 * IDX_DIM ** -0.5)).T
    iw_ref[0, 0] = iw_t[MISC_IW:MISC_IW + IDX_HEADS, :]


def _proj_call(x, g, w_main, w_idx, tabs, bf_row, *, tm):
    b, s, d = x.shape
    tok = lambda w: pl.BlockSpec((1, tm, w), lambda bi, ti: (bi, ti, 0))
    full = lambda a: pl.BlockSpec(a.shape, lambda bi, ti: (0,) * a.ndim)
    return pl.pallas_call(
        functools.partial(_proj_kernel, tm=tm),
        grid=(b, s // tm),
        in_specs=[tok(d), full(g), full(w_main), full(w_idx), tok(LANES), tok(LANES), tok(LANES), tok(LANES),
                  full(bf_row)],
        out_specs=[tok(QK_W),
                   pl.BlockSpec((1, 1, len(V_GROUPS) * GROUP, tm), lambda bi, ti: (bi, ti, 0, 0)),
                   pl.BlockSpec((1, IDX_HEADS, 1, IDX_CAT, tm), lambda bi, ti: (bi, 0, ti, 0, 0)),
                   tok(IDX_CAT),
                   pl.BlockSpec((1, 1, IDX_HEADS, tm), lambda bi, ti: (bi, ti, 0, 0))],
        out_shape=[jax.ShapeDtypeStruct((b, s, QK_W), BF16),
                   jax.ShapeDtypeStruct((b, s // tm, len(V_GROUPS) * GROUP, tm), BF16),
                   jax.ShapeDtypeStruct((b, IDX_HEADS, s // tm, IDX_CAT, tm), BF16),
                   jax.ShapeDtypeStruct((b, s, IDX_CAT), BF16),
                   jax.ShapeDtypeStruct((b, s // tm, IDX_HEADS, tm), F32)],
        scratch_shapes=[pltpu.VMEM((1, LANES), F32)],
        compiler_params=_cparams(("arbitrary", "arbitrary")),
        name="proj",
    )(x, g, w_main, w_idx, *tabs, bf_row)


def _softmax_step(s, m_ref, l_ref, idx):
    m_old = m_ref[idx]
    m_new = jnp.maximum(m_old, jnp.max(s, axis=0, keepdims=True))
    alpha = jnp.exp2(m_old - m_new)
    p = jnp.exp2(s - m_new)
    l_ref[idx] = alpha * l_ref[idx] + jnp.sum(p, axis=0, keepdims=True)
    m_ref[idx] = m_new
    return p, alpha


def _init_state(m_ref, l_ref, acc_ref):
    m_ref[...] = jnp.full(m_ref.shape, NEG_INF, F32)
    l_ref[...] = jnp.zeros(l_ref.shape, F32)
    acc_ref[...] = jnp.zeros(acc_ref.shape, F32)


def _causal_chunk(w, t):
    return lax.broadcasted_iota(I32, (w * t, t), 0) <= lax.broadcasted_iota(I32, (w * t, t), 1) + (w - 1) * t


def _head_mask(r):
    lane = lax.broadcasted_iota(I32, (1, LANES), 1)
    return (lane >= HEAD_DIM) if r else (lane < HEAD_DIM)


def _key_chunks(n, chunk_fn, width):
    def _body(c, carry):
        chunk_fn(width * c, width, False)
        return carry

    n_full = n // width
    lax.fori_loop(0, n_full, _body, 0)
    rest = n - n_full * width
    for k in range(width):
        @pl.when(rest == k)
        def _(k=k):
            chunk_fn(n - k, k + 1, True)


def _pipelined_key_chunks(n, scores_fn, rest_fn, width):
    n_full = n // width
    rest = n - n_full * width
    odd = n_full & 1

    @pl.when(odd == 1)
    def _():
        scores_fn(0, 1)
        scores_fn(width, 0)
        rest_fn(0, width, False, 1)

    @pl.when(odd == 0)
    def _():
        scores_fn(0, 0)

    def _body(pair, carry):
        j0 = (odd + 2 * pair) * width
        scores_fn(j0 + width, 1)
        rest_fn(j0, width, False, 0)
        scores_fn(j0 + 2 * width, 0)
        rest_fn(j0 + width, width, False, 1)
        return carry

    lax.fori_loop(0, (n_full - odd) >> 1, _body, 0)
    for k in range(width):
        @pl.when(rest == k)
        def _(k=k):
            rest_fn(n - k, k + 1, True, 0)


def _accumulate(acc_ref, a, r, alpha, vt_tiles, p, t):
    rows = slice(r * HEAD_DIM, (r + 1) * HEAD_DIM)
    pb = p.astype(BF16)
    pv = jnp.dot(vt_tiles[0], pb[0:t], preferred_element_type=F32)
    for i in range(1, len(vt_tiles)):
        pv = pv + jnp.dot(vt_tiles[i], pb[i * t:(i + 1) * t], preferred_element_type=F32)
    acc_ref[a, rows, :] = acc_ref[a, rows, :] * alpha + pv


def _normalised_t(acc_ref, a, l_ref, idx0, idx1):
    return jnp.concatenate([acc_ref[a, 0:HEAD_DIM, :] / l_ref[idx0],
                            acc_ref[a, HEAD_DIM:LANES, :] / l_ref[idx1]], axis=0)


def _dsa_kernel(iq_ref, iw_ref, ik_ref, q_ref, k_ref, vt_ref, o_ref,
                key_ref, pack_ref, pack4_ref, m_ref, l_ref, acc_ref, carry_ref, qzt_ref, s_ref, thr_ref, nge_ref,
                *, t, topk):
    n = pl.program_id(1)
    for h in range(4):
        qzt_ref[h] = _transpose_bf16(jnp.where(_head_mask(h % 2), q_ref[0, :, (h // 2) * LANES:(h // 2 + 1) * LANES],
                                               jnp.zeros((), BF16)))

    def score_chunk(j0, w, diag):
        start = pl.multiple_of(j0 * t, t)
        ik = ik_ref[0, pl.ds(start, w * t), :]
        sc = jnp.zeros((w * t, t), F32)
        for h in range(IDX_HEADS):
            d = _dot(ik, iq_ref[0, h, 0])
            sc = sc + iw_ref[0, 0, h:h + 1, :] * jnp.maximum(d, 0.0)
        if diag:
            sc = jnp.where(_causal_chunk(w, t), sc, NEG_INF)
        sc = jnp.where(sc == 0.0, 0.0, sc)
        bits = lax.bitcast_convert_type(sc, I32)
        key = bits ^ ((bits >> 31) & jnp.int32(0x7FFFFFFF))
        top = lax.shift_right_logical(key ^ jnp.int32(INT_MIN), 32 - FIELD_BITS)
        for i in range(w):
            key_ref[j0 + i] = key[i * t:(i + 1) * t]
            pack_ref[j0 + i] = (jnp.left_shift(top[i * t:i * t + t // 2], 16) | top[i * t + t // 2:(i + 1) * t]
                                | jnp.int32(FIELD_GUARDS))
            lead = [lax.shift_right_logical(top[i * t + q * (t // 4):i * t + (q + 1) * (t // 4)], FIELD_BITS - BYTE_BITS)
                    for q in range(4)]
            pack4_ref[j0 + i] = (jnp.left_shift(lead[0], 24) | jnp.left_shift(lead[1], 16) | jnp.left_shift(lead[2], 8)
                                 | lead[3] | jnp.int32(BYTE_GUARDS))

    _key_chunks(n, score_chunk, CHUNK_TILES)

    def tree_sum(parts):
        while len(parts) > 1:
            parts = [parts[i] + parts[i + 1] for i in range(0, len(parts), 2)]
        return parts[0]

    def count(pred):
        def body(j, acc):
            c = jnp.where(pred(key_ref[j]), 1.0, 0.0)
            return acc + tree_sum([c[r * SUBLANES:(r + 1) * SUBLANES, :] for r in range(t // SUBLANES)])
        acc = lax.fori_loop(0, n + 1, body, jnp.zeros((SUBLANES, t), F32))
        return jnp.sum(acc, axis=0, keepdims=True)

    def count_top(cand):
        both = jnp.left_shift(cand, 16) | cand

        def body(j, acc):
            z = pack_ref[j] - both
            c = lax.shift_right_logical(z, FIELD_BITS) & jnp.int32(0x00010001)
            return acc + tree_sum([c[r * SUBLANES:(r + 1) * SUBLANES, :] for r in range(t // 2 // SUBLANES)])
        acc = lax.fori_loop(0, n + 1, body, jnp.zeros((SUBLANES, t), I32))
        per_lane = lax.shift_right_logical(acc, 16) + (acc & jnp.int32(0xFFFF))
        return jnp.sum(per_lane.astype(F32), axis=0, keepdims=True)

    def count_lead(cand):
        four = cand * jnp.int32(0x01010101)
        pairs = jnp.int32(0x00FF00FF)

        def body(j, acc):
            even, odd = acc
            z = pack4_ref[j] - four
            c = lax.shift_right_logical(z, BYTE_BITS) & jnp.int32(0x01010101)
            c = tree_sum([c[r * SUBLANES:(r + 1) * SUBLANES, :] for r in range(t // 4 // SUBLANES)])
            return even + (c & pairs), odd + (lax.shift_right_logical(c, 8) & pairs)
        zeros = jnp.zeros((SUBLANES, t), I32)
        even, odd = lax.fori_loop(0, n + 1, body, (zeros, zeros))
        both = even + odd
        per_lane = lax.shift_right_logical(both, 16) + (both & jnp.int32(0xFFFF))
        return jnp.sum(per_lane.astype(F32), axis=0, keepdims=True)

    def bisect(n_bits, low_bit, count_fn, carry):
        def body(i, carry):
            u, n_ge, n_rej = carry
            cand = u | jnp.left_shift(jnp.int32(1), low_bit + n_bits - 1 - i)
            cnt = count_fn(cand)
            keep = cnt >= float(topk)
            return jnp.where(keep, cand, u), jnp.where(keep, cnt, n_ge), jnp.where(keep, n_rej, cnt)
        return lax.fori_loop(0, n_bits, body, carry)

    def count_full(cand):
        return count(lambda kt: kt >= (cand ^ jnp.int32(INT_MIN)))

    in_scope = ((n + 1) * t).astype(F32)
    zero_row = jnp.zeros((1, t), F32)
    lead, n_ge, n_rej = bisect(BYTE_BITS, 0, count_lead, (jnp.zeros((1, t), I32), zero_row + in_scope, zero_row))
    top, n_ge, n_rej = bisect(FIELD_BITS - BYTE_BITS, 0, count_top,
                              (jnp.left_shift(lead, FIELD_BITS - BYTE_BITS), n_ge, n_rej))
    u, n_ge, n_rej = bisect(32 - FIELD_BITS - LOW_BITS, LOW_BITS, count_full,
                            (jnp.left_shift(top, 32 - FIELD_BITS), n_ge, n_rej))

    prefix = u ^ jnp.int32(INT_MIN)
    low_mask = jnp.int32((1 << LOW_BITS) - 1)
    big = jnp.int32(1 << (LOW_BITS + 1))

    def tree(parts, op):
        while len(parts) > 1:
            parts = [op(parts[i], parts[i + 1]) for i in range(0, len(parts), 2)]
        return parts[0]

    def band_body(j, carry):
        mx, mn, sm = carry
        kt = key_ref[j]
        inside = lax.shift_right_logical(kt ^ prefix, LOW_BITS) == 0
        low = kt & low_mask
        rows = lambda a: [a[r * SUBLANES:(r + 1) * SUBLANES, :] for r in range(t // SUBLANES)]
        low_or_zero = rows(jnp.where(inside, low, 0))
        return (jnp.maximum(mx, tree(low_or_zero, jnp.maximum)),
                jnp.minimum(mn, tree(rows(jnp.where(inside, low, big)), jnp.minimum)),
                sm + tree(low_or_zero, jnp.add))

    mx, mn, sm = lax.fori_loop(0, n + 1, band_body, (jnp.zeros((SUBLANES, t), I32),
                                                       jnp.full((SUBLANES, t), 1 << (LOW_BITS + 1), I32),
                                                       jnp.zeros((SUBLANES, t), I32)))
    mx = jnp.max(mx.astype(F32), axis=0, keepdims=True)
    mn = jnp.min(mn.astype(F32), axis=0, keepdims=True)
    sm = jnp.sum(sm.astype(F32), axis=0, keepdims=True)
    short = n_ge < float(topk)
    in_band = n_ge - n_rej
    need = float(topk) - n_rej
    second = jnp.where(in_band > 2.5, sm - mx - mn, mn)
    thr_low = jnp.where(need < 1.5, mx, jnp.where(need < 2.5, second, mn))
    kept = (1.0 + jnp.where((in_band > 1.5) & (second >= thr_low), 1.0, 0.0)
            + jnp.where((in_band > 2.5) & (mn >= thr_low), 1.0, 0.0))
    resolved = jnp.min(jnp.where(short | (in_band < 3.5), 1.0, 0.0)) > 0.5

    @pl.when(resolved)
    def _():
        thr_ref[...] = jnp.where(short, prefix, prefix | thr_low.astype(I32))
        nge_ref[...] = jnp.where(short, n_ge, n_rej + kept)

    @pl.when(jnp.logical_not(resolved))
    def _():
        u_all, n_ge_all, _ = bisect(LOW_BITS, 0, count_full, (u, n_ge, n_rej))
        thr_ref[...] = u_all ^ jnp.int32(INT_MIN)
        nge_ref[...] = n_ge_all

    thr = thr_ref[...]
    has_ties = jnp.max(nge_ref[...]) > float(topk)

    _init_state(m_ref, l_ref, acc_ref)

    def attend(ties):
        if ties:
            carry_ref[...] = jnp.zeros(carry_ref.shape, F32)
            need = float(topk) - count(lambda kt: kt > thr)
            lower = (lax.broadcasted_iota(I32, (t, t), 1) <= lax.broadcasted_iota(I32, (t, t), 0)).astype(BF16)

        def scores(j0, buf):
            start = pl.multiple_of(j0 * t, t)
            ks = k_ref[0, pl.ds(start, CHUNK_TILES * t), :]
            for h in range(4):
                s_ref[buf, h] = _dot(ks[:, (h // 2) * LANES:(h // 2 + 1) * LANES], qzt_ref[h])

        def rest(j0, w, diag, buf):
            if ties:
                parts = []
                for i in range(w):
                    kt = key_ref[j0 + i]
                    eq = kt == thr
                    incl = jnp.dot(lower, jnp.where(eq, 1.0, 0.0).astype(BF16),
                                   preferred_element_type=F32) + carry_ref[...]
                    carry_ref[...] = incl[t - 1:t, :]
                    parts.append(jnp.where((kt > thr) | (eq & (incl <= need)), 1.0, 0.0))
                sel = (parts[0] if w == 1 else jnp.concatenate(parts, axis=0)) > 0.5
            else:
                kt = key_ref[j0] if w == 1 else jnp.concatenate([key_ref[j0 + i] for i in range(w)], axis=0)
                sel = kt >= thr
            if diag:
                sel = sel & _causal_chunk(w, t)
            for h in range(4):
                g, r = divmod(h, 2)
                p, alpha = _softmax_step(jnp.where(sel, s_ref[buf, h, 0:w * t, :], -jnp.inf), m_ref, l_ref, h)
                _accumulate(acc_ref, g, r, alpha,
                            [vt_ref[0, j0 + i, h * HEAD_DIM:(h + 1) * HEAD_DIM, :] for i in range(w)], p, t)

        _pipelined_key_chunks(n, scores, rest, CHUNK_TILES)

    @pl.when(has_ties)
    def _():
        attend(True)

    @pl.when(jnp.logical_not(has_ties))
    def _():
        attend(False)

    for g in range(2):
        o_ref[0, :, g * LANES:(g + 1) * LANES] = _normalised_t(acc_ref, g, l_ref, 2 * g, 2 * g + 1).T.astype(BF16)


def _dsa_call(qk, vt, iq, ik, iw, *, t, topk):
    b, s, _ = qk.shape
    nt = s // t
    return pl.pallas_call(
        functools.partial(_dsa_kernel, t=t, topk=topk),
        grid=(b, nt),
        in_specs=[pl.BlockSpec((1, IDX_HEADS, 1, IDX_CAT, t), lambda bi, ni: (bi, 0, ni, 0, 0)),
                  pl.BlockSpec((1, 1, IDX_HEADS, t), lambda bi, ni: (bi, ni, 0, 0)),
                  pl.BlockSpec((1, s, IDX_CAT), lambda bi, ni: (bi, 0, 0), pipeline_mode=RESIDENT),
                  pl.BlockSpec((1, t, GROUP), lambda bi, ni: (bi, ni, QK_BLK["a_q"])),
                  pl.BlockSpec((1, s, GROUP), lambda bi, ni: (bi, 0, QK_BLK["a_k"]), pipeline_mode=RESIDENT),
                  pl.BlockSpec((1, nt, GROUP, t), lambda bi, ni: (bi, 0, V_ROW_BLK["a_v"], 0), pipeline_mode=RESIDENT)],
        out_specs=pl.BlockSpec((1, t, GROUP), lambda bi, ni: (bi, ni, 0)),
        out_shape=jax.ShapeDtypeStruct((b, s, GROUP), BF16),
        scratch_shapes=[pltpu.VMEM((nt, t, t), I32), pltpu.VMEM((nt, t // 2, t), I32), pltpu.VMEM((nt, t // 4, t), I32),
                        pltpu.VMEM((4, 1, t), F32), pltpu.VMEM((4, 1, t), F32),
                        pltpu.VMEM((2, LANES, t), F32), pltpu.VMEM((1, t), F32),
                        pltpu.VMEM((4, LANES, t), BF16),
                        pltpu.VMEM((2, 4, CHUNK_TILES * t, t), F32),
                        pltpu.VMEM((1, t), I32), pltpu.VMEM((1, t), F32)],
        compiler_params=_cparams(("arbitrary", "arbitrary")),
        name="dsa",
    )(iq, iw, ik, qk, qk, vt)


def _fox_kernel(q_ref, k0_ref, k1_ref, vt_ref, o_ref, m_ref, l_ref, acc_ref, qxt_ref, s_ref, *, t):
    n = pl.program_id(1)
    _init_state(m_ref, l_ref, acc_ref)
    lane = lax.broadcasted_iota(I32, (1, LANES), 1)
    for h in range(4):
        qz = jnp.where(_head_mask(h % 2), q_ref[0, :, (h // 2) * LANES:(h // 2 + 1) * LANES], jnp.zeros((), BF16))
        ones = jnp.where((lane == h) | (lane == 4 + h) | (lane == 8 + h), 1.0, 0.0).astype(BF16)
        qxt_ref[h] = _transpose_bf16(jnp.concatenate([qz, jnp.broadcast_to(ones, (t, LANES))], axis=1))
    k_refs = (k0_ref, k1_ref)

    def scores(j0, buf):
        start = pl.multiple_of(j0 * t, t)
        for h in range(4):
            s_ref[buf, h] = _dot(k_refs[h // 2][0, pl.ds(start, CHUNK_TILES * t), :], qxt_ref[h])

    def rest(j0, w, diag, buf):
        for h in range(4):
            g, r = divmod(h, 2)
            s = s_ref[buf, h, 0:w * t, :]
            if diag:
                s = jnp.where(_causal_chunk(w, t), s, -jnp.inf)
            p, alpha = _softmax_step(s, m_ref, l_ref, h)
            _accumulate(acc_ref, g, r, alpha,
                        [vt_ref[0, j0 + i, h * HEAD_DIM:(h + 1) * HEAD_DIM, :] for i in range(w)], p, t)

    _pipelined_key_chunks(n, scores, rest, CHUNK_TILES)
    for g in range(2):
        o_ref[0, :, g * LANES:(g + 1) * LANES] = _normalised_t(acc_ref, g, l_ref, 2 * g, 2 * g + 1).T.astype(BF16)


def _fox_call(qk, vt, *, t):
    b, s, _ = qk.shape
    nt = s // t
    return pl.pallas_call(
        functools.partial(_fox_kernel, t=t),
        grid=(b, nt),
        in_specs=[pl.BlockSpec((1, t, GROUP), lambda bi, ni: (bi, ni, QK_BLK["b_q"])),
                  pl.BlockSpec((1, s, GROUP), lambda bi, ni: (bi, 0, QK_BLK["b_k"]), pipeline_mode=RESIDENT),
                  pl.BlockSpec((1, s, GROUP), lambda bi, ni: (bi, 0, QK_BLK["b_k"] + 1), pipeline_mode=RESIDENT),
                  pl.BlockSpec((1, nt, GROUP, t), lambda bi, ni: (bi, 0, V_ROW_BLK["b_v"], 0), pipeline_mode=RESIDENT)],
        out_specs=pl.BlockSpec((1, t, GROUP), lambda bi, ni: (bi, ni, 0)),
        out_shape=jax.ShapeDtypeStruct((b, s, GROUP), BF16),
        scratch_shapes=[pltpu.VMEM((4, 1, t), F32), pltpu.VMEM((4, 1, t), F32), pltpu.VMEM((2, LANES, t), F32),
                        pltpu.VMEM((4, GROUP, t), BF16), pltpu.VMEM((2, 4, CHUNK_TILES * t, t), F32)],
        compiler_params=_cparams(("arbitrary", "arbitrary")),
        name="fox",
    )(qk, qk, qk, vt)


def _diff_kernel(lam_ref, q_ref, k_ref, vt_ref, gs_ref, o_ref, m_ref, l_ref, acc_ref, qzt_ref, s_ref, *, t):
    n = pl.program_id(1)
    _init_state(m_ref, l_ref, acc_ref)
    lane = lax.broadcasted_iota(I32, (1, LANES), 1)
    for h in range(4):
        for mm in range(2):
            lo = (h % 2) * HEAD_DIM + mm * DIFF_DIM
            qzt_ref[2 * h + mm] = _transpose_bf16(
                jnp.where((lane >= lo) & (lane < lo + DIFF_DIM),
                          q_ref[0, :, (h // 2) * LANES:(h // 2 + 1) * LANES], jnp.zeros((), BF16)))

    def scores(j0, buf):
        start = pl.multiple_of(j0 * t, t)
        ks = k_ref[0, pl.ds(start, CHUNK_TILES_DIFF * t), :]
        for i in range(8):
            s_ref[buf, i] = _dot(ks[:, (i // 4) * LANES:(i // 4 + 1) * LANES], qzt_ref[i])

    def rest(j0, w, diag, buf):
        for h in range(4):
            g, r = divmod(h, 2)
            vts = [vt_ref[0, j0 + i, h * HEAD_DIM:(h + 1) * HEAD_DIM, :] for i in range(w)]
            for mm in range(2):
                s = s_ref[buf, 2 * h + mm, 0:w * t, :]
                if diag:
                    s = jnp.where(_causal_chunk(w, t), s, -jnp.inf)
                p, alpha = _softmax_step(s, m_ref, l_ref, 2 * h + mm)
                _accumulate(acc_ref, 2 * mm + g, r, alpha, vts, p, t)

    _pipelined_key_chunks(n, scores, rest, CHUNK_TILES_DIFF)
    lam = lam_ref[0]
    out_scale = lam_ref[1]
    for g in range(2):
        o1 = _normalised_t(acc_ref, g, l_ref, 4 * g, 4 * g + 2)
        o2 = _normalised_t(acc_ref, 2 + g, l_ref, 4 * g + 1, 4 * g + 3)
        o = o1 - lam * o2
        sq = o * o
        ms = jnp.concatenate(
            [jnp.broadcast_to(jnp.mean(sq[r * HEAD_DIM:(r + 1) * HEAD_DIM], axis=0, keepdims=True), (HEAD_DIM, t))
             for r in range(2)], axis=0)
        y = o * lax.rsqrt(ms + EPS) * gs_ref[...] * out_scale
        o_ref[0, :, g * LANES:(g + 1) * LANES] = y.T.astype(BF16)


def _diff_call(lam2, qk, vt, gs_full, *, t):
    b, s, _ = qk.shape
    nt = s // t
    return pl.pallas_call(
        functools.partial(_diff_kernel, t=t),
        grid=(b, nt),
        in_specs=[pl.BlockSpec(memory_space=pltpu.SMEM),
                  pl.BlockSpec((1, t, GROUP), lambda bi, ni: (bi, ni, QK_BLK["c_q"])),
                  pl.BlockSpec((1, s, GROUP), lambda bi, ni: (bi, 0, QK_BLK["c_k"]), pipeline_mode=RESIDENT),
                  pl.BlockSpec((1, nt, GROUP, t), lambda bi, ni: (bi, 0, V_ROW_BLK["c_v"], 0), pipeline_mode=RESIDENT),
                  pl.BlockSpec((LANES, t), lambda bi, ni: (0, 0))],
        out_specs=pl.BlockSpec((1, t, GROUP), lambda bi, ni: (bi, ni, 0)),
        out_shape=jax.ShapeDtypeStruct((b, s, GROUP), BF16),
        scratch_shapes=[pltpu.VMEM((8, 1, t), F32), pltpu.VMEM((8, 1, t), F32), pltpu.VMEM((4, LANES, t), F32),
                        pltpu.VMEM((8, LANES, t), BF16), pltpu.VMEM((2, 8, CHUNK_TILES_DIFF * t, t), F32)],
        compiler_params=_cparams(("arbitrary", "arbitrary")),
        name="diff",
    )(lam2, qk, qk, vt, gs_full)


def _swa_kernel(sink_ref, q_ref, kp_ref, kc_ref, vtp_ref, vtc_ref, o_ref, *, t):
    n = pl.program_id(1)
    prow = lax.broadcasted_iota(I32, (WINDOW, t), 0)
    pcol = lax.broadcasted_iota(I32, (WINDOW, t), 1)
    mask_prev = (prow > pcol) & (pcol + jnp.where(n > 0, 0, t) < WINDOW)
    crow = lax.broadcasted_iota(I32, (t, t), 0)
    ccol = lax.broadcasted_iota(I32, (t, t), 1)
    mask_cur = (crow <= ccol) & (crow > ccol - WINDOW)
    vtp = vtp_ref[0, 0][:, t - WINDOW:t]
    vtc = vtc_ref[0, 0]
    for g in range(2):
        outs = []
        for r in range(2):
            h = 2 * g + r
            qzt = _transpose_bf16(jnp.where(_head_mask(r), q_ref[0, :, g * LANES:(g + 1) * LANES],
                                            jnp.zeros((), BF16)))
            sp = jnp.where(mask_prev, _dot(kp_ref[0, :, g * LANES:(g + 1) * LANES], qzt), -jnp.inf)
            sc = jnp.where(mask_cur, _dot(kc_ref[0, :, g * LANES:(g + 1) * LANES], qzt), -jnp.inf)
            sink = sink_ref[h] * LOG2E
            m = jnp.maximum(jnp.maximum(jnp.max(sp, axis=0, keepdims=True),
                                        jnp.max(sc, axis=0, keepdims=True)), sink)
            pp = jnp.exp2(sp - m)
            pc = jnp.exp2(sc - m)
            den = jnp.sum(pp, axis=0, keepdims=True) + jnp.sum(pc, axis=0, keepdims=True) + jnp.exp2(sink - m)
            rows = slice(h * HEAD_DIM, (h + 1) * HEAD_DIM)
            o = (jnp.dot(vtp[rows, :], pp.astype(BF16), preferred_element_type=F32)
                 + jnp.dot(vtc[rows, :], pc.astype(BF16), preferred_element_type=F32))
            outs.append(o / den)
        o_ref[0, :, g * LANES:(g + 1) * LANES] = jnp.concatenate(outs, axis=0).T.astype(BF16)


def _swa_call(sinks, qk, vt, *, t):
    b, s, _ = qk.shape
    nt = s // t
    per = t // WINDOW
    return pl.pallas_call(
        functools.partial(_swa_kernel, t=t),
        grid=(b, nt),
        in_specs=[pl.BlockSpec(memory_space=pltpu.SMEM),
                  pl.BlockSpec((1, t, GROUP), lambda bi, ni: (bi, ni, QK_BLK["d_q"])),
                  pl.BlockSpec((1, WINDOW, GROUP), lambda bi, ni: (bi, jnp.maximum(ni * per - 1, 0), QK_BLK["d_k"])),
                  pl.BlockSpec((1, t, GROUP), lambda bi, ni: (bi, ni, QK_BLK["d_k"])),
                  pl.BlockSpec((1, 1, GROUP, t), lambda bi, ni: (bi, jnp.maximum(ni - 1, 0), V_ROW_BLK["d_v"], 0)),
                  pl.BlockSpec((1, 1, GROUP, t), lambda bi, ni: (bi, ni, V_ROW_BLK["d_v"], 0))],
        out_specs=pl.BlockSpec((1, t, GROUP), lambda bi, ni: (bi, ni, 0)),
        out_shape=jax.ShapeDtypeStruct((b, s, GROUP), BF16),
        compiler_params=_cparams(("arbitrary", "arbitrary")),
        name="swa",
    )(sinks, qk, qk, qk, vt, vt)


def _post_kernel(oa_ref, ob_ref, oc_ref, od_ref, x_ref, p_ref, wo_ref, gmix_ref, gpre_ref, wu_ref, wd_ref, gpost_ref,
                 wg_ref, wp_ref, y_ref, *, chunk):
    acc = jnp.zeros(x_ref.shape[1:], F32)
    for i, o_ref in enumerate((oa_ref, ob_ref, oc_ref, od_ref)):
        acc = acc + jnp.dot(o_ref[0], wo_ref[i * GROUP:(i + 1) * GROUP, :], preferred_element_type=F32)
    x = x_ref[0] + _rms(acc, gmix_ref[...])
    h = _rms(x, gpre_ref[...]).astype(BF16)
    acc = jnp.zeros(x.shape, F32)
    for c in range(D_FF // chunk):
        u = jnp.dot(h, wu_ref[:, c * chunk:(c + 1) * chunk], preferred_element_type=F32)
        u = jnp.square(jnp.maximum(u, 0.0)).astype(BF16)
        acc = acc + jnp.dot(u, wd_ref[c * chunk:(c + 1) * chunk, :], preferred_element_type=F32)
    x = x + _rms(acc, gpost_ref[...])
    gate = jax.nn.sigmoid(jnp.dot(x.astype(BF16), wg_ref[...], preferred_element_type=F32))
    emb = jnp.dot(p_ref[0].astype(BF16), wp_ref[...], preferred_element_type=F32)
    y_ref[0] = x + gate * emb


def _post_call(tok_inputs, full_inputs, *, tm):
    b, s, d = tok_inputs[4].shape
    tok = lambda a: pl.BlockSpec((1, tm, a.shape[2]), lambda bi, ti: (bi, ti, 0))
    full = lambda a: pl.BlockSpec(a.shape, lambda bi, ti: (0,) * a.ndim, pipeline_mode=pl.Buffered(1))
    return pl.pallas_call(
        functools.partial(_post_kernel, chunk=1024),
        grid=(b, s // tm),
        in_specs=[tok(a) for a in tok_inputs] + [full(a) for a in full_inputs],
        out_specs=pl.BlockSpec((1, tm, d), lambda bi, ti: (bi, ti, 0)),
        out_shape=jax.ShapeDtypeStruct((b, s, d), F32),
        compiler_params=_cparams(("arbitrary", "arbitrary")),
        name="post",
    )(*tok_inputs, *full_inputs)


def _dup_kv(w):
    return jnp.concatenate([w[:, :HEAD_DIM], w[:, :HEAD_DIM], w[:, HEAD_DIM:], w[:, HEAD_DIM:]], axis=1)


def _prep_w_in(w):
    sec = lambda n: w[:, _SEC[n][0]:_SEC[n][0] + _SEC[n][1]]
    cols = [_dup_kv(sec(n)) if n in ("d_k", "d_v") else sec(n) for n in QK_GROUPS + V_GROUPS]
    w_main = jnp.concatenate(cols, axis=1).astype(BF16)
    pad = jnp.zeros((w.shape[0], LANES - IDX_DIM - IDX_HEADS - 4), w.dtype)
    w_idx = jnp.concatenate([sec("iq"), sec("ik"), sec("iw"), sec("b_f"), pad], axis=1)
    w_hi = w_idx.astype(BF16)
    w_lo = (w_idx - w_hi.astype(F32)).astype(BF16)
    return w_main, jnp.concatenate([w_hi, w_hi, w_lo], axis=0)


def _rope_tables(positions):
    pos = positions.astype(F32)[..., None]
    lane = jnp.arange(LANES)

    def tabs(dim):
        half = dim // 2
        inv_freq = ROPE_THETA ** (-jnp.arange(half, dtype=F32) / half)
        ang = pos * inv_freq
        sign = jnp.where((lane % dim) < half, -1.0, 1.0).astype(F32)
        reps = (1, 1, LANES // half)
        return jnp.tile(jnp.cos(ang), reps), jnp.tile(jnp.sin(ang), reps) * sign

    c64, s64 = tabs(HEAD_DIM)
    c32, s32 = tabs(DIFF_DIM)
    return c64, s64, c32, s32


def kernel(x, p, positions, w_in, b_forget, lambda_q1, lambda_k1, lambda_q2, lambda_k2, diff_subln, sinks,
           w_out, norm_pre_mix, norm_post_mix, norm_pre_mlp, norm_post_mlp, w_mlp_up, w_mlp_down,
           w_ple_proj, w_ple_gate):
    b, s, d = x.shape
    depth = w_in.shape[0]
    t = min(256, s)
    assert s % (CHUNK_TILES * t) == 0, "sequence length must be a multiple of the attention key chunk"
    topk = min(TOPK_MAX, s // 4)
    tabs = _rope_tables(positions)
    row = lambda v: v.reshape(1, -1).astype(F32)

    for i in range(depth):
        lam_init = 0.8 - 0.6 * math.exp(-0.3 * i)
        w_main, w_idx = _prep_w_in(w_in[i])
        bf_row = jnp.zeros((1, LANES), F32).at[0, MISC_F:MISC_F + 4].set(b_forget[i])
        qk, vt, iq, ik, iw = _proj_call(x, row(norm_pre_mix[i]), w_main, w_idx, tabs, bf_row, tm=t)

        o_a = _dsa_call(qk, vt, iq, ik, iw, t=t, topk=topk)
        o_b = _fox_call(qk, vt, t=t)
        lam = (jnp.exp(jnp.sum(lambda_q1[i] * lambda_k1[i])) - jnp.exp(jnp.sum(lambda_q2[i] * lambda_k2[i]))
               + lam_init)
        lam2 = jnp.stack([lam, jnp.asarray(1.0 - lam_init, F32)]).astype(F32)
        gs_full = jnp.broadcast_to(jnp.concatenate([diff_subln[i], diff_subln[i]]).astype(F32)[:, None], (LANES, t))
        o_c = _diff_call(lam2, qk, vt, gs_full, t=t)
        o_d = _swa_call(sinks[i].astype(F32), qk, vt, t=t)

        x = _post_call([o_a, o_b, o_c, o_d, x, p[i]],
                       [w_out[i].astype(BF16), row(norm_post_mix[i]), row(norm_pre_mlp[i]),
                        w_mlp_up[i].astype(BF16), w_mlp_down[i].astype(BF16), row(norm_post_mlp[i]),
                        w_ple_gate[i].astype(BF16), w_ple_proj[i].astype(BF16)], tm=min(POST_TILE, s))
    return x
```

```python
import functools
import math

import jax
import jax.numpy as jnp
from jax import lax
from jax.experimental import pallas as pl
from jax.experimental.pallas import tpu as pltpu

F32, BF16, I32 = jnp.float32, jnp.bfloat16, jnp.int32

D_MODEL = 1024
HEAD_DIM = 64
DIFF_DIM = 32
IDX_HEADS = 8
IDX_DIM = 64
TOPK_MAX = 256
WINDOW = 128
D_FF = 4 * D_MODEL
ROPE_THETA = 10000.0
EPS = 1e-6
NEG_INF = -1e30
LANES = 128
SUBLANES = 8
GROUP = 256
INT_MIN = -2147483648
FIELD_BITS = 15
FIELD_GUARDS = -2147450880
LOW_BITS = 10
BYTE_BITS = 7
BYTE_GUARDS = -2139062144

_SEC = {}
_o = 0
for _name, _w in (("a_q", 256), ("a_k", 256), ("a_v", 256), ("iq", 512), ("ik", 64), ("iw", 8),
                  ("b_q", 256), ("b_k", 256), ("b_v", 256), ("b_f", 4),
                  ("c_q", 256), ("c_k", 256), ("c_v", 256), ("d_q", 256), ("d_k", 128), ("d_v", 128)):
    _SEC[_name] = (_o, _w)
    _o += _w

QK_GROUPS = ("a_q", "a_k", "b_q", "b_k", "c_q", "c_k", "d_q", "d_k")
V_GROUPS = ("a_v", "b_v", "c_v", "d_v")
MAIN_W = GROUP * (len(QK_GROUPS) + len(V_GROUPS))
V_BASE = GROUP * len(QK_GROUPS)
QK_BLK = {"a_q": 0, "a_k": 1, "b_q": 2, "b_k": 3, "c_q": 5, "c_k": 6, "d_q": 7, "d_k": 8}
QK_W = GROUP * 9
V_ROW_BLK = {n: i for i, n in enumerate(V_GROUPS)}
ROPE64_GROUPS = ("a_q", "a_k", "d_q", "d_k")
ROPE32_GROUPS = ("c_q", "c_k")
LOG2E = math.log2(math.e)
Q_SCALE = {"a_q": HEAD_DIM ** -0.5 * LOG2E, "b_q": HEAD_DIM ** -0.5 * LOG2E, "c_q": DIFF_DIM ** -0.5 * LOG2E,
           "d_q": HEAD_DIM ** -0.5 * LOG2E}
IDX_W = 512 + LANES
MISC_IW = 64
MISC_F = 72
IDX_CAT = 256

CHUNK_TILES = 4
CHUNK_TILES_DIFF = 2
POST_TILE = 1024
VMEM_LIMIT = 56 * 1024 * 1024
RESIDENT = pl.Buffered(1)


def _cparams(sem):
    return pltpu.CompilerParams(dimension_semantics=sem, vmem_limit_bytes=VMEM_LIMIT)


def _rms(x, g):
    return x * lax.rsqrt(jnp.mean(x * x, axis=-1, keepdims=True) + EPS) * g


def _dot(a, b):
    return jnp.dot(a, b, preferred_element_type=F32)


def _transpose_bf16(a):
    return a.astype(F32).T.astype(BF16)


def _bf16_part(x):
    return x.astype(BF16).astype(F32)


def _split2(x):
    hi = _bf16_part(x)
    return hi, _bf16_part(x - hi)


def _split3(x):
    hi = _bf16_part(x)
    mid = _bf16_part(x - hi)
    return hi, mid, _bf16_part(x - hi - mid)


def _rope_chunk(xc, cos, sin_signed, half, lane):
    fwd = pltpu.roll(xc, LANES - half, axis=1)
    bwd = pltpu.roll(xc, half, axis=1)
    partner = jnp.where((lane % (2 * half)) < half, fwd, bwd)
    return xc * cos + partner * sin_signed


def _proj_kernel(x_ref, g_ref, wm_ref, wi_ref, c64_ref, s64_ref, c32_ref, s32_ref, bf_ref,
                 qk_ref, vt_ref, iq_ref, ik_ref, iw_ref, carry_ref, *, tm):
    t = pl.program_id(1)
    x = x_ref[0]
    h = _rms(x, g_ref[...])
    pm = jnp.dot(h.astype(BF16), wm_ref[...], preferred_element_type=F32)
    h_hi, h_lo = _split2(h)
    pi = jnp.dot(jnp.concatenate([h_hi, h_lo, h_hi], axis=1).astype(BF16), wi_ref[...],
                 preferred_element_type=F32)
    lane = lax.broadcasted_iota(I32, (1, LANES), 1)
    lo_half = lane < HEAD_DIM
    c64, s64, c32, s32 = c64_ref[0], s64_ref[0], c32_ref[0], s32_ref[0]

    misc = pi[:, 512:512 + LANES]
    z = misc + bf_ref[...]
    logf = jnp.minimum(z, 0.0) - jnp.log1p(jnp.exp(-jnp.abs(z)))
    logf = jnp.where((lane >= MISC_F) & (lane < MISC_F + 4), logf, 0.0)
    tri = (lax.broadcasted_iota(I32, (tm, tm), 1) <= lax.broadcasted_iota(I32, (tm, tm), 0)).astype(BF16)

    @pl.when(t == 0)
    def _():
        carry_ref[...] = jnp.zeros_like(carry_ref)

    pieces = jnp.dot(tri, jnp.concatenate(_split3(logf), axis=1).astype(BF16), preferred_element_type=F32)
    cum = pieces[:, 0:LANES] + pieces[:, LANES:2 * LANES] + pieces[:, 2 * LANES:3 * LANES] + carry_ref[...]
    carry_ref[...] = cum[tm - 1:tm, :]
    nhi, nmid, nlo = _split3(-LOG2E * cum)
    gate_bias = jnp.where(lane < 4, pltpu.roll(nhi, LANES - MISC_F, axis=1),
                          jnp.where(lane < 8, pltpu.roll(nmid, LANES - MISC_F + 4, axis=1),
                                    jnp.where(lane < 12, pltpu.roll(nlo, LANES - MISC_F + 8, axis=1),
                                              jnp.zeros_like(nlo))))

    for gi, name in enumerate(QK_GROUPS):
        for c in range(GROUP // LANES):
            lo = gi * GROUP + c * LANES
            v = pm[:, lo:lo + LANES]
            if name in ROPE64_GROUPS:
                v = _rope_chunk(v, c64, s64, HEAD_DIM // 2, lane)
            elif name in ROPE32_GROUPS:
                v = _rope_chunk(v, c32, s32, DIFF_DIM // 2, lane)
            if name in Q_SCALE:
                v = v * Q_SCALE[name]
            if name == "b_k":
                out = (QK_BLK[name] + c) * GROUP
                qk_ref[0, :, out:out + LANES] = v.astype(BF16)
                qk_ref[0, :, out + LANES:out + GROUP] = gate_bias.astype(BF16)
            else:
                out = QK_BLK[name] * GROUP + c * LANES
                qk_ref[0, :, out:out + LANES] = v.astype(BF16)

    vt_ref[0, 0] = pm[:, V_BASE:V_BASE + len(V_GROUPS) * GROUP].T.astype(BF16)

    def cat_q(q):
        hi, lo = _split2(q)
        return jnp.where(lo_half, hi, pltpu.roll(lo, HEAD_DIM, axis=1)), hi

    for c in range(512 // LANES):
        v = _rope_chunk(pi[:, c * LANES:(c + 1) * LANES], c64, s64, IDX_DIM // 2, lane)
        for r, q in enumerate((jnp.where(lo_half, v, 0.0), jnp.where(lo_half, pltpu.roll(v, HEAD_DIM, axis=1), 0.0))):
            a, b2 = cat_q(q)
            iq_ref[0, 2 * c + r, :, 0:LANES] = a.astype(BF16)
            iq_ref[0, 2 * c + r, :, LANES:IDX_CAT] = b2.astype(BF16)
    ik = jnp.where(lo_half, _rope_chunk(misc, c64, s64, IDX_DIM // 2, lane), 0.0)
    khi, klo = _split2(ik)
    ik_ref[0, :, 0:LANES] = jnp.where(lo_half, khi, pltpu.roll(khi, HEAD_DIM, axis=1)).astype(BF16)
    ik_ref[0, :, LANES:IDX_CAT] = klo.astype(BF16)
    iw_t = (misc * (IDX_HEADS ** -0.5 * IDX_DIM ** -0.5)).T
    iw_ref[0, 0] = iw_t[MISC_IW:MISC_IW + IDX_HEADS, :]


def _proj_call(x, g, w_main, w_idx, tabs, bf_row, *, tm):
    b, s, d = x.shape
    tok = lambda w: pl.BlockSpec((1, tm, w), lambda bi, ti: (bi, ti, 0))
    full = lambda a: pl.BlockSpec(a.shape, lambda bi, ti: (0,) * a.ndim)
    return pl.pallas_call(
        functools.partial(_proj_kernel, tm=tm),
        grid=(b, s // tm),
        in_specs=[tok(d), full(g), full(w_main), full(w_idx), tok(LANES), tok(LANES), tok(LANES), tok(LANES),
                  full(bf_row)],
        out_specs=[tok(QK_W),
                   pl.BlockSpec((1, 1, len(V_GROUPS) * GROUP, tm), lambda bi, ti: (bi, ti, 0, 0)),
                   pl.BlockSpec((1, IDX_HEADS, tm, IDX_CAT), lambda bi, ti: (bi, 0, ti, 0)),
                   tok(IDX_CAT),
                   pl.BlockSpec((1, 1, IDX_HEADS, tm), lambda bi, ti: (bi, ti, 0, 0))],
        out_shape=[jax.ShapeDtypeStruct((b, s, QK_W), BF16),
                   jax.ShapeDtypeStruct((b, s // tm, len(V_GROUPS) * GROUP, tm), BF16),
                   jax.ShapeDtypeStruct((b, IDX_HEADS, s, IDX_CAT), BF16),
                   jax.ShapeDtypeStruct((b, s, IDX_CAT), BF16),
                   jax.ShapeDtypeStruct((b, s // tm, IDX_HEADS, tm), F32)],
        scratch_shapes=[pltpu.VMEM((1, LANES), F32)],
        compiler_params=_cparams(("arbitrary", "arbitrary")),
        name="proj",
    )(x, g, w_main, w_idx, *tabs, bf_row)


def _softmax_step(s, m_ref, l_ref, idx):
    m_old = m_ref[idx]
    m_new = jnp.maximum(m_old, jnp.max(s, axis=0, keepdims=True))
    alpha = jnp.exp2(m_old - m_new)
    p = jnp.exp2(s - m_new)
    l_ref[idx] = alpha * l_ref[idx] + jnp.sum(p, axis=0, keepdims=True)
    m_ref[idx] = m_new
    return p, alpha


def _init_state(m_ref, l_ref, acc_ref):
    m_ref[...] = jnp.full(m_ref.shape, NEG_INF, F32)
    l_ref[...] = jnp.zeros(l_ref.shape, F32)
    acc_ref[...] = jnp.zeros(acc_ref.shape, F32)


def _causal_chunk(w, t):
    return lax.broadcasted_iota(I32, (w * t, t), 0) <= lax.broadcasted_iota(I32, (w * t, t), 1) + (w - 1) * t


def _head_mask(r):
    lane = lax.broadcasted_iota(I32, (1, LANES), 1)
    return (lane >= HEAD_DIM) if r else (lane < HEAD_DIM)


def _key_chunks(n, chunk_fn, width):
    def _body(c, carry):
        chunk_fn(width * c, width, False)
        return carry

    n_full = n // width
    lax.fori_loop(0, n_full, _body, 0)
    rest = n - n_full * width
    for k in range(width):
        @pl.when(rest == k)
        def _(k=k):
            chunk_fn(n - k, k + 1, True)


def _pipelined_key_chunks(n, scores_fn, rest_fn, width):
    n_full = n // width
    rest = n - n_full * width
    odd = n_full & 1

    @pl.when(odd == 1)
    def _():
        scores_fn(0, 1)
        scores_fn(width, 0)
        rest_fn(0, width, False, 1)

    @pl.when(odd == 0)
    def _():
        scores_fn(0, 0)

    def _body(pair, carry):
        j0 = (odd + 2 * pair) * width
        scores_fn(j0 + width, 1)
        rest_fn(j0, width, False, 0)
        scores_fn(j0 + 2 * width, 0)
        rest_fn(j0 + width, width, False, 1)
        return carry

    lax.fori_loop(0, (n_full - odd) >> 1, _body, 0)
    for k in range(width):
        @pl.when(rest == k)
        def _(k=k):
            rest_fn(n - k, k + 1, True, 0)


def _accumulate(acc_ref, a, r, alpha, vt_tiles, p, t):
    rows = slice(r * HEAD_DIM, (r + 1) * HEAD_DIM)
    pb = p.astype(BF16)
    pv = jnp.dot(vt_tiles[0], pb[0:t], preferred_element_type=F32)
    for i in range(1, len(vt_tiles)):
        pv = pv + jnp.dot(vt_tiles[i], pb[i * t:(i + 1) * t], preferred_element_type=F32)
    acc_ref[a, rows, :] = acc_ref[a, rows, :] * alpha + pv


def _normalised_t(acc_ref, a, l_ref, idx0, idx1):
    return jnp.concatenate([acc_ref[a, 0:HEAD_DIM, :] / l_ref[idx0],
                            acc_ref[a, HEAD_DIM:LANES, :] / l_ref[idx1]], axis=0)


def _dsa_kernel(iq_ref, iw_ref, ik_ref, q_ref, k_ref, vt_ref, o_ref,
                key_ref, pack_ref, pack4_ref, m_ref, l_ref, acc_ref, carry_ref, iqt_ref, qzt_ref, s_ref, thr_ref, nge_ref,
                *, t, topk):
    n = pl.program_id(1)
    for h in range(IDX_HEADS):
        iqt_ref[h] = _transpose_bf16(iq_ref[0, h])
    for h in range(4):
        qzt_ref[h] = _transpose_bf16(jnp.where(_head_mask(h % 2), q_ref[0, :, (h // 2) * LANES:(h // 2 + 1) * LANES],
                                               jnp.zeros((), BF16)))

    def score_chunk(j0, w, diag):
        start = pl.multiple_of(j0 * t, t)
        ik = ik_ref[0, pl.ds(start, w * t), :]
        sc = jnp.zeros((w * t, t), F32)
        for h in range(IDX_HEADS):
            d = _dot(ik, iqt_ref[h])
            sc = sc + iw_ref[0, 0, h:h + 1, :] * jnp.maximum(d, 0.0)
        if diag:
            sc = jnp.where(_causal_chunk(w, t), sc, NEG_INF)
        sc = jnp.where(sc == 0.0, 0.0, sc)
        bits = lax.bitcast_convert_type(sc, I32)
        key = bits ^ ((bits >> 31) & jnp.int32(0x7FFFFFFF))
        top = lax.shift_right_logical(key ^ jnp.int32(INT_MIN), 32 - FIELD_BITS)
        for i in range(w):
            key_ref[j0 + i] = key[i * t:(i + 1) * t]
            pack_ref[j0 + i] = (jnp.left_shift(top[i * t:i * t + t // 2], 16) | top[i * t + t // 2:(i + 1) * t]
                                | jnp.int32(FIELD_GUARDS))
            lead = [lax.shift_right_logical(top[i * t + q * (t // 4):i * t + (q + 1) * (t // 4)], FIELD_BITS - BYTE_BITS)
                    for q in range(4)]
            pack4_ref[j0 + i] = (jnp.left_shift(lead[0], 24) | jnp.left_shift(lead[1], 16) | jnp.left_shift(lead[2], 8)
                                 | lead[3] | jnp.int32(BYTE_GUARDS))

    _key_chunks(n, score_chunk, CHUNK_TILES)

    def tree_sum(parts):
        while len(parts) > 1:
            parts = [parts[i] + parts[i + 1] for i in range(0, len(parts), 2)]
        return parts[0]

    def count(pred):
        def body(j, acc):
            c = jnp.where(pred(key_ref[j]), 1.0, 0.0)
            return acc + tree_sum([c[r * SUBLANES:(r + 1) * SUBLANES, :] for r in range(t // SUBLANES)])
        acc = lax.fori_loop(0, n + 1, body, jnp.zeros((SUBLANES, t), F32))
        return jnp.sum(acc, axis=0, keepdims=True)

    def count_top(cand):
        both = jnp.left_shift(cand, 16) | cand

        def body(j, acc):
            z = pack_ref[j] - both
            c = lax.shift_right_logical(z, FIELD_BITS) & jnp.int32(0x00010001)
            return acc + tree_sum([c[r * SUBLANES:(r + 1) * SUBLANES, :] for r in range(t // 2 // SUBLANES)])
        acc = lax.fori_loop(0, n + 1, body, jnp.zeros((SUBLANES, t), I32))
        per_lane = lax.shift_right_logical(acc, 16) + (acc & jnp.int32(0xFFFF))
        return jnp.sum(per_lane.astype(F32), axis=0, keepdims=True)

    def count_lead(cand):
        four = cand * jnp.int32(0x01010101)
        pairs = jnp.int32(0x00FF00FF)

        def body(j, acc):
            even, odd = acc
            z = pack4_ref[j] - four
            c = lax.shift_right_logical(z, BYTE_BITS) & jnp.int32(0x01010101)
            c = tree_sum([c[r * SUBLANES:(r + 1) * SUBLANES, :] for r in range(t // 4 // SUBLANES)])
            return even + (c & pairs), odd + (lax.shift_right_logical(c, 8) & pairs)
        zeros = jnp.zeros((SUBLANES, t), I32)
        even, odd = lax.fori_loop(0, n + 1, body, (zeros, zeros))
        both = even + odd
        per_lane = lax.shift_right_logical(both, 16) + (both & jnp.int32(0xFFFF))
        return jnp.sum(per_lane.astype(F32), axis=0, keepdims=True)

    def bisect(n_bits, low_bit, count_fn, carry):
        def body(i, carry):
            u, n_ge, n_rej = carry
            cand = u | jnp.left_shift(jnp.int32(1), low_bit + n_bits - 1 - i)
            cnt = count_fn(cand)
            keep = cnt >= float(topk)
            return jnp.where(keep, cand, u), jnp.where(keep, cnt, n_ge), jnp.where(keep, n_rej, cnt)
        return lax.fori_loop(0, n_bits, body, carry)

    def count_full(cand):
        return count(lambda kt: kt >= (cand ^ jnp.int32(INT_MIN)))

    in_scope = ((n + 1) * t).astype(F32)
    zero_row = jnp.zeros((1, t), F32)
    lead, n_ge, n_rej = bisect(BYTE_BITS, 0, count_lead, (jnp.zeros((1, t), I32), zero_row + in_scope, zero_row))
    top, n_ge, n_rej = bisect(FIELD_BITS - BYTE_BITS, 0, count_top,
                              (jnp.left_shift(lead, FIELD_BITS - BYTE_BITS), n_ge, n_rej))
    u, n_ge, n_rej = bisect(32 - FIELD_BITS - LOW_BITS, LOW_BITS, count_full,
                            (jnp.left_shift(top, 32 - FIELD_BITS), n_ge, n_rej))

    prefix = u ^ jnp.int32(INT_MIN)
    low_mask = jnp.int32((1 << LOW_BITS) - 1)
    big = jnp.int32(1 << (LOW_BITS + 1))

    def tree(parts, op):
        while len(parts) > 1:
            parts = [op(parts[i], parts[i + 1]) for i in range(0, len(parts), 2)]
        return parts[0]

    def band_body(j, carry):
        mx, mn, sm = carry
        kt = key_ref[j]
        inside = lax.shift_right_logical(kt ^ prefix, LOW_BITS) == 0
        low = kt & low_mask
        rows = lambda a: [a[r * SUBLANES:(r + 1) * SUBLANES, :] for r in range(t // SUBLANES)]
        low_or_zero = rows(jnp.where(inside, low, 0))
        return (jnp.maximum(mx, tree(low_or_zero, jnp.maximum)),
                jnp.minimum(mn, tree(rows(jnp.where(inside, low, big)), jnp.minimum)),
                sm + tree(low_or_zero, jnp.add))

    mx, mn, sm = lax.fori_loop(0, n + 1, band_body, (jnp.zeros((SUBLANES, t), I32),
                                                       jnp.full((SUBLANES, t), 1 << (LOW_BITS + 1), I32),
                                                       jnp.zeros((SUBLANES, t), I32)))
    mx = jnp.max(mx.astype(F32), axis=0, keepdims=True)
    mn = jnp.min(mn.astype(F32), axis=0, keepdims=True)
    sm = jnp.sum(sm.astype(F32), axis=0, keepdims=True)
    short = n_ge < float(topk)
    in_band = n_ge - n_rej
    need = float(topk) - n_rej
    second = jnp.where(in_band > 2.5, sm - mx - mn, mn)
    thr_low = jnp.where(need < 1.5, mx, jnp.where(need < 2.5, second, mn))
    kept = (1.0 + jnp.where((in_band > 1.5) & (second >= thr_low), 1.0, 0.0)
            + jnp.where((in_band > 2.5) & (mn >= thr_low), 1.0, 0.0))
    resolved = jnp.min(jnp.where(short | (in_band < 3.5), 1.0, 0.0)) > 0.5

    @pl.when(resolved)
    def _():
        thr_ref[...] = jnp.where(short, prefix, prefix | thr_low.astype(I32))
        nge_ref[...] = jnp.where(short, n_ge, n_rej + kept)

    @pl.when(jnp.logical_not(resolved))
    def _():
        u_all, n_ge_all, _ = bisect(LOW_BITS, 0, count_full, (u, n_ge, n_rej))
        thr_ref[...] = u_all ^ jnp.int32(INT_MIN)
        nge_ref[...] = n_ge_all

    thr = thr_ref[...]
    has_ties = jnp.max(nge_ref[...]) > float(topk)

    _init_state(m_ref, l_ref, acc_ref)

    def attend(ties):
        if ties:
            carry_ref[...] = jnp.zeros(carry_ref.shape, F32)
            need = float(topk) - count(lambda kt: kt > thr)
            lower = (lax.broadcasted_iota(I32, (t, t), 1) <= lax.broadcasted_iota(I32, (t, t), 0)).astype(BF16)

        def scores(j0, buf):
            start = pl.multiple_of(j0 * t, t)
            ks = k_ref[0, pl.ds(start, CHUNK_TILES * t), :]
            for h in range(4):
                s_ref[buf, h] = _dot(ks[:, (h // 2) * LANES:(h // 2 + 1) * LANES], qzt_ref[h])

        def rest(j0, w, diag, buf):
            if ties:
                parts = []
                for i in range(w):
                    kt = key_ref[j0 + i]
                    eq = kt == thr
                    incl = jnp.dot(lower, jnp.where(eq, 1.0, 0.0).astype(BF16),
                                   preferred_element_type=F32) + carry_ref[...]
                    carry_ref[...] = incl[t - 1:t, :]
                    parts.append(jnp.where((kt > thr) | (eq & (incl <= need)), 1.0, 0.0))
                sel = (parts[0] if w == 1 else jnp.concatenate(parts, axis=0)) > 0.5
            else:
                kt = key_ref[j0] if w == 1 else jnp.concatenate([key_ref[j0 + i] for i in range(w)], axis=0)
                sel = kt >= thr
            if diag:
                sel = sel & _causal_chunk(w, t)
            for h in range(4):
                g, r = divmod(h, 2)
                p, alpha = _softmax_step(jnp.where(sel, s_ref[buf, h, 0:w * t, :], -jnp.inf), m_ref, l_ref, h)
                _accumulate(acc_ref, g, r, alpha,
                            [vt_ref[0, j0 + i, h * HEAD_DIM:(h + 1) * HEAD_DIM, :] for i in range(w)], p, t)

        _pipelined_key_chunks(n, scores, rest, CHUNK_TILES)

    @pl.when(has_ties)
    def _():
        attend(True)

    @pl.when(jnp.logical_not(has_ties))
    def _():
        attend(False)

    for g in range(2):
        o_ref[0, :, g * LANES:(g + 1) * LANES] = _normalised_t(acc_ref, g, l_ref, 2 * g, 2 * g + 1).T.astype(BF16)


def _dsa_call(qk, vt, iq, ik, iw, *, t, topk):
    b, s, _ = qk.shape
    nt = s // t
    return pl.pallas_call(
        functools.partial(_dsa_kernel, t=t, topk=topk),
        grid=(b, nt),
        in_specs=[pl.BlockSpec((1, IDX_HEADS, t, IDX_CAT), lambda bi, ni: (bi, 0, ni, 0)),
                  pl.BlockSpec((1, 1, IDX_HEADS, t), lambda bi, ni: (bi, ni, 0, 0)),
                  pl.BlockSpec((1, s, IDX_CAT), lambda bi, ni: (bi, 0, 0), pipeline_mode=RESIDENT),
                  pl.BlockSpec((1, t, GROUP), lambda bi, ni: (bi, ni, QK_BLK["a_q"])),
                  pl.BlockSpec((1, s, GROUP), lambda bi, ni: (bi, 0, QK_BLK["a_k"]), pipeline_mode=RESIDENT),
                  pl.BlockSpec((1, nt, GROUP, t), lambda bi, ni: (bi, 0, V_ROW_BLK["a_v"], 0), pipeline_mode=RESIDENT)],
        out_specs=pl.BlockSpec((1, t, GROUP), lambda bi, ni: (bi, ni, 0)),
        out_shape=jax.ShapeDtypeStruct((b, s, GROUP), BF16),
        scratch_shapes=[pltpu.VMEM((nt, t, t), I32), pltpu.VMEM((nt, t // 2, t), I32), pltpu.VMEM((nt, t // 4, t), I32),
                        pltpu.VMEM((4, 1, t), F32), pltpu.VMEM((4, 1, t), F32),
                        pltpu.VMEM((2, LANES, t), F32), pltpu.VMEM((1, t), F32),
                        pltpu.VMEM((IDX_HEADS, IDX_CAT, t), BF16), pltpu.VMEM((4, LANES, t), BF16),
                        pltpu.VMEM((2, 4, CHUNK_TILES * t, t), F32),
                        pltpu.VMEM((1, t), I32), pltpu.VMEM((1, t), F32)],
        compiler_params=_cparams(("arbitrary", "arbitrary")),
        name="dsa",
    )(iq, iw, ik, qk, qk, vt)


def _fox_kernel(q_ref, k0_ref, k1_ref, vt_ref, o_ref, m_ref, l_ref, acc_ref, qxt_ref, s_ref, *, t):
    n = pl.program_id(1)
    _init_state(m_ref, l_ref, acc_ref)
    lane = lax.broadcasted_iota(I32, (1, LANES), 1)
    for h in range(4):
        qz = jnp.where(_head_mask(h % 2), q_ref[0, :, (h // 2) * LANES:(h // 2 + 1) * LANES], jnp.zeros((), BF16))
        ones = jnp.where((lane == h) | (lane == 4 + h) | (lane == 8 + h), 1.0, 0.0).astype(BF16)
        qxt_ref[h] = _transpose_bf16(jnp.concatenate([qz, jnp.broadcast_to(ones, (t, LANES))], axis=1))
    k_refs = (k0_ref, k1_ref)

    def scores(j0, buf):
        start = pl.multiple_of(j0 * t, t)
        for h in range(4):
            s_ref[buf, h] = _dot(k_refs[h // 2][0, pl.ds(start, CHUNK_TILES * t), :], qxt_ref[h])

    def rest(j0, w, diag, buf):
        for h in range(4):
            g, r = divmod(h, 2)
            s = s_ref[buf, h, 0:w * t, :]
            if diag:
                s = jnp.where(_causal_chunk(w, t), s, -jnp.inf)
            p, alpha = _softmax_step(s, m_ref, l_ref, h)
            _accumulate(acc_ref, g, r, alpha,
                        [vt_ref[0, j0 + i, h * HEAD_DIM:(h + 1) * HEAD_DIM, :] for i in range(w)], p, t)

    _pipelined_key_chunks(n, scores, rest, CHUNK_TILES)
    for g in range(2):
        o_ref[0, :, g * LANES:(g + 1) * LANES] = _normalised_t(acc_ref, g, l_ref, 2 * g, 2 * g + 1).T.astype(BF16)


def _fox_call(qk, vt, *, t):
    b, s, _ = qk.shape
    nt = s // t
    return pl.pallas_call(
        functools.partial(_fox_kernel, t=t),
        grid=(b, nt),
        in_specs=[pl.BlockSpec((1, t, GROUP), lambda bi, ni: (bi, ni, QK_BLK["b_q"])),
                  pl.BlockSpec((1, s, GROUP), lambda bi, ni: (bi, 0, QK_BLK["b_k"]), pipeline_mode=RESIDENT),
                  pl.BlockSpec((1, s, GROUP), lambda bi, ni: (bi, 0, QK_BLK["b_k"] + 1), pipeline_mode=RESIDENT),
                  pl.BlockSpec((1, nt, GROUP, t), lambda bi, ni: (bi, 0, V_ROW_BLK["b_v"], 0), pipeline_mode=RESIDENT)],
        out_specs=pl.BlockSpec((1, t, GROUP), lambda bi, ni: (bi, ni, 0)),
        out_shape=jax.ShapeDtypeStruct((b, s, GROUP), BF16),
        scratch_shapes=[pltpu.VMEM((4, 1, t), F32), pltpu.VMEM((4, 1, t), F32), pltpu.VMEM((2, LANES, t), F32),
                        pltpu.VMEM((4, GROUP, t), BF16), pltpu.VMEM((2, 4, CHUNK_TILES * t, t), F32)],
        compiler_params=_cparams(("arbitrary", "arbitrary")),
        name="fox",
    )(qk, qk, qk, vt)


def _diff_kernel(lam_ref, q_ref, k_ref, vt_ref, gs_ref, o_ref, m_ref, l_ref, acc_ref, qzt_ref, s_ref, *, t):
    n = pl.program_id(1)
    _init_state(m_ref, l_ref, acc_ref)
    lane = lax.broadcasted_iota(I32, (1, LANES), 1)
    for h in range(4):
        for mm in range(2):
            lo = (h % 2) * HEAD_DIM + mm * DIFF_DIM
            qzt_ref[2 * h + mm] = _transpose_bf16(
                jnp.where((lane >= lo) & (lane < lo + DIFF_DIM),
                          q_ref[0, :, (h // 2) * LANES:(h // 2 + 1) * LANES], jnp.zeros((), BF16)))

    def scores(j0, buf):
        start = pl.multiple_of(j0 * t, t)
        ks = k_ref[0, pl.ds(start, CHUNK_TILES_DIFF * t), :]
        for i in range(8):
            s_ref[buf, i] = _dot(ks[:, (i // 4) * LANES:(i // 4 + 1) * LANES], qzt_ref[i])

    def rest(j0, w, diag, buf):
        for h in range(4):
            g, r = divmod(h, 2)
            vts = [vt_ref[0, j0 + i, h * HEAD_DIM:(h + 1) * HEAD_DIM, :] for i in range(w)]
            for mm in range(2):
                s = s_ref[buf, 2 * h + mm, 0:w * t, :]
                if diag:
                    s = jnp.where(_causal_chunk(w, t), s, -jnp.inf)
                p, alpha = _softmax_step(s, m_ref, l_ref, 2 * h + mm)
                _accumulate(acc_ref, 2 * mm + g, r, alpha, vts, p, t)

    _pipelined_key_chunks(n, scores, rest, CHUNK_TILES_DIFF)
    lam = lam_ref[0]
    out_scale = lam_ref[1]
    for g in range(2):
        o1 = _normalised_t(acc_ref, g, l_ref, 4 * g, 4 * g + 2)
        o2 = _normalised_t(acc_ref, 2 + g, l_ref, 4 * g + 1, 4 * g + 3)
        o = o1 - lam * o2
        sq = o * o
        ms = jnp.concatenate(
            [jnp.broadcast_to(jnp.mean(sq[r * HEAD_DIM:(r + 1) * HEAD_DIM], axis=0, keepdims=True), (HEAD_DIM, t))
             for r in range(2)], axis=0)
        y = o * lax.rsqrt(ms + EPS) * gs_ref[...] * out_scale
        o_ref[0, :, g * LANES:(g + 1) * LANES] = y.T.astype(BF16)


def _diff_call(lam2, qk, vt, gs_full, *, t):
    b, s, _ = qk.shape
    nt = s // t
    return pl.pallas_call(
        functools.partial(_diff_kernel, t=t),
        grid=(b, nt),
        in_specs=[pl.BlockSpec(memory_space=pltpu.SMEM),
                  pl.BlockSpec((1, t, GROUP), lambda bi, ni: (bi, ni, QK_BLK["c_q"])),
                  pl.BlockSpec((1, s, GROUP), lambda bi, ni: (bi, 0, QK_BLK["c_k"]), pipeline_mode=RESIDENT),
                  pl.BlockSpec((1, nt, GROUP, t), lambda bi, ni: (bi, 0, V_ROW_BLK["c_v"], 0), pipeline_mode=RESIDENT),
                  pl.BlockSpec((LANES, t), lambda bi, ni: (0, 0))],
        out_specs=pl.BlockSpec((1, t, GROUP), lambda bi, ni: (bi, ni, 0)),
        out_shape=jax.ShapeDtypeStruct((b, s, GROUP), BF16),
        scratch_shapes=[pltpu.VMEM((8, 1, t), F32), pltpu.VMEM((8, 1, t), F32), pltpu.VMEM((4, LANES, t), F32),
                        pltpu.VMEM((8, LANES, t), BF16), pltpu.VMEM((2, 8, CHUNK_TILES_DIFF * t, t), F32)],
        compiler_params=_cparams(("arbitrary", "arbitrary")),
        name="diff",
    )(lam2, qk, qk, vt, gs_full)


def _swa_kernel(sink_ref, q_ref, kp_ref, kc_ref, vtp_ref, vtc_ref, o_ref, *, t):
    n = pl.program_id(1)
    prow = lax.broadcasted_iota(I32, (WINDOW, t), 0)
    pcol = lax.broadcasted_iota(I32, (WINDOW, t), 1)
    mask_prev = (prow > pcol) & (pcol + jnp.where(n > 0, 0, t) < WINDOW)
    crow = lax.broadcasted_iota(I32, (t, t), 0)
    ccol = lax.broadcasted_iota(I32, (t, t), 1)
    mask_cur = (crow <= ccol) & (crow > ccol - WINDOW)
    vtp = vtp_ref[0, 0][:, t - WINDOW:t]
    vtc = vtc_ref[0, 0]
    for g in range(2):
        outs = []
        for r in range(2):
            h = 2 * g + r
            qzt = _transpose_bf16(jnp.where(_head_mask(r), q_ref[0, :, g * LANES:(g + 1) * LANES],
                                            jnp.zeros((), BF16)))
            sp = jnp.where(mask_prev, _dot(kp_ref[0, :, g * LANES:(g + 1) * LANES], qzt), -jnp.inf)
            sc = jnp.where(mask_cur, _dot(kc_ref[0, :, g * LANES:(g + 1) * LANES], qzt), -jnp.inf)
            sink = sink_ref[h] * LOG2E
            m = jnp.maximum(jnp.maximum(jnp.max(sp, axis=0, keepdims=True),
                                        jnp.max(sc, axis=0, keepdims=True)), sink)
            pp = jnp.exp2(sp - m)
            pc = jnp.exp2(sc - m)
            den = jnp.sum(pp, axis=0, keepdims=True) + jnp.sum(pc, axis=0, keepdims=True) + jnp.exp2(sink - m)
            rows = slice(h * HEAD_DIM, (h + 1) * HEAD_DIM)
            o = (jnp.dot(vtp[rows, :], pp.astype(BF16), preferred_element_type=F32)
                 + jnp.dot(vtc[rows, :], pc.astype(BF16), preferred_element_type=F32))
            outs.append(o / den)
        o_ref[0, :, g * LANES:(g + 1) * LANES] = jnp.concatenate(outs, axis=0).T.astype(BF16)


def _swa_call(sinks, qk, vt, *, t):
    b, s, _ = qk.shape
    nt = s // t
    per = t // WINDOW
    return pl.pallas_call(
        functools.partial(_swa_kernel, t=t),
        grid=(b, nt),
        in_specs=[pl.BlockSpec(memory_space=pltpu.SMEM),
                  pl.BlockSpec((1, t, GROUP), lambda bi, ni: (bi, ni, QK_BLK["d_q"])),
                  pl.BlockSpec((1, WINDOW, GROUP), lambda bi, ni: (bi, jnp.maximum(ni * per - 1, 0), QK_BLK["d_k"])),
                  pl.BlockSpec((1, t, GROUP), lambda bi, ni: (bi, ni, QK_BLK["d_k"])),
                  pl.BlockSpec((1, 1, GROUP, t), lambda bi, ni: (bi, jnp.maximum(ni - 1, 0), V_ROW_BLK["d_v"], 0)),
                  pl.BlockSpec((1, 1, GROUP, t), lambda bi, ni: (bi, ni, V_ROW_BLK["d_v"], 0))],
        out_specs=pl.BlockSpec((1, t, GROUP), lambda bi, ni: (bi, ni, 0)),
        out_shape=jax.ShapeDtypeStruct((b, s, GROUP), BF16),
        compiler_params=_cparams(("arbitrary", "arbitrary")),
        name="swa",
    )(sinks, qk, qk, qk, vt, vt)


def _post_kernel(oa_ref, ob_ref, oc_ref, od_ref, x_ref, p_ref, wo_ref, gmix_ref, gpre_ref, wu_ref, wd_ref, gpost_ref,
                 wg_ref, wp_ref, y_ref, *, chunk):
    acc = jnp.zeros(x_ref.shape[1:], F32)
    for i, o_ref in enumerate((oa_ref, ob_ref, oc_ref, od_ref)):
        acc = acc + jnp.dot(o_ref[0], wo_ref[i * GROUP:(i + 1) * GROUP, :], preferred_element_type=F32)
    x = x_ref[0] + _rms(acc, gmix_ref[...])
    h = _rms(x, gpre_ref[...]).astype(BF16)
    acc = jnp.zeros(x.shape, F32)
    for c in range(D_FF // chunk):
        u = jnp.dot(h, wu_ref[:, c * chunk:(c + 1) * chunk], preferred_element_type=F32)
        u = jnp.square(jnp.maximum(u, 0.0)).astype(BF16)
        acc = acc + jnp.dot(u, wd_ref[c * chunk:(c + 1) * chunk, :], preferred_element_type=F32)
    x = x + _rms(acc, gpost_ref[...])
    gate = jax.nn.sigmoid(jnp.dot(x.astype(BF16), wg_ref[...], preferred_element_type=F32))
    emb = jnp.dot(p_ref[0].astype(BF16), wp_ref[...], preferred_element_type=F32)
    y_ref[0] = x + gate * emb


def _post_call(tok_inputs, full_inputs, *, tm):
    b, s, d = tok_inputs[4].shape
    tok = lambda a: pl.BlockSpec((1, tm, a.shape[2]), lambda bi, ti: (bi, ti, 0))
    full = lambda a: pl.BlockSpec(a.shape, lambda bi, ti: (0,) * a.ndim, pipeline_mode=pl.Buffered(1))
    return pl.pallas_call(
        functools.partial(_post_kernel, chunk=1024),
        grid=(b, s // tm),
        in_specs=[tok(a) for a in tok_inputs] + [full(a) for a in full_inputs],
        out_specs=pl.BlockSpec((1, tm, d), lambda bi, ti: (bi, ti, 0)),
        out_shape=jax.ShapeDtypeStruct((b, s, d), F32),
        compiler_params=_cparams(("arbitrary", "arbitrary")),
        name="post",
    )(*tok_inputs, *full_inputs)


def _dup_kv(w):
    return jnp.concatenate([w[:, :HEAD_DIM], w[:, :HEAD_DIM], w[:, HEAD_DIM:], w[:, HEAD_DIM:]], axis=1)


def _prep_w_in(w):
    sec = lambda n: w[:, _SEC[n][0]:_SEC[n][0] + _SEC[n][1]]
    cols = [_dup_kv(sec(n)) if n in ("d_k", "d_v") else sec(n) for n in QK_GROUPS + V_GROUPS]
    w_main = jnp.concatenate(cols, axis=1).astype(BF16)
    pad = jnp.zeros((w.shape[0], LANES - IDX_DIM - IDX_HEADS - 4), w.dtype)
    w_idx = jnp.concatenate([sec("iq"), sec("ik"), sec("iw"), sec("b_f"), pad], axis=1)
    w_hi = w_idx.astype(BF16)
    w_lo = (w_idx - w_hi.astype(F32)).astype(BF16)
    return w_main, jnp.concatenate([w_hi, w_hi, w_lo], axis=0)


def _rope_tables(positions):
    pos = positions.astype(F32)[..., None]
    lane = jnp.arange(LANES)

    def tabs(dim):
        half = dim // 2
        inv_freq = ROPE_THETA ** (-jnp.arange(half, dtype=F32) / half)
        ang = pos * inv_freq
        sign = jnp.where((lane % dim) < half, -1.0, 1.0).astype(F32)
        reps = (1, 1, LANES // half)
        return jnp.tile(jnp.cos(ang), reps), jnp.tile(jnp.sin(ang), reps) * sign

    c64, s64 = tabs(HEAD_DIM)
    c32, s32 = tabs(DIFF_DIM)
    return c64, s64, c32, s32


def kernel(x, p, positions, w_in, b_forget, lambda_q1, lambda_k1, lambda_q2, lambda_k2, diff_subln, sinks,
           w_out, norm_pre_mix, norm_post_mix, norm_pre_mlp, norm_post_mlp, w_mlp_up, w_mlp_down,
           w_ple_proj, w_ple_gate):
    b, s, d = x.shape
    depth = w_in.shape[0]
    t = min(256, s)
    assert s % (CHUNK_TILES * t) == 0, "sequence length must be a multiple of the attention key chunk"
    topk = min(TOPK_MAX, s // 4)
    tabs = _rope_tables(positions)
    row = lambda v: v.reshape(1, -1).astype(F32)

    for i in range(depth):
        lam_init = 0.8 - 0.6 * math.exp(-0.3 * i)
        w_main, w_idx = _prep_w_in(w_in[i])
        bf_row = jnp.zeros((1, LANES), F32).at[0, MISC_F:MISC_F + 4].set(b_forget[i])
        qk, vt, iq, ik, iw = _proj_call(x, row(norm_pre_mix[i]), w_main, w_idx, tabs, bf_row, tm=t)

        o_a = _dsa_call(qk, vt, iq, ik, iw, t=t, topk=topk)
        o_b = _fox_call(qk, vt, t=t)
        lam = (jnp.exp(jnp.sum(lambda_q1[i] * lambda_k1[i])) - jnp.exp(jnp.sum(lambda_q2[i] * lambda_k2[i]))
               + lam_init)
        lam2 = jnp.stack([lam, jnp.asarray(1.0 - lam_init, F32)]).astype(F32)
        gs_full = jnp.broadcast_to(jnp.concatenate([diff_subln[i], diff_subln[i]]).astype(F32)[:, None], (LANES, t))
        o_c = _diff_call(lam2, qk, vt, gs_full, t=t)
        o_d = _swa_call(sinks[i].astype(F32), qk, vt, t=t)

        x = _post_call([o_a, o_b, o_c, o_d, x, p[i]],
                       [w_out[i].astype(BF16), row(norm_post_mix[i]), row(norm_pre_mlp[i]),
                        w_mlp_up[i].astype(BF16), w_mlp_down[i].astype(BF16), row(norm_post_mlp[i]),
                        w_ple_gate[i].astype(BF16), w_ple_proj[i].astype(BF16)], tm=min(POST_TILE, s))
    return x
```

```python
import functools
import math

import jax
import jax.numpy as jnp
from jax import lax
from jax.experimental import pallas as pl
from jax.experimental.pallas import tpu as pltpu

F32, BF16, I32 = jnp.float32, jnp.bfloat16, jnp.int32

D_MODEL = 1024
HEAD_DIM = 64
DIFF_DIM = 32
IDX_HEADS = 8
IDX_DIM = 64
TOPK_MAX = 256
WINDOW = 128
D_FF = 4 * D_MODEL
ROPE_THETA = 10000.0
EPS = 1e-6
NEG_INF = -1e30
LANES = 128
SUBLANES = 8
GROUP = 256
INT_MIN = -2147483648
FIELD_BITS = 15
FIELD_GUARDS = -2147450880
LOW_BITS = 10
BYTE_BITS = 7
BYTE_GUARDS = -2139062144

_SEC = {}
_o = 0
for _name, _w in (("a_q", 256), ("a_k", 256), ("a_v", 256), ("iq", 512), ("ik", 64), ("iw", 8),
                  ("b_q", 256), ("b_k", 256), ("b_v", 256), ("b_f", 4),
                  ("c_q", 256), ("c_k", 256), ("c_v", 256), ("d_q", 256), ("d_k", 128), ("d_v", 128)):
    _SEC[_name] = (_o, _w)
    _o += _w

QK_GROUPS = ("a_q", "a_k", "b_q", "b_k", "c_q", "c_k", "d_q", "d_k")
V_GROUPS = ("a_v", "b_v", "c_v", "d_v")
MAIN_W = GROUP * (len(QK_GROUPS) + len(V_GROUPS))
V_BASE = GROUP * len(QK_GROUPS)
QK_BLK = {"a_q": 0, "a_k": 1, "b_q": 2, "b_k": 3, "c_q": 5, "c_k": 6, "d_q": 7, "d_k": 8}
QK_W = GROUP * 9
V_ROW_BLK = {n: i for i, n in enumerate(V_GROUPS)}
ROPE64_GROUPS = ("a_q", "a_k", "d_q", "d_k")
ROPE32_GROUPS = ("c_q", "c_k")
LOG2E = math.log2(math.e)
Q_SCALE = {"a_q": HEAD_DIM ** -0.5 * LOG2E, "b_q": HEAD_DIM ** -0.5 * LOG2E, "c_q": DIFF_DIM ** -0.5 * LOG2E,
           "d_q": HEAD_DIM ** -0.5 * LOG2E}
IDX_W = 512 + LANES
MISC_IW = 64
MISC_F = 72
IDX_CAT = 256

CHUNK_TILES = 4
CHUNK_TILES_DIFF = 2
POST_TILE = 1024
VMEM_LIMIT = 56 * 1024 * 1024
RESIDENT = pl.Buffered(1)


def _cparams(sem):
    return pltpu.CompilerParams(dimension_semantics=sem, vmem_limit_bytes=VMEM_LIMIT)


def _rms(x, g):
    return x * lax.rsqrt(jnp.mean(x * x, axis=-1, keepdims=True) + EPS) * g


def _dot(a, b):
    return jnp.dot(a, b, preferred_element_type=F32)


def _transpose_bf16(a):
    return a.astype(F32).T.astype(BF16)


def _bf16_part(x):
    return x.astype(BF16).astype(F32)


def _split2(x):
    hi = _bf16_part(x)
    return hi, _bf16_part(x - hi)


def _split3(x):
    hi = _bf16_part(x)
    mid = _bf16_part(x - hi)
    return hi, mid, _bf16_part(x - hi - mid)


def _rope_chunk(xc, cos, sin_signed, half, lane):
    fwd = pltpu.roll(xc, LANES - half, axis=1)
    bwd = pltpu.roll(xc, half, axis=1)
    partner = jnp.where((lane % (2 * half)) < half, fwd, bwd)
    return xc * cos + partner * sin_signed


def _proj_kernel(x_ref, g_ref, wm_ref, wi_ref, c64_ref, s64_ref, c32_ref, s32_ref, bf_ref,
                 qk_ref, vt_ref, iq_ref, ik_ref, iw_ref, carry_ref, *, tm):
    t = pl.program_id(1)
    x = x_ref[0]
    h = _rms(x, g_ref[...])
    pm = jnp.dot(h.astype(BF16), wm_ref[...], preferred_element_type=F32)
    h_hi, h_lo = _split2(h)
    pi = jnp.dot(jnp.concatenate([h_hi, h_lo, h_hi], axis=1).astype(BF16), wi_ref[...],
                 preferred_element_type=F32)
    lane = lax.broadcasted_iota(I32, (1, LANES), 1)
    lo_half = lane < HEAD_DIM
    c64, s64, c32, s32 = c64_ref[0], s64_ref[0], c32_ref[0], s32_ref[0]

    misc = pi[:, 512:512 + LANES]
    z = misc + bf_ref[...]
    logf = jnp.minimum(z, 0.0) - jnp.log1p(jnp.exp(-jnp.abs(z)))
    logf = jnp.where((lane >= MISC_F) & (lane < MISC_F + 4), logf, 0.0)
    tri = (lax.broadcasted_iota(I32, (tm, tm), 1) <= lax.broadcasted_iota(I32, (tm, tm), 0)).astype(BF16)

    @pl.when(t == 0)
    def _():
        carry_ref[...] = jnp.zeros_like(carry_ref)

    pieces = jnp.dot(tri, jnp.concatenate(_split3(logf), axis=1).astype(BF16), preferred_element_type=F32)
    cum = pieces[:, 0:LANES] + pieces[:, LANES:2 * LANES] + pieces[:, 2 * LANES:3 * LANES] + carry_ref[...]
    carry_ref[...] = cum[tm - 1:tm, :]
    nhi, nmid, nlo = _split3(-LOG2E * cum)
    gate_bias = jnp.where(lane < 4, pltpu.roll(nhi, LANES - MISC_F, axis=1),
                          jnp.where(lane < 8, pltpu.roll(nmid, LANES - MISC_F + 4, axis=1),
                                    jnp.where(lane < 12, pltpu.roll(nlo, LANES - MISC_F + 8, axis=1),
                                              jnp.zeros_like(nlo))))

    for gi, name in enumerate(QK_GROUPS):
        for c in range(GROUP // LANES):
            lo = gi * GROUP + c * LANES
            v = pm[:, lo:lo + LANES]
            if name in ROPE64_GROUPS:
                v = _rope_chunk(v, c64, s64, HEAD_DIM // 2, lane)
            elif name in ROPE32_GROUPS:
                v = _rope_chunk(v, c32, s32, DIFF_DIM // 2, lane)
            if name in Q_SCALE:
                v = v * Q_SCALE[name]
            if name == "b_k":
                out = (QK_BLK[name] + c) * GROUP
                qk_ref[0, :, out:out + LANES] = v.astype(BF16)
                qk_ref[0, :, out + LANES:out + GROUP] = gate_bias.astype(BF16)
            else:
                out = QK_BLK[name] * GROUP + c * LANES
                qk_ref[0, :, out:out + LANES] = v.astype(BF16)

    vt_ref[0, 0] = pm[:, V_BASE:V_BASE + len(V_GROUPS) * GROUP].T.astype(BF16)

    def cat_q(q):
        hi, lo = _split2(q)
        return jnp.where(lo_half, hi, pltpu.roll(lo, HEAD_DIM, axis=1)), hi

    for c in range(512 // LANES):
        v = _rope_chunk(pi[:, c * LANES:(c + 1) * LANES], c64, s64, IDX_DIM // 2, lane)
        for r, q in enumerate((jnp.where(lo_half, v, 0.0), jnp.where(lo_half, pltpu.roll(v, HEAD_DIM, axis=1), 0.0))):
            a, b2 = cat_q(q)
            iq_ref[0, 2 * c + r, :, 0:LANES] = a.astype(BF16)
            iq_ref[0, 2 * c + r, :, LANES:IDX_CAT] = b2.astype(BF16)
    ik = jnp.where(lo_half, _rope_chunk(misc, c64, s64, IDX_DIM // 2, lane), 0.0)
    khi, klo = _split2(ik)
    ik_ref[0, :, 0:LANES] = jnp.where(lo_half, khi, pltpu.roll(khi, HEAD_DIM, axis=1)).astype(BF16)
    ik_ref[0, :, LANES:IDX_CAT] = klo.astype(BF16)
    iw_t = (misc * (IDX_HEADS ** -0.5 * IDX_DIM ** -0.5)).T
    iw_ref[0, 0] = iw_t[MISC_IW:MISC_IW + IDX_HEADS, :]


def _proj_call(x, g, w_main, w_idx, tabs, bf_row, *, tm):
    b, s, d = x.shape
    tok = lambda w: pl.BlockSpec((1, tm, w), lambda bi, ti: (bi, ti, 0))
    full = lambda a: pl.BlockSpec(a.shape, lambda bi, ti: (0,) * a.ndim)
    return pl.pallas_call(
        functools.partial(_proj_kernel, tm=tm),
        grid=(b, s // tm),
        in_specs=[tok(d), full(g), full(w_main), full(w_idx), tok(LANES), tok(LANES), tok(LANES), tok(LANES),
                  full(bf_row)],
        out_specs=[tok(QK_W),
                   pl.BlockSpec((1, 1, len(V_GROUPS) * GROUP, tm), lambda bi, ti: (bi, ti, 0, 0)),
                   pl.BlockSpec((1, IDX_HEADS, tm, IDX_CAT), lambda bi, ti: (bi, 0, ti, 0)),
                   tok(IDX_CAT),
                   pl.BlockSpec((1, 1, IDX_HEADS, tm), lambda bi, ti: (bi, ti, 0, 0))],
        out_shape=[jax.ShapeDtypeStruct((b, s, QK_W), BF16),
                   jax.ShapeDtypeStruct((b, s // tm, len(V_GROUPS) * GROUP, tm), BF16),
                   jax.ShapeDtypeStruct((b, IDX_HEADS, s, IDX_CAT), BF16),
                   jax.ShapeDtypeStruct((b, s, IDX_CAT), BF16),
                   jax.ShapeDtypeStruct((b, s // tm, IDX_HEADS, tm), F32)],
        scratch_shapes=[pltpu.VMEM((1, LANES), F32)],
        compiler_params=_cparams(("arbitrary", "arbitrary")),
        name="proj",
    )(x, g, w_main, w_idx, *tabs, bf_row)


def _softmax_step(s, m_ref, l_ref, idx):
    m_old = m_ref[idx]
    m_new = jnp.maximum(m_old, jnp.max(s, axis=0, keepdims=True))
    alpha = jnp.exp2(m_old - m_new)
    p = jnp.exp2(s - m_new)
    l_ref[idx] = alpha * l_ref[idx] + jnp.sum(p, axis=0, keepdims=True)
    m_ref[idx] = m_new
    return p, alpha


def _init_state(m_ref, l_ref, acc_ref):
    m_ref[...] = jnp.full(m_ref.shape, NEG_INF, F32)
    l_ref[...] = jnp.zeros(l_ref.shape, F32)
    acc_ref[...] = jnp.zeros(acc_ref.shape, F32)


def _causal_chunk(w, t):
    return lax.broadcasted_iota(I32, (w * t, t), 0) <= lax.broadcasted_iota(I32, (w * t, t), 1) + (w - 1) * t


def _head_mask(r):
    lane = lax.broadcasted_iota(I32, (1, LANES), 1)
    return (lane >= HEAD_DIM) if r else (lane < HEAD_DIM)


def _key_chunks(n, chunk_fn, width):
    def _body(c, carry):
        chunk_fn(width * c, width, False)
        return carry

    n_full = n // width
    lax.fori_loop(0, n_full, _body, 0)
    rest = n - n_full * width
    for k in range(width):
        @pl.when(rest == k)
        def _(k=k):
            chunk_fn(n - k, k + 1, True)


def _pipelined_key_chunks(n, scores_fn, rest_fn, width):
    n_full = n // width
    rest = n - n_full * width
    odd = n_full & 1

    @pl.when(odd == 1)
    def _():
        scores_fn(0, 1)
        scores_fn(width, 0)
        rest_fn(0, width, False, 1)

    @pl.when(odd == 0)
    def _():
        scores_fn(0, 0)

    def _body(pair, carry):
        j0 = (odd + 2 * pair) * width
        scores_fn(j0 + width, 1)
        rest_fn(j0, width, False, 0)
        scores_fn(j0 + 2 * width, 0)
        rest_fn(j0 + width, width, False, 1)
        return carry

    lax.fori_loop(0, (n_full - odd) >> 1, _body, 0)
    for k in range(width):
        @pl.when(rest == k)
        def _(k=k):
            rest_fn(n - k, k + 1, True, 0)


def _accumulate(acc_ref, a, r, alpha, vt_tiles, p, t):
    rows = slice(r * HEAD_DIM, (r + 1) * HEAD_DIM)
    pb = p.astype(BF16)
    pv = jnp.dot(vt_tiles[0], pb[0:t], preferred_element_type=F32)
    for i in range(1, len(vt_tiles)):
        pv = pv + jnp.dot(vt_tiles[i], pb[i * t:(i + 1) * t], preferred_element_type=F32)
    acc_ref[a, rows, :] = acc_ref[a, rows, :] * alpha + pv


def _normalised_t(acc_ref, a, l_ref, idx0, idx1):
    return jnp.concatenate([acc_ref[a, 0:HEAD_DIM, :] / l_ref[idx0],
                            acc_ref[a, HEAD_DIM:LANES, :] / l_ref[idx1]], axis=0)


def _dsa_kernel(iq_ref, iw_ref, ik_ref, q_ref, k_ref, vt_ref, o_ref,
                key_ref, pack_ref, pack4_ref, m_ref, l_ref, acc_ref, carry_ref, iqt_ref, qzt_ref, s_ref, thr_ref, nge_ref,
                *, t, topk):
    n = pl.program_id(1)
    for h in range(IDX_HEADS):
        iqt_ref[h] = _transpose_bf16(iq_ref[0, h])
    for h in range(4):
        qzt_ref[h] = _transpose_bf16(jnp.where(_head_mask(h % 2), q_ref[0, :, (h // 2) * LANES:(h // 2 + 1) * LANES],
                                               jnp.zeros((), BF16)))

    def score_chunk(j0, w, diag):
        start = pl.multiple_of(j0 * t, t)
        ik = ik_ref[0, pl.ds(start, w * t), :]
        sc = jnp.zeros((w * t, t), F32)
        for h in range(IDX_HEADS):
            d = _dot(ik, iqt_ref[h])
            sc = sc + iw_ref[0, 0, h:h + 1, :] * jnp.maximum(d, 0.0)
        if diag:
            sc = jnp.where(_causal_chunk(w, t), sc, NEG_INF)
        sc = jnp.where(sc == 0.0, 0.0, sc)
        bits = lax.bitcast_convert_type(sc, I32)
        key = bits ^ ((bits >> 31) & jnp.int32(0x7FFFFFFF))
        top = lax.shift_right_logical(key ^ jnp.int32(INT_MIN), 32 - FIELD_BITS)
        for i in range(w):
            key_ref[j0 + i] = key[i * t:(i + 1) * t]
            pack_ref[j0 + i] = (jnp.left_shift(top[i * t:i * t + t // 2], 16) | top[i * t + t // 2:(i + 1) * t]
                                | jnp.int32(FIELD_GUARDS))
            lead = [lax.shift_right_logical(top[i * t + q * (t // 4):i * t + (q + 1) * (t // 4)], FIELD_BITS - BYTE_BITS)
                    for q in range(4)]
            pack4_ref[j0 + i] = (jnp.left_shift(lead[0], 24) | jnp.left_shift(lead[1], 16) | jnp.left_shift(lead[2], 8)
                                 | lead[3] | jnp.int32(BYTE_GUARDS))

    _key_chunks(n, score_chunk, CHUNK_TILES)

    def tree_sum(parts):
        while len(parts) > 1:
            parts = [parts[i] + parts[i + 1] for i in range(0, len(parts), 2)]
        return parts[0]

    def count(pred):
        def body(j, acc):
            c = jnp.where(pred(key_ref[j]), 1.0, 0.0)
            return acc + tree_sum([c[r * SUBLANES:(r + 1) * SUBLANES, :] for r in range(t // SUBLANES)])
        acc = lax.fori_loop(0, n + 1, body, jnp.zeros((SUBLANES, t), F32))
        return jnp.sum(acc, axis=0, keepdims=True)

    def count_top(cand):
        both = jnp.left_shift(cand, 16) | cand

        def body(j, acc):
            z = pack_ref[j] - both
            c = lax.shift_right_logical(z, FIELD_BITS) & jnp.int32(0x00010001)
            return acc + tree_sum([c[r * SUBLANES:(r + 1) * SUBLANES, :] for r in range(t // 2 // SUBLANES)])
        acc = lax.fori_loop(0, n + 1, body, jnp.zeros((SUBLANES, t), I32))
        per_lane = lax.shift_right_logical(acc, 16) + (acc & jnp.int32(0xFFFF))
        return jnp.sum(per_lane.astype(F32), axis=0, keepdims=True)

    def count_lead(cand):
        four = cand * jnp.int32(0x01010101)
        pairs = jnp.int32(0x00FF00FF)

        def body(j, acc):
            even, odd = acc
            z = pack4_ref[j] - four
            c = lax.shift_right_logical(z, BYTE_BITS) & jnp.int32(0x01010101)
            c = tree_sum([c[r * SUBLANES:(r + 1) * SUBLANES, :] for r in range(t // 4 // SUBLANES)])
            return even + (c & pairs), odd + (lax.shift_right_logical(c, 8) & pairs)
        zeros = jnp.zeros((SUBLANES, t), I32)
        even, odd = lax.fori_loop(0, n + 1, body, (zeros, zeros))
        both = even + odd
        per_lane = lax.shift_right_logical(both, 16) + (both & jnp.int32(0xFFFF))
        return jnp.sum(per_lane.astype(F32), axis=0, keepdims=True)

    def bisect(n_bits, low_bit, count_fn, carry):
        def body(i, carry):
            u, n_ge, n_rej = carry
            cand = u | jnp.left_shift(jnp.int32(1), low_bit + n_bits - 1 - i)
            cnt = count_fn(cand)
            keep = cnt >= float(topk)
            return jnp.where(keep, cand, u), jnp.where(keep, cnt, n_ge), jnp.where(keep, n_rej, cnt)
        return lax.fori_loop(0, n_bits, body, carry)

    def count_full(cand):
        return count(lambda kt: kt >= (cand ^ jnp.int32(INT_MIN)))

    in_scope = ((n + 1) * t).astype(F32)
    zero_row = jnp.zeros((1, t), F32)
    lead, n_ge, n_rej = bisect(BYTE_BITS, 0, count_lead, (jnp.zeros((1, t), I32), zero_row + in_scope, zero_row))
    top, n_ge, n_rej = bisect(FIELD_BITS - BYTE_BITS, 0, count_top,
                              (jnp.left_shift(lead, FIELD_BITS - BYTE_BITS), n_ge, n_rej))
    u, n_ge, n_rej = bisect(32 - FIELD_BITS - LOW_BITS, LOW_BITS, count_full,
                            (jnp.left_shift(top, 32 - FIELD_BITS), n_ge, n_rej))

    prefix = u ^ jnp.int32(INT_MIN)
    low_mask = jnp.int32((1 << LOW_BITS) - 1)
    big = jnp.int32(1 << (LOW_BITS + 1))

    def tree(parts, op):
        while len(parts) > 1:
            parts = [op(parts[i], parts[i + 1]) for i in range(0, len(parts), 2)]
        return parts[0]

    def band_body(j, carry):
        mx, mn, sm = carry
        kt = key_ref[j]
        inside = lax.shift_right_logical(kt ^ prefix, LOW_BITS) == 0
        low = kt & low_mask
        rows = lambda a: [a[r * SUBLANES:(r + 1) * SUBLANES, :] for r in range(t // SUBLANES)]
        low_or_zero = rows(jnp.where(inside, low, 0))
        return (jnp.maximum(mx, tree(low_or_zero, jnp.maximum)),
                jnp.minimum(mn, tree(rows(jnp.where(inside, low, big)), jnp.minimum)),
                sm + tree(low_or_zero, jnp.add))

    mx, mn, sm = lax.fori_loop(0, n + 1, band_body, (jnp.zeros((SUBLANES, t), I32),
                                                       jnp.full((SUBLANES, t), 1 << (LOW_BITS + 1), I32),
                                                       jnp.zeros((SUBLANES, t), I32)))
    mx = jnp.max(mx.astype(F32), axis=0, keepdims=True)
    mn = jnp.min(mn.astype(F32), axis=0, keepdims=True)
    sm = jnp.sum(sm.astype(F32), axis=0, keepdims=True)
    short = n_ge < float(topk)
    in_band = n_ge - n_rej
    need = float(topk) - n_rej
    second = jnp.where(in_band > 2.5, sm - mx - mn, mn)
    thr_low = jnp.where(need < 1.5, mx, jnp.where(need < 2.5, second, mn))
    kept = (1.0 + jnp.where((in_band > 1.5) & (second >= thr_low), 1.0, 0.0)
            + jnp.where((in_band > 2.5) & (mn >= thr_low), 1.0, 0.0))
    resolved = jnp.min(jnp.where(short | (in_band < 3.5), 1.0, 0.0)) > 0.5

    @pl.when(resolved)
    def _():
        thr_ref[...] = jnp.where(short, prefix, prefix | thr_low.astype(I32))
        nge_ref[...] = jnp.where(short, n_ge, n_rej + kept)

    @pl.when(jnp.logical_not(resolved))
    def _():
        u_all, n_ge_all, _ = bisect(LOW_BITS, 0, count_full, (u, n_ge, n_rej))
        thr_ref[...] = u_all ^ jnp.int32(INT_MIN)
        nge_ref[...] = n_ge_all

    thr = thr_ref[...]
    has_ties = jnp.max(nge_ref[...]) > float(topk)

    _init_state(m_ref, l_ref, acc_ref)

    def attend(ties):
        if ties:
            carry_ref[...] = jnp.zeros(carry_ref.shape, F32)
            need = float(topk) - count(lambda kt: kt > thr)
            lower = (lax.broadcasted_iota(I32, (t, t), 1) <= lax.broadcasted_iota(I32, (t, t), 0)).astype(BF16)

        def scores(j0, buf):
            start = pl.multiple_of(j0 * t, t)
            ks = k_ref[0, pl.ds(start, CHUNK_TILES * t), :]
            for h in range(4):
                s_ref[buf, h] = _dot(ks[:, (h // 2) * LANES:(h // 2 + 1) * LANES], qzt_ref[h])

        def rest(j0, w, diag, buf):
            if ties:
                parts = []
                for i in range(w):
                    kt = key_ref[j0 + i]
                    eq = kt == thr
                    incl = jnp.dot(lower, jnp.where(eq, 1.0, 0.0).astype(BF16),
                                   preferred_element_type=F32) + carry_ref[...]
                    carry_ref[...] = incl[t - 1:t, :]
                    parts.append(jnp.where((kt > thr) | (eq & (incl <= need)), 1.0, 0.0))
                sel = (parts[0] if w == 1 else jnp.concatenate(parts, axis=0)) > 0.5
            else:
                kt = key_ref[j0] if w == 1 else jnp.concatenate([key_ref[j0 + i] for i in range(w)], axis=0)
                sel = kt >= thr
            if diag:
                sel = sel & _causal_chunk(w, t)
            for h in range(4):
                g, r = divmod(h, 2)
                p, alpha = _softmax_step(jnp.where(sel, s_ref[buf, h, 0:w * t, :], -jnp.inf), m_ref, l_ref, h)
                _accumulate(acc_ref, g, r, alpha,
                            [vt_ref[0, j0 + i, h * HEAD_DIM:(h + 1) * HEAD_DIM, :] for i in range(w)], p, t)

        _pipelined_key_chunks(n, scores, rest, CHUNK_TILES)

    @pl.when(has_ties)
    def _():
        attend(True)

    @pl.when(jnp.logical_not(has_ties))
    def _():
        attend(False)

    for g in range(2):
        o_ref[0, :, g * LANES:(g + 1) * LANES] = _normalised_t(acc_ref, g, l_ref, 2 * g, 2 * g + 1).T.astype(BF16)


def _dsa_call(qk, vt, iq, ik, iw, *, t, topk):
    b, s, _ = qk.shape
    nt = s // t
    return pl.pallas_call(
        functools.partial(_dsa_kernel, t=t, topk=topk),
        grid=(b, nt),
        in_specs=[pl.BlockSpec((1, IDX_HEADS, t, IDX_CAT), lambda bi, ni: (bi, 0, ni, 0)),
                  pl.BlockSpec((1, 1, IDX_HEADS, t), lambda bi, ni: (bi, ni, 0, 0)),
                  pl.BlockSpec((1, s, IDX_CAT), lambda bi, ni: (bi, 0, 0), pipeline_mode=RESIDENT),
                  pl.BlockSpec((1, t, GROUP), lambda bi, ni: (bi, ni, QK_BLK["a_q"])),
                  pl.BlockSpec((1, s, GROUP), lambda bi, ni: (bi, 0, QK_BLK["a_k"]), pipeline_mode=RESIDENT),
                  pl.BlockSpec((1, nt, GROUP, t), lambda bi, ni: (bi, 0, V_ROW_BLK["a_v"], 0), pipeline_mode=RESIDENT)],
        out_specs=pl.BlockSpec((1, t, GROUP), lambda bi, ni: (bi, ni, 0)),
        out_shape=jax.ShapeDtypeStruct((b, s, GROUP), BF16),
        scratch_shapes=[pltpu.VMEM((nt, t, t), I32), pltpu.VMEM((nt, t // 2, t), I32), pltpu.VMEM((nt, t // 4, t), I32),
                        pltpu.VMEM((4, 1, t), F32), pltpu.VMEM((4, 1, t), F32),
                        pltpu.VMEM((2, LANES, t), F32), pltpu.VMEM((1, t), F32),
                        pltpu.VMEM((IDX_HEADS, IDX_CAT, t), BF16), pltpu.VMEM((4, LANES, t), BF16),
                        pltpu.VMEM((2, 4, CHUNK_TILES * t, t), F32),
                        pltpu.VMEM((1, t), I32), pltpu.VMEM((1, t), F32)],
        compiler_params=_cparams(("arbitrary", "arbitrary")),
        name="dsa",
    )(iq, iw, ik, qk, qk, vt)


def _fox_kernel(q_ref, k0_ref, k1_ref, vt_ref, o_ref, m_ref, l_ref, acc_ref, qxt_ref, s_ref, *, t):
    n = pl.program_id(1)
    _init_state(m_ref, l_ref, acc_ref)
    lane = lax.broadcasted_iota(I32, (1, LANES), 1)
    for h in range(4):
        qz = jnp.where(_head_mask(h % 2), q_ref[0, :, (h // 2) * LANES:(h // 2 + 1) * LANES], jnp.zeros((), BF16))
        ones = jnp.where((lane == h) | (lane == 4 + h) | (lane == 8 + h), 1.0, 0.0).astype(BF16)
        qxt_ref[h] = _transpose_bf16(jnp.concatenate([qz, jnp.broadcast_to(ones, (t, LANES))], axis=1))
    k_refs = (k0_ref, k1_ref)

    def scores(j0, buf):
        start = pl.multiple_of(j0 * t, t)
        for h in range(4):
            s_ref[buf, h] = _dot(k_refs[h // 2][0, pl.ds(start, CHUNK_TILES * t), :], qxt_ref[h])

    def rest(j0, w, diag, buf):
        for h in range(4):
            g, r = divmod(h, 2)
            s = s_ref[buf, h, 0:w * t, :]
            if diag:
                s = jnp.where(_causal_chunk(w, t), s, -jnp.inf)
            p, alpha = _softmax_step(s, m_ref, l_ref, h)
            _accumulate(acc_ref, g, r, alpha,
                        [vt_ref[0, j0 + i, h * HEAD_DIM:(h + 1) * HEAD_DIM, :] for i in range(w)], p, t)

    _pipelined_key_chunks(n, scores, rest, CHUNK_TILES)
    for g in range(2):
        o_ref[0, :, g * LANES:(g + 1) * LANES] = _normalised_t(acc_ref, g, l_ref, 2 * g, 2 * g + 1).T.astype(BF16)


def _fox_call(qk, vt, *, t):
    b, s, _ = qk.shape
    nt = s // t
    return pl.pallas_call(
        functools.partial(_fox_kernel, t=t),
        grid=(b, nt),
        in_specs=[pl.BlockSpec((1, t, GROUP), lambda bi, ni: (bi, ni, QK_BLK["b_q"])),
                  pl.BlockSpec((1, s, GROUP), lambda bi, ni: (bi, 0, QK_BLK["b_k"]), pipeline_mode=RESIDENT),
                  pl.BlockSpec((1, s, GROUP), lambda bi, ni: (bi, 0, QK_BLK["b_k"] + 1), pipeline_mode=RESIDENT),
                  pl.BlockSpec((1, nt, GROUP, t), lambda bi, ni: (bi, 0, V_ROW_BLK["b_v"], 0), pipeline_mode=RESIDENT)],
        out_specs=pl.BlockSpec((1, t, GROUP), lambda bi, ni: (bi, ni, 0)),
        out_shape=jax.ShapeDtypeStruct((b, s, GROUP), BF16),
        scratch_shapes=[pltpu.VMEM((4, 1, t), F32), pltpu.VMEM((4, 1, t), F32), pltpu.VMEM((2, LANES, t), F32),
                        pltpu.VMEM((4, GROUP, t), BF16), pltpu.VMEM((2, 4, CHUNK_TILES * t, t), F32)],
        compiler_params=_cparams(("arbitrary", "arbitrary")),
        name="fox",
    )(qk, qk, qk, vt)


def _diff_kernel(lam_ref, q_ref, k_ref, vt_ref, gs_ref, o_ref, m_ref, l_ref, acc_ref, qzt_ref, s_ref, *, t):
    n = pl.program_id(1)
    _init_state(m_ref, l_ref, acc_ref)
    lane = lax.broadcasted_iota(I32, (1, LANES), 1)
    for h in range(4):
        for mm in range(2):
            lo = (h % 2) * HEAD_DIM + mm * DIFF_DIM
            qzt_ref[2 * h + mm] = _transpose_bf16(
                jnp.where((lane >= lo) & (lane < lo + DIFF_DIM),
                          q_ref[0, :, (h // 2) * LANES:(h // 2 + 1) * LANES], jnp.zeros((), BF16)))

    def scores(j0, buf):
        start = pl.multiple_of(j0 * t, t)
        ks = k_ref[0, pl.ds(start, CHUNK_TILES_DIFF * t), :]
        for i in range(8):
            s_ref[buf, i] = _dot(ks[:, (i // 4) * LANES:(i // 4 + 1) * LANES], qzt_ref[i])

    def rest(j0, w, diag, buf):
        for h in range(4):
            g, r = divmod(h, 2)
            vts = [vt_ref[0, j0 + i, h * HEAD_DIM:(h + 1) * HEAD_DIM, :] for i in range(w)]
            for mm in range(2):
                s = s_ref[buf, 2 * h + mm, 0:w * t, :]
                if diag:
                    s = jnp.where(_causal_chunk(w, t), s, -jnp.inf)
                p, alpha = _softmax_step(s, m_ref, l_ref, 2 * h + mm)
                _accumulate(acc_ref, 2 * mm + g, r, alpha, vts, p, t)

    _pipelined_key_chunks(n, scores, rest, CHUNK_TILES_DIFF)
    lam = lam_ref[0]
    out_scale = lam_ref[1]
    for g in range(2):
        o1 = _normalised_t(acc_ref, g, l_ref, 4 * g, 4 * g + 2)
        o2 = _normalised_t(acc_ref, 2 + g, l_ref, 4 * g + 1, 4 * g + 3)
        o = o1 - lam * o2
        sq = o * o
        ms = jnp.concatenate(
            [jnp.broadcast_to(jnp.mean(sq[r * HEAD_DIM:(r + 1) * HEAD_DIM], axis=0, keepdims=True), (HEAD_DIM, t))
             for r in range(2)], axis=0)
        y = o * lax.rsqrt(ms + EPS) * gs_ref[...] * out_scale
        o_ref[0, :, g * LANES:(g + 1) * LANES] = y.T.astype(BF16)


def _diff_call(lam2, qk, vt, gs_full, *, t):
    b, s, _ = qk.shape
    nt = s // t
    return pl.pallas_call(
        functools.partial(_diff_kernel, t=t),
        grid=(b, nt),
        in_specs=[pl.BlockSpec(memory_space=pltpu.SMEM),
                  pl.BlockSpec((1, t, GROUP), lambda bi, ni: (bi, ni, QK_BLK["c_q"])),
                  pl.BlockSpec((1, s, GROUP), lambda bi, ni: (bi, 0, QK_BLK["c_k"]), pipeline_mode=RESIDENT),
                  pl.BlockSpec((1, nt, GROUP, t), lambda bi, ni: (bi, 0, V_ROW_BLK["c_v"], 0), pipeline_mode=RESIDENT),
                  pl.BlockSpec((LANES, t), lambda bi, ni: (0, 0))],
        out_specs=pl.BlockSpec((1, t, GROUP), lambda bi, ni: (bi, ni, 0)),
        out_shape=jax.ShapeDtypeStruct((b, s, GROUP), BF16),
        scratch_shapes=[pltpu.VMEM((8, 1, t), F32), pltpu.VMEM((8, 1, t), F32), pltpu.VMEM((4, LANES, t), F32),
                        pltpu.VMEM((8, LANES, t), BF16), pltpu.VMEM((2, 8, CHUNK_TILES_DIFF * t, t), F32)],
        compiler_params=_cparams(("arbitrary", "arbitrary")),
        name="diff",
    )(lam2, qk, qk, vt, gs_full)


def _swa_kernel(sink_ref, q_ref, kp_ref, kc_ref, vtp_ref, vtc_ref, o_ref, *, t):
    n = pl.program_id(1)
    prow = lax.broadcasted_iota(I32, (WINDOW, t), 0)
    pcol = lax.broadcasted_iota(I32, (WINDOW, t), 1)
    mask_prev = (prow > pcol) & (pcol + jnp.where(n > 0, 0, t) < WINDOW)
    crow = lax.broadcasted_iota(I32, (t, t), 0)
    ccol = lax.broadcasted_iota(I32, (t, t), 1)
    mask_cur = (crow <= ccol) & (crow > ccol - WINDOW)
    vtp = vtp_ref[0, 0][:, t - WINDOW:t]
    vtc = vtc_ref[0, 0]
    raw = []
    for h in range(4):
        g, r = divmod(h, 2)
        qzt = _transpose_bf16(jnp.where(_head_mask(r), q_ref[0, :, g * LANES:(g + 1) * LANES], jnp.zeros((), BF16)))
        raw.append((_dot(kp_ref[0, :, g * LANES:(g + 1) * LANES], qzt), _dot(kc_ref[0, :, g * LANES:(g + 1) * LANES], qzt)))
    for g in range(2):
        outs = []
        for r in range(2):
            h = 2 * g + r
            sp = jnp.where(mask_prev, raw[h][0], -jnp.inf)
            sc = jnp.where(mask_cur, raw[h][1], -jnp.inf)
            sink = sink_ref[h] * LOG2E
            m = jnp.maximum(jnp.maximum(jnp.max(sp, axis=0, keepdims=True),
                                        jnp.max(sc, axis=0, keepdims=True)), sink)
            pp = jnp.exp2(sp - m)
            pc = jnp.exp2(sc - m)
            den = jnp.sum(pp, axis=0, keepdims=True) + jnp.sum(pc, axis=0, keepdims=True) + jnp.exp2(sink - m)
            rows = slice(h * HEAD_DIM, (h + 1) * HEAD_DIM)
            o = (jnp.dot(vtp[rows, :], pp.astype(BF16), preferred_element_type=F32)
                 + jnp.dot(vtc[rows, :], pc.astype(BF16), preferred_element_type=F32))
            outs.append(o / den)
        o_ref[0, :, g * LANES:(g + 1) * LANES] = jnp.concatenate(outs, axis=0).T.astype(BF16)


def _swa_call(sinks, qk, vt, *, t):
    b, s, _ = qk.shape
    nt = s // t
    per = t // WINDOW
    return pl.pallas_call(
        functools.partial(_swa_kernel, t=t),
        grid=(b, nt),
        in_specs=[pl.BlockSpec(memory_space=pltpu.SMEM),
                  pl.BlockSpec((1, t, GROUP), lambda bi, ni: (bi, ni, QK_BLK["d_q"])),
                  pl.BlockSpec((1, WINDOW, GROUP), lambda bi, ni: (bi, jnp.maximum(ni * per - 1, 0), QK_BLK["d_k"])),
                  pl.BlockSpec((1, t, GROUP), lambda bi, ni: (bi, ni, QK_BLK["d_k"])),
                  pl.BlockSpec((1, 1, GROUP, t), lambda bi, ni: (bi, jnp.maximum(ni - 1, 0), V_ROW_BLK["d_v"], 0)),
                  pl.BlockSpec((1, 1, GROUP, t), lambda bi, ni: (bi, ni, V_ROW_BLK["d_v"], 0))],
        out_specs=pl.BlockSpec((1, t, GROUP), lambda bi, ni: (bi, ni, 0)),
        out_shape=jax.ShapeDtypeStruct((b, s, GROUP), BF16),
        compiler_params=_cparams(("arbitrary", "arbitrary")),
        name="swa",
    )(sinks, qk, qk, qk, vt, vt)


def _post_kernel(oa_ref, ob_ref, oc_ref, od_ref, x_ref, p_ref, wo_ref, gmix_ref, gpre_ref, wu_ref, wd_ref, gpost_ref,
                 wg_ref, wp_ref, y_ref, *, chunk):
    acc = jnp.zeros(x_ref.shape[1:], F32)
    for i, o_ref in enumerate((oa_ref, ob_ref, oc_ref, od_ref)):
        acc = acc + jnp.dot(o_ref[0], wo_ref[i * GROUP:(i + 1) * GROUP, :], preferred_element_type=F32)
    x = x_ref[0] + _rms(acc, gmix_ref[...])
    h = _rms(x, gpre_ref[...]).astype(BF16)
    acc = jnp.zeros(x.shape, F32)
    for c in range(D_FF // chunk):
        u = jnp.dot(h, wu_ref[:, c * chunk:(c + 1) * chunk], preferred_element_type=F32)
        u = jnp.square(jnp.maximum(u, 0.0)).astype(BF16)
        acc = acc + jnp.dot(u, wd_ref[c * chunk:(c + 1) * chunk, :], preferred_element_type=F32)
    x = x + _rms(acc, gpost_ref[...])
    gate = jax.nn.sigmoid(jnp.dot(x.astype(BF16), wg_ref[...], preferred_element_type=F32))
    emb = jnp.dot(p_ref[0].astype(BF16), wp_ref[...], preferred_element_type=F32)
    y_ref[0] = x + gate * emb


def _post_call(tok_inputs, full_inputs, *, tm):
    b, s, d = tok_inputs[4].shape
    tok = lambda a: pl.BlockSpec((1, tm, a.shape[2]), lambda bi, ti: (bi, ti, 0))
    full = lambda a: pl.BlockSpec(a.shape, lambda bi, ti: (0,) * a.ndim, pipeline_mode=pl.Buffered(1))
    return pl.pallas_call(
        functools.partial(_post_kernel, chunk=1024),
        grid=(b, s // tm),
        in_specs=[tok(a) for a in tok_inputs] + [full(a) for a in full_inputs],
        out_specs=pl.BlockSpec((1, tm, d), lambda bi, ti: (bi, ti, 0)),
        out_shape=jax.ShapeDtypeStruct((b, s, d), F32),
        compiler_params=_cparams(("arbitrary", "arbitrary")),
        name="post",
    )(*tok_inputs, *full_inputs)


def _dup_kv(w):
    return jnp.concatenate([w[:, :HEAD_DIM], w[:, :HEAD_DIM], w[:, HEAD_DIM:], w[:, HEAD_DIM:]], axis=1)


def _prep_w_in(w):
    sec = lambda n: w[:, _SEC[n][0]:_SEC[n][0] + _SEC[n][1]]
    cols = [_dup_kv(sec(n)) if n in ("d_k", "d_v") else sec(n) for n in QK_GROUPS + V_GROUPS]
    w_main = jnp.concatenate(cols, axis=1).astype(BF16)
    pad = jnp.zeros((w.shape[0], LANES - IDX_DIM - IDX_HEADS - 4), w.dtype)
    w_idx = jnp.concatenate([sec("iq"), sec("ik"), sec("iw"), sec("b_f"), pad], axis=1)
    w_hi = w_idx.astype(BF16)
    w_lo = (w_idx - w_hi.astype(F32)).astype(BF16)
    return w_main, jnp.concatenate([w_hi, w_hi, w_lo], axis=0)


def _rope_tables(positions):
    pos = positions.astype(F32)[..., None]
    lane = jnp.arange(LANES)

    def tabs(dim):
        half = dim // 2
        inv_freq = ROPE_THETA ** (-jnp.arange(half, dtype=F32) / half)
        ang = pos * inv_freq
        sign = jnp.where((lane % dim) < half, -1.0, 1.0).astype(F32)
        reps = (1, 1, LANES // half)
        return jnp.tile(jnp.cos(ang), reps), jnp.tile(jnp.sin(ang), reps) * sign

    c64, s64 = tabs(HEAD_DIM)
    c32, s32 = tabs(DIFF_DIM)
    return c64, s64, c32, s32


def kernel(x, p, positions, w_in, b_forget, lambda_q1, lambda_k1, lambda_q2, lambda_k2, diff_subln, sinks,
           w_out, norm_pre_mix, norm_post_mix, norm_pre_mlp, norm_post_mlp, w_mlp_up, w_mlp_down,
           w_ple_proj, w_ple_gate):
    b, s, d = x.shape
    depth = w_in.shape[0]
    t = min(256, s)
    assert s % (CHUNK_TILES * t) == 0, "sequence length must be a multiple of the attention key chunk"
    topk = min(TOPK_MAX, s // 4)
    tabs = _rope_tables(positions)
    row = lambda v: v.reshape(1, -1).astype(F32)

    for i in range(depth):
        lam_init = 0.8 - 0.6 * math.exp(-0.3 * i)
        w_main, w_idx = _prep_w_in(w_in[i])
        bf_row = jnp.zeros((1, LANES), F32).at[0, MISC_F:MISC_F + 4].set(b_forget[i])
        qk, vt, iq, ik, iw = _proj_call(x, row(norm_pre_mix[i]), w_main, w_idx, tabs, bf_row, tm=t)

        o_a = _dsa_call(qk, vt, iq, ik, iw, t=t, topk=topk)
        o_b = _fox_call(qk, vt, t=t)
        lam = (jnp.exp(jnp.sum(lambda_q1[i] * lambda_k1[i])) - jnp.exp(jnp.sum(lambda_q2[i] * lambda_k2[i]))
               + lam_init)
        lam2 = jnp.stack([lam, jnp.asarray(1.0 - lam_init, F32)]).astype(F32)
        gs_full = jnp.broadcast_to(jnp.concatenate([diff_subln[i], diff_subln[i]]).astype(F32)[:, None], (LANES, t))
        o_c = _diff_call(lam2, qk, vt, gs_full, t=t)
        o_d = _swa_call(sinks[i].astype(F32), qk, vt, t=t)

        x = _post_call([o_a, o_b, o_c, o_d, x, p[i]],
                       [w_out[i].astype(BF16), row(norm_post_mix[i]), row(norm_pre_mlp[i]),
                        w_mlp_up[i].astype(BF16), w_mlp_down[i].astype(BF16), row(norm_post_mlp[i]),
                        w_ple_gate[i].astype(BF16), w_ple_proj[i].astype(BF16)], tm=min(POST_TILE, s))
    return x
```
